```python
import math
import jax, jax.numpy as jnp
from jax import lax
import numpy as np

D_MODEL = 1024
BATCH = 8
SEQ = 2048
DEPTH = 4

GRID_W = 64
CTX_LEN = 256

HEAD_DIM = 128
N_Q_HEADS = D_MODEL // HEAD_DIM
N_KV_HEADS = 2
GQA_GROUP = N_Q_HEADS // N_KV_HEADS
Q_BLOCK = 128
ROPE_THETA = 10000.0
ROPE_AXIS_DIM = HEAD_DIM // 2
ATTN_SCALE = HEAD_DIM ** -0.5

D_HYENA = D_MODEL
HYENA_ORDER = 2
HYENA_BANDS = 16
HYENA_EMB = 1 + 2 * HYENA_BANDS
HYENA_FILTER_HIDDEN = 64
SHORT_CONV = 3
DECAY_TARGET = 1e-2
FAST_DECAY_PCT = 0.3
SLOW_DECAY_PCT = 1.5
MIN_DECAY = math.log(DECAY_TARGET) / FAST_DECAY_PCT
MAX_DECAY = math.log(DECAY_TARGET) / SLOW_DECAY_PCT

D_FNET = D_MODEL
FNET_GROUPS = 4
FNET_GROUP_DIM = D_FNET // FNET_GROUPS

D_FF = 4 * D_MODEL

N_BRANCH = 3
N_MOD = 6

Q_W = N_Q_HEADS * HEAD_DIM
KV_W = N_KV_HEADS * HEAD_DIM
HY_W = (HYENA_ORDER + 1) * D_HYENA
OFF_Q = 0
OFF_K = OFF_Q + Q_W
OFF_V = OFF_K + KV_W
OFF_HY = OFF_V + KV_W
OFF_FN = OFF_HY + HY_W
OFF_G = OFF_FN + D_FNET
D_IN = OFF_G + N_BRANCH * D_MODEL

ALPHA = (2 * DEPTH) ** 0.25
BETA = (8 * DEPTH) ** -0.25
LN_EPS = 1e-6
RMS_EPS = 1e-6
KERN_EPS = 1e-6

kernel_name = 'hybrid_hyena_fnet_gqa_flow_block'


def layer_norm(x, g, b):
    xf = x.astype(jnp.float32)
    mu = jnp.mean(xf, -1, keepdims=True)
    var = jnp.mean(jnp.square(xf - mu), -1, keepdims=True)
    return ((xf - mu) * lax.rsqrt(var + LN_EPS)).astype(x.dtype) * g + b


def rms_norm(x, g):
    xf = x.astype(jnp.float32)
    return (xf * lax.rsqrt(jnp.mean(jnp.square(xf), -1, keepdims=True) + RMS_EPS)).astype(x.dtype) * g


def modulate(x, shift, scale):
    return x * (1.0 + scale) + shift


def split_heads(t, n_heads):
    b, n, _ = t.shape
    return t.reshape(b, n, n_heads, HEAD_DIM).transpose(0, 2, 1, 3)


def merge_heads(t):
    b, h, n, d = t.shape
    return t.transpose(0, 2, 1, 3).reshape(b, n, h * d)


def axial_rope_tables(rows):
    row = jnp.repeat(jnp.arange(rows, dtype=jnp.float32), GRID_W)
    col = jnp.tile(jnp.arange(GRID_W, dtype=jnp.float32), rows)
    inv_freq = ROPE_THETA ** (-jnp.arange(ROPE_AXIS_DIM // 2, dtype=jnp.float32) * 2.0 / ROPE_AXIS_DIM)
    ang_r = row[:, None] * inv_freq[None, :]
    ang_c = col[:, None] * inv_freq[None, :]
    return jnp.cos(ang_r), jnp.sin(ang_r), jnp.cos(ang_c), jnp.sin(ang_c)


def rope_1d(x, cos, sin):
    x1, x2 = jnp.split(x, 2, -1)
    return jnp.concatenate([x1 * cos - x2 * sin, x1 * sin + x2 * cos], -1)


def apply_axial_rope(x, cos_r, sin_r, cos_c, sin_c):
    x_row, x_col = jnp.split(x, 2, -1)
    return jnp.concatenate([rope_1d(x_row, cos_r, sin_r), rope_1d(x_col, cos_c, sin_c)], -1)


def gqa_attend(q, k, v):
    s = jnp.einsum('bgrqd,bgkd->bgrqk', q, k).astype(jnp.float32) * ATTN_SCALE
    p = jax.nn.softmax(s, axis=-1).astype(v.dtype)
    return jnp.einsum('bgrqk,bgkd->bgrqd', p, v)


def blockwise_attention(q, k, v):
    b, h, n, d = q.shape
    n_blk = n // Q_BLOCK
    qb = q.reshape(b, N_KV_HEADS, GQA_GROUP, n_blk, Q_BLOCK, d).transpose(3, 0, 1, 2, 4, 5)
    ob = lax.map(lambda qi: gqa_attend(qi, k, v), qb)
    return ob.transpose(1, 2, 3, 0, 4, 5).reshape(b, h, n, d)


def kv_heads(p_kv, k_gain):
    k = rms_norm(split_heads(p_kv[..., :KV_W], N_KV_HEADS), k_gain)
    v = split_heads(p_kv[..., KV_W:], N_KV_HEADS)
    return k, v


def short_conv(u, w, b):
    n = u.shape[1]
    pad = SHORT_CONV // 2
    up = jnp.pad(u, ((0, 0), (pad, SHORT_CONV - 1 - pad), (0, 0)))
    out = b
    for j in range(SHORT_CONV):
        out = out + up[:, j:j + n] * w[j]
    return out


def hyena_kernel(n, lp):
    dt = lp['hy_w1'].dtype
    t = jnp.linspace(0.0, 1.0, n, dtype=jnp.float32)[:, None]
    w = (2.0 * math.pi / n) * jnp.arange(n, dtype=jnp.float32)[:, None]
    f = jnp.linspace(1e-4, HYENA_BANDS - 1, HYENA_BANDS, dtype=jnp.float32)[None, :]
    emb = jnp.concatenate([t, jnp.cos(f * w), -jnp.sin(f * w)], -1).astype(dt)
    h = jnp.sin(lp['hy_freq1'] * (emb @ lp['hy_w1'] + lp['hy_b1']))
    h = jnp.sin(lp['hy_freq2'] * (h @ lp['hy_w2'] + lp['hy_b2']))
    h = (h @ lp['hy_w3']).astype(jnp.float32).reshape(n, 2, HYENA_ORDER, D_HYENA)
    deltas = jnp.linspace(MIN_DECAY, MAX_DECAY, D_HYENA, dtype=jnp.float32)
    h = h * jnp.exp(-t * jnp.abs(deltas))[:, None, None, :]
    kern = jnp.concatenate([h[:, 0], jnp.zeros_like(h[:1, 0]), h[:0:-1, 1]], 0)
    return kern * lax.rsqrt(jnp.sum(jnp.square(kern), 0, keepdims=True) + KERN_EPS)


def bidirectional_long_conv(u, kf, skip):
    n = u.shape[1]
    uf = jnp.fft.rfft(u.astype(jnp.float32), n=2 * n, axis=1)
    y = jnp.fft.irfft(uf * kf[None], n=2 * n, axis=1)[:, :n]
    return (y + u.astype(jnp.float32) * skip.astype(jnp.float32)).astype(u.dtype)


def hyena_branch(p_hy, lp, kern):
    u = short_conv(p_hy, lp['conv_w'], lp['conv_b'])
    xs = jnp.split(u, HYENA_ORDER + 1, -1)
    kf = jnp.fft.rfft(kern, axis=0)
    z = xs[HYENA_ORDER]
    for o in range(HYENA_ORDER):
        z = xs[o] * bidirectional_long_conv(z, kf[:, o], lp['hy_skip'][o])
    return z


def fnet_mix(u):
    b, n, _ = u.shape
    ug = u.reshape(b, n, FNET_GROUPS, FNET_GROUP_DIM).astype(jnp.float32)
    y = jnp.fft.fft2(ug, axes=(1, 3), norm='ortho').real
    return y.reshape(b, n, D_FNET).astype(u.dtype)


def parallel_merge(p, o_att, lp, kern):
    y_att = o_att @ lp['w_att_o']
    y_hy = hyena_branch(p[..., OFF_HY:OFF_FN], lp, kern) @ lp['w_hy_o']
    y_fn = fnet_mix(p[..., OFF_FN:OFF_G]) @ lp['w_fn_o']
    g_att, g_hy, g_fn = jnp.split(jax.nn.sigmoid(p[..., OFF_G:D_IN]), N_BRANCH, -1)
    return (g_att * y_att + g_hy * y_hy + g_fn * y_fn) @ lp['w_out'] + lp['b_out']


def context_mixer(p, lp, kern):
    q = rms_norm(split_heads(p[..., OFF_Q:OFF_K], N_Q_HEADS), lp['q_gain'])
    k, v = kv_heads(p[..., OFF_K:OFF_HY], lp['k_gain'])
    b, h, n, d = q.shape
    o = gqa_attend(q.reshape(b, N_KV_HEADS, GQA_GROUP, n, d), k, v).reshape(b, h, n, d)
    return parallel_merge(p, merge_heads(o), lp, kern), k, v


def latent_mixer(p, lp, kern, rope, k_ctx, v_ctx):
    q = apply_axial_rope(rms_norm(split_heads(p[..., OFF_Q:OFF_K], N_Q_HEADS), lp['q_gain']), *rope)
    k_lat, v_lat = kv_heads(p[..., OFF_K:OFF_HY], lp['k_gain'])
    k_lat = apply_axial_rope(k_lat, *rope)
    k = jnp.concatenate([k_lat, k_ctx], 2)
    v = jnp.concatenate([v_lat, v_ctx], 2)
    o = blockwise_attention(q, k, v)
    return parallel_merge(p, merge_heads(o), lp, kern)


def sq_relu_mlp(h, lp):
    return jnp.square(jax.nn.relu(h @ lp['w_mlp1'] + lp['b_mlp1'])) @ lp['w_mlp2'] + lp['b_mlp2']


def setup_inputs(seed: int = 0) -> dict:
    key = jax.random.key(seed)
    ks = iter(jax.random.split(key, 40))

    def nrm(shape, scale):
        return jax.random.normal(next(ks), shape, jnp.float32) * scale

    def gain(shape):
        return 1.0 + nrm(shape, 0.01)

    fh = HYENA_FILTER_HIDDEN
    return {
        'x': nrm((BATCH, SEQ, D_MODEL), 1.0),
        'c': nrm((BATCH, D_MODEL), 1.0),
        'ctx': nrm((BATCH, CTX_LEN, D_MODEL), 1.0),
        'c_ctx': nrm((D_MODEL,), 1.0),
        'w_ada': nrm((DEPTH, D_MODEL, N_MOD * D_MODEL), D_MODEL ** -0.5),
        'b_ada': nrm((DEPTH, N_MOD * D_MODEL), 0.01),
        'w_in': nrm((DEPTH, D_MODEL, D_IN), D_MODEL ** -0.5),
        'b_in': nrm((DEPTH, D_IN), 0.01),
        'conv_w': nrm((DEPTH, SHORT_CONV, HY_W), SHORT_CONV ** -0.5),
        'conv_b': nrm((DEPTH, HY_W), 0.01),
        'hy_w1': nrm((DEPTH, HYENA_EMB, fh), HYENA_EMB ** -0.5),
        'hy_b1': nrm((DEPTH, fh), 0.01),
        'hy_freq1': gain((DEPTH, fh)),
        'hy_w2': nrm((DEPTH, fh, fh), fh ** -0.5),
        'hy_b2': nrm((DEPTH, fh), 0.01),
        'hy_freq2': gain((DEPTH, fh)),
        'hy_w3': nrm((DEPTH, fh, 2 * HYENA_ORDER * D_HYENA), fh ** -0.5),
        'hy_skip': nrm((DEPTH, HYENA_ORDER, D_HYENA), 0.1),
        'q_gain': gain((DEPTH, HEAD_DIM)),
        'k_gain': gain((DEPTH, HEAD_DIM)),
        'w_att_o': nrm((DEPTH, Q_W, D_MODEL), Q_W ** -0.5),
        'w_hy_o': nrm((DEPTH, D_HYENA, D_MODEL), D_HYENA ** -0.5),
        'w_fn_o': nrm((DEPTH, D_FNET, D_MODEL), D_FNET ** -0.5),
        'w_out': nrm((DEPTH, D_MODEL, D_MODEL), BETA * D_MODEL ** -0.5),
        'b_out': nrm((DEPTH, D_MODEL), 0.01),
        'ln1_g': gain((DEPTH, D_MODEL)),
        'ln1_b': nrm((DEPTH, D_MODEL), 0.01),
        'w_mlp1': nrm((DEPTH, D_MODEL, D_FF), D_MODEL ** -0.5),
        'b_mlp1': nrm((DEPTH, D_FF), 0.01),
        'w_mlp2': nrm((DEPTH, D_FF, D_MODEL), BETA * D_FF ** -0.5),
        'b_mlp2': nrm((DEPTH, D_MODEL), 0.01),
        'ln2_g': gain((DEPTH, D_MODEL)),
        'ln2_b': nrm((DEPTH, D_MODEL), 0.01),
    }


def reference(x, c, ctx, c_ctx, w_ada, b_ada, w_in, b_in, conv_w, conv_b,
              hy_w1, hy_b1, hy_freq1, hy_w2, hy_b2, hy_freq2, hy_w3, hy_skip,
              q_gain, k_gain, w_att_o, w_hy_o, w_fn_o, w_out, b_out, ln1_g, ln1_b,
              w_mlp1, b_mlp1, w_mlp2, b_mlp2, ln2_g, ln2_b):
    n_lat = x.shape[1]
    n_ctx = ctx.shape[1]
    ROWS = n_lat // GRID_W
    rope = tuple(t.astype(x.dtype) for t in axial_rope_tables(ROWS))
    s_lat = jax.nn.silu(c)
    s_ctx = jax.nn.silu(c_ctx)
    for i in range(DEPTH):
        last = i == DEPTH - 1
        lp = dict(conv_w=conv_w[i], conv_b=conv_b[i], hy_w1=hy_w1[i], hy_b1=hy_b1[i],
                  hy_freq1=hy_freq1[i], hy_w2=hy_w2[i], hy_b2=hy_b2[i], hy_freq2=hy_freq2[i],
                  hy_w3=hy_w3[i], hy_skip=hy_skip[i], q_gain=q_gain[i], k_gain=k_gain[i],
                  w_att_o=w_att_o[i], w_hy_o=w_hy_o[i], w_fn_o=w_fn_o[i], w_out=w_out[i],
                  b_out=b_out[i], w_mlp1=w_mlp1[i], b_mlp1=b_mlp1[i], w_mlp2=w_mlp2[i],
                  b_mlp2=b_mlp2[i])
        mod_l = (s_lat @ w_ada[i] + b_ada[i]).reshape(x.shape[0], 1, N_MOD, D_MODEL)
        mod_c = (s_ctx @ w_ada[i] + b_ada[i]).reshape(1, 1, N_MOD, D_MODEL)

        h_c = modulate(ctx, mod_c[:, :, 0], mod_c[:, :, 1])
        if last:
            k_c, v_c = kv_heads(h_c @ w_in[i][:, OFF_K:OFF_HY] + b_in[i][OFF_K:OFF_HY], lp['k_gain'])
        else:
            y_c, k_c, v_c = context_mixer(h_c @ w_in[i] + b_in[i], lp, hyena_kernel(n_ctx, lp))
            ctx = layer_norm(ALPHA * ctx + mod_c[:, :, 2] * y_c, ln1_g[i], ln1_b[i])
            y_c = sq_relu_mlp(modulate(ctx, mod_c[:, :, 3], mod_c[:, :, 4]), lp)
            ctx = layer_norm(ALPHA * ctx + mod_c[:, :, 5] * y_c, ln2_g[i], ln2_b[i])

        h_l = modulate(x, mod_l[:, :, 0], mod_l[:, :, 1])
        y_l = latent_mixer(h_l @ w_in[i] + b_in[i], lp, hyena_kernel(n_lat, lp), rope, k_c, v_c)
        x = layer_norm(ALPHA * x + mod_l[:, :, 2] * y_l, ln1_g[i], ln1_b[i])
        y_l = sq_relu_mlp(modulate(x, mod_l[:, :, 3], mod_l[:, :, 4]), lp)
        x = layer_norm(ALPHA * x + mod_l[:, :, 5] * y_l, ln2_g[i], ln2_b[i])
    return x
```

```python
import functools
import math

import jax
import jax.numpy as jnp
from jax import lax
from jax.experimental import pallas as pl
from jax.experimental.pallas import tpu as pltpu

F32 = jnp.float32
BF16 = jnp.bfloat16

D_MODEL = 1024
GRID_W = 64
HEAD_DIM = 128
N_Q_HEADS = D_MODEL // HEAD_DIM
N_KV_HEADS = 2
GQA_GROUP = N_Q_HEADS // N_KV_HEADS
ROPE_THETA = 10000.0
ROPE_AXIS_DIM = HEAD_DIM // 2
ATTN_SCALE = HEAD_DIM ** -0.5

D_HYENA = D_MODEL
HYENA_ORDER = 2
HYENA_BANDS = 16
HYENA_EMB = 1 + 2 * HYENA_BANDS
SHORT_CONV = 3
DECAY_TARGET = 1e-2
MIN_DECAY = math.log(DECAY_TARGET) / 0.3
MAX_DECAY = math.log(DECAY_TARGET) / 1.5

FNET_GROUPS = 4
FNET_GROUP_DIM = D_MODEL // FNET_GROUPS
D_FF = 4 * D_MODEL
N_MOD = 6
DEPTH = 4

Q_W = N_Q_HEADS * HEAD_DIM
KV_W = N_KV_HEADS * HEAD_DIM
HY_W = (HYENA_ORDER + 1) * D_HYENA
OFF_Q = 0
OFF_K = OFF_Q + Q_W
OFF_V = OFF_K + KV_W
OFF_HY = OFF_V + KV_W
OFF_FN = OFF_HY + HY_W
OFF_G = OFF_FN + D_MODEL
D_IN = OFF_G + 3 * D_MODEL

P_Q = 0
P_G = P_Q + Q_W
P_HY = P_G + 3 * D_MODEL
P_FN = P_HY + HY_W
P_K = P_FN + D_MODEL
P_V = P_K + KV_W

ALPHA = (2 * DEPTH) ** 0.25
LN_EPS = 1e-6
RMS_EPS = 1e-6
KERN_EPS = 1e-6

V7X_VMEM_LIMIT_BYTES = 56 * 1024 * 1024


def _params(*sem):
    return pltpu.CompilerParams(dimension_semantics=sem, vmem_limit_bytes=V7X_VMEM_LIMIT_BYTES)


def _dot(a, b):
    return jnp.dot(a, b, preferred_element_type=F32)


def _layer_norm(r, g, b):
    mu = jnp.mean(r, -1, keepdims=True)
    d = r - mu
    var = jnp.mean(d * d, -1, keepdims=True)
    return d * lax.rsqrt(var + LN_EPS) * g + b


def _sigmoid(x):
    return 1.0 / (1.0 + jnp.exp(-x))


def _matmul_kernel(a_ref, b_ref, bias_ref, o_ref, *, silu):
    a = a_ref[...]
    if silu:
        a = a * _sigmoid(a)
    o_ref[...] = _dot(a.astype(BF16), b_ref[...]) + bias_ref[...]


def _matmul(a, b, bias, *, silu=False, tn=1024):
    m, k = a.shape
    n = b.shape[1]
    return pl.pallas_call(
        functools.partial(_matmul_kernel, silu=silu),
        grid=(n // tn,),
        in_specs=[pl.BlockSpec((m, k), lambda j: (0, 0)),
                  pl.BlockSpec((k, tn), lambda j: (0, j)),
                  pl.BlockSpec((1, tn), lambda j: (0, j))],
        out_specs=pl.BlockSpec((m, tn), lambda j: (0, j)),
        out_shape=jax.ShapeDtypeStruct((m, n), F32),
        compiler_params=_params("arbitrary"),
        name="ada_matmul",
    )(a, b, bias)


def _inproj_kernel(x_ref, mod_ref, w_ref, b_ref, o_ref, h_ref):
    @pl.when(pl.program_id(2) == 0)
    def _():
        h_ref[...] = (x_ref[...] * (1.0 + mod_ref[1:2, :]) + mod_ref[0:1, :]).astype(BF16)

    o_ref[...] = (_dot(h_ref[...], w_ref[...]) + b_ref[...]).astype(o_ref.dtype)


def _inproj(x, mod, w, b):
    bsz, n, d = x.shape
    nout = w.shape[1]
    tm = min(n, 2048)
    tn = min(nout, 512)
    return pl.pallas_call(
        _inproj_kernel,
        grid=(bsz, n // tm, nout // tn),
        in_specs=[pl.BlockSpec((None, tm, d), lambda bi, i, j: (bi, i, 0)),
                  pl.BlockSpec((None, N_MOD, d), lambda bi, i, j: (bi, 0, 0)),
                  pl.BlockSpec((d, tn), lambda bi, i, j: (0, j)),
                  pl.BlockSpec((1, tn), lambda bi, i, j: (0, j))],
        out_specs=pl.BlockSpec((None, tm, tn), lambda bi, i, j: (bi, i, j)),
        out_shape=jax.ShapeDtypeStruct((bsz, n, nout), BF16),
        scratch_shapes=[pltpu.VMEM((tm, d), BF16)],
        compiler_params=_params("arbitrary", "arbitrary", "arbitrary"),
        name="inproj",
    )(x, mod, w, b)


def _headnorm_kernel(*refs, n_heads, rope, scale):
    if rope:
        p_ref, g_ref, cos_ref, sin_ref, o_ref = refs
    else:
        p_ref, g_ref, o_ref = refs
    gain = g_ref[...]
    for h in range(n_heads):
        sl = slice(h * HEAD_DIM, (h + 1) * HEAD_DIM)
        xf = p_ref[:, sl].astype(F32)
        y = xf * lax.rsqrt(jnp.mean(xf * xf, -1, keepdims=True) + RMS_EPS) * gain
        if rope:
            lane = lax.broadcasted_iota(jnp.int32, y.shape, 1)
            first = (lane % (ROPE_AXIS_DIM)) < (ROPE_AXIS_DIM // 2)
            swapped = jnp.where(first, pltpu.roll(y, HEAD_DIM - ROPE_AXIS_DIM // 2, 1),
                                pltpu.roll(y, ROPE_AXIS_DIM // 2, 1))
            y = y * cos_ref[...] + swapped * sin_ref[...]
        if scale != 1.0:
            y = y * scale
        o_ref[:, sl] = y.astype(o_ref.dtype)


def _headnorm(p, col_block, n_heads, gain, rope_tabs, scale):
    bsz, n, _ = p.shape
    w = n_heads * HEAD_DIM
    tm = min(n, 512)
    rope = rope_tabs is not None
    in_specs = [pl.BlockSpec((None, tm, w), lambda bi, i: (bi, i, col_block)),
                pl.BlockSpec((1, HEAD_DIM), lambda bi, i: (0, 0))]
    args = [p, gain]
    if rope:
        in_specs += [pl.BlockSpec((tm, HEAD_DIM), lambda bi, i: (i, 0))] * 2
        args += list(rope_tabs)
    return pl.pallas_call(
        functools.partial(_headnorm_kernel, n_heads=n_heads, rope=rope, scale=scale),
        grid=(bsz, n // tm),
        in_specs=in_specs,
        out_specs=pl.BlockSpec((None, tm, w), lambda bi, i: (bi, i, 0)),
        out_shape=jax.ShapeDtypeStruct((bsz, n, w), BF16),
        compiler_params=_params("arbitrary", "arbitrary"),
        name="headnorm",
    )(*args)


def _attn_kernel(*refs, n_parts):
    q_ref = refs[0]
    kv_refs = refs[1:1 + 2 * n_parts]
    o_ref = refs[1 + 2 * n_parts]
    for g in range(N_KV_HEADS):
        gsl = slice(g * HEAD_DIM, (g + 1) * HEAD_DIM)
        ks = [kv_refs[2 * i][:, gsl] for i in range(n_parts)]
        vs = [kv_refs[2 * i + 1][:, gsl] for i in range(n_parts)]
        for r in range(GQA_GROUP):
            h = g * GQA_GROUP + r
            hsl = slice(h * HEAD_DIM, (h + 1) * HEAD_DIM)
            q = q_ref[:, hsl]
            ss = [lax.dot_general(q, k, (((1,), (1,)), ((), ())), preferred_element_type=F32) for k in ks]
            m = jnp.max(ss[0], -1, keepdims=True)
            for s in ss[1:]:
                m = jnp.maximum(m, jnp.max(s, -1, keepdims=True))
            ps = [jnp.exp(s - m) for s in ss]
            l = jnp.sum(ps[0], -1, keepdims=True)
            for p in ps[1:]:
                l = l + jnp.sum(p, -1, keepdims=True)
            o = _dot(ps[0].astype(BF16), vs[0])
            for p, v in zip(ps[1:], vs[1:]):
                o = o + _dot(p.astype(BF16), v)
            o_ref[:, hsl] = (o / l).astype(o_ref.dtype)


def _attention(q, parts):
    bsz, n, w = q.shape
    tq = min(n, 256)
    in_specs = [pl.BlockSpec((None, tq, w), lambda bi, i: (bi, i, 0))]
    args = [q]
    for k, v, vblk in parts:
        nk = k.shape[1]
        in_specs.append(pl.BlockSpec((None, nk, KV_W), lambda bi, i: (bi, 0, 0)))
        in_specs.append(pl.BlockSpec((None, nk, KV_W), functools.partial(lambda bi, i, c: (bi, 0, c), c=vblk)))
        args += [k, v]
    return pl.pallas_call(
        functools.partial(_attn_kernel, n_parts=len(parts)),
        grid=(bsz, n // tq),
        in_specs=in_specs,
        out_specs=pl.BlockSpec((None, tq, w), lambda bi, i: (bi, i, 0)),
        out_shape=jax.ShapeDtypeStruct((bsz, n, w), BF16),
        compiler_params=_params("arbitrary", "arbitrary"),
        name="attention",
    )(*args)


def _filter_kernel(emb_ref, w1_ref, b1_ref, f1_ref, w2_ref, b2_ref, f2_ref, w3f_ref, w3b_ref, dl_ref,
                   hs_ref, hd_ref, krn_ref, *, n):
    h = jnp.sin(f1_ref[...] * (_dot(emb_ref[...], w1_ref[...]) + b1_ref[...]))
    h = jnp.sin(f2_ref[...] * (_dot(h, w2_ref[...]) + b2_ref[...]))
    hf = _dot(h, w3f_ref[...])
    hb = _dot(h, w3b_ref[...])
    row = lax.broadcasted_iota(jnp.int32, hf.shape, 0)
    t = row.astype(F32) * (1.0 / (n - 1))
    dec = jnp.exp(-t * dl_ref[...])
    hf = hf * dec
    hb = jnp.where(row == 0, 0.0, hb * dec)
    ssq = jnp.sum(hf * hf, 0, keepdims=True) + jnp.sum(hb * hb, 0, keepdims=True)
    nrm = lax.rsqrt(ssq + KERN_EPS) * (1.0 / n)
    hs = (hf + hb) * nrm
    hs_ref[...] = hs.astype(hs_ref.dtype)
    hd_ref[...] = ((hf - hb) * nrm).astype(hd_ref.dtype)
    sign = jnp.where(row % 2 == 0, 1.0, -1.0)
    krn_ref[...] = 0.5 * jnp.sum(hs * sign, 0, keepdims=True)


def _filters(n, emb, w1, b1, f1, w2, b2, f2, w3, absdelta):
    od = HYENA_ORDER * D_HYENA
    tn = 512
    fh = w2.shape[0]
    ke = emb.shape[1]
    full = lambda shape: pl.BlockSpec(shape, lambda j: (0, 0))
    return pl.pallas_call(
        functools.partial(_filter_kernel, n=n),
        grid=(od // tn,),
        in_specs=[full((n, ke)), full((ke, fh)), full((1, fh)), full((1, fh)),
                  full((fh, fh)), full((1, fh)), full((1, fh)),
                  pl.BlockSpec((fh, tn), lambda j: (0, j)),
                  pl.BlockSpec((fh, tn), lambda j: (0, j + od // tn)),
                  pl.BlockSpec((1, tn), lambda j: (0, j))],
        out_specs=[pl.BlockSpec((n, tn), lambda j: (0, j)),
                   pl.BlockSpec((n, tn), lambda j: (0, j)),
                   pl.BlockSpec((1, tn), lambda j: (0, j))],
        out_shape=[jax.ShapeDtypeStruct((n, od), BF16), jax.ShapeDtypeStruct((n, od), BF16),
                   jax.ShapeDtypeStruct((1, od), F32)],
        compiler_params=_params("arbitrary"),
        name="hyena_filters",
    )(emb, w1, b1, f1, w2, b2, f2, w3, w3, absdelta)


def _spectrum_kernel(mre_ref, mim_ref, hs_ref, hd_ref, kr_ref, ki_ref):
    kr = _dot(mre_ref[...], hs_ref[...])
    ki = _dot(mim_ref[...], hd_ref[...])
    row = lax.broadcasted_iota(jnp.int32, kr.shape, 0) + pl.program_id(0) * kr.shape[0]
    kr_ref[...] = jnp.where(row == 0, 0.5 * kr, kr)
    ki_ref[...] = jnp.where(row == 0, 0.0, ki)


def _spectrum(mf, hs, hd):
    n, od = hs.shape
    tm = min(n, 512)
    tn = 512
    nb = n // tm
    return pl.pallas_call(
        _spectrum_kernel,
        grid=(nb, od // tn),
        in_specs=[pl.BlockSpec((tm, n), lambda i, j: (i, 0)),
                  pl.BlockSpec((tm, n), lambda i, j: (i + nb, 0)),
                  pl.BlockSpec((n, tn), lambda i, j: (0, j)),
                  pl.BlockSpec((n, tn), lambda i, j: (0, j))],
        out_specs=[pl.BlockSpec((tm, tn), lambda i, j: (i, j))] * 2,
        out_shape=[jax.ShapeDtypeStruct((n, od), F32)] * 2,
        compiler_params=_params("arbitrary", "arbitrary"),
        name="hyena_spectrum",
    )(mf, mf, hs, hd)


def _short_conv(x, w_ref, b_ref):
    n = x.shape[0]
    row = lax.broadcasted_iota(jnp.int32, x.shape, 0)
    prev = jnp.where(row == 0, 0.0, pltpu.roll(x, 1, 0))
    nxt = jnp.where(row == n - 1, 0.0, pltpu.roll(x, n - 1, 0))
    return b_ref[...] + prev * w_ref[0:1, :] + x * w_ref[1:2, :] + nxt * w_ref[2:3, :]


def _hyconv_kernel(*refs, conv_z, fk):
    if conv_z:
        z_in, cwz, cbz, x_in, cwx, cbx, mre, mim, ire, iim, kr_ref, ki_ref, krn_ref, skip_ref, o_ref, z_ref, acc_ref = refs
    else:
        z_in, x_in, cwx, cbx, mre, mim, ire, iim, kr_ref, ki_ref, krn_ref, skip_ref, o_ref, z_ref, acc_ref = refs
    j = pl.program_id(2)

    @pl.when(j == 0)
    def _():
        if conv_z:
            z_ref[...] = _short_conv(z_in[...].astype(F32), cwz, cbz).astype(BF16)
        else:
            z_ref[...] = z_in[...]
        acc_ref[...] = jnp.zeros_like(acc_ref)

    z = z_ref[...]
    fr = _dot(mre[...], z)
    fi = _dot(mim[...], z)
    kr = kr_ref[...]
    ki = ki_ref[...]
    row = lax.broadcasted_iota(jnp.int32, kr.shape, 0) + j * fk
    kr_nyq = jnp.where(row == 0, krn_ref[...], kr)
    w_re = (fr * kr - fi * ki).astype(BF16)
    w_im = (fr * ki + fi * kr_nyq).astype(BF16)
    acc_ref[...] += _dot(ire[...], w_re) + _dot(iim[...], w_im)

    @pl.when(j == pl.num_programs(2) - 1)
    def _():
        xm = _short_conv(x_in[...].astype(F32), cwx, cbx)
        y = acc_ref[...] + z_ref[...].astype(F32) * skip_ref[...]
        o_ref[...] = (xm * y).astype(o_ref.dtype)


def _hyconv(order, z_arr, p, conv_w, conv_b, mf, mi, kr, ki, krn, skip):
    bsz, n, _ = p.shape
    tn = 512
    fk = min(n, 512)
    nf = n // fk
    nct = D_HYENA // tn
    conv_z = z_arr is None
    hy_blk = P_HY // tn

    def sect(part):
        return functools.partial(lambda bi, c, j, o: (bi, 0, o + c), o=hy_blk + part * nct)

    def wsect(part):
        return functools.partial(lambda bi, c, j, o: (0, o + c), o=part * nct)

    in_specs, args = [], []
    if conv_z:
        in_specs += [pl.BlockSpec((None, n, tn), sect(HYENA_ORDER)),
                     pl.BlockSpec((SHORT_CONV, tn), wsect(HYENA_ORDER)),
                     pl.BlockSpec((1, tn), wsect(HYENA_ORDER))]
        args += [p, conv_w, conv_b]
    else:
        in_specs += [pl.BlockSpec((None, n, tn), lambda bi, c, j: (bi, 0, c))]
        args += [z_arr]
    ocol = functools.partial(lambda bi, c, j, o: (0, o + c), o=order * nct)
    in_specs += [pl.BlockSpec((None, n, tn), sect(order)),
                 pl.BlockSpec((SHORT_CONV, tn), wsect(order)),
                 pl.BlockSpec((1, tn), wsect(order)),
                 pl.BlockSpec((fk, n), lambda bi, c, j: (j, 0)),
                 pl.BlockSpec((fk, n), lambda bi, c, j: (j + nf, 0)),
                 pl.BlockSpec((n, fk), lambda bi, c, j: (0, j)),
                 pl.BlockSpec((n, fk), lambda bi, c, j: (0, j + nf)),
                 pl.BlockSpec((fk, tn), functools.partial(lambda bi, c, j, o: (j, o + c), o=order * nct)),
                 pl.BlockSpec((fk, tn), functools.partial(lambda bi, c, j, o: (j, o + c), o=order * nct)),
                 pl.BlockSpec((1, tn), ocol),
                 pl.BlockSpec((1, tn), ocol)]
    args += [p, conv_w, conv_b, mf, mf, mi, mi, kr, ki, krn, skip]
    return pl.pallas_call(
        functools.partial(_hyconv_kernel, conv_z=conv_z, fk=fk),
        grid=(bsz, nct, nf),
        in_specs=in_specs,
        out_specs=pl.BlockSpec((None, n, tn), lambda bi, c, j: (bi, 0, c)),
        out_shape=jax.ShapeDtypeStruct((bsz, n, D_HYENA), BF16),
        scratch_shapes=[pltpu.VMEM((n, tn), BF16), pltpu.VMEM((n, tn), F32)],
        compiler_params=_params("arbitrary", "arbitrary", "arbitrary"),
        name="hyena_conv",
    )(*args)


def _fnet_kernel(u_ref, csc_ref, mn_ref, o_ref, t_ref, *, n, scale):
    @pl.when(pl.program_id(1) == 0)
    def _():
        for g in range(FNET_GROUPS):
            gsl = slice(g * FNET_GROUP_DIM, (g + 1) * FNET_GROUP_DIM)
            t = _dot(u_ref[:, gsl], csc_ref[...])
            t_ref[0:n, gsl] = t[:, :FNET_GROUP_DIM].astype(BF16)
            t_ref[n:2 * n, gsl] = t[:, FNET_GROUP_DIM:].astype(BF16)

    o_ref[...] = (_dot(mn_ref[...], t_ref[...]) * scale).astype(o_ref.dtype)


def _fnet(p, csc, mn):
    bsz, n, _ = p.shape
    tr = min(n, 512)
    scale = 1.0 / math.sqrt(n * FNET_GROUP_DIM)
    return pl.pallas_call(
        functools.partial(_fnet_kernel, n=n, scale=scale),
        grid=(bsz, n // tr),
        in_specs=[pl.BlockSpec((None, n, D_MODEL), lambda bi, i: (bi, 0, P_FN // D_MODEL)),
                  pl.BlockSpec((FNET_GROUP_DIM, 2 * FNET_GROUP_DIM), lambda bi, i: (0, 0)),
                  pl.BlockSpec((tr, 2 * n), lambda bi, i: (i, 0))],
        out_specs=pl.BlockSpec((None, tr, D_MODEL), lambda bi, i: (bi, i, 0)),
        out_shape=jax.ShapeDtypeStruct((bsz, n, D_MODEL), BF16),
        scratch_shapes=[pltpu.VMEM((2 * n, D_MODEL), BF16)],
        compiler_params=_params("arbitrary", "arbitrary"),
        name="fnet",
    )(p, csc, mn)


def _merge_kernel(oa_ref, hy_ref, fn_ref, ga_ref, gh_ref, gf_ref, x_ref, mod_ref,
                  wa_ref, wh_ref, wf_ref, wo_ref, bo_ref, lg_ref, lb_ref, o_ref):
    m = _sigmoid(ga_ref[...].astype(F32)) * _dot(oa_ref[...], wa_ref[...])
    m = m + _sigmoid(gh_ref[...].astype(F32)) * _dot(hy_ref[...], wh_ref[...])
    m = m + _sigmoid(gf_ref[...].astype(F32)) * _dot(fn_ref[...], wf_ref[...])
    y = _dot(m.astype(BF16), wo_ref[...]) + bo_ref[...]
    r = ALPHA * x_ref[...] + mod_ref[2:3, :] * y
    o_ref[...] = _layer_norm(r, lg_ref[...], lb_ref[...])


def _merge(o_att, z_hy, y_fn, p, x, mod, wa, wh, wf, wo, bo, lg, lb):
    bsz, n, d = x.shape
    tm = min(n, 512)
    tok = lambda c: pl.BlockSpec((None, tm, d), functools.partial(lambda bi, i, c: (bi, i, c), c=c))
    wspec = pl.BlockSpec((d, d), lambda bi, i: (0, 0))
    vspec = pl.BlockSpec((1, d), lambda bi, i: (0, 0))
    g0 = P_G // d
    return pl.pallas_call(
        _merge_kernel,
        grid=(bsz, n // tm),
        in_specs=[tok(0), tok(0), tok(0), tok(g0), tok(g0 + 1), tok(g0 + 2), tok(0),
                  pl.BlockSpec((None, N_MOD, d), lambda bi, i: (bi, 0, 0)),
                  wspec, wspec, wspec, wspec, vspec, vspec, vspec],
        out_specs=tok(0),
        out_shape=jax.ShapeDtypeStruct((bsz, n, d), F32),
        compiler_params=_params("arbitrary", "arbitrary"),
        name="merge_ln",
    )(o_att, z_hy, y_fn, p, p, p, x, mod, wa, wh, wf, wo, bo, lg, lb)


def _mlp_kernel(x_ref, mod_ref, w1_ref, b1_ref, w2_ref, b2_ref, lg_ref, lb_ref, o_ref, h_ref, acc_ref):
    j = pl.program_id(2)

    @pl.when(j == 0)
    def _():
        h_ref[...] = (x_ref[...] * (1.0 + mod_ref[4:5, :]) + mod_ref[3:4, :]).astype(BF16)
        acc_ref[...] = jnp.zeros_like(acc_ref)

    a = jnp.maximum(_dot(h_ref[...], w1_ref[...]) + b1_ref[...], 0.0)
    acc_ref[...] += _dot((a * a).astype(BF16), w2_ref[...])

    @pl.when(j == pl.num_programs(2) - 1)
    def _():
        y = acc_ref[...] + b2_ref[...]
        r = ALPHA * x_ref[...] + mod_ref[5:6, :] * y
        o_ref[...] = _layer_norm(r, lg_ref[...], lb_ref[...])


def _mlp(x, mod, w1, b1, w2, b2, lg, lb):
    bsz, n, d = x.shape
    ff = w1.shape[1]
    tm = min(n, 1024)
    fc = 1024
    vspec = pl.BlockSpec((1, d), lambda bi, i, j: (0, 0))
    return pl.pallas_call(
        _mlp_kernel,
        grid=(bsz, n // tm, ff // fc),
        in_specs=[pl.BlockSpec((None, tm, d), lambda bi, i, j: (bi, i, 0)),
                  pl.BlockSpec((None, N_MOD, d), lambda bi, i, j: (bi, 0, 0)),
                  pl.BlockSpec((d, fc), lambda bi, i, j: (0, j)),
                  pl.BlockSpec((1, fc), lambda bi, i, j: (0, j)),
                  pl.BlockSpec((fc, d), lambda bi, i, j: (j, 0)),
                  vspec, vspec, vspec],
        out_specs=pl.BlockSpec((None, tm, d), lambda bi, i, j: (bi, i, 0)),
        out_shape=jax.ShapeDtypeStruct((bsz, n, d), F32),
        scratch_shapes=[pltpu.VMEM((tm, d), BF16), pltpu.VMEM((tm, d), F32)],
        compiler_params=_params("arbitrary", "arbitrary", "arbitrary"),
        name="mlp_ln",
    )(x, mod, w1, b1, w2, b2, lg, lb)


def _int_grid(rows, cols):
    k = lax.broadcasted_iota(jnp.int32, (rows, cols), 0)
    s = lax.broadcasted_iota(jnp.int32, (rows, cols), 1)
    return k, s


def _hyena_dft(n):
    k, s = _int_grid(n, n)
    ang = ((k * s) % (2 * n)).astype(F32) * (math.pi / n)
    nyq = jnp.where(s % 2 == 0, 1.0, -1.0)
    mf = jnp.concatenate([jnp.cos(ang), jnp.where(k == 0, nyq, jnp.sin(ang))], 0)
    return mf.astype(BF16), mf.T.astype(BF16)


def _fnet_dft(n):
    k, s = _int_grid(n, n)
    ang = ((k * s) % n).astype(F32) * (2.0 * math.pi / n)
    mn = jnp.concatenate([jnp.cos(ang), -jnp.sin(ang)], 1)
    k, s = _int_grid(FNET_GROUP_DIM, FNET_GROUP_DIM)
    ang = ((k * s) % FNET_GROUP_DIM).astype(F32) * (2.0 * math.pi / FNET_GROUP_DIM)
    csc = jnp.concatenate([jnp.cos(ang), jnp.sin(ang)], 1)
    return csc.astype(BF16), mn.astype(BF16)


def _filter_embedding(n, width):
    t = jnp.linspace(0.0, 1.0, n, dtype=F32)[:, None]
    w = (2.0 * math.pi / n) * jnp.arange(n, dtype=F32)[:, None]
    f = jnp.linspace(1e-4, HYENA_BANDS - 1, HYENA_BANDS, dtype=F32)[None, :]
    emb = jnp.concatenate([t, jnp.cos(f * w), -jnp.sin(f * w)], -1)
    return jnp.pad(emb, ((0, 0), (0, width - HYENA_EMB)))


def _rope_tables(n):
    rows = n // GRID_W
    row = jnp.repeat(jnp.arange(rows, dtype=F32), GRID_W)
    col = jnp.tile(jnp.arange(GRID_W, dtype=F32), rows)
    inv_freq = ROPE_THETA ** (-jnp.arange(ROPE_AXIS_DIM // 2, dtype=F32) * 2.0 / ROPE_AXIS_DIM)
    ar = row[:, None] * inv_freq[None, :]
    ac = col[:, None] * inv_freq[None, :]
    cos = jnp.concatenate([jnp.cos(ar), jnp.cos(ar), jnp.cos(ac), jnp.cos(ac)], -1)
    sin = jnp.concatenate([-jnp.sin(ar), jnp.sin(ar), -jnp.sin(ac), jnp.sin(ac)], -1)
    return cos, sin


def _permute_in_cols(a):
    return jnp.concatenate([a[..., OFF_Q:OFF_K], a[..., OFF_G:D_IN], a[..., OFF_HY:OFF_FN],
                            a[..., OFF_FN:OFF_G], a[..., OFF_K:OFF_V], a[..., OFF_V:OFF_HY]], -1)


def _mixer(p_tok, seq_shape, lw, tabs, rope_tabs, ctx_parts):
    bsz, n = seq_shape
    p = p_tok.reshape(bsz, n, p_tok.shape[-1])
    q = _headnorm(p, P_Q // Q_W, N_Q_HEADS, lw['q_gain'], rope_tabs, ATTN_SCALE)
    k = _headnorm(p, P_K // KV_W, N_KV_HEADS, lw['k_gain'], rope_tabs, 1.0)
    o_att = _attention(q, [(k, p, P_V // KV_W)] + ctx_parts)

    hs, hd, krn = _filters(n, tabs['emb'], lw['hy_w1'], lw['hy_b1'], lw['hy_freq1'], lw['hy_w2'], lw['hy_b2'],
                           lw['hy_freq2'], lw['hy_w3'], tabs['absdelta'])
    kr, ki = _spectrum(tabs['mf'], hs, hd)
    z = None
    for o in range(HYENA_ORDER):
        z = _hyconv(o, z, p, lw['conv_w'], lw['conv_b'], tabs['mf'], tabs['mi'], kr, ki, krn, lw['hy_skip'])
    y_fn = _fnet(p, tabs['csc'], tabs['mn'])
    tok = lambda a: a.reshape(p_tok.shape[0], p_tok.shape[1], a.shape[-1])
    return tok(o_att), tok(z), tok(y_fn), k, p


def _block(x_tok, mod, seq_shape, lw, tabs, rope_tabs, ctx_parts):
    p_tok = _inproj(x_tok, mod, lw['w_in'], lw['b_in'])
    o_att, z_hy, y_fn, k, p = _mixer(p_tok, seq_shape, lw, tabs, rope_tabs, ctx_parts)
    x_tok = _merge(o_att, z_hy, y_fn, p_tok, x_tok, mod, lw['w_att_o'], lw['w_hy_o'], lw['w_fn_o'],
                   lw['w_out'], lw['b_out'], lw['ln1_g'], lw['ln1_b'])
    x_tok = _mlp(x_tok, mod, lw['w_mlp1'], lw['b_mlp1'], lw['w_mlp2'], lw['b_mlp2'], lw['ln2_g'], lw['ln2_b'])
    return x_tok, k, p


def _seq_tables(n):
    mf, mi = _hyena_dft(n)
    csc, mn = _fnet_dft(n)
    deltas = jnp.abs(jnp.linspace(MIN_DECAY, MAX_DECAY, D_HYENA, dtype=F32))
    return dict(mf=mf, mi=mi, csc=csc, mn=mn, emb=_filter_embedding(n, 128),
                absdelta=jnp.tile(deltas, HYENA_ORDER)[None, :])


def kernel(x, c, ctx, c_ctx, w_ada, b_ada, w_in, b_in, conv_w, conv_b, hy_w1, hy_b1, hy_freq1, hy_w2, hy_b2,
           hy_freq2, hy_w3, hy_skip, q_gain, k_gain, w_att_o, w_hy_o, w_fn_o, w_out, b_out, ln1_g, ln1_b,
           w_mlp1, b_mlp1, w_mlp2, b_mlp2, ln2_g, ln2_b):
    bsz, n_lat, d = x.shape
    n_ctx = ctx.shape[1]
    depth = w_ada.shape[0]
    tabs_lat = _seq_tables(n_lat)
    tabs_ctx = _seq_tables(n_ctx)
    rope_tabs = _rope_tables(n_lat)

    n_rows = -(-(bsz + 1) // 16) * 16
    cond = jnp.concatenate([c, c_ctx[None, :], jnp.zeros((n_rows - bsz - 1, d), F32)], 0)

    ctx_tok = ctx.reshape(1, bsz * n_ctx, d)
    row = lambda a: a[None, :]
    for i in range(depth):
        last = i == depth - 1
        lw = dict(
            w_in=_permute_in_cols(w_in[i]).astype(BF16), b_in=row(_permute_in_cols(b_in[i])),
            conv_w=conv_w[i], conv_b=row(conv_b[i]),
            hy_w1=jnp.pad(hy_w1[i], ((0, 128 - HYENA_EMB), (0, 0))), hy_b1=row(hy_b1[i]), hy_freq1=row(hy_freq1[i]),
            hy_w2=hy_w2[i], hy_b2=row(hy_b2[i]), hy_freq2=row(hy_freq2[i]), hy_w3=hy_w3[i],
            hy_skip=hy_skip[i].reshape(1, HYENA_ORDER * D_HYENA),
            q_gain=row(q_gain[i]), k_gain=row(k_gain[i]),
            w_att_o=w_att_o[i].astype(BF16), w_hy_o=w_hy_o[i].astype(BF16), w_fn_o=w_fn_o[i].astype(BF16),
            w_out=w_out[i].astype(BF16), b_out=row(b_out[i]), ln1_g=row(ln1_g[i]), ln1_b=row(ln1_b[i]),
            w_mlp1=w_mlp1[i].astype(BF16), b_mlp1=row(b_mlp1[i]), w_mlp2=w_mlp2[i].astype(BF16),
            b_mlp2=row(b_mlp2[i]), ln2_g=row(ln2_g[i]), ln2_b=row(ln2_b[i]))
        mod = _matmul(cond, w_ada[i].astype(BF16), row(b_ada[i]), silu=True)
        mod_l = mod[:bsz].reshape(bsz, N_MOD, d)
        mod_c = mod[bsz:bsz + 1].reshape(1, N_MOD, d)

        if last:
            kv_cols = slice(P_K, P_V + KV_W)
            p_c = _inproj(ctx_tok, mod_c, lw['w_in'][:, kv_cols], lw['b_in'][:, kv_cols])
            p_c = p_c.reshape(bsz, n_ctx, 2 * KV_W)
            k_c = _headnorm(p_c, 0, N_KV_HEADS, lw['k_gain'], None, 1.0)
            v_blk = 1
        else:
            ctx_tok, k_c, p_c = _block(ctx_tok, mod_c, (bsz, n_ctx), lw, tabs_ctx, None, [])
            v_blk = P_V // KV_W
        x, _, _ = _block(x, mod_l, (bsz, n_lat), lw, tabs_lat, rope_tabs, [(k_c, p_c, v_blk)])
    return x
```

```python
import functools
import math

import jax
import jax.numpy as jnp
from jax import lax
from jax.experimental import pallas as pl
from jax.experimental.pallas import tpu as pltpu

F32 = jnp.float32
BF16 = jnp.bfloat16

D_MODEL = 1024
GRID_W = 64
HEAD_DIM = 128
N_Q_HEADS = D_MODEL // HEAD_DIM
N_KV_HEADS = 2
GQA_GROUP = N_Q_HEADS // N_KV_HEADS
ROPE_THETA = 10000.0
ROPE_AXIS_DIM = HEAD_DIM // 2
ATTN_SCALE = HEAD_DIM ** -0.5

D_HYENA = D_MODEL
HYENA_ORDER = 2
HYENA_BANDS = 16
HYENA_EMB = 1 + 2 * HYENA_BANDS
SHORT_CONV = 3
DECAY_TARGET = 1e-2
MIN_DECAY = math.log(DECAY_TARGET) / 0.3
MAX_DECAY = math.log(DECAY_TARGET) / 1.5

FNET_GROUPS = 4
FNET_GROUP_DIM = D_MODEL // FNET_GROUPS
D_FF = 4 * D_MODEL
N_MOD = 6
DEPTH = 4

Q_W = N_Q_HEADS * HEAD_DIM
KV_W = N_KV_HEADS * HEAD_DIM
HY_W = (HYENA_ORDER + 1) * D_HYENA
OFF_Q = 0
OFF_K = OFF_Q + Q_W
OFF_V = OFF_K + KV_W
OFF_HY = OFF_V + KV_W
OFF_FN = OFF_HY + HY_W
OFF_G = OFF_FN + D_MODEL
D_IN = OFF_G + 3 * D_MODEL

P_Q = 0
P_G = P_Q + Q_W
P_HY = P_G + 3 * D_MODEL
P_FN = P_HY + HY_W
P_K = P_FN + D_MODEL
P_V = P_K + KV_W

ALPHA = (2 * DEPTH) ** 0.25
LN_EPS = 1e-6
RMS_EPS = 1e-6
KERN_EPS = 1e-6

V7X_VMEM_LIMIT_BYTES = 56 * 1024 * 1024


def _params(*sem):
    return pltpu.CompilerParams(dimension_semantics=sem, vmem_limit_bytes=V7X_VMEM_LIMIT_BYTES)


def _dot(a, b):
    return jnp.dot(a, b, preferred_element_type=F32)


def _layer_norm(r, g, b):
    mu = jnp.mean(r, -1, keepdims=True)
    d = r - mu
    var = jnp.mean(d * d, -1, keepdims=True)
    return d * lax.rsqrt(var + LN_EPS) * g + b


def _sigmoid(x):
    return 1.0 / (1.0 + jnp.exp(-x))


def _copy_kernel(x_ref, o_ref):
    o_ref[...] = x_ref[...]


def _deinterleave(x):
    bsz, n, d = x.shape
    h = n // 2
    return pl.pallas_call(
        _copy_kernel,
        grid=(bsz, 2),
        in_specs=[pl.BlockSpec((None, h, d), lambda bi, a: (bi, 0, a))],
        out_specs=pl.BlockSpec((None, h, d), lambda bi, a: (bi, a, 0)),
        out_shape=jax.ShapeDtypeStruct((bsz, n, d), x.dtype),
        compiler_params=_params("arbitrary", "arbitrary"),
        name="deinterleave",
    )(x.reshape(bsz, h, 2 * d))


def _interleave(x):
    bsz, n, d = x.shape
    h = n // 2
    out = pl.pallas_call(
        _copy_kernel,
        grid=(bsz, 2),
        in_specs=[pl.BlockSpec((None, h, d), lambda bi, a: (bi, a, 0))],
        out_specs=pl.BlockSpec((None, h, d), lambda bi, a: (bi, 0, a)),
        out_shape=jax.ShapeDtypeStruct((bsz, h, 2 * d), x.dtype),
        compiler_params=_params("arbitrary", "arbitrary"),
        name="interleave",
    )(x)
    return out.reshape(bsz, n, d)


def _even_odd_rows(a):
    return jnp.concatenate([a[0::2], a[1::2]], 0)


def _matmul_kernel(a_ref, b_ref, bias_ref, o_ref, *, silu):
    a = a_ref[...]
    if silu:
        a = a * _sigmoid(a)
    o_ref[...] = _dot(a.astype(BF16), b_ref[...]) + bias_ref[...]


def _matmul(a, b, bias, *, silu=False, tn=1024):
    m, k = a.shape
    n = b.shape[1]
    return pl.pallas_call(
        functools.partial(_matmul_kernel, silu=silu),
        grid=(n // tn,),
        in_specs=[pl.BlockSpec((m, k), lambda j: (0, 0)),
                  pl.BlockSpec((k, tn), lambda j: (0, j)),
                  pl.BlockSpec((1, tn), lambda j: (0, j))],
        out_specs=pl.BlockSpec((m, tn), lambda j: (0, j)),
        out_shape=jax.ShapeDtypeStruct((m, n), F32),
        compiler_params=_params("arbitrary"),
        name="ada_matmul",
    )(a, b, bias)


def _inproj_kernel(x_ref, mod_ref, w_ref, b_ref, o_ref, h_ref):
    @pl.when(pl.program_id(2) == 0)
    def _():
        h_ref[...] = (x_ref[...] * (1.0 + mod_ref[1:2, :]) + mod_ref[0:1, :]).astype(BF16)

    o_ref[...] = (_dot(h_ref[...], w_ref[...]) + b_ref[...]).astype(o_ref.dtype)


def _inproj(x, mod, w, b):
    bsz, n, d = x.shape
    nout = w.shape[1]
    tm = min(n, 2048)
    tn = min(nout, 512)
    return pl.pallas_call(
        _inproj_kernel,
        grid=(bsz, n // tm, nout // tn),
        in_specs=[pl.BlockSpec((None, tm, d), lambda bi, i, j: (bi, i, 0)),
                  pl.BlockSpec((None, N_MOD, d), lambda bi, i, j: (bi, 0, 0)),
                  pl.BlockSpec((d, tn), lambda bi, i, j: (0, j)),
                  pl.BlockSpec((1, tn), lambda bi, i, j: (0, j))],
        out_specs=pl.BlockSpec((None, tm, tn), lambda bi, i, j: (bi, i, j)),
        out_shape=jax.ShapeDtypeStruct((bsz, n, nout), BF16),
        scratch_shapes=[pltpu.VMEM((tm, d), BF16)],
        compiler_params=_params("arbitrary", "arbitrary", "arbitrary"),
        name="inproj",
    )(x, mod, w, b)


def _headnorm_kernel(*refs, n_heads, rope, scale):
    if rope:
        p_ref, g_ref, cos_ref, sin_ref, o_ref = refs
    else:
        p_ref, g_ref, o_ref = refs
    gain = g_ref[...]
    for h in range(n_heads):
        sl = slice(h * HEAD_DIM, (h + 1) * HEAD_DIM)
        xf = p_ref[:, sl].astype(F32)
        y = xf * lax.rsqrt(jnp.mean(xf * xf, -1, keepdims=True) + RMS_EPS) * gain
        if rope:
            lane = lax.broadcasted_iota(jnp.int32, y.shape, 1)
            first = (lane % (ROPE_AXIS_DIM)) < (ROPE_AXIS_DIM // 2)
            swapped = jnp.where(first, pltpu.roll(y, HEAD_DIM - ROPE_AXIS_DIM // 2, 1),
                                pltpu.roll(y, ROPE_AXIS_DIM // 2, 1))
            y = y * cos_ref[...] + swapped * sin_ref[...]
        if scale != 1.0:
            y = y * scale
        o_ref[:, sl] = y.astype(o_ref.dtype)


def _headnorm(p, col_block, n_heads, gain, rope_tabs, scale):
    bsz, n, _ = p.shape
    w = n_heads * HEAD_DIM
    tm = min(n, 512)
    rope = rope_tabs is not None
    in_specs = [pl.BlockSpec((None, tm, w), lambda bi, i: (bi, i, col_block)),
                pl.BlockSpec((1, HEAD_DIM), lambda bi, i: (0, 0))]
    args = [p, gain]
    if rope:
        in_specs += [pl.BlockSpec((tm, HEAD_DIM), lambda bi, i: (i, 0))] * 2
        args += list(rope_tabs)
    return pl.pallas_call(
        functools.partial(_headnorm_kernel, n_heads=n_heads, rope=rope, scale=scale),
        grid=(bsz, n // tm),
        in_specs=in_specs,
        out_specs=pl.BlockSpec((None, tm, w), lambda bi, i: (bi, i, 0)),
        out_shape=jax.ShapeDtypeStruct((bsz, n, w), BF16),
        compiler_params=_params("arbitrary", "arbitrary"),
        name="headnorm",
    )(*args)


def _attn_kernel(*refs, n_parts):
    q_ref = refs[0]
    kv_refs = refs[1:1 + 2 * n_parts]
    o_ref = refs[1 + 2 * n_parts]
    for g in range(N_KV_HEADS):
        gsl = slice(g * HEAD_DIM, (g + 1) * HEAD_DIM)
        ks = [kv_refs[2 * i][:, gsl] for i in range(n_parts)]
        vs = [kv_refs[2 * i + 1][:, gsl] for i in range(n_parts)]
        for r in range(GQA_GROUP):
            h = g * GQA_GROUP + r
            hsl = slice(h * HEAD_DIM, (h + 1) * HEAD_DIM)
            q = q_ref[:, hsl]
            ss = [lax.dot_general(q, k, (((1,), (1,)), ((), ())), preferred_element_type=F32) for k in ks]
            m = jnp.max(ss[0], -1, keepdims=True)
            for s in ss[1:]:
                m = jnp.maximum(m, jnp.max(s, -1, keepdims=True))
            ps = [jnp.exp(s - m) for s in ss]
            l = jnp.sum(ps[0], -1, keepdims=True)
            for p in ps[1:]:
                l = l + jnp.sum(p, -1, keepdims=True)
            o = _dot(ps[0].astype(BF16), vs[0])
            for p, v in zip(ps[1:], vs[1:]):
                o = o + _dot(p.astype(BF16), v)
            o_ref[:, hsl] = (o / l).astype(o_ref.dtype)


def _attention(q, parts):
    bsz, n, w = q.shape
    tq = min(n, 256)
    in_specs = [pl.BlockSpec((None, tq, w), lambda bi, i: (bi, i, 0))]
    args = [q]
    for k, v, vblk in parts:
        nk = k.shape[1]
        in_specs.append(pl.BlockSpec((None, nk, KV_W), lambda bi, i: (bi, 0, 0)))
        in_specs.append(pl.BlockSpec((None, nk, KV_W), functools.partial(lambda bi, i, c: (bi, 0, c), c=vblk)))
        args += [k, v]
    return pl.pallas_call(
        functools.partial(_attn_kernel, n_parts=len(parts)),
        grid=(bsz, n // tq),
        in_specs=in_specs,
        out_specs=pl.BlockSpec((None, tq, w), lambda bi, i: (bi, i, 0)),
        out_shape=jax.ShapeDtypeStruct((bsz, n, w), BF16),
        compiler_params=_params("arbitrary", "arbitrary"),
        name="attention",
    )(*args)


def _filter_kernel(emb_ref, w1_ref, b1_ref, f1_ref, w2_ref, b2_ref, f2_ref, w3f_ref, w3b_ref, dl_ref,
                   hs_ref, hd_ref, mid_ref, *, n):
    hh = n // 2
    h = jnp.sin(f1_ref[...] * (_dot(emb_ref[...], w1_ref[...]) + b1_ref[...]))
    h = jnp.sin(f2_ref[...] * (_dot(h, w2_ref[...]) + b2_ref[...]))
    hf = _dot(h, w3f_ref[...])
    hb = _dot(h, w3b_ref[...])
    row = lax.broadcasted_iota(jnp.int32, hf.shape, 0)
    lag = jnp.where(row < hh, 2 * row, 2 * (row - hh) + 1)
    t = lag.astype(F32) * (1.0 / (n - 1))
    dec = jnp.exp(-t * dl_ref[...])
    hf = hf * dec
    hb = jnp.where(lag == 0, 0.0, hb * dec)
    ssq = jnp.sum(hf * hf, 0, keepdims=True) + jnp.sum(hb * hb, 0, keepdims=True)
    nrm = lax.rsqrt(ssq + KERN_EPS) * (1.0 / n)
    hs = (hf + hb) * nrm
    hd = (hf - hb) * nrm
    hs_ref[...] = hs.astype(hs_ref.dtype)
    hd_ref[...] = hd.astype(hd_ref.dtype)
    sign = jnp.where(row % 2 == 0, 1.0, -1.0)
    mid_ref[0:1, :] = jnp.sum(jnp.where(row < hh, hs * sign, 0.0), 0, keepdims=True)
    mid_ref[1:2, :] = jnp.sum(jnp.where(row < hh, 0.0, hd * sign), 0, keepdims=True)


def _filters(n, emb, w1, b1, f1, w2, b2, f2, w3, absdelta):
    od = HYENA_ORDER * D_HYENA
    tn = 512
    fh = w2.shape[0]
    ke = emb.shape[1]
    full = lambda shape: pl.BlockSpec(shape, lambda j: (0, 0))
    return pl.pallas_call(
        functools.partial(_filter_kernel, n=n),
        grid=(od // tn,),
        in_specs=[full((n, ke)), full((ke, fh)), full((1, fh)), full((1, fh)),
                  full((fh, fh)), full((1, fh)), full((1, fh)),
                  pl.BlockSpec((fh, tn), lambda j: (0, j)),
                  pl.BlockSpec((fh, tn), lambda j: (0, j + od // tn)),
                  pl.BlockSpec((1, tn), lambda j: (0, j))],
        out_specs=[pl.BlockSpec((n, tn), lambda j: (0, j)),
                   pl.BlockSpec((n, tn), lambda j: (0, j)),
                   pl.BlockSpec((2, tn), lambda j: (0, j))],
        out_shape=[jax.ShapeDtypeStruct((n, od), BF16), jax.ShapeDtypeStruct((n, od), BF16),
                   jax.ShapeDtypeStruct((2, od), F32)],
        compiler_params=_params("arbitrary"),
        name="hyena_filters",
    )(emb, w1, b1, f1, w2, b2, f2, w3, w3, absdelta)


def _spectrum_kernel(fe_re, fe_im, fo_re, fo_im, hse, hso, hde, hdo, k4_ref):
    p = _dot(fe_re[...], hse[...])
    q = _dot(fo_re[...], hso[...])
    pp = _dot(fe_im[...], hde[...])
    qp = _dot(fo_im[...], hdo[...])
    row0 = (lax.broadcasted_iota(jnp.int32, p.shape, 0) + pl.program_id(0) * p.shape[0]) == 0
    k4_ref[0] = jnp.where(row0, 0.5 * (p + q), p + q)
    k4_ref[1] = jnp.where(row0, 0.0, pp + qp)
    k4_ref[2] = jnp.where(row0, 0.5 * (p - q), p - q)
    k4_ref[3] = jnp.where(row0, 0.0, qp - pp)


def _spectrum(mfe, mfo, hs, hd):
    n, od = hs.shape
    h = n // 2
    tm = min(h, 512)
    tn = 512
    nb = h // tm
    mat = lambda off: pl.BlockSpec((tm, h), functools.partial(lambda i, j, o: (i + o, 0), o=off))
    half = lambda a: pl.BlockSpec((h, tn), functools.partial(lambda i, j, a: (a, j), a=a))
    return pl.pallas_call(
        _spectrum_kernel,
        grid=(nb, od // tn),
        in_specs=[mat(0), mat(nb), mat(0), mat(nb), half(0), half(1), half(0), half(1)],
        out_specs=pl.BlockSpec((4, tm, tn), lambda i, j: (0, i, j)),
        out_shape=jax.ShapeDtypeStruct((4, h, od), F32),
        compiler_params=_params("arbitrary", "arbitrary"),
        name="hyena_spectrum",
    )(mfe, mfe, mfo, mfo, hs, hs, hd, hd)


def _short_conv(x, w_ref, b_ref):
    hh = x.shape[0] // 2
    e = x[:hh]
    o = x[hh:]
    row = lax.broadcasted_iota(jnp.int32, e.shape, 0)
    o_prev = jnp.where(row == 0, 0.0, pltpu.roll(o, 1, 0))
    e_next = jnp.where(row == hh - 1, 0.0, pltpu.roll(e, hh - 1, 0))
    w0, w1, w2 = w_ref[0:1, :], w_ref[1:2, :], w_ref[2:3, :]
    return (b_ref[...] + o_prev * w0 + e * w1 + o * w2,
            b_ref[...] + e * w0 + o * w1 + e_next * w2)


def _hyconv_kernel(*refs, conv_z, fk, n):
    if conv_z:
        z_in, cwz, cbz = refs[:3]
        refs = refs[3:]
    else:
        z_in = refs[0]
        refs = refs[1:]
    (x_in, cwx, cbx, fe_re, fe_im, fo_re, fo_im, ie_re, ie_im, io_re, io_im,
     k4_ref, mid_ref, skip_ref, o_ref, z_ref, acc_ref) = refs
    hh = n // 2
    j = pl.program_id(2)

    @pl.when(j == 0)
    def _():
        if conv_z:
            ze, zo = _short_conv(z_in[...].astype(F32), cwz, cbz)
            z_ref[0:hh, :] = ze.astype(BF16)
            z_ref[hh:n, :] = zo.astype(BF16)
        else:
            z_ref[...] = z_in[...]
        acc_ref[...] = jnp.zeros_like(acc_ref)

    ze = z_ref[0:hh, :]
    zo = z_ref[hh:n, :]
    p = _dot(fe_re[...], ze)
    pp = _dot(fe_im[...], ze)
    q = _dot(fo_re[...], zo)
    qp = _dot(fo_im[...], zo)
    kra, kia, krb, kib = k4_ref[0], k4_ref[1], k4_ref[2], k4_ref[3]
    fra, fia = p + q, pp + qp
    frb, fib = p - q, qp - pp
    wra = fra * kra - fia * kia
    wia = fra * kia + fia * kra
    wrb = frb * krb - fib * kib
    wib = frb * kib + fib * krb
    row0 = (lax.broadcasted_iota(jnp.int32, p.shape, 0) + j * fk) == 0
    kmr, kmi = mid_ref[0:1, :], mid_ref[1:2, :]
    ge_re = (wra + wrb).astype(BF16)
    ge_im = jnp.where(row0, pp * kmr - qp * kmi, wia - wib).astype(BF16)
    go_re = (wra - wrb).astype(BF16)
    go_im = jnp.where(row0, pp * kmi + qp * kmr, wia + wib).astype(BF16)
    acc_ref[0:hh, :] += _dot(ie_re[...], ge_re) + _dot(ie_im[...], ge_im)
    acc_ref[hh:n, :] += _dot(io_re[...], go_re) + _dot(io_im[...], go_im)

    @pl.when(j == pl.num_programs(2) - 1)
    def _():
        xe, xo = _short_conv(x_in[...].astype(F32), cwx, cbx)
        skip = skip_ref[...]
        o_ref[0:hh, :] = (xe * (acc_ref[0:hh, :] + z_ref[0:hh, :].astype(F32) * skip)).astype(o_ref.dtype)
        o_ref[hh:n, :] = (xo * (acc_ref[hh:n, :] + z_ref[hh:n, :].astype(F32) * skip)).astype(o_ref.dtype)


def _hyconv(order, z_arr, p, conv_w, conv_b, tabs, k4, mid, skip):
    bsz, n, _ = p.shape
    h = n // 2
    tn = 512
    fk = min(h, 512)
    nf = h // fk
    nct = D_HYENA // tn
    conv_z = z_arr is None
    hy_blk = P_HY // tn

    def sect(part):
        return functools.partial(lambda bi, c, j, o: (bi, 0, o + c), o=hy_blk + part * nct)

    def wsect(part):
        return functools.partial(lambda bi, c, j, o: (0, o + c), o=part * nct)

    in_specs, args = [], []
    if conv_z:
        in_specs += [pl.BlockSpec((None, n, tn), sect(HYENA_ORDER)),
                     pl.BlockSpec((SHORT_CONV, tn), wsect(HYENA_ORDER)),
                     pl.BlockSpec((1, tn), wsect(HYENA_ORDER))]
        args += [p, conv_w, conv_b]
    else:
        in_specs += [pl.BlockSpec((None, n, tn), lambda bi, c, j: (bi, 0, c))]
        args += [z_arr]
    fwd = lambda off: pl.BlockSpec((fk, h), functools.partial(lambda bi, c, j, o: (j + o, 0), o=off))
    inv = lambda off: pl.BlockSpec((h, fk), functools.partial(lambda bi, c, j, o: (0, j + o), o=off))
    ocol = functools.partial(lambda bi, c, j, o: (0, o + c), o=order * nct)
    in_specs += [pl.BlockSpec((None, n, tn), sect(order)),
                 pl.BlockSpec((SHORT_CONV, tn), wsect(order)),
                 pl.BlockSpec((1, tn), wsect(order)),
                 fwd(0), fwd(nf), fwd(0), fwd(nf), inv(0), inv(nf), inv(0), inv(nf),
                 pl.BlockSpec((4, fk, tn), functools.partial(lambda bi, c, j, o: (0, j, o + c), o=order * nct)),
                 pl.BlockSpec((2, tn), ocol),
                 pl.BlockSpec((1, tn), ocol)]
    args += [p, conv_w, conv_b, tabs['mfe'], tabs['mfe'], tabs['mfo'], tabs['mfo'],
             tabs['mie'], tabs['mie'], tabs['mio'], tabs['mio'], k4, mid, skip]
    return pl.pallas_call(
        functools.partial(_hyconv_kernel, conv_z=conv_z, fk=fk, n=n),
        grid=(bsz, nct, nf),
        in_specs=in_specs,
        out_specs=pl.BlockSpec((None, n, tn), lambda bi, c, j: (bi, 0, c)),
        out_shape=jax.ShapeDtypeStruct((bsz, n, D_HYENA), BF16),
        scratch_shapes=[pltpu.VMEM((n, tn), BF16), pltpu.VMEM((n, tn), F32)],
        compiler_params=_params("arbitrary", "arbitrary", "arbitrary"),
        name="hyena_conv",
    )(*args)


def _fnet_kernel(u_ref, csc_ref, mne_ref, mno_ref, o_ref, t_ref, *, n, tr, scale):
    hh = n // 2
    qq = n // 4
    i = pl.program_id(1)

    @pl.when(i == 0)
    def _():
        for g in range(FNET_GROUPS):
            gsl = slice(g * FNET_GROUP_DIM, (g + 1) * FNET_GROUP_DIM)
            t = _dot(u_ref[:, gsl], csc_ref[...])
            tc = t[:, :FNET_GROUP_DIM].astype(BF16)
            ts = t[:, FNET_GROUP_DIM:].astype(BF16)
            t_ref[0:hh, gsl] = tc[0:hh]
            t_ref[hh:n, gsl] = ts[0:hh]
            t_ref[n:n + hh, gsl] = tc[hh:n]
            t_ref[n + hh:2 * n, gsl] = ts[hh:n]

    a = _dot(mne_ref[...], t_ref[0:n, :])
    b = _dot(mno_ref[...], t_ref[n:2 * n, :])
    steps = qq // tr
    start = (i // steps) * hh + (i % steps) * tr
    o_ref[pl.ds(pl.multiple_of(start, tr), tr), :] = ((a + b) * scale).astype(o_ref.dtype)
    o_ref[pl.ds(pl.multiple_of(start + qq, tr), tr), :] = ((a - b) * scale).astype(o_ref.dtype)


def _fnet(p, csc, mne, mno):
    bsz, n, _ = p.shape
    tr = min(n // 4, 256)
    scale = 1.0 / math.sqrt(n * FNET_GROUP_DIM)
    return pl.pallas_call(
        functools.partial(_fnet_kernel, n=n, tr=tr, scale=scale),
        grid=(bsz, (n // 2) // tr),
        in_specs=[pl.BlockSpec((None, n, D_MODEL), lambda bi, i: (bi, 0, P_FN // D_MODEL)),
                  pl.BlockSpec((FNET_GROUP_DIM, 2 * FNET_GROUP_DIM), lambda bi, i: (0, 0)),
                  pl.BlockSpec((tr, n), lambda bi, i: (i, 0)),
                  pl.BlockSpec((tr, n), lambda bi, i: (i, 0))],
        out_specs=pl.BlockSpec((None, n, D_MODEL), lambda bi, i: (bi, 0, 0)),
        out_shape=jax.ShapeDtypeStruct((bsz, n, D_MODEL), BF16),
        scratch_shapes=[pltpu.VMEM((2 * n, D_MODEL), BF16)],
        compiler_params=_params("arbitrary", "arbitrary"),
        name="fnet",
    )(p, csc, mne, mno)


def _merge_kernel(oa_ref, hy_ref, fn_ref, ga_ref, gh_ref, gf_ref, x_ref, mod_ref,
                  wa_ref, wh_ref, wf_ref, wo_ref, bo_ref, lg_ref, lb_ref, o_ref):
    m = _sigmoid(ga_ref[...].astype(F32)) * _dot(oa_ref[...], wa_ref[...])
    m = m + _sigmoid(gh_ref[...].astype(F32)) * _dot(hy_ref[...], wh_ref[...])
    m = m + _sigmoid(gf_ref[...].astype(F32)) * _dot(fn_ref[...], wf_ref[...])
    y = _dot(m.astype(BF16), wo_ref[...]) + bo_ref[...]
    r = ALPHA * x_ref[...] + mod_ref[2:3, :] * y
    o_ref[...] = _layer_norm(r, lg_ref[...], lb_ref[...])


def _merge(o_att, z_hy, y_fn, p, x, mod, wa, wh, wf, wo, bo, lg, lb):
    bsz, n, d = x.shape
    tm = min(n, 512)
    tok = lambda c: pl.BlockSpec((None, tm, d), functools.partial(lambda bi, i, c: (bi, i, c), c=c))
    wspec = pl.BlockSpec((d, d), lambda bi, i: (0, 0))
    vspec = pl.BlockSpec((1, d), lambda bi, i: (0, 0))
    g0 = P_G // d
    return pl.pallas_call(
        _merge_kernel,
        grid=(bsz, n // tm),
        in_specs=[tok(0), tok(0), tok(0), tok(g0), tok(g0 + 1), tok(g0 + 2), tok(0),
                  pl.BlockSpec((None, N_MOD, d), lambda bi, i: (bi, 0, 0)),
                  wspec, wspec, wspec, wspec, vspec, vspec, vspec],
        out_specs=tok(0),
        out_shape=jax.ShapeDtypeStruct((bsz, n, d), F32),
        compiler_params=_params("arbitrary", "arbitrary"),
        name="merge_ln",
    )(o_att, z_hy, y_fn, p, p, p, x, mod, wa, wh, wf, wo, bo, lg, lb)


def _mlp_kernel(x_ref, mod_ref, w1_ref, b1_ref, w2_ref, b2_ref, lg_ref, lb_ref, o_ref, h_ref, acc_ref):
    j = pl.program_id(2)

    @pl.when(j == 0)
    def _():
        h_ref[...] = (x_ref[...] * (1.0 + mod_ref[4:5, :]) + mod_ref[3:4, :]).astype(BF16)
        acc_ref[...] = jnp.zeros_like(acc_ref)

    a = jnp.maximum(_dot(h_ref[...], w1_ref[...]) + b1_ref[...], 0.0)
    acc_ref[...] += _dot((a * a).astype(BF16), w2_ref[...])

    @pl.when(j == pl.num_programs(2) - 1)
    def _():
        y = acc_ref[...] + b2_ref[...]
        r = ALPHA * x_ref[...] + mod_ref[5:6, :] * y
        o_ref[...] = _layer_norm(r, lg_ref[...], lb_ref[...])


def _mlp(x, mod, w1, b1, w2, b2, lg, lb):
    bsz, n, d = x.shape
    ff = w1.shape[1]
    tm = min(n, 1024)
    fc = 1024
    vspec = pl.BlockSpec((1, d), lambda bi, i, j: (0, 0))
    return pl.pallas_call(
        _mlp_kernel,
        grid=(bsz, n // tm, ff // fc),
        in_specs=[pl.BlockSpec((None, tm, d), lambda bi, i, j: (bi, i, 0)),
                  pl.BlockSpec((None, N_MOD, d), lambda bi, i, j: (bi, 0, 0)),
                  pl.BlockSpec((d, fc), lambda bi, i, j: (0, j)),
                  pl.BlockSpec((1, fc), lambda bi, i, j: (0, j)),
                  pl.BlockSpec((fc, d), lambda bi, i, j: (j, 0)),
                  vspec, vspec, vspec],
        out_specs=pl.BlockSpec((None, tm, d), lambda bi, i, j: (bi, i, 0)),
        out_shape=jax.ShapeDtypeStruct((bsz, n, d), F32),
        scratch_shapes=[pltpu.VMEM((tm, d), BF16), pltpu.VMEM((tm, d), F32)],
        compiler_params=_params("arbitrary", "arbitrary", "arbitrary"),
        name="mlp_ln",
    )(x, mod, w1, b1, w2, b2, lg, lb)


def _int_grid(rows, cols):
    k = lax.broadcasted_iota(jnp.int32, (rows, cols), 0)
    s = lax.broadcasted_iota(jnp.int32, (rows, cols), 1)
    return k, s


def _hyena_dft(n):
    h = n // 2
    k, s = _int_grid(h, h)
    alt = jnp.where(s % 2 == 0, 1.0, -1.0)
    out = {}
    for name, samp in (('e', 2 * s), ('o', 2 * s + 1)):
        ang = ((k * samp) % (2 * n)).astype(F32) * (math.pi / n)
        mf = jnp.concatenate([jnp.cos(ang), jnp.where(k == 0, alt, jnp.sin(ang))], 0)
        out['mf' + name] = mf.astype(BF16)
        out['mi' + name] = mf.T.astype(BF16)
    return out


def _fnet_dft(n):
    h, q = n // 2, n // 4
    i, s = _int_grid(h, h)
    kk = 2 * (i % q) + i // q
    out = {}
    for name, samp in (('e', 2 * s), ('o', 2 * s + 1)):
        ang = ((kk * samp) % n).astype(F32) * (2.0 * math.pi / n)
        out['mn' + name] = jnp.concatenate([jnp.cos(ang), -jnp.sin(ang)], 1).astype(BF16)
    k, s = _int_grid(FNET_GROUP_DIM, FNET_GROUP_DIM)
    ang = ((k * s) % FNET_GROUP_DIM).astype(F32) * (2.0 * math.pi / FNET_GROUP_DIM)
    out['csc'] = jnp.concatenate([jnp.cos(ang), jnp.sin(ang)], 1).astype(BF16)
    return out


def _filter_embedding(n, width):
    t = jnp.linspace(0.0, 1.0, n, dtype=F32)[:, None]
    w = (2.0 * math.pi / n) * jnp.arange(n, dtype=F32)[:, None]
    f = jnp.linspace(1e-4, HYENA_BANDS - 1, HYENA_BANDS, dtype=F32)[None, :]
    emb = jnp.concatenate([t, jnp.cos(f * w), -jnp.sin(f * w)], -1)
    return _even_odd_rows(jnp.pad(emb, ((0, 0), (0, width - HYENA_EMB))))


def _rope_tables(n):
    rows = n // GRID_W
    row = jnp.repeat(jnp.arange(rows, dtype=F32), GRID_W)
    col = jnp.tile(jnp.arange(GRID_W, dtype=F32), rows)
    inv_freq = ROPE_THETA ** (-jnp.arange(ROPE_AXIS_DIM // 2, dtype=F32) * 2.0 / ROPE_AXIS_DIM)
    ar = row[:, None] * inv_freq[None, :]
    ac = col[:, None] * inv_freq[None, :]
    cos = jnp.concatenate([jnp.cos(ar), jnp.cos(ar), jnp.cos(ac), jnp.cos(ac)], -1)
    sin = jnp.concatenate([-jnp.sin(ar), jnp.sin(ar), -jnp.sin(ac), jnp.sin(ac)], -1)
    return _even_odd_rows(cos), _even_odd_rows(sin)


def _permute_in_cols(a):
    return jnp.concatenate([a[..., OFF_Q:OFF_K], a[..., OFF_G:D_IN], a[..., OFF_HY:OFF_FN],
                            a[..., OFF_FN:OFF_G], a[..., OFF_K:OFF_V], a[..., OFF_V:OFF_HY]], -1)


def _mixer(p_tok, seq_shape, lw, tabs, rope_tabs, ctx_parts):
    bsz, n = seq_shape
    p = p_tok.reshape(bsz, n, p_tok.shape[-1])
    q = _headnorm(p, P_Q // Q_W, N_Q_HEADS, lw['q_gain'], rope_tabs, ATTN_SCALE)
    k = _headnorm(p, P_K // KV_W, N_KV_HEADS, lw['k_gain'], rope_tabs, 1.0)
    o_att = _attention(q, [(k, p, P_V // KV_W)] + ctx_parts)

    hs, hd, mid = _filters(n, tabs['emb'], lw['hy_w1'], lw['hy_b1'], lw['hy_freq1'], lw['hy_w2'], lw['hy_b2'],
                           lw['hy_freq2'], lw['hy_w3'], tabs['absdelta'])
    k4 = _spectrum(tabs['mfe'], tabs['mfo'], hs, hd)
    z = None
    for o in range(HYENA_ORDER):
        z = _hyconv(o, z, p, lw['conv_w'], lw['conv_b'], tabs, k4, mid, lw['hy_skip'])
    y_fn = _fnet(p, tabs['csc'], tabs['mne'], tabs['mno'])
    tok = lambda a: a.reshape(p_tok.shape[0], p_tok.shape[1], a.shape[-1])
    return tok(o_att), tok(z), tok(y_fn), k, p


def _block(x_tok, mod, seq_shape, lw, tabs, rope_tabs, ctx_parts):
    p_tok = _inproj(x_tok, mod, lw['w_in'], lw['b_in'])
    o_att, z_hy, y_fn, k, p = _mixer(p_tok, seq_shape, lw, tabs, rope_tabs, ctx_parts)
    x_tok = _merge(o_att, z_hy, y_fn, p_tok, x_tok, mod, lw['w_att_o'], lw['w_hy_o'], lw['w_fn_o'],
                   lw['w_out'], lw['b_out'], lw['ln1_g'], lw['ln1_b'])
    x_tok = _mlp(x_tok, mod, lw['w_mlp1'], lw['b_mlp1'], lw['w_mlp2'], lw['b_mlp2'], lw['ln2_g'], lw['ln2_b'])
    return x_tok, k, p


def _seq_tables(n):
    tabs = dict(_hyena_dft(n))
    tabs.update(_fnet_dft(n))
    deltas = jnp.abs(jnp.linspace(MIN_DECAY, MAX_DECAY, D_HYENA, dtype=F32))
    tabs.update(emb=_filter_embedding(n, 128), absdelta=jnp.tile(deltas, HYENA_ORDER)[None, :])
    return tabs


def kernel(x, c, ctx, c_ctx, w_ada, b_ada, w_in, b_in, conv_w, conv_b, hy_w1, hy_b1, hy_freq1, hy_w2, hy_b2,
           hy_freq2, hy_w3, hy_skip, q_gain, k_gain, w_att_o, w_hy_o, w_fn_o, w_out, b_out, ln1_g, ln1_b,
           w_mlp1, b_mlp1, w_mlp2, b_mlp2, ln2_g, ln2_b):
    bsz, n_lat, d = x.shape
    n_ctx = ctx.shape[1]
    depth = w_ada.shape[0]
    tabs_lat = _seq_tables(n_lat)
    tabs_ctx = _seq_tables(n_ctx)
    rope_tabs = _rope_tables(n_lat)

    n_rows = -(-(bsz + 1) // 16) * 16
    cond = jnp.concatenate([c, c_ctx[None, :], jnp.zeros((n_rows - bsz - 1, d), F32)], 0)

    x = _deinterleave(x)
    ctx_tok = _deinterleave(ctx).reshape(1, bsz * n_ctx, d)
    row = lambda a: a[None, :]
    for i in range(depth):
        last = i == depth - 1
        lw = dict(
            w_in=_permute_in_cols(w_in[i]).astype(BF16), b_in=row(_permute_in_cols(b_in[i])),
            conv_w=conv_w[i], conv_b=row(conv_b[i]),
            hy_w1=jnp.pad(hy_w1[i], ((0, 128 - HYENA_EMB), (0, 0))), hy_b1=row(hy_b1[i]), hy_freq1=row(hy_freq1[i]),
            hy_w2=hy_w2[i], hy_b2=row(hy_b2[i]), hy_freq2=row(hy_freq2[i]), hy_w3=hy_w3[i],
            hy_skip=hy_skip[i].reshape(1, HYENA_ORDER * D_HYENA),
            q_gain=row(q_gain[i]), k_gain=row(k_gain[i]),
            w_att_o=w_att_o[i].astype(BF16), w_hy_o=w_hy_o[i].astype(BF16), w_fn_o=w_fn_o[i].astype(BF16),
            w_out=w_out[i].astype(BF16), b_out=row(b_out[i]), ln1_g=row(ln1_g[i]), ln1_b=row(ln1_b[i]),
            w_mlp1=w_mlp1[i].astype(BF16), b_mlp1=row(b_mlp1[i]), w_mlp2=w_mlp2[i].astype(BF16),
            b_mlp2=row(b_mlp2[i]), ln2_g=row(ln2_g[i]), ln2_b=row(ln2_b[i]))
        mod = _matmul(cond, w_ada[i].astype(BF16), row(b_ada[i]), silu=True)
        mod_l = mod[:bsz].reshape(bsz, N_MOD, d)
        mod_c = mod[bsz:bsz + 1].reshape(1, N_MOD, d)

        if last:
            kv_cols = slice(P_K, P_V + KV_W)
            p_c = _inproj(ctx_tok, mod_c, lw['w_in'][:, kv_cols], lw['b_in'][:, kv_cols])
            p_c = p_c.reshape(bsz, n_ctx, 2 * KV_W)
            k_c = _headnorm(p_c, 0, N_KV_HEADS, lw['k_gain'], None, 1.0)
            v_blk = 1
        else:
            ctx_tok, k_c, p_c = _block(ctx_tok, mod_c, (bsz, n_ctx), lw, tabs_ctx, None, [])
            v_blk = P_V // KV_W
        x, _, _ = _block(x, mod_l, (bsz, n_lat), lw, tabs_lat, rope_tabs, [(k_c, p_c, v_blk)])
    return _interleave(x)
```

```python
import functools
import math

import jax
import jax.numpy as jnp
from jax import lax
from jax.experimental import pallas as pl
from jax.experimental.pallas import tpu as pltpu

F32 = jnp.float32
BF16 = jnp.bfloat16

D_MODEL = 1024
GRID_W = 64
HEAD_DIM = 128
N_Q_HEADS = D_MODEL // HEAD_DIM
N_KV_HEADS = 2
GQA_GROUP = N_Q_HEADS // N_KV_HEADS
ROPE_THETA = 10000.0
ROPE_AXIS_DIM = HEAD_DIM // 2
ATTN_SCALE = HEAD_DIM ** -0.5

D_HYENA = D_MODEL
HYENA_ORDER = 2
HYENA_BANDS = 16
HYENA_EMB = 1 + 2 * HYENA_BANDS
SHORT_CONV = 3
DECAY_TARGET = 1e-2
MIN_DECAY = math.log(DECAY_TARGET) / 0.3
MAX_DECAY = math.log(DECAY_TARGET) / 1.5

FNET_GROUPS = 4
FNET_GROUP_DIM = D_MODEL // FNET_GROUPS
D_FF = 4 * D_MODEL
N_MOD = 6
DEPTH = 4

Q_W = N_Q_HEADS * HEAD_DIM
KV_W = N_KV_HEADS * HEAD_DIM
HY_W = (HYENA_ORDER + 1) * D_HYENA
OFF_Q = 0
OFF_K = OFF_Q + Q_W
OFF_V = OFF_K + KV_W
OFF_HY = OFF_V + KV_W
OFF_FN = OFF_HY + HY_W
OFF_G = OFF_FN + D_MODEL
D_IN = OFF_G + 3 * D_MODEL

P_Q = 0
P_G = P_Q + Q_W
P_FN = P_G + 3 * D_MODEL
P_K = P_FN + D_MODEL
P_V = P_K + KV_W

ALPHA = (2 * DEPTH) ** 0.25
LN_EPS = 1e-6
RMS_EPS = 1e-6
KERN_EPS = 1e-6

V7X_VMEM_LIMIT_BYTES = 56 * 1024 * 1024


def _params(*sem):
    return pltpu.CompilerParams(dimension_semantics=sem, vmem_limit_bytes=V7X_VMEM_LIMIT_BYTES)


def _dot(a, b):
    return jnp.dot(a, b, preferred_element_type=F32)


def _layer_norm(r, g, b):
    mu = jnp.mean(r, -1, keepdims=True)
    d = r - mu
    var = jnp.mean(d * d, -1, keepdims=True)
    return d * lax.rsqrt(var + LN_EPS) * g + b


def _sigmoid(x):
    return 1.0 / (1.0 + jnp.exp(-x))


def _deinterleave_kernel(x_ref, o_ref):
    hh = x_ref.shape[0] // 2
    o_ref[0:hh, :] = x_ref[pl.ds(0, hh, stride=2), :]
    o_ref[hh:2 * hh, :] = x_ref[pl.ds(1, hh, stride=2), :]


def _interleave_kernel(x_ref, o_ref):
    hh = x_ref.shape[0] // 2
    o_ref[pl.ds(0, hh, stride=2), :] = x_ref[0:hh, :]
    o_ref[pl.ds(1, hh, stride=2), :] = x_ref[hh:2 * hh, :]


def _reorder_tokens(x, body, name):
    bsz, n, d = x.shape
    tc = 128
    spec = pl.BlockSpec((None, n, tc), lambda bi, c: (bi, 0, c))
    return pl.pallas_call(
        body,
        grid=(bsz, d // tc),
        in_specs=[spec],
        out_specs=spec,
        out_shape=jax.ShapeDtypeStruct(x.shape, x.dtype),
        compiler_params=_params("arbitrary", "arbitrary"),
        name=name,
    )(x)


def _deinterleave(x):
    return _reorder_tokens(x, _deinterleave_kernel, "deinterleave")


def _interleave(x):
    return _reorder_tokens(x, _interleave_kernel, "interleave")


def _even_odd_rows(a):
    return jnp.concatenate([a[0::2], a[1::2]], 0)


def _matmul_kernel(a_ref, b_ref, bias_ref, o_ref, *, silu):
    a = a_ref[...]
    if silu:
        a = a * _sigmoid(a)
    o_ref[...] = _dot(a.astype(BF16), b_ref[...]) + bias_ref[...]


def _matmul(a, b, bias, *, silu=False, tn=1024):
    m, k = a.shape
    n = b.shape[1]
    return pl.pallas_call(
        functools.partial(_matmul_kernel, silu=silu),
        grid=(n // tn,),
        in_specs=[pl.BlockSpec((m, k), lambda j: (0, 0)),
                  pl.BlockSpec((k, tn), lambda j: (0, j)),
                  pl.BlockSpec((1, tn), lambda j: (0, j))],
        out_specs=pl.BlockSpec((m, tn), lambda j: (0, j)),
        out_shape=jax.ShapeDtypeStruct((m, n), F32),
        compiler_params=_params("arbitrary"),
        name="ada_matmul",
    )(a, b, bias)


def _inproj_kernel(*refs, conv):
    if conv:
        x_ref, mod_ref, w_ref, b_ref, cw_ref, cb_ref, o_ref, h_ref = refs
    else:
        x_ref, mod_ref, w_ref, b_ref, o_ref, h_ref = refs

    @pl.when(pl.program_id(2) == 0)
    def _():
        h_ref[...] = (x_ref[...] * (1.0 + mod_ref[1:2, :]) + mod_ref[0:1, :]).astype(BF16)

    y = _dot(h_ref[...], w_ref[...]) + b_ref[...]
    if conv:
        hh = y.shape[0] // 2
        ye, yo = _short_conv(y, cw_ref, cb_ref)
        o_ref[0:hh, :] = ye.astype(o_ref.dtype)
        o_ref[hh:2 * hh, :] = yo.astype(o_ref.dtype)
    else:
        o_ref[...] = y.astype(o_ref.dtype)


def _inproj(x, mod, w, b, conv=None):
    bsz, n, d = x.shape
    nout = w.shape[1]
    tm = min(n, 2048)
    tn = min(nout, 512)
    assert conv is None or tm == n
    mod_map = (lambda bi, i, j: (bi, 0, 0)) if mod.shape[0] == bsz else (lambda bi, i, j: (0, 0, 0))
    col = lambda rows: pl.BlockSpec((rows, tn), lambda bi, i, j: (0, j))
    in_specs = [pl.BlockSpec((None, tm, d), lambda bi, i, j: (bi, i, 0)),
                pl.BlockSpec((None, N_MOD, d), mod_map),
                pl.BlockSpec((d, tn), lambda bi, i, j: (0, j)),
                col(1)]
    args = [x, mod, w, b]
    if conv is not None:
        in_specs += [col(SHORT_CONV), col(1)]
        args += list(conv)
    return pl.pallas_call(
        functools.partial(_inproj_kernel, conv=conv is not None),
        grid=(bsz, n // tm, nout // tn),
        in_specs=in_specs,
        out_specs=pl.BlockSpec((None, tm, tn), lambda bi, i, j: (bi, i, j)),
        out_shape=jax.ShapeDtypeStruct((bsz, n, nout), BF16),
        scratch_shapes=[pltpu.VMEM((tm, d), BF16)],
        compiler_params=_params("arbitrary", "arbitrary", "arbitrary"),
        name="inproj_conv" if conv is not None else "inproj",
    )(*args)


def _headnorm_kernel(*refs, n_heads, rope, scale):
    if rope:
        p_ref, g_ref, cos_ref, sin_ref, o_ref = refs
    else:
        p_ref, g_ref, o_ref = refs
    gain = g_ref[...]
    for h in range(n_heads):
        sl = slice(h * HEAD_DIM, (h + 1) * HEAD_DIM)
        xf = p_ref[:, sl].astype(F32)
        y = xf * lax.rsqrt(jnp.mean(xf * xf, -1, keepdims=True) + RMS_EPS) * gain
        if rope:
            lane = lax.broadcasted_iota(jnp.int32, y.shape, 1)
            first = (lane % (ROPE_AXIS_DIM)) < (ROPE_AXIS_DIM // 2)
            swapped = jnp.where(first, pltpu.roll(y, HEAD_DIM - ROPE_AXIS_DIM // 2, 1),
                                pltpu.roll(y, ROPE_AXIS_DIM // 2, 1))
            y = y * cos_ref[...] + swapped * sin_ref[...]
        if scale != 1.0:
            y = y * scale
        o_ref[:, sl] = y.astype(o_ref.dtype)


def _headnorm(p, col_block, n_heads, gain, rope_tabs, scale):
    bsz, n, _ = p.shape
    w = n_heads * HEAD_DIM
    tm = min(n, 512)
    rope = rope_tabs is not None
    in_specs = [pl.BlockSpec((None, tm, w), lambda bi, i: (bi, i, col_block)),
                pl.BlockSpec((1, HEAD_DIM), lambda bi, i: (0, 0))]
    args = [p, gain]
    if rope:
        in_specs += [pl.BlockSpec((tm, HEAD_DIM), lambda bi, i: (i, 0))] * 2
        args += list(rope_tabs)
    return pl.pallas_call(
        functools.partial(_headnorm_kernel, n_heads=n_heads, rope=rope, scale=scale),
        grid=(bsz, n // tm),
        in_specs=in_specs,
        out_specs=pl.BlockSpec((None, tm, w), lambda bi, i: (bi, i, 0)),
        out_shape=jax.ShapeDtypeStruct((bsz, n, w), BF16),
        compiler_params=_params("arbitrary", "arbitrary"),
        name="headnorm",
    )(*args)


def _attn_kernel(*refs, n_parts):
    q_ref = refs[0]
    kv_refs = refs[1:1 + 2 * n_parts]
    o_ref = refs[1 + 2 * n_parts]
    for g in range(N_KV_HEADS):
        gsl = slice(g * HEAD_DIM, (g + 1) * HEAD_DIM)
        ks = [kv_refs[2 * i][:, gsl] for i in range(n_parts)]
        vs = [jnp.concatenate([kv_refs[2 * i + 1][:, gsl], jnp.ones((k.shape[0], HEAD_DIM), BF16)], 1)
              for i, k in enumerate(ks)]
        for r in range(GQA_GROUP):
            h = g * GQA_GROUP + r
            hsl = slice(h * HEAD_DIM, (h + 1) * HEAD_DIM)
            q = q_ref[:, hsl]
            ss = [lax.dot_general(q, k, (((1,), (1,)), ((), ())), preferred_element_type=F32) for k in ks]
            m = jnp.max(ss[0], -1, keepdims=True)
            for s in ss[1:]:
                m = jnp.maximum(m, jnp.max(s, -1, keepdims=True))
            o = None
            for s, v in zip(ss, vs):
                pv = _dot(jnp.exp((s - m).astype(BF16)), v)
                o = pv if o is None else o + pv
            o_ref[:, hsl] = (o[:, :HEAD_DIM] / o[:, HEAD_DIM:]).astype(o_ref.dtype)


def _attention(q, parts):
    bsz, n, w = q.shape
    tq = min(n, 256)
    in_specs = [pl.BlockSpec((None, tq, w), lambda bi, i: (bi, i, 0))]
    args = [q]
    for k, v, vblk in parts:
        nk = k.shape[1]
        in_specs.append(pl.BlockSpec((None, nk, KV_W), lambda bi, i: (bi, 0, 0)))
        in_specs.append(pl.BlockSpec((None, nk, KV_W), functools.partial(lambda bi, i, c: (bi, 0, c), c=vblk)))
        args += [k, v]
    return pl.pallas_call(
        functools.partial(_attn_kernel, n_parts=len(parts)),
        grid=(bsz, n // tq),
        in_specs=in_specs,
        out_specs=pl.BlockSpec((None, tq, w), lambda bi, i: (bi, i, 0)),
        out_shape=jax.ShapeDtypeStruct((bsz, n, w), BF16),
        compiler_params=_params("arbitrary", "arbitrary"),
        name="attention",
    )(*args)


def _filter_kernel(emb_ref, w1_ref, b1_ref, f1_ref, w2_ref, b2_ref, f2_ref, w3f_ref, w3b_ref, dl_ref,
                   hs_ref, hd_ref, mid_ref, *, n):
    hh = n // 2
    h = jnp.sin(f1_ref[...] * (_dot(emb_ref[...], w1_ref[...]) + b1_ref[...]))
    h = jnp.sin(f2_ref[...] * (_dot(h, w2_ref[...]) + b2_ref[...]))
    hf = _dot(h, w3f_ref[...])
    hb = _dot(h, w3b_ref[...])
    row = lax.broadcasted_iota(jnp.int32, hf.shape, 0)
    lag = jnp.where(row < hh, 2 * row, 2 * (row - hh) + 1)
    t = lag.astype(F32) * (1.0 / (n - 1))
    dec = jnp.exp(-t * dl_ref[...])
    hf = hf * dec
    hb = jnp.where(lag == 0, 0.0, hb * dec)
    ssq = jnp.sum(hf * hf, 0, keepdims=True) + jnp.sum(hb * hb, 0, keepdims=True)
    nrm = lax.rsqrt(ssq + KERN_EPS) * (1.0 / n)
    hs = (hf + hb) * nrm
    hd = (hf - hb) * nrm
    hs_ref[...] = hs.astype(hs_ref.dtype)
    hd_ref[...] = hd.astype(hd_ref.dtype)
    sign = jnp.where(row % 2 == 0, 1.0, -1.0)
    mid_ref[0:1, :] = jnp.sum(jnp.where(row < hh, hs * sign, 0.0), 0, keepdims=True)
    mid_ref[1:2, :] = jnp.sum(jnp.where(row < hh, 0.0, hd * sign), 0, keepdims=True)


def _filters(n, emb, w1, b1, f1, w2, b2, f2, w3, absdelta):
    od = HYENA_ORDER * D_HYENA
    tn = 512
    fh = w2.shape[0]
    ke = emb.shape[1]
    full = lambda shape: pl.BlockSpec(shape, lambda j: (0, 0))
    return pl.pallas_call(
        functools.partial(_filter_kernel, n=n),
        grid=(od // tn,),
        in_specs=[full((n, ke)), full((ke, fh)), full((1, fh)), full((1, fh)),
                  full((fh, fh)), full((1, fh)), full((1, fh)),
                  pl.BlockSpec((fh, tn), lambda j: (0, j)),
                  pl.BlockSpec((fh, tn), lambda j: (0, j + od // tn)),
                  pl.BlockSpec((1, tn), lambda j: (0, j))],
        out_specs=[pl.BlockSpec((n, tn), lambda j: (0, j)),
                   pl.BlockSpec((n, tn), lambda j: (0, j)),
                   pl.BlockSpec((2, tn), lambda j: (0, j))],
        out_shape=[jax.ShapeDtypeStruct((n, od), BF16), jax.ShapeDtypeStruct((n, od), BF16),
                   jax.ShapeDtypeStruct((2, od), F32)],
        compiler_params=_params("arbitrary"),
        name="hyena_filters",
    )(emb, w1, b1, f1, w2, b2, f2, w3, w3, absdelta)


def _spectrum_kernel(fe_re, fe_im, fo_re, fo_im, hse, hso, hde, hdo, k4_ref):
    p = _dot(fe_re[...], hse[...])
    q = _dot(fo_re[...], hso[...])
    pp = _dot(fe_im[...], hde[...])
    qp = _dot(fo_im[...], hdo[...])
    row0 = (lax.broadcasted_iota(jnp.int32, p.shape, 0) + pl.program_id(0) * p.shape[0]) == 0
    k4_ref[0] = jnp.where(row0, 0.5 * (p + q), p + q)
    k4_ref[1] = jnp.where(row0, 0.0, pp + qp)
    k4_ref[2] = jnp.where(row0, 0.5 * (p - q), p - q)
    k4_ref[3] = jnp.where(row0, 0.0, qp - pp)


def _spectrum(mfe, mfo, hs, hd):
    n, od = hs.shape
    h = n // 2
    tm = min(h, 512)
    tn = 512
    nb = h // tm
    mat = lambda off: pl.BlockSpec((tm, h), functools.partial(lambda i, j, o: (i + o, 0), o=off))
    half = lambda a: pl.BlockSpec((h, tn), functools.partial(lambda i, j, a: (a, j), a=a))
    return pl.pallas_call(
        _spectrum_kernel,
        grid=(nb, od // tn),
        in_specs=[mat(0), mat(nb), mat(0), mat(nb), half(0), half(1), half(0), half(1)],
        out_specs=pl.BlockSpec((4, tm, tn), lambda i, j: (0, i, j)),
        out_shape=jax.ShapeDtypeStruct((4, h, od), F32),
        compiler_params=_params("arbitrary", "arbitrary"),
        name="hyena_spectrum",
    )(mfe, mfe, mfo, mfo, hs, hs, hd, hd)


def _short_conv(x, w_ref, b_ref):
    hh = x.shape[0] // 2
    e = x[:hh]
    o = x[hh:]
    row = lax.broadcasted_iota(jnp.int32, e.shape, 0)
    o_prev = jnp.where(row == 0, 0.0, pltpu.roll(o, 1, 0))
    e_next = jnp.where(row == hh - 1, 0.0, pltpu.roll(e, hh - 1, 0))
    w0, w1, w2 = w_ref[0:1, :], w_ref[1:2, :], w_ref[2:3, :]
    return (b_ref[...] + o_prev * w0 + e * w1 + o * w2,
            b_ref[...] + e * w0 + o * w1 + e_next * w2)


def _hyconv_kernel(z_ref, x_ref, fe_re, fe_im, fo_re, fo_im, ie_re, ie_im, io_re, io_im,
                   k4_ref, mid_ref, skip_ref, o_ref, acc_ref, *, fk, n):
    hh = n // 2
    j = pl.program_id(2)

    @pl.when(j == 0)
    def _():
        acc_ref[...] = jnp.zeros_like(acc_ref)

    ze = z_ref[0:hh, :]
    zo = z_ref[hh:n, :]
    p = _dot(fe_re[...], ze)
    pp = _dot(fe_im[...], ze)
    q = _dot(fo_re[...], zo)
    qp = _dot(fo_im[...], zo)
    kra, kia, krb, kib = k4_ref[0], k4_ref[1], k4_ref[2], k4_ref[3]
    fra, fia = p + q, pp + qp
    frb, fib = p - q, qp - pp
    wra = fra * kra - fia * kia
    wia = fra * kia + fia * kra
    wrb = frb * krb - fib * kib
    wib = frb * kib + fib * krb
    row0 = (lax.broadcasted_iota(jnp.int32, p.shape, 0) + j * fk) == 0
    kmr, kmi = mid_ref[0:1, :], mid_ref[1:2, :]
    ge_re = (wra + wrb).astype(BF16)
    ge_im = jnp.where(row0, pp * kmr - qp * kmi, wia - wib).astype(BF16)
    go_re = (wra - wrb).astype(BF16)
    go_im = jnp.where(row0, pp * kmi + qp * kmr, wia + wib).astype(BF16)
    acc_ref[0:hh, :] += _dot(ie_re[...], ge_re) + _dot(ie_im[...], ge_im)
    acc_ref[hh:n, :] += _dot(io_re[...], go_re) + _dot(io_im[...], go_im)

    @pl.when(j == pl.num_programs(2) - 1)
    def _():
        y = acc_ref[...] + z_ref[...].astype(F32) * skip_ref[...]
        o_ref[...] = (x_ref[...].astype(F32) * y).astype(o_ref.dtype)


def _hyconv(order, z_arr, z_part, u, tabs, k4, mid, skip):
    bsz, n, _ = u.shape
    h = n // 2
    tn = 512
    fk = min(h, 512)
    nf = h // fk
    nct = D_HYENA // tn
    sect = lambda part: pl.BlockSpec((None, n, tn),
                                     functools.partial(lambda bi, c, j, o: (bi, 0, o + c), o=part * nct))
    fwd = lambda off: pl.BlockSpec((fk, h), functools.partial(lambda bi, c, j, o: (j + o, 0), o=off))
    inv = lambda off: pl.BlockSpec((h, fk), functools.partial(lambda bi, c, j, o: (0, j + o), o=off))
    ocol = functools.partial(lambda bi, c, j, o: (0, o + c), o=order * nct)
    in_specs = [sect(z_part), sect(order),
                fwd(0), fwd(nf), fwd(0), fwd(nf), inv(0), inv(nf), inv(0), inv(nf),
                pl.BlockSpec((4, fk, tn), functools.partial(lambda bi, c, j, o: (0, j, o + c), o=order * nct)),
                pl.BlockSpec((2, tn), ocol),
                pl.BlockSpec((1, tn), ocol)]
    return pl.pallas_call(
        functools.partial(_hyconv_kernel, fk=fk, n=n),
        grid=(bsz, nct, nf),
        in_specs=in_specs,
        out_specs=pl.BlockSpec((None, n, tn), lambda bi, c, j: (bi, 0, c)),
        out_shape=jax.ShapeDtypeStruct((bsz, n, D_HYENA), BF16),
        scratch_shapes=[pltpu.VMEM((n, tn), F32)],
        compiler_params=_params("arbitrary", "arbitrary", "arbitrary"),
        name="hyena_conv",
    )(z_arr, u, tabs['mfe'], tabs['mfe'], tabs['mfo'], tabs['mfo'],
      tabs['mie'], tabs['mie'], tabs['mio'], tabs['mio'], k4, mid, skip)


def _fnet_kernel(u_ref, csc_ref, mne_ref, mno_ref, o_ref, t_ref, *, n, tr, scale):
    hh = n // 2
    qq = n // 4
    i = pl.program_id(1)

    @pl.when(i == 0)
    def _():
        for g in range(FNET_GROUPS):
            gsl = slice(g * FNET_GROUP_DIM, (g + 1) * FNET_GROUP_DIM)
            t = _dot(u_ref[:, gsl], csc_ref[...])
            tc = t[:, :FNET_GROUP_DIM].astype(BF16)
            ts = t[:, FNET_GROUP_DIM:].astype(BF16)
            t_ref[0:hh, gsl] = tc[0:hh]
            t_ref[hh:n, gsl] = ts[0:hh]
            t_ref[n:n + hh, gsl] = tc[hh:n]
            t_ref[n + hh:2 * n, gsl] = ts[hh:n]

    a = _dot(mne_ref[...], t_ref[0:n, :])
    b = _dot(mno_ref[...], t_ref[n:2 * n, :])
    steps = qq // tr
    start = (i // steps) * hh + (i % steps) * tr
    o_ref[pl.ds(pl.multiple_of(start, tr), tr), :] = ((a + b) * scale).astype(o_ref.dtype)
    o_ref[pl.ds(pl.multiple_of(start + qq, tr), tr), :] = ((a - b) * scale).astype(o_ref.dtype)


def _fnet(p, csc, mne, mno):
    bsz, n, _ = p.shape
    tr = min(n // 4, 256)
    scale = 1.0 / math.sqrt(n * FNET_GROUP_DIM)
    return pl.pallas_call(
        functools.partial(_fnet_kernel, n=n, tr=tr, scale=scale),
        grid=(bsz, (n // 2) // tr),
        in_specs=[pl.BlockSpec((None, n, D_MODEL), lambda bi, i: (bi, 0, P_FN // D_MODEL)),
                  pl.BlockSpec((FNET_GROUP_DIM, 2 * FNET_GROUP_DIM), lambda bi, i: (0, 0)),
                  pl.BlockSpec((tr, n), lambda bi, i: (i, 0)),
                  pl.BlockSpec((tr, n), lambda bi, i: (i, 0))],
        out_specs=pl.BlockSpec((None, n, D_MODEL), lambda bi, i: (bi, 0, 0)),
        out_shape=jax.ShapeDtypeStruct((bsz, n, D_MODEL), BF16),
        scratch_shapes=[pltpu.VMEM((2 * n, D_MODEL), BF16)],
        compiler_params=_params("arbitrary", "arbitrary"),
        name="fnet",
    )(p, csc, mne, mno)


def _merge_kernel(oa_ref, hy_ref, fn_ref, ga_ref, gh_ref, gf_ref, x_ref, mod_ref,
                  wa_ref, wh_ref, wf_ref, wo_ref, bo_ref, lg_ref, lb_ref, o_ref):
    m = _sigmoid(ga_ref[...].astype(F32)) * _dot(oa_ref[...], wa_ref[...])
    m = m + _sigmoid(gh_ref[...].astype(F32)) * _dot(hy_ref[...], wh_ref[...])
    m = m + _sigmoid(gf_ref[...].astype(F32)) * _dot(fn_ref[...], wf_ref[...])
    y = _dot(m.astype(BF16), wo_ref[...]) + bo_ref[...]
    r = ALPHA * x_ref[...] + mod_ref[2:3, :] * y
    o_ref[...] = _layer_norm(r, lg_ref[...], lb_ref[...])


def _merge(o_att, z_hy, y_fn, p, x, mod, wa, wh, wf, wo, bo, lg, lb):
    bsz, n, d = x.shape
    tm = min(n, 512)
    tok = lambda c: pl.BlockSpec((None, tm, d), functools.partial(lambda bi, i, c: (bi, i, c), c=c))
    wspec = pl.BlockSpec((d, d), lambda bi, i: (0, 0))
    vspec = pl.BlockSpec((1, d), lambda bi, i: (0, 0))
    g0 = P_G // d
    return pl.pallas_call(
        _merge_kernel,
        grid=(bsz, n // tm),
        in_specs=[tok(0), tok(0), tok(0), tok(g0), tok(g0 + 1), tok(g0 + 2), tok(0),
                  pl.BlockSpec((None, N_MOD, d), lambda bi, i: (bi, 0, 0)),
                  wspec, wspec, wspec, wspec, vspec, vspec, vspec],
        out_specs=tok(0),
        out_shape=jax.ShapeDtypeStruct((bsz, n, d), F32),
        compiler_params=_params("arbitrary", "arbitrary"),
        name="merge_ln",
    )(o_att, z_hy, y_fn, p, p, p, x, mod, wa, wh, wf, wo, bo, lg, lb)


def _mlp_kernel(x_ref, mod_ref, w1_ref, b1_ref, w2_ref, b2_ref, lg_ref, lb_ref, o_ref, h_ref, acc_ref):
    j = pl.program_id(2)

    @pl.when(j == 0)
    def _():
        h_ref[...] = (x_ref[...] * (1.0 + mod_ref[4:5, :]) + mod_ref[3:4, :]).astype(BF16)
        acc_ref[...] = jnp.zeros_like(acc_ref)

    a = jnp.maximum(_dot(h_ref[...], w1_ref[...]) + b1_ref[...], 0.0)
    acc_ref[...] += _dot((a * a).astype(BF16), w2_ref[...])

    @pl.when(j == pl.num_programs(2) - 1)
    def _():
        y = acc_ref[...] + b2_ref[...]
        r = ALPHA * x_ref[...] + mod_ref[5:6, :] * y
        o_ref[...] = _layer_norm(r, lg_ref[...], lb_ref[...])


def _mlp(x, mod, w1, b1, w2, b2, lg, lb):
    bsz, n, d = x.shape
    ff = w1.shape[1]
    tm = min(n, 1024)
    fc = 1024
    vspec = pl.BlockSpec((1, d), lambda bi, i, j: (0, 0))
    return pl.pallas_call(
        _mlp_kernel,
        grid=(bsz, n // tm, ff // fc),
        in_specs=[pl.BlockSpec((None, tm, d), lambda bi, i, j: (bi, i, 0)),
                  pl.BlockSpec((None, N_MOD, d), lambda bi, i, j: (bi, 0, 0)),
                  pl.BlockSpec((d, fc), lambda bi, i, j: (0, j)),
                  pl.BlockSpec((1, fc), lambda bi, i, j: (0, j)),
                  pl.BlockSpec((fc, d), lambda bi, i, j: (j, 0)),
                  vspec, vspec, vspec],
        out_specs=pl.BlockSpec((None, tm, d), lambda bi, i, j: (bi, i, 0)),
        out_shape=jax.ShapeDtypeStruct((bsz, n, d), F32),
        scratch_shapes=[pltpu.VMEM((tm, d), BF16), pltpu.VMEM((tm, d), F32)],
        compiler_params=_params("arbitrary", "arbitrary", "arbitrary"),
        name="mlp_ln",
    )(x, mod, w1, b1, w2, b2, lg, lb)


def _int_grid(rows, cols):
    k = lax.broadcasted_iota(jnp.int32, (rows, cols), 0)
    s = lax.broadcasted_iota(jnp.int32, (rows, cols), 1)
    return k, s


TRIG_ROW_BLOCK = 32


def _trig_table(n_rows, samp, period):
    def base(kv):
        ang = ((kv[:, None] * samp[None, :]) % period).astype(F32) * (2.0 * math.pi / period)
        return jnp.cos(ang), jnp.sin(ang)

    ch, sh = base(jnp.arange(n_rows // TRIG_ROW_BLOCK, dtype=jnp.int32) * TRIG_ROW_BLOCK)
    cl, sl = base(jnp.arange(TRIG_ROW_BLOCK, dtype=jnp.int32))
    ch, sh, cl, sl = ch[:, None, :], sh[:, None, :], cl[None], sl[None]
    shape = (n_rows, samp.shape[0])
    return (ch * cl - sh * sl).reshape(shape), (sh * cl + ch * sl).reshape(shape)


def _hyena_dft(n):
    h = n // 2
    s = jnp.arange(h, dtype=jnp.int32)
    alt = jnp.where(s % 2 == 0, 1.0, -1.0)[None, :]
    first = lax.broadcasted_iota(jnp.int32, (h, h), 0) == 0
    out = {}
    for name, samp in (('e', 2 * s), ('o', 2 * s + 1)):
        cos, sin = _trig_table(h, samp, 2 * n)
        mf = jnp.concatenate([cos, jnp.where(first, alt, sin)], 0)
        out['mf' + name] = mf.astype(BF16)
        out['mi' + name] = mf.T.astype(BF16)
    return out


def _fnet_dft(n):
    h = n // 2
    s = jnp.arange(h, dtype=jnp.int32)
    out = {}
    for name, samp in (('e', 2 * s), ('o', 2 * s + 1)):
        cos, sin = _trig_table(h, samp, n)
        out['mn' + name] = _even_odd_rows(jnp.concatenate([cos, -sin], 1)).astype(BF16)
    k, s = _int_grid(FNET_GROUP_DIM, FNET_GROUP_DIM)
    ang = ((k * s) % FNET_GROUP_DIM).astype(F32) * (2.0 * math.pi / FNET_GROUP_DIM)
    out['csc'] = jnp.concatenate([jnp.cos(ang), jnp.sin(ang)], 1).astype(BF16)
    return out


def _filter_embedding(n, width):
    t = jnp.linspace(0.0, 1.0, n, dtype=F32)[:, None]
    w = (2.0 * math.pi / n) * jnp.arange(n, dtype=F32)[:, None]
    f = jnp.linspace(1e-4, HYENA_BANDS - 1, HYENA_BANDS, dtype=F32)[None, :]
    emb = jnp.concatenate([t, jnp.cos(f * w), -jnp.sin(f * w)], -1)
    return _even_odd_rows(jnp.pad(emb, ((0, 0), (0, width - HYENA_EMB))))


def _rope_tables(n):
    rows = n // GRID_W
    row = jnp.repeat(jnp.arange(rows, dtype=F32), GRID_W)
    col = jnp.tile(jnp.arange(GRID_W, dtype=F32), rows)
    inv_freq = ROPE_THETA ** (-jnp.arange(ROPE_AXIS_DIM // 2, dtype=F32) * 2.0 / ROPE_AXIS_DIM)
    ar = row[:, None] * inv_freq[None, :]
    ac = col[:, None] * inv_freq[None, :]
    cos = jnp.concatenate([jnp.cos(ar), jnp.cos(ar), jnp.cos(ac), jnp.cos(ac)], -1)
    sin = jnp.concatenate([-jnp.sin(ar), jnp.sin(ar), -jnp.sin(ac), jnp.sin(ac)], -1)
    return _even_odd_rows(cos), _even_odd_rows(sin)


def _permute_in_cols(a):
    return jnp.concatenate([a[..., OFF_Q:OFF_K], a[..., OFF_G:D_IN], a[..., OFF_FN:OFF_G],
                            a[..., OFF_K:OFF_V], a[..., OFF_V:OFF_HY]], -1)


def _mixer(p_tok, u, lw, tabs, rope_tabs, ctx_parts):
    bsz, n, _ = u.shape
    p = p_tok.reshape(bsz, n, p_tok.shape[-1])
    q = _headnorm(p, P_Q // Q_W, N_Q_HEADS, lw['q_gain'], rope_tabs, ATTN_SCALE)
    k = _headnorm(p, P_K // KV_W, N_KV_HEADS, lw['k_gain'], rope_tabs, 1.0)
    o_att = _attention(q, [(k, p, P_V // KV_W)] + ctx_parts)

    hs, hd, mid = _filters(n, tabs['emb'], lw['hy_w1'], lw['hy_b1'], lw['hy_freq1'], lw['hy_w2'], lw['hy_b2'],
                           lw['hy_freq2'], lw['hy_w3'], tabs['absdelta'])
    k4 = _spectrum(tabs['mfe'], tabs['mfo'], hs, hd)
    z = _hyconv(0, u, HYENA_ORDER, u, tabs, k4, mid, lw['hy_skip'])
    for o in range(1, HYENA_ORDER):
        z = _hyconv(o, z, 0, u, tabs, k4, mid, lw['hy_skip'])
    y_fn = _fnet(p, tabs['csc'], tabs['mne'], tabs['mno'])
    tok = lambda a: a.reshape(p_tok.shape[0], p_tok.shape[1], a.shape[-1])
    return tok(o_att), tok(z), tok(y_fn), k, p


def _block(x_tok, mod, seq_shape, lw, tabs, rope_tabs, ctx_parts):
    p_tok = _inproj(x_tok, mod, lw['w_in'], lw['b_in'])
    u = _inproj(x_tok.reshape(*seq_shape, x_tok.shape[-1]), mod, lw['w_hy'], lw['b_hy'],
                conv=(lw['conv_w'], lw['conv_b']))
    o_att, z_hy, y_fn, k, p = _mixer(p_tok, u, lw, tabs, rope_tabs, ctx_parts)
    x_tok = _merge(o_att, z_hy, y_fn, p_tok, x_tok, mod, lw['w_att_o'], lw['w_hy_o'], lw['w_fn_o'],
                   lw['w_out'], lw['b_out'], lw['ln1_g'], lw['ln1_b'])
    x_tok = _mlp(x_tok, mod, lw['w_mlp1'], lw['b_mlp1'], lw['w_mlp2'], lw['b_mlp2'], lw['ln2_g'], lw['ln2_b'])
    return x_tok, k, p


def _seq_tables(n):
    tabs = dict(_hyena_dft(n))
    tabs.update(_fnet_dft(n))
    deltas = jnp.abs(jnp.linspace(MIN_DECAY, MAX_DECAY, D_HYENA, dtype=F32))
    tabs.update(emb=_filter_embedding(n, 128), absdelta=jnp.tile(deltas, HYENA_ORDER)[None, :])
    return tabs


def kernel(x, c, ctx, c_ctx, w_ada, b_ada, w_in, b_in, conv_w, conv_b, hy_w1, hy_b1, hy_freq1, hy_w2, hy_b2,
           hy_freq2, hy_w3, hy_skip, q_gain, k_gain, w_att_o, w_hy_o, w_fn_o, w_out, b_out, ln1_g, ln1_b,
           w_mlp1, b_mlp1, w_mlp2, b_mlp2, ln2_g, ln2_b):
    bsz, n_lat, d = x.shape
    n_ctx = ctx.shape[1]
    depth = w_ada.shape[0]
    tabs_lat = _seq_tables(n_lat)
    tabs_ctx = _seq_tables(n_ctx)
    rope_tabs = _rope_tables(n_lat)

    n_rows = -(-(bsz + 1) // 16) * 16
    cond = jnp.concatenate([c, c_ctx[None, :], jnp.zeros((n_rows - bsz - 1, d), F32)], 0)

    x = _deinterleave(x)
    ctx_tok = _deinterleave(ctx).reshape(1, bsz * n_ctx, d)
    row = lambda a: a[None, :]
    for i in range(depth):
        last = i == depth - 1
        lw = dict(
            w_in=_permute_in_cols(w_in[i]).astype(BF16), b_in=row(_permute_in_cols(b_in[i])),
            w_hy=w_in[i][:, OFF_HY:OFF_FN].astype(BF16), b_hy=row(b_in[i][OFF_HY:OFF_FN]),
            conv_w=conv_w[i], conv_b=row(conv_b[i]),
            hy_w1=jnp.pad(hy_w1[i], ((0, 128 - HYENA_EMB), (0, 0))), hy_b1=row(hy_b1[i]), hy_freq1=row(hy_freq1[i]),
            hy_w2=hy_w2[i], hy_b2=row(hy_b2[i]), hy_freq2=row(hy_freq2[i]), hy_w3=hy_w3[i],
            hy_skip=hy_skip[i].reshape(1, HYENA_ORDER * D_HYENA),
            q_gain=row(q_gain[i]), k_gain=row(k_gain[i]),
            w_att_o=w_att_o[i].astype(BF16), w_hy_o=w_hy_o[i].astype(BF16), w_fn_o=w_fn_o[i].astype(BF16),
            w_out=w_out[i].astype(BF16), b_out=row(b_out[i]), ln1_g=row(ln1_g[i]), ln1_b=row(ln1_b[i]),
            w_mlp1=w_mlp1[i].astype(BF16), b_mlp1=row(b_mlp1[i]), w_mlp2=w_mlp2[i].astype(BF16),
            b_mlp2=row(b_mlp2[i]), ln2_g=row(ln2_g[i]), ln2_b=row(ln2_b[i]))
        mod = _matmul(cond, w_ada[i].astype(BF16), row(b_ada[i]), silu=True)
        mod_l = mod[:bsz].reshape(bsz, N_MOD, d)
        mod_c = mod[bsz:bsz + 1].reshape(1, N_MOD, d)

        if last:
            kv_cols = slice(P_K, P_V + KV_W)
            p_c = _inproj(ctx_tok, mod_c, lw['w_in'][:, kv_cols], lw['b_in'][:, kv_cols])
            p_c = p_c.reshape(bsz, n_ctx, 2 * KV_W)
            k_c = _headnorm(p_c, 0, N_KV_HEADS, lw['k_gain'], None, 1.0)
            v_blk = 1
        else:
            ctx_tok, k_c, p_c = _block(ctx_tok, mod_c, (bsz, n_ctx), lw, tabs_ctx, None, [])
            v_blk = P_V // KV_W
        x, _, _ = _block(x, mod_l, (bsz, n_lat), lw, tabs_lat, rope_tabs, [(k_c, p_c, v_blk)])
    return _interleave(x)
```

```python
import functools
import math

import jax
import jax.numpy as jnp
from jax import lax
from jax.experimental import pallas as pl
from jax.experimental.pallas import tpu as pltpu

F32 = jnp.float32
BF16 = jnp.bfloat16

D_MODEL = 1024
GRID_W = 64
HEAD_DIM = 128
N_Q_HEADS = D_MODEL // HEAD_DIM
N_KV_HEADS = 2
GQA_GROUP = N_Q_HEADS // N_KV_HEADS
ROPE_THETA = 10000.0
ROPE_AXIS_DIM = HEAD_DIM // 2
ATTN_SCALE = HEAD_DIM ** -0.5

D_HYENA = D_MODEL
HYENA_ORDER = 2
HYENA_BANDS = 16
HYENA_EMB = 1 + 2 * HYENA_BANDS
SHORT_CONV = 3
DECAY_TARGET = 1e-2
MIN_DECAY = math.log(DECAY_TARGET) / 0.3
MAX_DECAY = math.log(DECAY_TARGET) / 1.5

FNET_GROUPS = 4
FNET_GROUP_DIM = D_MODEL // FNET_GROUPS
D_FF = 4 * D_MODEL
N_MOD = 6
DEPTH = 4

Q_W = N_Q_HEADS * HEAD_DIM
KV_W = N_KV_HEADS * HEAD_DIM
HY_W = (HYENA_ORDER + 1) * D_HYENA
OFF_Q = 0
OFF_K = OFF_Q + Q_W
OFF_V = OFF_K + KV_W
OFF_HY = OFF_V + KV_W
OFF_FN = OFF_HY + HY_W
OFF_G = OFF_FN + D_MODEL
D_IN = OFF_G + 3 * D_MODEL

P_Q = 0
P_G = P_Q + Q_W
P_FN = P_G + 3 * D_MODEL
P_K = P_FN + D_MODEL
P_V = P_K + KV_W
P_HY = P_V + KV_W

ALPHA = (2 * DEPTH) ** 0.25
LN_EPS = 1e-6
RMS_EPS = 1e-6
KERN_EPS = 1e-6

V7X_VMEM_LIMIT_BYTES = 56 * 1024 * 1024


def _params(*sem):
    return pltpu.CompilerParams(dimension_semantics=sem, vmem_limit_bytes=V7X_VMEM_LIMIT_BYTES)


def _dot(a, b):
    return jnp.dot(a, b, preferred_element_type=F32)


def _layer_norm(r, g, b):
    mu = jnp.mean(r, -1, keepdims=True)
    d = r - mu
    var = jnp.mean(d * d, -1, keepdims=True)
    return d * lax.rsqrt(var + LN_EPS) * g + b


def _sigmoid(x):
    return 1.0 / (1.0 + jnp.exp(-x))


def _deinterleave_kernel(x_ref, o_ref):
    hh = x_ref.shape[0] // 2
    o_ref[0:hh, :] = x_ref[pl.ds(0, hh, stride=2), :]
    o_ref[hh:2 * hh, :] = x_ref[pl.ds(1, hh, stride=2), :]


def _interleave_kernel(x_ref, o_ref):
    hh = x_ref.shape[0] // 2
    o_ref[pl.ds(0, hh, stride=2), :] = x_ref[0:hh, :]
    o_ref[pl.ds(1, hh, stride=2), :] = x_ref[hh:2 * hh, :]


def _reorder_tokens(x, body, name):
    bsz, n, d = x.shape
    tc = 128
    spec = pl.BlockSpec((None, n, tc), lambda bi, c: (bi, 0, c))
    return pl.pallas_call(
        body,
        grid=(bsz, d // tc),
        in_specs=[spec],
        out_specs=spec,
        out_shape=jax.ShapeDtypeStruct(x.shape, x.dtype),
        compiler_params=_params("arbitrary", "arbitrary"),
        name=name,
    )(x)


def _deinterleave(x):
    return _reorder_tokens(x, _deinterleave_kernel, "deinterleave")


def _interleave(x):
    return _reorder_tokens(x, _interleave_kernel, "interleave")


def _even_odd_rows(a):
    return jnp.concatenate([a[0::2], a[1::2]], 0)


def _matmul_kernel(a_ref, b_ref, bias_ref, o_ref, *, silu):
    a = a_ref[...]
    if silu:
        a = a * _sigmoid(a)
    o_ref[...] = _dot(a.astype(BF16), b_ref[...].astype(BF16)) + bias_ref[...]


def _matmul(a, b, bias, *, silu=False, tn=1024):
    m, k = a.shape
    n = b.shape[1]
    return pl.pallas_call(
        functools.partial(_matmul_kernel, silu=silu),
        grid=(n // tn,),
        in_specs=[pl.BlockSpec((m, k), lambda j: (0, 0)),
                  pl.BlockSpec((k, tn), lambda j: (0, j)),
                  pl.BlockSpec((1, tn), lambda j: (0, j))],
        out_specs=pl.BlockSpec((m, tn), lambda j: (0, j)),
        out_shape=jax.ShapeDtypeStruct((m, n), F32),
        compiler_params=_params("arbitrary"),
        name="ada_matmul",
    )(a, b, bias)


def _inproj_kernel(x_ref, mod_ref, w_ref, b_ref, o_ref, h_ref):
    @pl.when(pl.program_id(2) == 0)
    def _():
        h_ref[...] = (x_ref[...] * (1.0 + mod_ref[1:2, :]) + mod_ref[0:1, :]).astype(BF16)

    o_ref[...] = (_dot(h_ref[...], w_ref[...]) + b_ref[...]).astype(o_ref.dtype)


INPROJ_MAX_COLS = 2176


def _inproj(x, mod, w, b):
    bsz, n, d = x.shape
    nout = w.shape[1]
    tm = min(n, 1024)
    tn = max(t for t in range(128, min(nout, INPROJ_MAX_COLS) + 1, 128) if nout % t == 0)
    return pl.pallas_call(
        _inproj_kernel,
        grid=(bsz, n // tm, nout // tn),
        in_specs=[pl.BlockSpec((None, tm, d), lambda bi, i, j: (bi, i, 0)),
                  pl.BlockSpec((None, N_MOD, d), lambda bi, i, j: (bi, 0, 0)),
                  pl.BlockSpec((d, tn), lambda bi, i, j: (0, j)),
                  pl.BlockSpec((1, tn), lambda bi, i, j: (0, j))],
        out_specs=pl.BlockSpec((None, tm, tn), lambda bi, i, j: (bi, i, j)),
        out_shape=jax.ShapeDtypeStruct((bsz, n, nout), BF16),
        scratch_shapes=[pltpu.VMEM((tm, d), BF16)],
        compiler_params=_params("arbitrary", "arbitrary", "arbitrary"),
        name="inproj",
    )(x, mod, w, b)


def _headnorm_kernel(*refs, n_heads, rope, scale):
    if rope:
        p_ref, g_ref, cos_ref, sin_ref, o_ref = refs
    else:
        p_ref, g_ref, o_ref = refs
    gain = g_ref[...]
    for h in range(n_heads):
        sl = slice(h * HEAD_DIM, (h + 1) * HEAD_DIM)
        xf = p_ref[:, sl].astype(F32)
        y = xf * lax.rsqrt(jnp.mean(xf * xf, -1, keepdims=True) + RMS_EPS) * gain
        if rope:
            lane = lax.broadcasted_iota(jnp.int32, y.shape, 1)
            first = (lane % (ROPE_AXIS_DIM)) < (ROPE_AXIS_DIM // 2)
            swapped = jnp.where(first, pltpu.roll(y, HEAD_DIM - ROPE_AXIS_DIM // 2, 1),
                                pltpu.roll(y, ROPE_AXIS_DIM // 2, 1))
            y = y * cos_ref[...] + swapped * sin_ref[...]
        if scale != 1.0:
            y = y * scale
        o_ref[:, sl] = y.astype(o_ref.dtype)


def _headnorm(p, col_block, n_heads, gain, rope_tabs, scale):
    bsz, n, _ = p.shape
    w = n_heads * HEAD_DIM
    tm = min(n, 512)
    rope = rope_tabs is not None
    in_specs = [pl.BlockSpec((None, tm, w), lambda bi, i: (bi, i, col_block)),
                pl.BlockSpec((1, HEAD_DIM), lambda bi, i: (0, 0))]
    args = [p, gain]
    if rope:
        in_specs += [pl.BlockSpec((tm, HEAD_DIM), lambda bi, i: (i, 0))] * 2
        args += list(rope_tabs)
    return pl.pallas_call(
        functools.partial(_headnorm_kernel, n_heads=n_heads, rope=rope, scale=scale),
        grid=(bsz, n // tm),
        in_specs=in_specs,
        out_specs=pl.BlockSpec((None, tm, w), lambda bi, i: (bi, i, 0)),
        out_shape=jax.ShapeDtypeStruct((bsz, n, w), BF16),
        compiler_params=_params("arbitrary", "arbitrary"),
        name="headnorm",
    )(*args)


def _attn_kernel(*refs, n_parts):
    q_ref = refs[0]
    kv_refs = refs[1:1 + 2 * n_parts]
    o_ref = refs[1 + 2 * n_parts]
    for g in range(N_KV_HEADS):
        gsl = slice(g * HEAD_DIM, (g + 1) * HEAD_DIM)
        ks = [kv_refs[2 * i][:, gsl] for i in range(n_parts)]
        vs = [jnp.concatenate([kv_refs[2 * i + 1][:, gsl], jnp.ones((k.shape[0], HEAD_DIM), BF16)], 1)
              for i, k in enumerate(ks)]
        for r in range(GQA_GROUP):
            h = g * GQA_GROUP + r
            hsl = slice(h * HEAD_DIM, (h + 1) * HEAD_DIM)
            q = q_ref[:, hsl]
            ss = [lax.dot_general(q, k, (((1,), (1,)), ((), ())), preferred_element_type=F32) for k in ks]
            m = jnp.max(ss[0], -1, keepdims=True)
            for s in ss[1:]:
                m = jnp.maximum(m, jnp.max(s, -1, keepdims=True))
            o = None
            for s, v in zip(ss, vs):
                pv = _dot(jnp.exp((s - m).astype(BF16)), v)
                o = pv if o is None else o + pv
            o_ref[:, hsl] = (o[:, :HEAD_DIM] / o[:, HEAD_DIM:]).astype(o_ref.dtype)


def _attention(q, parts):
    bsz, n, w = q.shape
    tq = min(n, 256)
    in_specs = [pl.BlockSpec((None, tq, w), lambda bi, i: (bi, i, 0))]
    args = [q]
    for k, v, vblk in parts:
        nk = k.shape[1]
        in_specs.append(pl.BlockSpec((None, nk, KV_W), lambda bi, i: (bi, 0, 0)))
        in_specs.append(pl.BlockSpec((None, nk, KV_W), functools.partial(lambda bi, i, c: (bi, 0, c), c=vblk)))
        args += [k, v]
    return pl.pallas_call(
        functools.partial(_attn_kernel, n_parts=len(parts)),
        grid=(bsz, n // tq),
        in_specs=in_specs,
        out_specs=pl.BlockSpec((None, tq, w), lambda bi, i: (bi, i, 0)),
        out_shape=jax.ShapeDtypeStruct((bsz, n, w), BF16),
        compiler_params=_params("arbitrary", "arbitrary"),
        name="attention",
    )(*args)


def _filter_kernel(emb_ref, w1_ref, b1_ref, f1_ref, w2_ref, b2_ref, f2_ref, w3f_ref, w3b_ref, dl_ref,
                   hs_ref, hd_ref, mid_ref, *, n):
    hh = n // 2
    h = jnp.sin(f1_ref[...] * (_dot(emb_ref[...], w1_ref[...]) + b1_ref[...]))
    h = jnp.sin(f2_ref[...] * (_dot(h, w2_ref[...]) + b2_ref[...]))
    hf = _dot(h, w3f_ref[...])
    hb = _dot(h, w3b_ref[...])
    row = lax.broadcasted_iota(jnp.int32, hf.shape, 0)
    lag = jnp.where(row < hh, 2 * row, 2 * (row - hh) + 1)
    t = lag.astype(F32) * (1.0 / (n - 1))
    dec = jnp.exp(-t * dl_ref[...])
    hf = hf * dec
    hb = jnp.where(lag == 0, 0.0, hb * dec)
    ssq = jnp.sum(hf * hf, 0, keepdims=True) + jnp.sum(hb * hb, 0, keepdims=True)
    nrm = lax.rsqrt(ssq + KERN_EPS) * (1.0 / n)
    hs = (hf + hb) * nrm
    hd = (hf - hb) * nrm
    hs_ref[...] = hs.astype(hs_ref.dtype)
    hd_ref[...] = hd.astype(hd_ref.dtype)
    sign = jnp.where(row % 2 == 0, 1.0, -1.0)
    mid_ref[0:1, :] = jnp.sum(jnp.where(row < hh, hs * sign, 0.0), 0, keepdims=True)
    mid_ref[1:2, :] = jnp.sum(jnp.where(row < hh, 0.0, hd * sign), 0, keepdims=True)


def _filters(n, emb, w1, b1, f1, w2, b2, f2, w3, absdelta):
    od = HYENA_ORDER * D_HYENA
    tn = 512
    fh = w2.shape[0]
    ke = emb.shape[1]
    full = lambda shape: pl.BlockSpec(shape, lambda j: (0, 0))
    return pl.pallas_call(
        functools.partial(_filter_kernel, n=n),
        grid=(od // tn,),
        in_specs=[full((n, ke)), full((ke, fh)), full((1, fh)), full((1, fh)),
                  full((fh, fh)), full((1, fh)), full((1, fh)),
                  pl.BlockSpec((fh, tn), lambda j: (0, j)),
                  pl.BlockSpec((fh, tn), lambda j: (0, j + od // tn)),
                  pl.BlockSpec((1, tn), lambda j: (0, j))],
        out_specs=[pl.BlockSpec((n, tn), lambda j: (0, j)),
                   pl.BlockSpec((n, tn), lambda j: (0, j)),
                   pl.BlockSpec((2, tn), lambda j: (0, j))],
        out_shape=[jax.ShapeDtypeStruct((n, od), BF16), jax.ShapeDtypeStruct((n, od), BF16),
                   jax.ShapeDtypeStruct((2, od), F32)],
        compiler_params=_params("arbitrary"),
        name="hyena_filters",
    )(emb, w1, b1, f1, w2, b2, f2, w3, w3, absdelta)


def _spectrum_kernel(fe_re, fe_im, fo_re, fo_im, hse, hso, hde, hdo, k4_ref):
    p = _dot(fe_re[...], hse[...])
    q = _dot(fo_re[...], hso[...])
    pp = _dot(fe_im[...], hde[...])
    qp = _dot(fo_im[...], hdo[...])
    row0 = (lax.broadcasted_iota(jnp.int32, p.shape, 0) + pl.program_id(0) * p.shape[0]) == 0
    k4_ref[0] = jnp.where(row0, 0.5 * (p + q), p + q)
    k4_ref[1] = jnp.where(row0, 0.0, pp + qp)
    k4_ref[2] = jnp.where(row0, 0.5 * (p - q), p - q)
    k4_ref[3] = jnp.where(row0, 0.0, qp - pp)


def _spectrum(mfe, mfo, hs, hd):
    n, od = hs.shape
    h = n // 2
    tm = min(h, 512)
    tn = 512
    nb = h // tm
    mat = lambda off: pl.BlockSpec((tm, h), functools.partial(lambda i, j, o: (i + o, 0), o=off))
    half = lambda a: pl.BlockSpec((h, tn), functools.partial(lambda i, j, a: (a, j), a=a))
    return pl.pallas_call(
        _spectrum_kernel,
        grid=(nb, od // tn),
        in_specs=[mat(0), mat(nb), mat(0), mat(nb), half(0), half(1), half(0), half(1)],
        out_specs=pl.BlockSpec((4, tm, tn), lambda i, j: (0, i, j)),
        out_shape=jax.ShapeDtypeStruct((4, h, od), F32),
        compiler_params=_params("arbitrary", "arbitrary"),
        name="hyena_spectrum",
    )(mfe, mfe, mfo, mfo, hs, hs, hd, hd)


def _short_conv(x, w_ref, b_ref):
    hh = x.shape[0] // 2
    e = x[:hh]
    o = x[hh:]
    row = lax.broadcasted_iota(jnp.int32, e.shape, 0)
    o_prev = jnp.where(row == 0, 0.0, pltpu.roll(o, 1, 0))
    e_next = jnp.where(row == hh - 1, 0.0, pltpu.roll(e, hh - 1, 0))
    w0, w1, w2 = w_ref[0:1, :], w_ref[1:2, :], w_ref[2:3, :]
    return (b_ref[...] + o_prev * w0 + e * w1 + o * w2,
            b_ref[...] + e * w0 + o * w1 + e_next * w2)


def _hyconv_kernel(*refs, conv_z, fk, n):
    if conv_z:
        z_ref, cwz_ref, cbz_ref = refs[:3]
        refs = refs[3:]
    else:
        z_ref = refs[0]
        refs = refs[1:]
    x_ref, cwx_ref, cbx_ref, mfe_ref, mfo_ref, mie_ref, mio_ref, k4_ref, mid_ref, skip_ref, o_ref = refs
    hh = n // 2
    if conv_z:
        ze, zo = _short_conv(z_ref[...].astype(F32), cwz_ref, cbz_ref)
        ze, zo = ze.astype(BF16), zo.astype(BF16)
    else:
        ze, zo = z_ref[0:hh, :], z_ref[hh:n, :]
    kmr, kmi = mid_ref[0:1, :], mid_ref[1:2, :]
    ye = yo = None
    for c in range(hh // fk):
        re = slice(c * fk, (c + 1) * fk)
        im = slice(hh + c * fk, hh + (c + 1) * fk)
        p = _dot(mfe_ref[re, :], ze)
        pp = _dot(mfe_ref[im, :], ze)
        q = _dot(mfo_ref[re, :], zo)
        qp = _dot(mfo_ref[im, :], zo)
        kra, kia, krb, kib = k4_ref[0, re, :], k4_ref[1, re, :], k4_ref[2, re, :], k4_ref[3, re, :]
        fra, fia = p + q, pp + qp
        frb, fib = p - q, qp - pp
        wra = fra * kra - fia * kia
        wia = fra * kia + fia * kra
        wrb = frb * krb - fib * kib
        wib = frb * kib + fib * krb
        ge_im = wia - wib
        go_im = wia + wib
        if c == 0:
            row0 = lax.broadcasted_iota(jnp.int32, p.shape, 0) == 0
            ge_im = jnp.where(row0, pp * kmr - qp * kmi, ge_im)
            go_im = jnp.where(row0, pp * kmi + qp * kmr, go_im)
        de = _dot(mie_ref[:, re], (wra + wrb).astype(BF16)) + _dot(mie_ref[:, im], ge_im.astype(BF16))
        do = _dot(mio_ref[:, re], (wra - wrb).astype(BF16)) + _dot(mio_ref[:, im], go_im.astype(BF16))
        ye = de if ye is None else ye + de
        yo = do if yo is None else yo + do
    xe, xo = _short_conv(x_ref[...].astype(F32), cwx_ref, cbx_ref)
    skip = skip_ref[...]
    o_ref[0:hh, :] = (xe * (ye + ze.astype(F32) * skip)).astype(o_ref.dtype)
    o_ref[hh:n, :] = (xo * (yo + zo.astype(F32) * skip)).astype(o_ref.dtype)


def _hyconv(order, z_arr, p, conv_w, conv_b, tabs, k4, mid, skip):
    bsz, n, _ = p.shape
    h = n // 2
    tn = 512
    fk = min(h, 256)
    nct = D_HYENA // tn
    conv_z = z_arr is None
    once = pl.Buffered(1)

    def sect(part):
        off = P_HY // tn + part * nct
        return pl.BlockSpec((None, n, tn), functools.partial(lambda c, bi, o: (bi, 0, o + c), o=off))

    def wsect(rows, part):
        return pl.BlockSpec((rows, tn), functools.partial(lambda c, bi, o: (0, o + c), o=part * nct))

    whole = lambda a: pl.BlockSpec(a.shape, lambda c, bi: (0, 0), pipeline_mode=once)
    ocol = functools.partial(lambda c, bi, o: (0, o + c), o=order * nct)
    if conv_z:
        in_specs = [sect(HYENA_ORDER), wsect(SHORT_CONV, HYENA_ORDER), wsect(1, HYENA_ORDER)]
        args = [p, conv_w, conv_b]
    else:
        in_specs = [pl.BlockSpec((None, n, tn), lambda c, bi: (bi, 0, c))]
        args = [z_arr]
    mats = [tabs['mfe'], tabs['mfo'], tabs['mie'], tabs['mio']]
    in_specs += [sect(order), wsect(SHORT_CONV, order), wsect(1, order)] + [whole(m) for m in mats]
    in_specs += [pl.BlockSpec((4, h, tn), functools.partial(lambda c, bi, o: (0, 0, o + c), o=order * nct),
                              pipeline_mode=once),
                 pl.BlockSpec((2, tn), ocol),
                 pl.BlockSpec((1, tn), ocol)]
    args += [p, conv_w, conv_b] + mats + [k4, mid, skip]
    return pl.pallas_call(
        functools.partial(_hyconv_kernel, conv_z=conv_z, fk=fk, n=n),
        grid=(nct, bsz),
        in_specs=in_specs,
        out_specs=pl.BlockSpec((None, n, tn), lambda c, bi: (bi, 0, c)),
        out_shape=jax.ShapeDtypeStruct((bsz, n, D_HYENA), BF16),
        compiler_params=_params("arbitrary", "arbitrary"),
        name="hyena_conv",
    )(*args)


def _fnet_kernel(u_ref, csc_ref, mne_ref, mno_ref, o_ref, t_ref, *, n, tr, scale):
    hh = n // 2
    qq = n // 4
    i = pl.program_id(1)

    @pl.when(i == 0)
    def _():
        for g in range(FNET_GROUPS):
            gsl = slice(g * FNET_GROUP_DIM, (g + 1) * FNET_GROUP_DIM)
            t = _dot(u_ref[:, gsl], csc_ref[...])
            tc = t[:, :FNET_GROUP_DIM].astype(BF16)
            ts = t[:, FNET_GROUP_DIM:].astype(BF16)
            t_ref[0:hh, gsl] = tc[0:hh]
            t_ref[hh:n, gsl] = ts[0:hh]
            t_ref[n:n + hh, gsl] = tc[hh:n]
            t_ref[n + hh:2 * n, gsl] = ts[hh:n]

    a = _dot(mne_ref[...], t_ref[0:n, :])
    b = _dot(mno_ref[...], t_ref[n:2 * n, :])
    steps = qq // tr
    start = (i // steps) * hh + (i % steps) * tr
    o_ref[pl.ds(pl.multiple_of(start, tr), tr), :] = ((a + b) * scale).astype(o_ref.dtype)
    o_ref[pl.ds(pl.multiple_of(start + qq, tr), tr), :] = ((a - b) * scale).astype(o_ref.dtype)


def _fnet(p, csc, mne, mno):
    bsz, n, _ = p.shape
    tr = min(n // 4, 256)
    scale = 1.0 / math.sqrt(n * FNET_GROUP_DIM)
    return pl.pallas_call(
        functools.partial(_fnet_kernel, n=n, tr=tr, scale=scale),
        grid=(bsz, (n // 2) // tr),
        in_specs=[pl.BlockSpec((None, n, D_MODEL), lambda bi, i: (bi, 0, P_FN // D_MODEL)),
                  pl.BlockSpec((FNET_GROUP_DIM, 2 * FNET_GROUP_DIM), lambda bi, i: (0, 0)),
                  pl.BlockSpec((tr, n), lambda bi, i: (i, 0)),
                  pl.BlockSpec((tr, n), lambda bi, i: (i, 0))],
        out_specs=pl.BlockSpec((None, n, D_MODEL), lambda bi, i: (bi, 0, 0)),
        out_shape=jax.ShapeDtypeStruct((bsz, n, D_MODEL), BF16),
        scratch_shapes=[pltpu.VMEM((2 * n, D_MODEL), BF16)],
        compiler_params=_params("arbitrary", "arbitrary"),
        name="fnet",
    )(p, csc, mne, mno)


def _merge_kernel(oa_ref, hy_ref, fn_ref, ga_ref, gh_ref, gf_ref, x_ref, mod_ref,
                  wa_ref, wh_ref, wf_ref, wo_ref, bo_ref, lg_ref, lb_ref, o_ref):
    m = _sigmoid(ga_ref[...].astype(F32)) * _dot(oa_ref[...], wa_ref[...])
    m = m + _sigmoid(gh_ref[...].astype(F32)) * _dot(hy_ref[...], wh_ref[...])
    m = m + _sigmoid(gf_ref[...].astype(F32)) * _dot(fn_ref[...], wf_ref[...])
    y = _dot(m.astype(BF16), wo_ref[...]) + bo_ref[...]
    r = ALPHA * x_ref[...] + mod_ref[2:3, :] * y
    o_ref[...] = _layer_norm(r, lg_ref[...], lb_ref[...])


def _merge(o_att, z_hy, y_fn, p, x, mod, wa, wh, wf, wo, bo, lg, lb):
    bsz, n, d = x.shape
    tm = min(n, 512)
    tok = lambda c: pl.BlockSpec((None, tm, d), functools.partial(lambda bi, i, c: (bi, i, c), c=c))
    wspec = pl.BlockSpec((d, d), lambda bi, i: (0, 0))
    vspec = pl.BlockSpec((1, d), lambda bi, i: (0, 0))
    g0 = P_G // d
    return pl.pallas_call(
        _merge_kernel,
        grid=(bsz, n // tm),
        in_specs=[tok(0), tok(0), tok(0), tok(g0), tok(g0 + 1), tok(g0 + 2), tok(0),
                  pl.BlockSpec((None, N_MOD, d), lambda bi, i: (bi, 0, 0)),
                  wspec, wspec, wspec, wspec, vspec, vspec, vspec],
        out_specs=tok(0),
        out_shape=jax.ShapeDtypeStruct((bsz, n, d), F32),
        compiler_params=_params("arbitrary", "arbitrary"),
        name="merge_ln",
    )(o_att, z_hy, y_fn, p, p, p, x, mod, wa, wh, wf, wo, bo, lg, lb)


def _mlp_kernel(x_ref, mod_ref, w1_ref, b1_ref, w2_ref, b2_ref, lg_ref, lb_ref, o_ref, h_ref, acc_ref):
    j = pl.program_id(2)

    @pl.when(j == 0)
    def _():
        h_ref[...] = (x_ref[...] * (1.0 + mod_ref[4:5, :]) + mod_ref[3:4, :]).astype(BF16)
        acc_ref[...] = jnp.zeros_like(acc_ref)

    a = jnp.maximum(_dot(h_ref[...], w1_ref[...]) + b1_ref[...], 0.0)
    acc_ref[...] += _dot((a * a).astype(BF16), w2_ref[...])

    @pl.when(j == pl.num_programs(2) - 1)
    def _():
        y = acc_ref[...] + b2_ref[...]
        r = ALPHA * x_ref[...] + mod_ref[5:6, :] * y
        o_ref[...] = _layer_norm(r, lg_ref[...], lb_ref[...])


def _mlp(x, mod, w1, b1, w2, b2, lg, lb):
    bsz, n, d = x.shape
    ff = w1.shape[1]
    tm = min(n, 1024)
    fc = 1024
    vspec = pl.BlockSpec((1, d), lambda bi, i, j: (0, 0))
    return pl.pallas_call(
        _mlp_kernel,
        grid=(bsz, n // tm, ff // fc),
        in_specs=[pl.BlockSpec((None, tm, d), lambda bi, i, j: (bi, i, 0)),
                  pl.BlockSpec((None, N_MOD, d), lambda bi, i, j: (bi, 0, 0)),
                  pl.BlockSpec((d, fc), lambda bi, i, j: (0, j)),
                  pl.BlockSpec((1, fc), lambda bi, i, j: (0, j)),
                  pl.BlockSpec((fc, d), lambda bi, i, j: (j, 0)),
                  vspec, vspec, vspec],
        out_specs=pl.BlockSpec((None, tm, d), lambda bi, i, j: (bi, i, 0)),
        out_shape=jax.ShapeDtypeStruct((bsz, n, d), F32),
        scratch_shapes=[pltpu.VMEM((tm, d), BF16), pltpu.VMEM((tm, d), F32)],
        compiler_params=_params("arbitrary", "arbitrary", "arbitrary"),
        name="mlp_ln",
    )(x, mod, w1, b1, w2, b2, lg, lb)


def _int_grid(rows, cols):
    k = lax.broadcasted_iota(jnp.int32, (rows, cols), 0)
    s = lax.broadcasted_iota(jnp.int32, (rows, cols), 1)
    return k, s


TRIG_ROW_BLOCK = 32


def _trig_table(n_rows, samp, period):
    def base(kv):
        ang = ((kv[:, None] * samp[None, :]) % period).astype(F32) * (2.0 * math.pi / period)
        return jnp.cos(ang), jnp.sin(ang)

    ch, sh = base(jnp.arange(n_rows // TRIG_ROW_BLOCK, dtype=jnp.int32) * TRIG_ROW_BLOCK)
    cl, sl = base(jnp.arange(TRIG_ROW_BLOCK, dtype=jnp.int32))
    ch, sh, cl, sl = ch[:, None, :], sh[:, None, :], cl[None], sl[None]
    shape = (n_rows, samp.shape[0])
    return (ch * cl - sh * sl).reshape(shape), (sh * cl + ch * sl).reshape(shape)


def _hyena_dft(n):
    h = n // 2
    s = jnp.arange(h, dtype=jnp.int32)
    alt = jnp.where(s % 2 == 0, 1.0, -1.0)[None, :]
    first = lax.broadcasted_iota(jnp.int32, (h, h), 0) == 0
    out = {}
    for name, samp in (('e', 2 * s), ('o', 2 * s + 1)):
        cos, sin = _trig_table(h, samp, 2 * n)
        mf = jnp.concatenate([cos, jnp.where(first, alt, sin)], 0)
        out['mf' + name] = mf.astype(BF16)
        out['mi' + name] = mf.T.astype(BF16)
    return out


def _fnet_dft(n):
    h = n // 2
    s = jnp.arange(h, dtype=jnp.int32)
    out = {}
    for name, samp in (('e', 2 * s), ('o', 2 * s + 1)):
        cos, sin = _trig_table(h, samp, n)
        out['mn' + name] = _even_odd_rows(jnp.concatenate([cos, -sin], 1)).astype(BF16)
    k, s = _int_grid(FNET_GROUP_DIM, FNET_GROUP_DIM)
    ang = ((k * s) % FNET_GROUP_DIM).astype(F32) * (2.0 * math.pi / FNET_GROUP_DIM)
    out['csc'] = jnp.concatenate([jnp.cos(ang), jnp.sin(ang)], 1).astype(BF16)
    return out


def _filter_embedding(n, width):
    t = jnp.linspace(0.0, 1.0, n, dtype=F32)[:, None]
    w = (2.0 * math.pi / n) * jnp.arange(n, dtype=F32)[:, None]
    f = jnp.linspace(1e-4, HYENA_BANDS - 1, HYENA_BANDS, dtype=F32)[None, :]
    emb = jnp.concatenate([t, jnp.cos(f * w), -jnp.sin(f * w)], -1)
    return _even_odd_rows(jnp.pad(emb, ((0, 0), (0, width - HYENA_EMB))))


def _rope_tables(n):
    rows = n // GRID_W
    row = jnp.repeat(jnp.arange(rows, dtype=F32), GRID_W)
    col = jnp.tile(jnp.arange(GRID_W, dtype=F32), rows)
    inv_freq = ROPE_THETA ** (-jnp.arange(ROPE_AXIS_DIM // 2, dtype=F32) * 2.0 / ROPE_AXIS_DIM)
    ar = row[:, None] * inv_freq[None, :]
    ac = col[:, None] * inv_freq[None, :]
    cos = jnp.concatenate([jnp.cos(ar), jnp.cos(ar), jnp.cos(ac), jnp.cos(ac)], -1)
    sin = jnp.concatenate([-jnp.sin(ar), jnp.sin(ar), -jnp.sin(ac), jnp.sin(ac)], -1)
    return _even_odd_rows(cos), _even_odd_rows(sin)


def _permute_in_cols(a):
    return jnp.concatenate([a[..., OFF_Q:OFF_K], a[..., OFF_G:D_IN], a[..., OFF_FN:OFF_G],
                            a[..., OFF_K:OFF_V], a[..., OFF_V:OFF_HY], a[..., OFF_HY:OFF_FN]], -1)


def _mixer(p_tok, seq_shape, lw, tabs, rope_tabs, ctx_parts):
    bsz, n = seq_shape
    p = p_tok.reshape(bsz, n, p_tok.shape[-1])
    q = _headnorm(p, P_Q // Q_W, N_Q_HEADS, lw['q_gain'], rope_tabs, ATTN_SCALE)
    k = _headnorm(p, P_K // KV_W, N_KV_HEADS, lw['k_gain'], rope_tabs, 1.0)
    o_att = _attention(q, [(k, p, P_V // KV_W)] + ctx_parts)

    hs, hd, mid = _filters(n, tabs['emb'], lw['hy_w1'], lw['hy_b1'], lw['hy_freq1'], lw['hy_w2'], lw['hy_b2'],
                           lw['hy_freq2'], lw['hy_w3'], tabs['absdelta'])
    k4 = _spectrum(tabs['mfe'], tabs['mfo'], hs, hd)
    z = None
    for o in range(HYENA_ORDER):
        z = _hyconv(o, z, p, lw['conv_w'], lw['conv_b'], tabs, k4, mid, lw['hy_skip'])
    y_fn = _fnet(p, tabs['csc'], tabs['mne'], tabs['mno'])
    tok = lambda a: a.reshape(p_tok.shape[0], p_tok.shape[1], a.shape[-1])
    return tok(o_att), tok(z), tok(y_fn), k, p


def _block(x_tok, mod, seq_shape, lw, tabs, rope_tabs, ctx_parts):
    p_tok = _inproj(x_tok, mod, lw['w_in'], lw['b_in'])
    o_att, z_hy, y_fn, k, p = _mixer(p_tok, seq_shape, lw, tabs, rope_tabs, ctx_parts)
    x_tok = _merge(o_att, z_hy, y_fn, p_tok, x_tok, mod, lw['w_att_o'], lw['w_hy_o'], lw['w_fn_o'],
                   lw['w_out'], lw['b_out'], lw['ln1_g'], lw['ln1_b'])
    x_tok = _mlp(x_tok, mod, lw['w_mlp1'], lw['b_mlp1'], lw['w_mlp2'], lw['b_mlp2'], lw['ln2_g'], lw['ln2_b'])
    return x_tok, k, p


def _seq_tables(n):
    tabs = dict(_hyena_dft(n))
    tabs.update(_fnet_dft(n))
    deltas = jnp.abs(jnp.linspace(MIN_DECAY, MAX_DECAY, D_HYENA, dtype=F32))
    tabs.update(emb=_filter_embedding(n, 128), absdelta=jnp.tile(deltas, HYENA_ORDER)[None, :])
    return tabs


def kernel(x, c, ctx, c_ctx, w_ada, b_ada, w_in, b_in, conv_w, conv_b, hy_w1, hy_b1, hy_freq1, hy_w2, hy_b2,
           hy_freq2, hy_w3, hy_skip, q_gain, k_gain, w_att_o, w_hy_o, w_fn_o, w_out, b_out, ln1_g, ln1_b,
           w_mlp1, b_mlp1, w_mlp2, b_mlp2, ln2_g, ln2_b):
    bsz, n_lat, d = x.shape
    n_ctx = ctx.shape[1]
    depth = w_ada.shape[0]
    tabs_lat = _seq_tables(n_lat)
    tabs_ctx = _seq_tables(n_ctx)
    rope_tabs = _rope_tables(n_lat)

    n_rows = -(-(bsz + 1) // 16) * 16
    cond = jnp.concatenate([c, c_ctx[None, :], jnp.zeros((n_rows - bsz - 1, d), F32)], 0)

    x = _deinterleave(x)
    ctx_tok = _deinterleave(ctx).reshape(1, bsz * n_ctx, d)
    row = lambda a: a[None, :]
    for i in range(depth):
        last = i == depth - 1
        lw = dict(
            w_in=_permute_in_cols(w_in[i]).astype(BF16), b_in=row(_permute_in_cols(b_in[i])),
            conv_w=conv_w[i], conv_b=row(conv_b[i]),
            hy_w1=jnp.pad(hy_w1[i], ((0, 128 - HYENA_EMB), (0, 0))), hy_b1=row(hy_b1[i]), hy_freq1=row(hy_freq1[i]),
            hy_w2=hy_w2[i], hy_b2=row(hy_b2[i]), hy_freq2=row(hy_freq2[i]), hy_w3=hy_w3[i],
            hy_skip=hy_skip[i].reshape(1, HYENA_ORDER * D_HYENA),
            q_gain=row(q_gain[i]), k_gain=row(k_gain[i]),
            w_att_o=w_att_o[i].astype(BF16), w_hy_o=w_hy_o[i].astype(BF16), w_fn_o=w_fn_o[i].astype(BF16),
            w_out=w_out[i].astype(BF16), b_out=row(b_out[i]), ln1_g=row(ln1_g[i]), ln1_b=row(ln1_b[i]),
            w_mlp1=w_mlp1[i].astype(BF16), b_mlp1=row(b_mlp1[i]), w_mlp2=w_mlp2[i].astype(BF16),
            b_mlp2=row(b_mlp2[i]), ln2_g=row(ln2_g[i]), ln2_b=row(ln2_b[i]))
        mod = _matmul(cond, w_ada[i], row(b_ada[i]), silu=True)
        mod_l = mod[:bsz].reshape(bsz, N_MOD, d)
        mod_c = mod[bsz:bsz + 1].reshape(1, N_MOD, d)

        if last:
            kv_cols = slice(P_K, P_V + KV_W)
            p_c = _inproj(ctx_tok, mod_c, lw['w_in'][:, kv_cols], lw['b_in'][:, kv_cols])
            p_c = p_c.reshape(bsz, n_ctx, 2 * KV_W)
            k_c = _headnorm(p_c, 0, N_KV_HEADS, lw['k_gain'], None, 1.0)
            v_blk = 1
        else:
            ctx_tok, k_c, p_c = _block(ctx_tok, mod_c, (bsz, n_ctx), lw, tabs_ctx, None, [])
            v_blk = P_V // KV_W
        x, _, _ = _block(x, mod_l, (bsz, n_lat), lw, tabs_lat, rope_tabs, [(k_c, p_c, v_blk)])
    return _interleave(x)
```

```python
import functools
import math

import jax
import jax.numpy as jnp
from jax import lax
from jax.experimental import pallas as pl
from jax.experimental.pallas import tpu as pltpu

F32 = jnp.float32
BF16 = jnp.bfloat16

D_MODEL = 1024
GRID_W = 64
HEAD_DIM = 128
N_Q_HEADS = D_MODEL // HEAD_DIM
N_KV_HEADS = 2
GQA_GROUP = N_Q_HEADS // N_KV_HEADS
ROPE_THETA = 10000.0
ROPE_AXIS_DIM = HEAD_DIM // 2
ATTN_SCALE = HEAD_DIM ** -0.5

D_HYENA = D_MODEL
HYENA_ORDER = 2
HYENA_BANDS = 16
HYENA_EMB = 1 + 2 * HYENA_BANDS
SHORT_CONV = 3
DECAY_TARGET = 1e-2
MIN_DECAY = math.log(DECAY_TARGET) / 0.3
MAX_DECAY = math.log(DECAY_TARGET) / 1.5

FNET_GROUPS = 4
FNET_GROUP_DIM = D_MODEL // FNET_GROUPS
D_FF = 4 * D_MODEL
N_MOD = 6
DEPTH = 4

Q_W = N_Q_HEADS * HEAD_DIM
KV_W = N_KV_HEADS * HEAD_DIM
HY_W = (HYENA_ORDER + 1) * D_HYENA
OFF_Q = 0
OFF_K = OFF_Q + Q_W
OFF_V = OFF_K + KV_W
OFF_HY = OFF_V + KV_W
OFF_FN = OFF_HY + HY_W
OFF_G = OFF_FN + D_MODEL
D_IN = OFF_G + 3 * D_MODEL

P_Q = 0
P_G = P_Q + Q_W
P_FN = P_G + 3 * D_MODEL
P_K = P_FN + D_MODEL
P_V = P_K + KV_W
P_HY = P_V + KV_W

ALPHA = (2 * DEPTH) ** 0.25
LN_EPS = 1e-6
RMS_EPS = 1e-6
KERN_EPS = 1e-6

V7X_VMEM_LIMIT_BYTES = 56 * 1024 * 1024


def _params(*sem):
    return pltpu.CompilerParams(dimension_semantics=sem, vmem_limit_bytes=V7X_VMEM_LIMIT_BYTES)


def _dot(a, b):
    return jnp.dot(a, b, preferred_element_type=F32)


def _layer_norm(r, g, b):
    mu = jnp.mean(r, -1, keepdims=True)
    d = r - mu
    var = jnp.mean(d * d, -1, keepdims=True)
    return d * lax.rsqrt(var + LN_EPS) * g + b


def _sigmoid(x):
    return 1.0 / (1.0 + jnp.exp(-x))


def _deinterleave_kernel(x_ref, o_ref):
    hh = x_ref.shape[0] // 2
    o_ref[0:hh, :] = x_ref[pl.ds(0, hh, stride=2), :]
    o_ref[hh:2 * hh, :] = x_ref[pl.ds(1, hh, stride=2), :]


def _interleave_kernel(x_ref, o_ref):
    hh = x_ref.shape[0] // 2
    o_ref[pl.ds(0, hh, stride=2), :] = x_ref[0:hh, :]
    o_ref[pl.ds(1, hh, stride=2), :] = x_ref[hh:2 * hh, :]


def _reorder_tokens(x, body, name):
    bsz, n, d = x.shape
    tc = 128
    spec = pl.BlockSpec((None, n, tc), lambda bi, c: (bi, 0, c))
    return pl.pallas_call(
        body,
        grid=(bsz, d // tc),
        in_specs=[spec],
        out_specs=spec,
        out_shape=jax.ShapeDtypeStruct(x.shape, x.dtype),
        compiler_params=_params("arbitrary", "arbitrary"),
        name=name,
    )(x)


def _deinterleave(x):
    return _reorder_tokens(x, _deinterleave_kernel, "deinterleave")


def _interleave(x):
    return _reorder_tokens(x, _interleave_kernel, "interleave")


def _even_odd_rows(a):
    return jnp.concatenate([a[0::2], a[1::2]], 0)


def _matmul_kernel(a_ref, b_ref, bias_ref, o_ref, *, silu):
    a = a_ref[...]
    if silu:
        a = a * _sigmoid(a)
    o_ref[...] = _dot(a.astype(BF16), b_ref[...].astype(BF16)) + bias_ref[...]


def _matmul(a, b, bias, *, silu=False, tn=1024):
    m, k = a.shape
    n = b.shape[1]
    return pl.pallas_call(
        functools.partial(_matmul_kernel, silu=silu),
        grid=(n // tn,),
        in_specs=[pl.BlockSpec((m, k), lambda j: (0, 0)),
                  pl.BlockSpec((k, tn), lambda j: (0, j)),
                  pl.BlockSpec((1, tn), lambda j: (0, j))],
        out_specs=pl.BlockSpec((m, tn), lambda j: (0, j)),
        out_shape=jax.ShapeDtypeStruct((m, n), F32),
        compiler_params=_params("arbitrary"),
        name="ada_matmul",
    )(a, b, bias)


def _inproj_kernel(x_ref, mod_ref, w_ref, b_ref, o_ref, h_ref):
    @pl.when(pl.program_id(2) == 0)
    def _():
        h_ref[...] = (x_ref[...] * (1.0 + mod_ref[1:2, :]) + mod_ref[0:1, :]).astype(BF16)

    o_ref[...] = (_dot(h_ref[...], w_ref[...]) + b_ref[...]).astype(o_ref.dtype)


INPROJ_MAX_COLS = 2176


def _inproj(x, mod, w, b):
    bsz, n, d = x.shape
    nout = w.shape[1]
    tm = min(n, 1024)
    tn = max(t for t in range(128, min(nout, INPROJ_MAX_COLS) + 1, 128) if nout % t == 0)
    return pl.pallas_call(
        _inproj_kernel,
        grid=(bsz, n // tm, nout // tn),
        in_specs=[pl.BlockSpec((None, tm, d), lambda bi, i, j: (bi, i, 0)),
                  pl.BlockSpec((None, N_MOD, d), lambda bi, i, j: (bi, 0, 0)),
                  pl.BlockSpec((d, tn), lambda bi, i, j: (0, j)),
                  pl.BlockSpec((1, tn), lambda bi, i, j: (0, j))],
        out_specs=pl.BlockSpec((None, tm, tn), lambda bi, i, j: (bi, i, j)),
        out_shape=jax.ShapeDtypeStruct((bsz, n, nout), BF16),
        scratch_shapes=[pltpu.VMEM((tm, d), BF16)],
        compiler_params=_params("arbitrary", "arbitrary", "arbitrary"),
        name="inproj",
    )(x, mod, w, b)


def _headnorm_kernel(*refs, n_heads, rope, scale):
    if rope:
        p_ref, g_ref, cos_ref, sin_ref, o_ref = refs
    else:
        p_ref, g_ref, o_ref = refs
    gain = g_ref[...]
    for h in range(n_heads):
        sl = slice(h * HEAD_DIM, (h + 1) * HEAD_DIM)
        xf = p_ref[:, sl].astype(F32)
        y = xf * lax.rsqrt(jnp.mean(xf * xf, -1, keepdims=True) + RMS_EPS) * gain
        if rope:
            lane = lax.broadcasted_iota(jnp.int32, y.shape, 1)
            first = (lane % (ROPE_AXIS_DIM)) < (ROPE_AXIS_DIM // 2)
            swapped = jnp.where(first, pltpu.roll(y, HEAD_DIM - ROPE_AXIS_DIM // 2, 1),
                                pltpu.roll(y, ROPE_AXIS_DIM // 2, 1))
            y = y * cos_ref[...] + swapped * sin_ref[...]
        if scale != 1.0:
            y = y * scale
        o_ref[:, sl] = y.astype(o_ref.dtype)


def _headnorm(p, col_block, n_heads, gain, rope_tabs, scale):
    bsz, n, _ = p.shape
    w = n_heads * HEAD_DIM
    tm = min(n, 512)
    rope = rope_tabs is not None
    in_specs = [pl.BlockSpec((None, tm, w), lambda bi, i: (bi, i, col_block)),
                pl.BlockSpec((1, HEAD_DIM), lambda bi, i: (0, 0))]
    args = [p, gain]
    if rope:
        in_specs += [pl.BlockSpec((tm, HEAD_DIM), lambda bi, i: (i, 0))] * 2
        args += list(rope_tabs)
    return pl.pallas_call(
        functools.partial(_headnorm_kernel, n_heads=n_heads, rope=rope, scale=scale),
        grid=(bsz, n // tm),
        in_specs=in_specs,
        out_specs=pl.BlockSpec((None, tm, w), lambda bi, i: (bi, i, 0)),
        out_shape=jax.ShapeDtypeStruct((bsz, n, w), BF16),
        compiler_params=_params("arbitrary", "arbitrary"),
        name="headnorm",
    )(*args)


def _attn_kernel(*refs, n_parts):
    q_ref = refs[0]
    kv_refs = refs[1:1 + 2 * n_parts]
    o_ref = refs[1 + 2 * n_parts]
    for g in range(N_KV_HEADS):
        gsl = slice(g * HEAD_DIM, (g + 1) * HEAD_DIM)
        ks = [kv_refs[2 * i][:, gsl] for i in range(n_parts)]
        vs = [jnp.concatenate([kv_refs[2 * i + 1][:, gsl], jnp.ones((k.shape[0], HEAD_DIM), BF16)], 1)
              for i, k in enumerate(ks)]
        for r in range(GQA_GROUP):
            h = g * GQA_GROUP + r
            hsl = slice(h * HEAD_DIM, (h + 1) * HEAD_DIM)
            q = q_ref[:, hsl]
            ss = [lax.dot_general(q, k, (((1,), (1,)), ((), ())), preferred_element_type=F32) for k in ks]
            m = jnp.max(ss[0], -1, keepdims=True)
            for s in ss[1:]:
                m = jnp.maximum(m, jnp.max(s, -1, keepdims=True))
            o = None
            for s, v in zip(ss, vs):
                pv = _dot(jnp.exp((s - m).astype(BF16)), v)
                o = pv if o is None else o + pv
            o_ref[:, hsl] = (o[:, :HEAD_DIM] / o[:, HEAD_DIM:]).astype(o_ref.dtype)


def _attention(q, parts):
    bsz, n, w = q.shape
    tq = min(n, 256)
    in_specs = [pl.BlockSpec((None, tq, w), lambda bi, i: (bi, i, 0))]
    args = [q]
    for k, v, vblk in parts:
        nk = k.shape[1]
        in_specs.append(pl.BlockSpec((None, nk, KV_W), lambda bi, i: (bi, 0, 0)))
        in_specs.append(pl.BlockSpec((None, nk, KV_W), functools.partial(lambda bi, i, c: (bi, 0, c), c=vblk)))
        args += [k, v]
    return pl.pallas_call(
        functools.partial(_attn_kernel, n_parts=len(parts)),
        grid=(bsz, n // tq),
        in_specs=in_specs,
        out_specs=pl.BlockSpec((None, tq, w), lambda bi, i: (bi, i, 0)),
        out_shape=jax.ShapeDtypeStruct((bsz, n, w), BF16),
        compiler_params=_params("arbitrary", "arbitrary"),
        name="attention",
    )(*args)


def _filter_kernel(emb_ref, w1_ref, b1_ref, f1_ref, w2_ref, b2_ref, f2_ref, w3f_ref, w3b_ref, dl_ref,
                   hs_ref, hd_ref, mid_ref, *, n):
    hh = n // 2
    h = jnp.sin(f1_ref[...] * (_dot(emb_ref[...], w1_ref[...]) + b1_ref[...]))
    h = jnp.sin(f2_ref[...] * (_dot(h, w2_ref[...]) + b2_ref[...]))
    hf = _dot(h, w3f_ref[...])
    hb = _dot(h, w3b_ref[...])
    row = lax.broadcasted_iota(jnp.int32, hf.shape, 0)
    lag = jnp.where(row < hh, 2 * row, 2 * (row - hh) + 1)
    t = lag.astype(F32) * (1.0 / (n - 1))
    dec = jnp.exp(-t * dl_ref[...])
    hf = hf * dec
    hb = jnp.where(lag == 0, 0.0, hb * dec)
    ssq = jnp.sum(hf * hf, 0, keepdims=True) + jnp.sum(hb * hb, 0, keepdims=True)
    nrm = lax.rsqrt(ssq + KERN_EPS) * (1.0 / n)
    hs = (hf + hb) * nrm
    hd = (hf - hb) * nrm
    hs_ref[...] = hs.astype(hs_ref.dtype)
    hd_ref[...] = hd.astype(hd_ref.dtype)
    sign = jnp.where(row % 2 == 0, 1.0, -1.0)
    mid_ref[0:1, :] = jnp.sum(jnp.where(row < hh, hs * sign, 0.0), 0, keepdims=True)
    mid_ref[1:2, :] = jnp.sum(jnp.where(row < hh, 0.0, hd * sign), 0, keepdims=True)


def _filters(n, emb, w1, b1, f1, w2, b2, f2, w3, absdelta):
    od = HYENA_ORDER * D_HYENA
    tn = 512
    fh = w2.shape[0]
    ke = emb.shape[1]
    full = lambda shape: pl.BlockSpec(shape, lambda j: (0, 0))
    return pl.pallas_call(
        functools.partial(_filter_kernel, n=n),
        grid=(od // tn,),
        in_specs=[full((n, ke)), full((ke, fh)), full((1, fh)), full((1, fh)),
                  full((fh, fh)), full((1, fh)), full((1, fh)),
                  pl.BlockSpec((fh, tn), lambda j: (0, j)),
                  pl.BlockSpec((fh, tn), lambda j: (0, j + od // tn)),
                  pl.BlockSpec((1, tn), lambda j: (0, j))],
        out_specs=[pl.BlockSpec((n, tn), lambda j: (0, j)),
                   pl.BlockSpec((n, tn), lambda j: (0, j)),
                   pl.BlockSpec((2, tn), lambda j: (0, j))],
        out_shape=[jax.ShapeDtypeStruct((n, od), BF16), jax.ShapeDtypeStruct((n, od), BF16),
                   jax.ShapeDtypeStruct((2, od), F32)],
        compiler_params=_params("arbitrary"),
        name="hyena_filters",
    )(emb, w1, b1, f1, w2, b2, f2, w3, w3, absdelta)


def _spectrum_kernel(fe_re, fe_im, fo_re, fo_im, hse, hso, hde, hdo, k4_ref):
    p = _dot(fe_re[...], hse[...])
    q = _dot(fo_re[...], hso[...])
    pp = _dot(fe_im[...], hde[...])
    qp = _dot(fo_im[...], hdo[...])
    row0 = (lax.broadcasted_iota(jnp.int32, p.shape, 0) + pl.program_id(0) * p.shape[0]) == 0
    k4_ref[0] = jnp.where(row0, 0.5 * (p + q), p + q)
    k4_ref[1] = jnp.where(row0, 0.0, pp + qp)
    k4_ref[2] = jnp.where(row0, 0.5 * (p - q), p - q)
    k4_ref[3] = jnp.where(row0, 0.0, qp - pp)


def _spectrum(mfe, mfo, hs, hd):
    n, od = hs.shape
    h = n // 2
    tm = min(h, 512)
    tn = 512
    nb = h // tm
    mat = lambda off: pl.BlockSpec((tm, h), functools.partial(lambda i, j, o: (i + o, 0), o=off))
    half = lambda a: pl.BlockSpec((h, tn), functools.partial(lambda i, j, a: (a, j), a=a))
    return pl.pallas_call(
        _spectrum_kernel,
        grid=(nb, od // tn),
        in_specs=[mat(0), mat(nb), mat(0), mat(nb), half(0), half(1), half(0), half(1)],
        out_specs=pl.BlockSpec((4, tm, tn), lambda i, j: (0, i, j)),
        out_shape=jax.ShapeDtypeStruct((4, h, od), F32),
        compiler_params=_params("arbitrary", "arbitrary"),
        name="hyena_spectrum",
    )(mfe, mfe, mfo, mfo, hs, hs, hd, hd)


def _short_conv(x, w_ref, b_ref):
    hh = x.shape[0] // 2
    e = x[:hh]
    o = x[hh:]
    row = lax.broadcasted_iota(jnp.int32, e.shape, 0)
    o_prev = jnp.where(row == 0, 0.0, pltpu.roll(o, 1, 0))
    e_next = jnp.where(row == hh - 1, 0.0, pltpu.roll(e, hh - 1, 0))
    w0, w1, w2 = w_ref[0:1, :], w_ref[1:2, :], w_ref[2:3, :]
    return (b_ref[...] + o_prev * w0 + e * w1 + o * w2,
            b_ref[...] + e * w0 + o * w1 + e_next * w2)


def _hyconv_kernel(*refs, conv_z, cw, n):
    if conv_z:
        z_ref, cwz_ref, cbz_ref = refs[:3]
        refs = refs[3:]
    else:
        z_ref = refs[0]
        refs = refs[1:]
    x_ref, cwx_ref, cbx_ref, mfe_ref, mfo_ref, mie_ref, mio_ref, k4_ref, mid_ref, skip_ref, o_ref = refs
    hh = n // 2
    first8 = lax.broadcasted_iota(jnp.int32, (8, cw), 0) == 0
    for c in range(o_ref.shape[1] // cw):
        cs = slice(c * cw, (c + 1) * cw)
        if conv_z:
            ze, zo = _short_conv(z_ref[:, cs].astype(F32), cwz_ref.at[:, cs], cbz_ref.at[:, cs])
            ze, zo = ze.astype(BF16), zo.astype(BF16)
        else:
            ze, zo = z_ref[0:hh, cs], z_ref[hh:n, cs]
        fe = _dot(mfe_ref[...], ze)
        fo = _dot(mfo_ref[...], zo)
        p, pp, q, qp = fe[:hh], fe[hh:], fo[:hh], fo[hh:]
        kra, kia, krb, kib = k4_ref[0, :, cs], k4_ref[1, :, cs], k4_ref[2, :, cs], k4_ref[3, :, cs]
        fra, fia = p + q, pp + qp
        frb, fib = p - q, qp - pp
        wra = fra * kra - fia * kia
        wia = fra * kia + fia * kra
        wrb = frb * krb - fib * kib
        wib = frb * kib + fib * krb
        ge_im = wia - wib
        go_im = wia + wib
        kmr, kmi = mid_ref[0:1, cs], mid_ref[1:2, cs]
        pp8, qp8 = pp[:8], qp[:8]
        ge_im = jnp.concatenate([jnp.where(first8, pp8 * kmr - qp8 * kmi, ge_im[:8]), ge_im[8:]], 0)
        go_im = jnp.concatenate([jnp.where(first8, pp8 * kmi + qp8 * kmr, go_im[:8]), go_im[8:]], 0)
        ge = jnp.concatenate([(wra + wrb).astype(BF16), ge_im.astype(BF16)], 0)
        go = jnp.concatenate([(wra - wrb).astype(BF16), go_im.astype(BF16)], 0)
        ye = _dot(mie_ref[...], ge)
        yo = _dot(mio_ref[...], go)
        xe, xo = _short_conv(x_ref[:, cs].astype(F32), cwx_ref.at[:, cs], cbx_ref.at[:, cs])
        skip = skip_ref[:, cs]
        o_ref[0:hh, cs] = (xe * (ye + ze.astype(F32) * skip)).astype(o_ref.dtype)
        o_ref[hh:n, cs] = (xo * (yo + zo.astype(F32) * skip)).astype(o_ref.dtype)


def _hyconv(order, z_arr, p, conv_w, conv_b, tabs, k4, mid, skip):
    bsz, n, _ = p.shape
    h = n // 2
    tn = 512
    cw = 256
    nct = D_HYENA // tn
    conv_z = z_arr is None
    once = pl.Buffered(1)

    def sect(part):
        off = P_HY // tn + part * nct
        return pl.BlockSpec((None, n, tn), functools.partial(lambda c, bi, o: (bi, 0, o + c), o=off))

    def wsect(rows, part):
        return pl.BlockSpec((rows, tn), functools.partial(lambda c, bi, o: (0, o + c), o=part * nct))

    whole = lambda a: pl.BlockSpec(a.shape, lambda c, bi: (0, 0), pipeline_mode=once)
    ocol = functools.partial(lambda c, bi, o: (0, o + c), o=order * nct)
    if conv_z:
        in_specs = [sect(HYENA_ORDER), wsect(SHORT_CONV, HYENA_ORDER), wsect(1, HYENA_ORDER)]
        args = [p, conv_w, conv_b]
    else:
        in_specs = [pl.BlockSpec((None, n, tn), lambda c, bi: (bi, 0, c))]
        args = [z_arr]
    mats = [tabs['mfe'], tabs['mfo'], tabs['mie'], tabs['mio']]
    in_specs += [sect(order), wsect(SHORT_CONV, order), wsect(1, order)] + [whole(m) for m in mats]
    in_specs += [pl.BlockSpec((4, h, tn), functools.partial(lambda c, bi, o: (0, 0, o + c), o=order * nct),
                              pipeline_mode=once),
                 pl.BlockSpec((2, tn), ocol),
                 pl.BlockSpec((1, tn), ocol)]
    args += [p, conv_w, conv_b] + mats + [k4, mid, skip]
    return pl.pallas_call(
        functools.partial(_hyconv_kernel, conv_z=conv_z, cw=cw, n=n),
        grid=(nct, bsz),
        in_specs=in_specs,
        out_specs=pl.BlockSpec((None, n, tn), lambda c, bi: (bi, 0, c)),
        out_shape=jax.ShapeDtypeStruct((bsz, n, D_HYENA), BF16),
        compiler_params=_params("arbitrary", "arbitrary"),
        name="hyena_conv",
    )(*args)


def _fnet_kernel(u_ref, csc_ref, mne_ref, mno_ref, o_ref, *, n, scale):
    hh = n // 2
    qq = n // 4
    for g in range(FNET_GROUPS):
        gsl = slice(g * FNET_GROUP_DIM, (g + 1) * FNET_GROUP_DIM)
        t = _dot(u_ref[:, gsl], csc_ref[...])
        tc = t[:, :FNET_GROUP_DIM].astype(BF16)
        ts = t[:, FNET_GROUP_DIM:].astype(BF16)
        a = _dot(mne_ref[...], jnp.concatenate([tc[:hh], ts[:hh]], 0))
        b = _dot(mno_ref[...], jnp.concatenate([tc[hh:], ts[hh:]], 0))
        lo = ((a + b) * scale).astype(o_ref.dtype)
        hi = ((a - b) * scale).astype(o_ref.dtype)
        o_ref[0:qq, gsl] = lo[:qq]
        o_ref[qq:hh, gsl] = hi[:qq]
        o_ref[hh:hh + qq, gsl] = lo[qq:]
        o_ref[hh + qq:n, gsl] = hi[qq:]


def _fnet(p, csc, mne, mno):
    bsz, n, _ = p.shape
    scale = 1.0 / math.sqrt(n * FNET_GROUP_DIM)
    whole = lambda a: pl.BlockSpec(a.shape, lambda bi: (0, 0), pipeline_mode=pl.Buffered(1))
    return pl.pallas_call(
        functools.partial(_fnet_kernel, n=n, scale=scale),
        grid=(bsz,),
        in_specs=[pl.BlockSpec((None, n, D_MODEL), lambda bi: (bi, 0, P_FN // D_MODEL)),
                  whole(csc), whole(mne), whole(mno)],
        out_specs=pl.BlockSpec((None, n, D_MODEL), lambda bi: (bi, 0, 0)),
        out_shape=jax.ShapeDtypeStruct((bsz, n, D_MODEL), BF16),
        compiler_params=_params("arbitrary"),
        name="fnet",
    )(p, csc, mne, mno)


def _merge_kernel(oa_ref, hy_ref, fn_ref, ga_ref, gh_ref, gf_ref, x_ref, mod_ref,
                  wa_ref, wh_ref, wf_ref, wo_ref, bo_ref, lg_ref, lb_ref, o_ref):
    m = _sigmoid(ga_ref[...].astype(F32)) * _dot(oa_ref[...], wa_ref[...])
    m = m + _sigmoid(gh_ref[...].astype(F32)) * _dot(hy_ref[...], wh_ref[...])
    m = m + _sigmoid(gf_ref[...].astype(F32)) * _dot(fn_ref[...], wf_ref[...])
    y = _dot(m.astype(BF16), wo_ref[...]) + bo_ref[...]
    r = ALPHA * x_ref[...] + mod_ref[2:3, :] * y
    o_ref[...] = _layer_norm(r, lg_ref[...], lb_ref[...])


def _merge(o_att, z_hy, y_fn, p, x, mod, wa, wh, wf, wo, bo, lg, lb):
    bsz, n, d = x.shape
    tm = min(n, 512)
    tok = lambda c: pl.BlockSpec((None, tm, d), functools.partial(lambda bi, i, c: (bi, i, c), c=c))
    wspec = pl.BlockSpec((d, d), lambda bi, i: (0, 0))
    vspec = pl.BlockSpec((1, d), lambda bi, i: (0, 0))
    g0 = P_G // d
    return pl.pallas_call(
        _merge_kernel,
        grid=(bsz, n // tm),
        in_specs=[tok(0), tok(0), tok(0), tok(g0), tok(g0 + 1), tok(g0 + 2), tok(0),
                  pl.BlockSpec((None, N_MOD, d), lambda bi, i: (bi, 0, 0)),
                  wspec, wspec, wspec, wspec, vspec, vspec, vspec],
        out_specs=tok(0),
        out_shape=jax.ShapeDtypeStruct((bsz, n, d), F32),
        compiler_params=_params("arbitrary", "arbitrary"),
        name="merge_ln",
    )(o_att, z_hy, y_fn, p, p, p, x, mod, wa, wh, wf, wo, bo, lg, lb)


def _mlp_kernel(x_ref, mod_ref, w1_ref, b1_ref, w2_ref, b2_ref, lg_ref, lb_ref, o_ref, h_ref, acc_ref):
    j = pl.program_id(2)

    @pl.when(j == 0)
    def _():
        h_ref[...] = (x_ref[...] * (1.0 + mod_ref[4:5, :]) + mod_ref[3:4, :]).astype(BF16)
        acc_ref[...] = jnp.zeros_like(acc_ref)

    a = jnp.maximum(_dot(h_ref[...], w1_ref[...]) + b1_ref[...], 0.0)
    acc_ref[...] += _dot((a * a).astype(BF16), w2_ref[...])

    @pl.when(j == pl.num_programs(2) - 1)
    def _():
        y = acc_ref[...] + b2_ref[...]
        r = ALPHA * x_ref[...] + mod_ref[5:6, :] * y
        o_ref[...] = _layer_norm(r, lg_ref[...], lb_ref[...])


def _mlp(x, mod, w1, b1, w2, b2, lg, lb):
    bsz, n, d = x.shape
    ff = w1.shape[1]
    tm = min(n, 1024)
    fc = 1024
    vspec = pl.BlockSpec((1, d), lambda bi, i, j: (0, 0))
    return pl.pallas_call(
        _mlp_kernel,
        grid=(bsz, n // tm, ff // fc),
        in_specs=[pl.BlockSpec((None, tm, d), lambda bi, i, j: (bi, i, 0)),
                  pl.BlockSpec((None, N_MOD, d), lambda bi, i, j: (bi, 0, 0)),
                  pl.BlockSpec((d, fc), lambda bi, i, j: (0, j)),
                  pl.BlockSpec((1, fc), lambda bi, i, j: (0, j)),
                  pl.BlockSpec((fc, d), lambda bi, i, j: (j, 0)),
                  vspec, vspec, vspec],
        out_specs=pl.BlockSpec((None, tm, d), lambda bi, i, j: (bi, i, 0)),
        out_shape=jax.ShapeDtypeStruct((bsz, n, d), F32),
        scratch_shapes=[pltpu.VMEM((tm, d), BF16), pltpu.VMEM((tm, d), F32)],
        compiler_params=_params("arbitrary", "arbitrary", "arbitrary"),
        name="mlp_ln",
    )(x, mod, w1, b1, w2, b2, lg, lb)


def _int_grid(rows, cols):
    k = lax.broadcasted_iota(jnp.int32, (rows, cols), 0)
    s = lax.broadcasted_iota(jnp.int32, (rows, cols), 1)
    return k, s


TRIG_ROW_BLOCK = 32


def _trig_table(n_rows, samp, period):
    def base(kv):
        ang = ((kv[:, None] * samp[None, :]) % period).astype(F32) * (2.0 * math.pi / period)
        return jnp.cos(ang), jnp.sin(ang)

    ch, sh = base(jnp.arange(n_rows // TRIG_ROW_BLOCK, dtype=jnp.int32) * TRIG_ROW_BLOCK)
    cl, sl = base(jnp.arange(TRIG_ROW_BLOCK, dtype=jnp.int32))
    ch, sh, cl, sl = ch[:, None, :], sh[:, None, :], cl[None], sl[None]
    shape = (n_rows, samp.shape[0])
    return (ch * cl - sh * sl).reshape(shape), (sh * cl + ch * sl).reshape(shape)


def _hyena_dft(n):
    h = n // 2
    s = jnp.arange(h, dtype=jnp.int32)
    alt = jnp.where(s % 2 == 0, 1.0, -1.0)[None, :]
    first = lax.broadcasted_iota(jnp.int32, (h, h), 0) == 0
    out = {}
    for name, samp in (('e', 2 * s), ('o', 2 * s + 1)):
        cos, sin = _trig_table(h, samp, 2 * n)
        mf = jnp.concatenate([cos, jnp.where(first, alt, sin)], 0)
        out['mf' + name] = mf.astype(BF16)
        out['mi' + name] = mf.T.astype(BF16)
    return out


def _fnet_dft(n):
    h = n // 2
    s = jnp.arange(h, dtype=jnp.int32)
    out = {}
    for name, samp in (('e', 2 * s), ('o', 2 * s + 1)):
        cos, sin = _trig_table(h, samp, n)
        out['mn' + name] = _even_odd_rows(jnp.concatenate([cos, -sin], 1)).astype(BF16)
    k, s = _int_grid(FNET_GROUP_DIM, FNET_GROUP_DIM)
    ang = ((k * s) % FNET_GROUP_DIM).astype(F32) * (2.0 * math.pi / FNET_GROUP_DIM)
    out['csc'] = jnp.concatenate([jnp.cos(ang), jnp.sin(ang)], 1).astype(BF16)
    return out


def _filter_embedding(n, width):
    t = jnp.linspace(0.0, 1.0, n, dtype=F32)[:, None]
    w = (2.0 * math.pi / n) * jnp.arange(n, dtype=F32)[:, None]
    f = jnp.linspace(1e-4, HYENA_BANDS - 1, HYENA_BANDS, dtype=F32)[None, :]
    emb = jnp.concatenate([t, jnp.cos(f * w), -jnp.sin(f * w)], -1)
    return _even_odd_rows(jnp.pad(emb, ((0, 0), (0, width - HYENA_EMB))))


def _rope_tables(n):
    rows = n // GRID_W
    row = jnp.repeat(jnp.arange(rows, dtype=F32), GRID_W)
    col = jnp.tile(jnp.arange(GRID_W, dtype=F32), rows)
    inv_freq = ROPE_THETA ** (-jnp.arange(ROPE_AXIS_DIM // 2, dtype=F32) * 2.0 / ROPE_AXIS_DIM)
    ar = row[:, None] * inv_freq[None, :]
    ac = col[:, None] * inv_freq[None, :]
    cos = jnp.concatenate([jnp.cos(ar), jnp.cos(ar), jnp.cos(ac), jnp.cos(ac)], -1)
    sin = jnp.concatenate([-jnp.sin(ar), jnp.sin(ar), -jnp.sin(ac), jnp.sin(ac)], -1)
    return _even_odd_rows(cos), _even_odd_rows(sin)


def _permute_in_cols(a):
    return jnp.concatenate([a[..., OFF_Q:OFF_K], a[..., OFF_G:D_IN], a[..., OFF_FN:OFF_G],
                            a[..., OFF_K:OFF_V], a[..., OFF_V:OFF_HY], a[..., OFF_HY:OFF_FN]], -1)


def _mixer(p_tok, seq_shape, lw, tabs, rope_tabs, ctx_parts):
    bsz, n = seq_shape
    p = p_tok.reshape(bsz, n, p_tok.shape[-1])
    q = _headnorm(p, P_Q // Q_W, N_Q_HEADS, lw['q_gain'], rope_tabs, ATTN_SCALE)
    k = _headnorm(p, P_K // KV_W, N_KV_HEADS, lw['k_gain'], rope_tabs, 1.0)
    o_att = _attention(q, [(k, p, P_V // KV_W)] + ctx_parts)

    hs, hd, mid = _filters(n, tabs['emb'], lw['hy_w1'], lw['hy_b1'], lw['hy_freq1'], lw['hy_w2'], lw['hy_b2'],
                           lw['hy_freq2'], lw['hy_w3'], tabs['absdelta'])
    k4 = _spectrum(tabs['mfe'], tabs['mfo'], hs, hd)
    z = None
    for o in range(HYENA_ORDER):
        z = _hyconv(o, z, p, lw['conv_w'], lw['conv_b'], tabs, k4, mid, lw['hy_skip'])
    y_fn = _fnet(p, tabs['csc'], tabs['mne'], tabs['mno'])
    tok = lambda a: a.reshape(p_tok.shape[0], p_tok.shape[1], a.shape[-1])
    return tok(o_att), tok(z), tok(y_fn), k, p


def _block(x_tok, mod, seq_shape, lw, tabs, rope_tabs, ctx_parts):
    p_tok = _inproj(x_tok, mod, lw['w_in'], lw['b_in'])
    o_att, z_hy, y_fn, k, p = _mixer(p_tok, seq_shape, lw, tabs, rope_tabs, ctx_parts)
    x_tok = _merge(o_att, z_hy, y_fn, p_tok, x_tok, mod, lw['w_att_o'], lw['w_hy_o'], lw['w_fn_o'],
                   lw['w_out'], lw['b_out'], lw['ln1_g'], lw['ln1_b'])
    x_tok = _mlp(x_tok, mod, lw['w_mlp1'], lw['b_mlp1'], lw['w_mlp2'], lw['b_mlp2'], lw['ln2_g'], lw['ln2_b'])
    return x_tok, k, p


def _seq_tables(n):
    tabs = dict(_hyena_dft(n))
    tabs.update(_fnet_dft(n))
    deltas = jnp.abs(jnp.linspace(MIN_DECAY, MAX_DECAY, D_HYENA, dtype=F32))
    tabs.update(emb=_filter_embedding(n, 128), absdelta=jnp.tile(deltas, HYENA_ORDER)[None, :])
    return tabs


def kernel(x, c, ctx, c_ctx, w_ada, b_ada, w_in, b_in, conv_w, conv_b, hy_w1, hy_b1, hy_freq1, hy_w2, hy_b2,
           hy_freq2, hy_w3, hy_skip, q_gain, k_gain, w_att_o, w_hy_o, w_fn_o, w_out, b_out, ln1_g, ln1_b,
           w_mlp1, b_mlp1, w_mlp2, b_mlp2, ln2_g, ln2_b):
    bsz, n_lat, d = x.shape
    n_ctx = ctx.shape[1]
    depth = w_ada.shape[0]
    tabs_lat = _seq_tables(n_lat)
    tabs_ctx = _seq_tables(n_ctx)
    rope_tabs = _rope_tables(n_lat)

    n_rows = -(-(bsz + 1) // 16) * 16
    cond = jnp.concatenate([c, c_ctx[None, :], jnp.zeros((n_rows - bsz - 1, d), F32)], 0)

    x = _deinterleave(x)
    ctx_tok = _deinterleave(ctx).reshape(1, bsz * n_ctx, d)
    row = lambda a: a[None, :]
    for i in range(depth):
        last = i == depth - 1
        lw = dict(
            w_in=_permute_in_cols(w_in[i]).astype(BF16), b_in=row(_permute_in_cols(b_in[i])),
            conv_w=conv_w[i], conv_b=row(conv_b[i]),
            hy_w1=jnp.pad(hy_w1[i], ((0, 128 - HYENA_EMB), (0, 0))), hy_b1=row(hy_b1[i]), hy_freq1=row(hy_freq1[i]),
            hy_w2=hy_w2[i], hy_b2=row(hy_b2[i]), hy_freq2=row(hy_freq2[i]), hy_w3=hy_w3[i],
            hy_skip=hy_skip[i].reshape(1, HYENA_ORDER * D_HYENA),
            q_gain=row(q_gain[i]), k_gain=row(k_gain[i]),
            w_att_o=w_att_o[i].astype(BF16), w_hy_o=w_hy_o[i].astype(BF16), w_fn_o=w_fn_o[i].astype(BF16),
            w_out=w_out[i].astype(BF16), b_out=row(b_out[i]), ln1_g=row(ln1_g[i]), ln1_b=row(ln1_b[i]),
            w_mlp1=w_mlp1[i].astype(BF16), b_mlp1=row(b_mlp1[i]), w_mlp2=w_mlp2[i].astype(BF16),
            b_mlp2=row(b_mlp2[i]), ln2_g=row(ln2_g[i]), ln2_b=row(ln2_b[i]))
        mod = _matmul(cond, w_ada[i], row(b_ada[i]), silu=True)
        mod_l = mod[:bsz].reshape(bsz, N_MOD, d)
        mod_c = mod[bsz:bsz + 1].reshape(1, N_MOD, d)

        if last:
            kv_cols = slice(P_K, P_V + KV_W)
            p_c = _inproj(ctx_tok, mod_c, lw['w_in'][:, kv_cols], lw['b_in'][:, kv_cols])
            p_c = p_c.reshape(bsz, n_ctx, 2 * KV_W)
            k_c = _headnorm(p_c, 0, N_KV_HEADS, lw['k_gain'], None, 1.0)
            v_blk = 1
        else:
            ctx_tok, k_c, p_c = _block(ctx_tok, mod_c, (bsz, n_ctx), lw, tabs_ctx, None, [])
            v_blk = P_V // KV_W
        x, _, _ = _block(x, mod_l, (bsz, n_lat), lw, tabs_lat, rope_tabs, [(k_c, p_c, v_blk)])
    return _interleave(x)
```

```python
import functools
import math

import jax
import jax.numpy as jnp
from jax import lax
from jax.experimental import pallas as pl
from jax.experimental.pallas import tpu as pltpu

F32 = jnp.float32
BF16 = jnp.bfloat16

D_MODEL = 1024
GRID_W = 64
HEAD_DIM = 128
N_Q_HEADS = D_MODEL // HEAD_DIM
N_KV_HEADS = 2
GQA_GROUP = N_Q_HEADS // N_KV_HEADS
ROPE_THETA = 10000.0
ROPE_AXIS_DIM = HEAD_DIM // 2
ATTN_SCALE = HEAD_DIM ** -0.5

D_HYENA = D_MODEL
HYENA_ORDER = 2
HYENA_BANDS = 16
HYENA_EMB = 1 + 2 * HYENA_BANDS
SHORT_CONV = 3
DECAY_TARGET = 1e-2
MIN_DECAY = math.log(DECAY_TARGET) / 0.3
MAX_DECAY = math.log(DECAY_TARGET) / 1.5

FNET_GROUPS = 4
FNET_GROUP_DIM = D_MODEL // FNET_GROUPS
D_FF = 4 * D_MODEL
N_MOD = 6
DEPTH = 4

Q_W = N_Q_HEADS * HEAD_DIM
KV_W = N_KV_HEADS * HEAD_DIM
HY_W = (HYENA_ORDER + 1) * D_HYENA
OFF_Q = 0
OFF_K = OFF_Q + Q_W
OFF_V = OFF_K + KV_W
OFF_HY = OFF_V + KV_W
OFF_FN = OFF_HY + HY_W
OFF_G = OFF_FN + D_MODEL
D_IN = OFF_G + 3 * D_MODEL

P_Q = 0
P_G = P_Q + Q_W
P_FN = P_G + 3 * D_MODEL
P_K = P_FN + D_MODEL
P_V = P_K + KV_W
P_HY = P_V + KV_W

ALPHA = (2 * DEPTH) ** 0.25
LN_EPS = 1e-6
RMS_EPS = 1e-6
KERN_EPS = 1e-6

V7X_VMEM_LIMIT_BYTES = 56 * 1024 * 1024


def _params(*sem):
    return pltpu.CompilerParams(dimension_semantics=sem, vmem_limit_bytes=V7X_VMEM_LIMIT_BYTES)


def _dot(a, b):
    return jnp.dot(a, b, preferred_element_type=F32)


def _layer_norm(r, g, b):
    mu = jnp.mean(r, -1, keepdims=True)
    d = r - mu
    var = jnp.mean(d * d, -1, keepdims=True)
    return d * lax.rsqrt(var + LN_EPS) * g + b


def _sigmoid(x):
    return 1.0 / (1.0 + jnp.exp(-x))


def _deinterleave_kernel(x_ref, o_ref):
    hh = x_ref.shape[0] // 2
    o_ref[0:hh, :] = x_ref[pl.ds(0, hh, stride=2), :]
    o_ref[hh:2 * hh, :] = x_ref[pl.ds(1, hh, stride=2), :]


def _interleave_kernel(x_ref, o_ref):
    hh = x_ref.shape[0] // 2
    o_ref[pl.ds(0, hh, stride=2), :] = x_ref[0:hh, :]
    o_ref[pl.ds(1, hh, stride=2), :] = x_ref[hh:2 * hh, :]


def _reorder_tokens(x, body, name):
    bsz, n, d = x.shape
    tc = 128
    spec = pl.BlockSpec((None, n, tc), lambda bi, c: (bi, 0, c))
    return pl.pallas_call(
        body,
        grid=(bsz, d // tc),
        in_specs=[spec],
        out_specs=spec,
        out_shape=jax.ShapeDtypeStruct(x.shape, x.dtype),
        compiler_params=_params("arbitrary", "arbitrary"),
        name=name,
    )(x)


def _deinterleave(x):
    return _reorder_tokens(x, _deinterleave_kernel, "deinterleave")


def _interleave(x):
    return _reorder_tokens(x, _interleave_kernel, "interleave")


def _even_odd_rows(a):
    return jnp.concatenate([a[0::2], a[1::2]], 0)


def _matmul_kernel(a_ref, b_ref, bias_ref, o_ref, *, silu):
    a = a_ref[...]
    if silu:
        a = a * _sigmoid(a)
    o_ref[...] = _dot(a.astype(BF16), b_ref[...].astype(BF16)) + bias_ref[...]


def _matmul(a, b, bias, *, silu=False, tn=1024):
    m, k = a.shape
    nl, _, n = b.shape
    return pl.pallas_call(
        functools.partial(_matmul_kernel, silu=silu),
        grid=(nl, n // tn),
        in_specs=[pl.BlockSpec((m, k), lambda l, j: (0, 0)),
                  pl.BlockSpec((None, k, tn), lambda l, j: (l, 0, j)),
                  pl.BlockSpec((None, 1, tn), lambda l, j: (l, 0, j))],
        out_specs=pl.BlockSpec((None, m, tn), lambda l, j: (l, 0, j)),
        out_shape=jax.ShapeDtypeStruct((nl, m, n), F32),
        compiler_params=_params("arbitrary", "arbitrary"),
        name="ada_matmul",
    )(a, b, bias)


def _inproj_kernel(x_ref, mod_ref, w_ref, b_ref, o_ref, h_ref):
    @pl.when(pl.program_id(2) == 0)
    def _():
        h_ref[...] = (x_ref[...] * (1.0 + mod_ref[1:2, :]) + mod_ref[0:1, :]).astype(BF16)

    o_ref[...] = (_dot(h_ref[...], w_ref[...]) + b_ref[...]).astype(o_ref.dtype)


INPROJ_MAX_COLS = 2176


def _inproj(x, mod, w, b):
    bsz, n, d = x.shape
    nout = w.shape[1]
    tm = min(n, 1024)
    tn = max(t for t in range(128, min(nout, INPROJ_MAX_COLS) + 1, 128) if nout % t == 0)
    return pl.pallas_call(
        _inproj_kernel,
        grid=(bsz, n // tm, nout // tn),
        in_specs=[pl.BlockSpec((None, tm, d), lambda bi, i, j: (bi, i, 0)),
                  pl.BlockSpec((None, N_MOD, d), lambda bi, i, j: (bi, 0, 0)),
                  pl.BlockSpec((d, tn), lambda bi, i, j: (0, j)),
                  pl.BlockSpec((1, tn), lambda bi, i, j: (0, j))],
        out_specs=pl.BlockSpec((None, tm, tn), lambda bi, i, j: (bi, i, j)),
        out_shape=jax.ShapeDtypeStruct((bsz, n, nout), BF16),
        scratch_shapes=[pltpu.VMEM((tm, d), BF16)],
        compiler_params=_params("arbitrary", "arbitrary", "arbitrary"),
        name="inproj",
    )(x, mod, w, b)


HEAD_PAIR = 2 * HEAD_DIM


def _headnorm_kernel(*refs, n_heads, rope):
    if rope:
        p_ref, g_ref, avg_ref, swap_ref, cos_ref, sin_ref, o_ref = refs
    else:
        p_ref, g_ref, avg_ref, o_ref = refs
    gain = g_ref[...]
    for hp in range(n_heads // 2):
        sl = slice(hp * HEAD_PAIR, (hp + 1) * HEAD_PAIR)
        xf = p_ref[:, sl].astype(F32)
        ms = _dot((xf * xf).astype(BF16), avg_ref[...])
        y = xf * lax.rsqrt(ms + RMS_EPS) * gain
        if rope:
            y = y * cos_ref[...] + _dot(y.astype(BF16), swap_ref[...]) * sin_ref[...]
        o_ref[:, sl] = y.astype(o_ref.dtype)


def _head_pair_matrices():
    i, j = _int_grid(HEAD_PAIR, HEAD_PAIR)
    avg = jnp.where(i // HEAD_DIM == j // HEAD_DIM, 1.0 / HEAD_DIM, 0.0)
    half = ROPE_AXIS_DIM // 2
    partner = jnp.where(j % ROPE_AXIS_DIM < half, j + half, j - half)
    return avg.astype(BF16), jnp.where(i == partner, 1.0, 0.0).astype(BF16)


def _headnorm(p, col_block, n_heads, gain, rope_tabs, scale):
    bsz, n, _ = p.shape
    w = n_heads * HEAD_DIM
    tm = min(n, 512)
    rope = rope_tabs is not None
    avg, swap = _head_pair_matrices()
    const = lambda shape: pl.BlockSpec(shape, lambda bi, i: (0, 0))
    in_specs = [pl.BlockSpec((None, tm, w), lambda bi, i: (bi, i, col_block)),
                const((1, HEAD_PAIR)), const((HEAD_PAIR, HEAD_PAIR))]
    args = [p, jnp.tile(gain * scale, (1, 2)), avg]
    if rope:
        in_specs += [const((HEAD_PAIR, HEAD_PAIR))] + [pl.BlockSpec((tm, HEAD_PAIR), lambda bi, i: (i, 0))] * 2
        args += [swap] + list(rope_tabs)
    return pl.pallas_call(
        functools.partial(_headnorm_kernel, n_heads=n_heads, rope=rope),
        grid=(bsz, n // tm),
        in_specs=in_specs,
        out_specs=pl.BlockSpec((None, tm, w), lambda bi, i: (bi, i, 0)),
        out_shape=jax.ShapeDtypeStruct((bsz, n, w), BF16),
        compiler_params=_params("arbitrary", "arbitrary"),
        name="headnorm",
    )(*args)


def _attn_kernel(*refs, n_parts):
    q_ref = refs[0]
    kv_refs = refs[1:1 + 2 * n_parts]
    o_ref = refs[1 + 2 * n_parts]
    for g in range(N_KV_HEADS):
        gsl = slice(g * HEAD_DIM, (g + 1) * HEAD_DIM)
        ks = [kv_refs[2 * i][:, gsl] for i in range(n_parts)]
        vs = [jnp.concatenate([kv_refs[2 * i + 1][:, gsl], jnp.ones((k.shape[0], HEAD_DIM), BF16)], 1)
              for i, k in enumerate(ks)]
        for r in range(GQA_GROUP):
            h = g * GQA_GROUP + r
            hsl = slice(h * HEAD_DIM, (h + 1) * HEAD_DIM)
            q = q_ref[:, hsl]
            ss = [lax.dot_general(q, k, (((1,), (1,)), ((), ())), preferred_element_type=F32) for k in ks]
            m = jnp.max(ss[0], -1, keepdims=True)
            for s in ss[1:]:
                m = jnp.maximum(m, jnp.max(s, -1, keepdims=True))
            o = None
            for s, v in zip(ss, vs):
                pv = _dot(jnp.exp((s - m).astype(BF16)), v)
                o = pv if o is None else o + pv
            o_ref[:, hsl] = (o[:, :HEAD_DIM] / o[:, HEAD_DIM:]).astype(o_ref.dtype)


def _attention(q, parts):
    bsz, n, w = q.shape
    tq = min(n, 256)
    in_specs = [pl.BlockSpec((None, tq, w), lambda bi, i: (bi, i, 0))]
    args = [q]
    for k, v, vblk in parts:
        nk = k.shape[1]
        in_specs.append(pl.BlockSpec((None, nk, KV_W), lambda bi, i: (bi, 0, 0)))
        in_specs.append(pl.BlockSpec((None, nk, KV_W), functools.partial(lambda bi, i, c: (bi, 0, c), c=vblk)))
        args += [k, v]
    return pl.pallas_call(
        functools.partial(_attn_kernel, n_parts=len(parts)),
        grid=(bsz, n // tq),
        in_specs=in_specs,
        out_specs=pl.BlockSpec((None, tq, w), lambda bi, i: (bi, i, 0)),
        out_shape=jax.ShapeDtypeStruct((bsz, n, w), BF16),
        compiler_params=_params("arbitrary", "arbitrary"),
        name="attention",
    )(*args)


def _filter_kernel(emb_ref, w1_ref, b1_ref, f1_ref, w2_ref, b2_ref, f2_ref, w3f_ref, w3b_ref, dl_ref,
                   hs_ref, hd_ref, mid_ref, h_ref, *, n):
    hh = n // 2

    @pl.when(pl.program_id(0) == 0)
    def _():
        h1 = jnp.sin(f1_ref[...] * (_dot(emb_ref[...], w1_ref[...]) + b1_ref[...]))
        h_ref[...] = jnp.sin(f2_ref[...] * (_dot(h1, w2_ref[...]) + b2_ref[...]))

    h = h_ref[...]
    hf = _dot(h, w3f_ref[...])
    hb = _dot(h, w3b_ref[...])
    row = lax.broadcasted_iota(jnp.int32, hf.shape, 0)
    lag = jnp.where(row < hh, 2 * row, 2 * (row - hh) + 1)
    t = lag.astype(F32) * (1.0 / (n - 1))
    dec = jnp.exp(-t * dl_ref[...])
    hf = hf * dec
    hb = jnp.where(lag == 0, 0.0, hb * dec)
    ssq = jnp.sum(hf * hf, 0, keepdims=True) + jnp.sum(hb * hb, 0, keepdims=True)
    nrm = lax.rsqrt(ssq + KERN_EPS) * (1.0 / n)
    hs = (hf + hb) * nrm
    hd = (hf - hb) * nrm
    hs_ref[...] = hs.astype(hs_ref.dtype)
    hd_ref[...] = hd.astype(hd_ref.dtype)
    sign = jnp.where(row % 2 == 0, 1.0, -1.0)
    mid_ref[0:1, :] = jnp.sum(jnp.where(row < hh, hs * sign, 0.0), 0, keepdims=True)
    mid_ref[1:2, :] = jnp.sum(jnp.where(row < hh, 0.0, hd * sign), 0, keepdims=True)


def _filters(n, emb, w1, b1, f1, w2, b2, f2, w3, absdelta):
    od = HYENA_ORDER * D_HYENA
    tn = 512
    fh = w2.shape[0]
    ke = emb.shape[1]
    full = lambda shape: pl.BlockSpec(shape, lambda j: (0, 0))
    return pl.pallas_call(
        functools.partial(_filter_kernel, n=n),
        grid=(od // tn,),
        in_specs=[full((n, ke)), full((ke, fh)), full((1, fh)), full((1, fh)),
                  full((fh, fh)), full((1, fh)), full((1, fh)),
                  pl.BlockSpec((fh, tn), lambda j: (0, j)),
                  pl.BlockSpec((fh, tn), lambda j: (0, j + od // tn)),
                  pl.BlockSpec((1, tn), lambda j: (0, j))],
        out_specs=[pl.BlockSpec((n, tn), lambda j: (0, j)),
                   pl.BlockSpec((n, tn), lambda j: (0, j)),
                   pl.BlockSpec((2, tn), lambda j: (0, j))],
        out_shape=[jax.ShapeDtypeStruct((n, od), BF16), jax.ShapeDtypeStruct((n, od), BF16),
                   jax.ShapeDtypeStruct((2, od), F32)],
        scratch_shapes=[pltpu.VMEM((n, fh), F32)],
        compiler_params=_params("arbitrary"),
        name="hyena_filters",
    )(emb, w1, b1, f1, w2, b2, f2, w3, w3, absdelta)


def _spectrum_kernel(fe_re, fe_im, fo_re, fo_im, hse, hso, hde, hdo, k4_ref):
    p = _dot(fe_re[...], hse[...])
    q = _dot(fo_re[...], hso[...])
    pp = _dot(fe_im[...], hde[...])
    qp = _dot(fo_im[...], hdo[...])
    row0 = (lax.broadcasted_iota(jnp.int32, p.shape, 0) + pl.program_id(0) * p.shape[0]) == 0
    k4_ref[0] = jnp.where(row0, 0.5 * (p + q), p + q)
    k4_ref[1] = jnp.where(row0, 0.0, pp + qp)
    k4_ref[2] = jnp.where(row0, 0.5 * (p - q), p - q)
    k4_ref[3] = jnp.where(row0, 0.0, qp - pp)


def _spectrum(mfe, mfo, hs, hd):
    n, od = hs.shape
    h = n // 2
    tm = min(h, 512)
    tn = 512
    nb = h // tm
    mat = lambda off: pl.BlockSpec((tm, h), functools.partial(lambda i, j, o: (i + o, 0), o=off))
    half = lambda a: pl.BlockSpec((h, tn), functools.partial(lambda i, j, a: (a, j), a=a))
    return pl.pallas_call(
        _spectrum_kernel,
        grid=(nb, od // tn),
        in_specs=[mat(0), mat(nb), mat(0), mat(nb), half(0), half(1), half(0), half(1)],
        out_specs=pl.BlockSpec((4, tm, tn), lambda i, j: (0, i, j)),
        out_shape=jax.ShapeDtypeStruct((4, h, od), F32),
        compiler_params=_params("arbitrary", "arbitrary"),
        name="hyena_spectrum",
    )(mfe, mfe, mfo, mfo, hs, hs, hd, hd)


def _short_conv(x, w_ref, b_ref):
    hh = x.shape[0] // 2
    e = x[:hh]
    o = x[hh:]
    row = lax.broadcasted_iota(jnp.int32, e.shape, 0)
    o_prev = jnp.where(row == 0, 0.0, pltpu.roll(o, 1, 0))
    e_next = jnp.where(row == hh - 1, 0.0, pltpu.roll(e, hh - 1, 0))
    w0, w1, w2 = w_ref[0:1, :], w_ref[1:2, :], w_ref[2:3, :]
    return (b_ref[...] + o_prev * w0 + e * w1 + o * w2,
            b_ref[...] + e * w0 + o * w1 + e_next * w2)


def _hyconv_kernel(*refs, conv_z, cw, n):
    if conv_z:
        z_ref, cwz_ref, cbz_ref = refs[:3]
        refs = refs[3:]
    else:
        z_ref = refs[0]
        refs = refs[1:]
    x_ref, cwx_ref, cbx_ref, mfe_ref, mfo_ref, mie_ref, mio_ref, k4_ref, mid_ref, skip_ref, o_ref = refs
    hh = n // 2
    first8 = lax.broadcasted_iota(jnp.int32, (8, cw), 0) == 0
    for c in range(o_ref.shape[1] // cw):
        cs = slice(c * cw, (c + 1) * cw)
        if conv_z:
            ze, zo = _short_conv(z_ref[:, cs].astype(F32), cwz_ref.at[:, cs], cbz_ref.at[:, cs])
            ze, zo = ze.astype(BF16), zo.astype(BF16)
        else:
            ze, zo = z_ref[0:hh, cs], z_ref[hh:n, cs]
        fe = _dot(mfe_ref[...], ze)
        fo = _dot(mfo_ref[...], zo)
        p, pp, q, qp = fe[:hh], fe[hh:], fo[:hh], fo[hh:]
        kra, kia, krb, kib = k4_ref[0, :, cs], k4_ref[1, :, cs], k4_ref[2, :, cs], k4_ref[3, :, cs]
        fra, fia = p + q, pp + qp
        frb, fib = p - q, qp - pp
        wra = fra * kra - fia * kia
        wia = fra * kia + fia * kra
        wrb = frb * krb - fib * kib
        wib = frb * kib + fib * krb
        ge_im = wia - wib
        go_im = wia + wib
        kmr, kmi = mid_ref[0:1, cs], mid_ref[1:2, cs]
        pp8, qp8 = pp[:8], qp[:8]
        ge_im = jnp.concatenate([jnp.where(first8, pp8 * kmr - qp8 * kmi, ge_im[:8]), ge_im[8:]], 0)
        go_im = jnp.concatenate([jnp.where(first8, pp8 * kmi + qp8 * kmr, go_im[:8]), go_im[8:]], 0)
        ge = jnp.concatenate([(wra + wrb).astype(BF16), ge_im.astype(BF16)], 0)
        go = jnp.concatenate([(wra - wrb).astype(BF16), go_im.astype(BF16)], 0)
        ye = _dot(mie_ref[...], ge)
        yo = _dot(mio_ref[...], go)
        xe, xo = _short_conv(x_ref[:, cs].astype(F32), cwx_ref.at[:, cs], cbx_ref.at[:, cs])
        skip = skip_ref[:, cs]
        o_ref[0:hh, cs] = (xe * (ye + ze.astype(F32) * skip)).astype(o_ref.dtype)
        o_ref[hh:n, cs] = (xo * (yo + zo.astype(F32) * skip)).astype(o_ref.dtype)


def _hyconv(order, z_arr, p, conv_w, conv_b, tabs, k4, mid, skip):
    bsz, n, _ = p.shape
    h = n // 2
    tn = 512
    cw = 256
    nct = D_HYENA // tn
    conv_z = z_arr is None
    once = pl.Buffered(1)

    def sect(part):
        off = P_HY // tn + part * nct
        return pl.BlockSpec((None, n, tn), functools.partial(lambda c, bi, o: (bi, 0, o + c), o=off))

    def wsect(rows, part):
        return pl.BlockSpec((rows, tn), functools.partial(lambda c, bi, o: (0, o + c), o=part * nct))

    whole = lambda a: pl.BlockSpec(a.shape, lambda c, bi: (0, 0), pipeline_mode=once)
    ocol = functools.partial(lambda c, bi, o: (0, o + c), o=order * nct)
    if conv_z:
        in_specs = [sect(HYENA_ORDER), wsect(SHORT_CONV, HYENA_ORDER), wsect(1, HYENA_ORDER)]
        args = [p, conv_w, conv_b]
    else:
        in_specs = [pl.BlockSpec((None, n, tn), lambda c, bi: (bi, 0, c))]
        args = [z_arr]
    mats = [tabs['mfe'], tabs['mfo'], tabs['mie'], tabs['mio']]
    in_specs += [sect(order), wsect(SHORT_CONV, order), wsect(1, order)] + [whole(m) for m in mats]
    in_specs += [pl.BlockSpec((4, h, tn), functools.partial(lambda c, bi, o: (0, 0, o + c), o=order * nct),
                              pipeline_mode=once),
                 pl.BlockSpec((2, tn), ocol),
                 pl.BlockSpec((1, tn), ocol)]
    args += [p, conv_w, conv_b] + mats + [k4, mid, skip]
    return pl.pallas_call(
        functools.partial(_hyconv_kernel, conv_z=conv_z, cw=cw, n=n),
        grid=(nct, bsz),
        in_specs=in_specs,
        out_specs=pl.BlockSpec((None, n, tn), lambda c, bi: (bi, 0, c)),
        out_shape=jax.ShapeDtypeStruct((bsz, n, D_HYENA), BF16),
        compiler_params=_params("arbitrary", "arbitrary"),
        name="hyena_conv",
    )(*args)


def _fnet_kernel(u_ref, csc_ref, mne_ref, mno_ref, o_ref, *, n, scale):
    hh = n // 2
    qq = n // 4
    for g in range(FNET_GROUPS):
        gsl = slice(g * FNET_GROUP_DIM, (g + 1) * FNET_GROUP_DIM)
        t = _dot(u_ref[:, gsl], csc_ref[...])
        tc = t[:, :FNET_GROUP_DIM].astype(BF16)
        ts = t[:, FNET_GROUP_DIM:].astype(BF16)
        a = _dot(mne_ref[...], jnp.concatenate([tc[:hh], ts[:hh]], 0))
        b = _dot(mno_ref[...], jnp.concatenate([tc[hh:], ts[hh:]], 0))
        lo = ((a + b) * scale).astype(o_ref.dtype)
        hi = ((a - b) * scale).astype(o_ref.dtype)
        o_ref[0:qq, gsl] = lo[:qq]
        o_ref[qq:hh, gsl] = hi[:qq]
        o_ref[hh:hh + qq, gsl] = lo[qq:]
        o_ref[hh + qq:n, gsl] = hi[qq:]


def _fnet(p, csc, mne, mno):
    bsz, n, _ = p.shape
    scale = 1.0 / math.sqrt(n * FNET_GROUP_DIM)
    whole = lambda a: pl.BlockSpec(a.shape, lambda bi: (0, 0), pipeline_mode=pl.Buffered(1))
    return pl.pallas_call(
        functools.partial(_fnet_kernel, n=n, scale=scale),
        grid=(bsz,),
        in_specs=[pl.BlockSpec((None, n, D_MODEL), lambda bi: (bi, 0, P_FN // D_MODEL)),
                  whole(csc), whole(mne), whole(mno)],
        out_specs=pl.BlockSpec((None, n, D_MODEL), lambda bi: (bi, 0, 0)),
        out_shape=jax.ShapeDtypeStruct((bsz, n, D_MODEL), BF16),
        compiler_params=_params("arbitrary"),
        name="fnet",
    )(p, csc, mne, mno)


def _merge_kernel(oa_ref, hy_ref, fn_ref, ga_ref, gh_ref, gf_ref, x_ref, mod_ref,
                  wa_ref, wh_ref, wf_ref, wo_ref, bo_ref, lg_ref, lb_ref, o_ref):
    m = _sigmoid(ga_ref[...].astype(F32)) * _dot(oa_ref[...], wa_ref[...])
    m = m + _sigmoid(gh_ref[...].astype(F32)) * _dot(hy_ref[...], wh_ref[...])
    m = m + _sigmoid(gf_ref[...].astype(F32)) * _dot(fn_ref[...], wf_ref[...])
    y = _dot(m.astype(BF16), wo_ref[...]) + bo_ref[...]
    r = ALPHA * x_ref[...] + mod_ref[2:3, :] * y
    o_ref[...] = _layer_norm(r, lg_ref[...], lb_ref[...])


def _merge(o_att, z_hy, y_fn, p, x, mod, wa, wh, wf, wo, bo, lg, lb):
    bsz, n, d = x.shape
    tm = min(n, 512)
    tok = lambda c: pl.BlockSpec((None, tm, d), functools.partial(lambda bi, i, c: (bi, i, c), c=c))
    wspec = pl.BlockSpec((d, d), lambda bi, i: (0, 0))
    vspec = pl.BlockSpec((1, d), lambda bi, i: (0, 0))
    g0 = P_G // d
    return pl.pallas_call(
        _merge_kernel,
        grid=(bsz, n // tm),
        in_specs=[tok(0), tok(0), tok(0), tok(g0), tok(g0 + 1), tok(g0 + 2), tok(0),
                  pl.BlockSpec((None, N_MOD, d), lambda bi, i: (bi, 0, 0)),
                  wspec, wspec, wspec, wspec, vspec, vspec, vspec],
        out_specs=tok(0),
        out_shape=jax.ShapeDtypeStruct((bsz, n, d), F32),
        compiler_params=_params("arbitrary", "arbitrary"),
        name="merge_ln",
    )(o_att, z_hy, y_fn, p, p, p, x, mod, wa, wh, wf, wo, bo, lg, lb)


MLP_ROW_CHUNK = 512
MLP_FF_CHUNK = 1024


def _mlp_kernel(x_ref, mod_ref, w1_ref, b1_ref, w2_ref, b2_ref, lg_ref, lb_ref, o_ref):
    ff = w1_ref.shape[1]
    rc = min(MLP_ROW_CHUNK, x_ref.shape[0])
    for r0 in range(0, x_ref.shape[0], rc):
        rs = slice(r0, r0 + rc)
        x = x_ref[rs, :]
        h = (x * (1.0 + mod_ref[4:5, :]) + mod_ref[3:4, :]).astype(BF16)
        y = b2_ref[...]
        for c0 in range(0, ff, MLP_FF_CHUNK):
            cs = slice(c0, c0 + MLP_FF_CHUNK)
            a = jnp.maximum(_dot(h, w1_ref[:, cs]) + b1_ref[:, cs], 0.0)
            y = y + _dot((a * a).astype(BF16), w2_ref[cs, :])
        r = ALPHA * x + mod_ref[5:6, :] * y
        o_ref[rs, :] = _layer_norm(r, lg_ref[...], lb_ref[...])


def _mlp(x, mod, w1, b1, w2, b2, lg, lb):
    bsz, n, d = x.shape
    tm = min(n, 2 * MLP_ROW_CHUNK)
    whole = lambda a: pl.BlockSpec(a.shape, lambda bi, i: (0, 0), pipeline_mode=pl.Buffered(1))
    return pl.pallas_call(
        _mlp_kernel,
        grid=(bsz, n // tm),
        in_specs=[pl.BlockSpec((None, tm, d), lambda bi, i: (bi, i, 0)),
                  pl.BlockSpec((None, N_MOD, d), lambda bi, i: (bi, 0, 0)),
                  whole(w1), whole(b1), whole(w2), whole(b2), whole(lg), whole(lb)],
        out_specs=pl.BlockSpec((None, tm, d), lambda bi, i: (bi, i, 0)),
        out_shape=jax.ShapeDtypeStruct((bsz, n, d), F32),
        compiler_params=_params("arbitrary", "arbitrary"),
        name="mlp_ln",
    )(x, mod, w1, b1, w2, b2, lg, lb)


def _int_grid(rows, cols):
    k = lax.broadcasted_iota(jnp.int32, (rows, cols), 0)
    s = lax.broadcasted_iota(jnp.int32, (rows, cols), 1)
    return k, s


TRIG_ROW_BLOCK = 32


def _trig_table(n_rows, samp, period):
    def base(kv):
        ang = ((kv[:, None] * samp[None, :]) % period).astype(F32) * (2.0 * math.pi / period)
        return jnp.cos(ang), jnp.sin(ang)

    ch, sh = base(jnp.arange(n_rows // TRIG_ROW_BLOCK, dtype=jnp.int32) * TRIG_ROW_BLOCK)
    cl, sl = base(jnp.arange(TRIG_ROW_BLOCK, dtype=jnp.int32))
    ch, sh, cl, sl = ch[:, None, :], sh[:, None, :], cl[None], sl[None]
    shape = (n_rows, samp.shape[0])
    return (ch * cl - sh * sl).reshape(shape), (sh * cl + ch * sl).reshape(shape)


def _hyena_dft(n):
    h = n // 2
    s = jnp.arange(h, dtype=jnp.int32)
    alt = jnp.where(s % 2 == 0, 1.0, -1.0)[None, :]
    first = lax.broadcasted_iota(jnp.int32, (h, h), 0) == 0
    out = {}
    for name, samp in (('e', 2 * s), ('o', 2 * s + 1)):
        cos, sin = _trig_table(h, samp, 2 * n)
        mf = jnp.concatenate([cos, jnp.where(first, alt, sin)], 0)
        out['mf' + name] = mf.astype(BF16)
        out['mi' + name] = mf.T.astype(BF16)
    return out


def _fnet_dft(n):
    h = n // 2
    s = jnp.arange(h, dtype=jnp.int32)
    out = {}
    for name, samp in (('e', 2 * s), ('o', 2 * s + 1)):
        cos, sin = _trig_table(h, samp, n)
        out['mn' + name] = _even_odd_rows(jnp.concatenate([cos, -sin], 1)).astype(BF16)
    k, s = _int_grid(FNET_GROUP_DIM, FNET_GROUP_DIM)
    ang = ((k * s) % FNET_GROUP_DIM).astype(F32) * (2.0 * math.pi / FNET_GROUP_DIM)
    out['csc'] = jnp.concatenate([jnp.cos(ang), jnp.sin(ang)], 1).astype(BF16)
    return out


def _filter_embedding(n, width):
    t = jnp.linspace(0.0, 1.0, n, dtype=F32)[:, None]
    w = (2.0 * math.pi / n) * jnp.arange(n, dtype=F32)[:, None]
    f = jnp.linspace(1e-4, HYENA_BANDS - 1, HYENA_BANDS, dtype=F32)[None, :]
    emb = jnp.concatenate([t, jnp.cos(f * w), -jnp.sin(f * w)], -1)
    return _even_odd_rows(jnp.pad(emb, ((0, 0), (0, width - HYENA_EMB))))


def _rope_tables(n):
    rows = n // GRID_W
    row = jnp.repeat(jnp.arange(rows, dtype=F32), GRID_W)
    col = jnp.tile(jnp.arange(GRID_W, dtype=F32), rows)
    inv_freq = ROPE_THETA ** (-jnp.arange(ROPE_AXIS_DIM // 2, dtype=F32) * 2.0 / ROPE_AXIS_DIM)
    ar = row[:, None] * inv_freq[None, :]
    ac = col[:, None] * inv_freq[None, :]
    cos = jnp.concatenate([jnp.cos(ar), jnp.cos(ar), jnp.cos(ac), jnp.cos(ac)] * 2, -1)
    sin = jnp.concatenate([-jnp.sin(ar), jnp.sin(ar), -jnp.sin(ac), jnp.sin(ac)] * 2, -1)
    return _even_odd_rows(cos), _even_odd_rows(sin)


def _permute_in_cols(a):
    return jnp.concatenate([a[..., OFF_Q:OFF_K], a[..., OFF_G:D_IN], a[..., OFF_FN:OFF_G],
                            a[..., OFF_K:OFF_V], a[..., OFF_V:OFF_HY], a[..., OFF_HY:OFF_FN]], -1)


def _mixer(p_tok, seq_shape, lw, tabs, rope_tabs, ctx_parts):
    bsz, n = seq_shape
    p = p_tok.reshape(bsz, n, p_tok.shape[-1])
    q = _headnorm(p, P_Q // Q_W, N_Q_HEADS, lw['q_gain'], rope_tabs, ATTN_SCALE)
    k = _headnorm(p, P_K // KV_W, N_KV_HEADS, lw['k_gain'], rope_tabs, 1.0)
    o_att = _attention(q, [(k, p, P_V // KV_W)] + ctx_parts)

    hs, hd, mid = _filters(n, tabs['emb'], lw['hy_w1'], lw['hy_b1'], lw['hy_freq1'], lw['hy_w2'], lw['hy_b2'],
                           lw['hy_freq2'], lw['hy_w3'], tabs['absdelta'])
    k4 = _spectrum(tabs['mfe'], tabs['mfo'], hs, hd)
    z = None
    for o in range(HYENA_ORDER):
        z = _hyconv(o, z, p, lw['conv_w'], lw['conv_b'], tabs, k4, mid, lw['hy_skip'])
    y_fn = _fnet(p, tabs['csc'], tabs['mne'], tabs['mno'])
    tok = lambda a: a.reshape(p_tok.shape[0], p_tok.shape[1], a.shape[-1])
    return tok(o_att), tok(z), tok(y_fn), k, p


def _block(x_tok, mod, seq_shape, lw, tabs, rope_tabs, ctx_parts):
    p_tok = _inproj(x_tok, mod, lw['w_in'], lw['b_in'])
    o_att, z_hy, y_fn, k, p = _mixer(p_tok, seq_shape, lw, tabs, rope_tabs, ctx_parts)
    x_tok = _merge(o_att, z_hy, y_fn, p_tok, x_tok, mod, lw['w_att_o'], lw['w_hy_o'], lw['w_fn_o'],
                   lw['w_out'], lw['b_out'], lw['ln1_g'], lw['ln1_b'])
    x_tok = _mlp(x_tok, mod, lw['w_mlp1'], lw['b_mlp1'], lw['w_mlp2'], lw['b_mlp2'], lw['ln2_g'], lw['ln2_b'])
    return x_tok, k, p


def _seq_tables(n):
    tabs = dict(_hyena_dft(n))
    tabs.update(_fnet_dft(n))
    deltas = jnp.abs(jnp.linspace(MIN_DECAY, MAX_DECAY, D_HYENA, dtype=F32))
    tabs.update(emb=_filter_embedding(n, 128), absdelta=jnp.tile(deltas, HYENA_ORDER)[None, :])
    return tabs


def kernel(x, c, ctx, c_ctx, w_ada, b_ada, w_in, b_in, conv_w, conv_b, hy_w1, hy_b1, hy_freq1, hy_w2, hy_b2,
           hy_freq2, hy_w3, hy_skip, q_gain, k_gain, w_att_o, w_hy_o, w_fn_o, w_out, b_out, ln1_g, ln1_b,
           w_mlp1, b_mlp1, w_mlp2, b_mlp2, ln2_g, ln2_b):
    bsz, n_lat, d = x.shape
    n_ctx = ctx.shape[1]
    depth = w_ada.shape[0]
    tabs_lat = _seq_tables(n_lat)
    tabs_ctx = _seq_tables(n_ctx)
    rope_tabs = _rope_tables(n_lat)

    n_rows = -(-(bsz + 1) // 16) * 16
    cond = jnp.concatenate([c, c_ctx[None, :], jnp.zeros((n_rows - bsz - 1, d), F32)], 0)

    x = _deinterleave(x)
    ctx_tok = _deinterleave(ctx).reshape(1, bsz * n_ctx, d)
    row = lambda a: a[None, :]
    mods = _matmul(cond, w_ada, b_ada[:, None, :], silu=True)
    for i in range(depth):
        last = i == depth - 1
        lw = dict(
            w_in=_permute_in_cols(w_in[i]).astype(BF16), b_in=row(_permute_in_cols(b_in[i])),
            conv_w=conv_w[i], conv_b=row(conv_b[i]),
            hy_w1=jnp.pad(hy_w1[i], ((0, 128 - HYENA_EMB), (0, 0))), hy_b1=row(hy_b1[i]), hy_freq1=row(hy_freq1[i]),
            hy_w2=hy_w2[i], hy_b2=row(hy_b2[i]), hy_freq2=row(hy_freq2[i]), hy_w3=hy_w3[i],
            hy_skip=hy_skip[i].reshape(1, HYENA_ORDER * D_HYENA),
            q_gain=row(q_gain[i]), k_gain=row(k_gain[i]),
            w_att_o=w_att_o[i].astype(BF16), w_hy_o=w_hy_o[i].astype(BF16), w_fn_o=w_fn_o[i].astype(BF16),
            w_out=w_out[i].astype(BF16), b_out=row(b_out[i]), ln1_g=row(ln1_g[i]), ln1_b=row(ln1_b[i]),
            w_mlp1=w_mlp1[i].astype(BF16), b_mlp1=row(b_mlp1[i]), w_mlp2=w_mlp2[i].astype(BF16),
            b_mlp2=row(b_mlp2[i]), ln2_g=row(ln2_g[i]), ln2_b=row(ln2_b[i]))
        mod_l = mods[i, :bsz].reshape(bsz, N_MOD, d)
        mod_c = mods[i, bsz:bsz + 1].reshape(1, N_MOD, d)

        if last:
            kv_cols = slice(P_K, P_V + KV_W)
            p_c = _inproj(ctx_tok, mod_c, lw['w_in'][:, kv_cols], lw['b_in'][:, kv_cols])
            p_c = p_c.reshape(bsz, n_ctx, 2 * KV_W)
            k_c = _headnorm(p_c, 0, N_KV_HEADS, lw['k_gain'], None, 1.0)
            v_blk = 1
        else:
            ctx_tok, k_c, p_c = _block(ctx_tok, mod_c, (bsz, n_ctx), lw, tabs_ctx, None, [])
            v_blk = P_V // KV_W
        x, _, _ = _block(x, mod_l, (bsz, n_lat), lw, tabs_lat, rope_tabs, [(k_c, p_c, v_blk)])
    return _interleave(x)
```

```python
import functools
import math

import jax
import jax.numpy as jnp
from jax import lax
from jax.experimental import pallas as pl
from jax.experimental.pallas import tpu as pltpu

F32 = jnp.float32
BF16 = jnp.bfloat16

D_MODEL = 1024
GRID_W = 64
HEAD_DIM = 128
N_Q_HEADS = D_MODEL // HEAD_DIM
N_KV_HEADS = 2
GQA_GROUP = N_Q_HEADS // N_KV_HEADS
ROPE_THETA = 10000.0
ROPE_AXIS_DIM = HEAD_DIM // 2
ATTN_SCALE = HEAD_DIM ** -0.5

D_HYENA = D_MODEL
HYENA_ORDER = 2
HYENA_BANDS = 16
HYENA_EMB = 1 + 2 * HYENA_BANDS
SHORT_CONV = 3
DECAY_TARGET = 1e-2
MIN_DECAY = math.log(DECAY_TARGET) / 0.3
MAX_DECAY = math.log(DECAY_TARGET) / 1.5

FNET_GROUPS = 4
FNET_GROUP_DIM = D_MODEL // FNET_GROUPS
D_FF = 4 * D_MODEL
N_MOD = 6
DEPTH = 4

Q_W = N_Q_HEADS * HEAD_DIM
KV_W = N_KV_HEADS * HEAD_DIM
HY_W = (HYENA_ORDER + 1) * D_HYENA
OFF_Q = 0
OFF_K = OFF_Q + Q_W
OFF_V = OFF_K + KV_W
OFF_HY = OFF_V + KV_W
OFF_FN = OFF_HY + HY_W
OFF_G = OFF_FN + D_MODEL
D_IN = OFF_G + 3 * D_MODEL

P_Q = 0
P_G = P_Q + Q_W
P_FN = P_G + 3 * D_MODEL
P_K = P_FN + D_MODEL
P_V = P_K + KV_W
P_HY = P_V + KV_W

ALPHA = (2 * DEPTH) ** 0.25
LN_EPS = 1e-6
RMS_EPS = 1e-6
KERN_EPS = 1e-6

V7X_VMEM_LIMIT_BYTES = 56 * 1024 * 1024


def _params(*sem):
    return pltpu.CompilerParams(dimension_semantics=sem, vmem_limit_bytes=V7X_VMEM_LIMIT_BYTES)


def _dot(a, b):
    return jnp.dot(a, b, preferred_element_type=F32)


def _layer_norm(r, g, b):
    mu = jnp.mean(r, -1, keepdims=True)
    d = r - mu
    var = jnp.mean(d * d, -1, keepdims=True)
    return d * lax.rsqrt(var + LN_EPS) * g + b


def _sigmoid(x):
    return 1.0 / (1.0 + jnp.exp(-x))


def _deinterleave_kernel(x_ref, o_ref):
    hh = x_ref.shape[0] // 2
    o_ref[0:hh, :] = x_ref[pl.ds(0, hh, stride=2), :]
    o_ref[hh:2 * hh, :] = x_ref[pl.ds(1, hh, stride=2), :]


def _interleave_kernel(x_ref, o_ref):
    hh = x_ref.shape[0] // 2
    o_ref[pl.ds(0, hh, stride=2), :] = x_ref[0:hh, :]
    o_ref[pl.ds(1, hh, stride=2), :] = x_ref[hh:2 * hh, :]


def _reorder_tokens(x, body, name):
    bsz, n, d = x.shape
    tc = 128
    spec = pl.BlockSpec((None, n, tc), lambda bi, c: (bi, 0, c))
    return pl.pallas_call(
        body,
        grid=(bsz, d // tc),
        in_specs=[spec],
        out_specs=spec,
        out_shape=jax.ShapeDtypeStruct(x.shape, x.dtype),
        compiler_params=_params("arbitrary", "arbitrary"),
        name=name,
    )(x)


def _deinterleave(x):
    return _reorder_tokens(x, _deinterleave_kernel, "deinterleave")


def _interleave(x):
    return _reorder_tokens(x, _interleave_kernel, "interleave")


def _even_odd_rows(a):
    return jnp.concatenate([a[0::2], a[1::2]], 0)


def _matmul_kernel(a_ref, b_ref, bias_ref, o_ref, *, silu):
    a = a_ref[...]
    if silu:
        a = a * _sigmoid(a)
    o_ref[...] = _dot(a.astype(BF16), b_ref[...].astype(BF16)) + bias_ref[...]


def _matmul(a, b, bias, *, silu=False, tn=1024):
    m, k = a.shape
    nl, _, n = b.shape
    return pl.pallas_call(
        functools.partial(_matmul_kernel, silu=silu),
        grid=(nl, n // tn),
        in_specs=[pl.BlockSpec((m, k), lambda l, j: (0, 0)),
                  pl.BlockSpec((None, k, tn), lambda l, j: (l, 0, j)),
                  pl.BlockSpec((None, 1, tn), lambda l, j: (l, 0, j))],
        out_specs=pl.BlockSpec((None, m, tn), lambda l, j: (l, 0, j)),
        out_shape=jax.ShapeDtypeStruct((nl, m, n), F32),
        compiler_params=_params("arbitrary", "arbitrary"),
        name="ada_matmul",
    )(a, b, bias)


def _inproj_kernel(x_ref, mod_ref, w_ref, b_ref, o_ref, h_ref):
    @pl.when(pl.program_id(2) == 0)
    def _():
        h_ref[...] = (x_ref[...] * (1.0 + mod_ref[1:2, :]) + mod_ref[0:1, :]).astype(BF16)

    o_ref[...] = (_dot(h_ref[...], w_ref[...]) + b_ref[...]).astype(o_ref.dtype)


INPROJ_MAX_COLS = 2176


def _inproj(x, mod, w, b):
    bsz, n, d = x.shape
    nout = w.shape[1]
    tm = min(n, 1024)
    tn = max(t for t in range(128, min(nout, INPROJ_MAX_COLS) + 1, 128) if nout % t == 0)
    return pl.pallas_call(
        _inproj_kernel,
        grid=(bsz, n // tm, nout // tn),
        in_specs=[pl.BlockSpec((None, tm, d), lambda bi, i, j: (bi, i, 0)),
                  pl.BlockSpec((None, N_MOD, d), lambda bi, i, j: (bi, 0, 0)),
                  pl.BlockSpec((d, tn), lambda bi, i, j: (0, j)),
                  pl.BlockSpec((1, tn), lambda bi, i, j: (0, j))],
        out_specs=pl.BlockSpec((None, tm, tn), lambda bi, i, j: (bi, i, j)),
        out_shape=jax.ShapeDtypeStruct((bsz, n, nout), BF16),
        scratch_shapes=[pltpu.VMEM((tm, d), BF16)],
        compiler_params=_params("arbitrary", "arbitrary", "arbitrary"),
        name="inproj",
    )(x, mod, w, b)


HEAD_PAIR = 2 * HEAD_DIM


def _headnorm_kernel(*refs, n_sect, rope):
    p_refs, g_refs = refs[:n_sect], refs[n_sect:2 * n_sect]
    avg_ref = refs[2 * n_sect]
    if rope:
        swap_ref, cos_ref, sin_ref = refs[2 * n_sect + 1:2 * n_sect + 4]
    o_refs = refs[-n_sect:]
    for p_ref, g_ref, o_ref in zip(p_refs, g_refs, o_refs):
        gain = g_ref[...]
        for hp in range(p_ref.shape[1] // HEAD_PAIR):
            sl = slice(hp * HEAD_PAIR, (hp + 1) * HEAD_PAIR)
            xf = p_ref[:, sl].astype(F32)
            ms = _dot((xf * xf).astype(BF16), avg_ref[...])
            y = xf * lax.rsqrt(ms + RMS_EPS) * gain
            if rope:
                y = y * cos_ref[...] + _dot(y.astype(BF16), swap_ref[...]) * sin_ref[...]
            o_ref[:, sl] = y.astype(o_ref.dtype)


def _head_pair_matrices():
    i, j = _int_grid(HEAD_PAIR, HEAD_PAIR)
    avg = jnp.where(i // HEAD_DIM == j // HEAD_DIM, 1.0 / HEAD_DIM, 0.0)
    half = ROPE_AXIS_DIM // 2
    partner = jnp.where(j % ROPE_AXIS_DIM < half, j + half, j - half)
    return avg.astype(BF16), jnp.where(i == partner, 1.0, 0.0).astype(BF16)


def _headnorm(p, sections, rope_tabs):
    bsz, n, _ = p.shape
    tm = min(n, 1024)
    rope = rope_tabs is not None
    avg, swap = _head_pair_matrices()
    const = lambda shape: pl.BlockSpec(shape, lambda bi, i: (0, 0))
    widths = [nh * HEAD_DIM for _, nh, _, _ in sections]
    in_specs = [pl.BlockSpec((None, tm, w), functools.partial(lambda bi, i, c: (bi, i, c), c=cb))
                for w, (cb, _, _, _) in zip(widths, sections)]
    in_specs += [const((1, HEAD_PAIR))] * len(sections) + [const((HEAD_PAIR, HEAD_PAIR))]
    args = [p] * len(sections) + [jnp.tile(g * s, (1, 2)) for _, _, g, s in sections] + [avg]
    if rope:
        in_specs += [const((HEAD_PAIR, HEAD_PAIR))] + [pl.BlockSpec((tm, HEAD_PAIR), lambda bi, i: (i, 0))] * 2
        args += [swap] + list(rope_tabs)
    return pl.pallas_call(
        functools.partial(_headnorm_kernel, n_sect=len(sections), rope=rope),
        grid=(bsz, n // tm),
        in_specs=in_specs,
        out_specs=[pl.BlockSpec((None, tm, w), lambda bi, i: (bi, i, 0)) for w in widths],
        out_shape=[jax.ShapeDtypeStruct((bsz, n, w), BF16) for w in widths],
        compiler_params=_params("arbitrary", "arbitrary"),
        name="headnorm",
    )(*args)


def _attn_kernel(*refs, n_parts):
    q_ref = refs[0]
    kv_refs = refs[1:1 + 2 * n_parts]
    o_ref = refs[1 + 2 * n_parts]
    for g in range(N_KV_HEADS):
        gsl = slice(g * HEAD_DIM, (g + 1) * HEAD_DIM)
        ks = [kv_refs[2 * i][:, gsl] for i in range(n_parts)]
        vs = [jnp.concatenate([kv_refs[2 * i + 1][:, gsl], jnp.ones((k.shape[0], HEAD_DIM), BF16)], 1)
              for i, k in enumerate(ks)]
        for r in range(GQA_GROUP):
            h = g * GQA_GROUP + r
            hsl = slice(h * HEAD_DIM, (h + 1) * HEAD_DIM)
            q = q_ref[:, hsl]
            ss = [lax.dot_general(q, k, (((1,), (1,)), ((), ())), preferred_element_type=F32) for k in ks]
            m = jnp.max(ss[0], -1, keepdims=True)
            for s in ss[1:]:
                m = jnp.maximum(m, jnp.max(s, -1, keepdims=True))
            o = None
            for s, v in zip(ss, vs):
                pv = _dot(jnp.exp((s - m).astype(BF16)), v)
                o = pv if o is None else o + pv
            o_ref[:, hsl] = (o[:, :HEAD_DIM] / o[:, HEAD_DIM:]).astype(o_ref.dtype)


def _attention(q, parts):
    bsz, n, w = q.shape
    tq = min(n, 512)
    in_specs = [pl.BlockSpec((None, tq, w), lambda bi, i: (bi, i, 0))]
    args = [q]
    for k, v, vblk in parts:
        nk = k.shape[1]
        in_specs.append(pl.BlockSpec((None, nk, KV_W), lambda bi, i: (bi, 0, 0)))
        in_specs.append(pl.BlockSpec((None, nk, KV_W), functools.partial(lambda bi, i, c: (bi, 0, c), c=vblk)))
        args += [k, v]
    return pl.pallas_call(
        functools.partial(_attn_kernel, n_parts=len(parts)),
        grid=(bsz, n // tq),
        in_specs=in_specs,
        out_specs=pl.BlockSpec((None, tq, w), lambda bi, i: (bi, i, 0)),
        out_shape=jax.ShapeDtypeStruct((bsz, n, w), BF16),
        compiler_params=_params("arbitrary", "arbitrary"),
        name="attention",
    )(*args)


def _filter_kernel(emb_ref, w1_ref, b1_ref, f1_ref, w2_ref, b2_ref, f2_ref, w3f_ref, w3b_ref, dl_ref,
                   hs_ref, hd_ref, mid_ref, h_ref, *, n):
    hh = n // 2

    @pl.when(pl.program_id(0) == 0)
    def _():
        h1 = jnp.sin(f1_ref[...] * (_dot(emb_ref[...], w1_ref[...]) + b1_ref[...]))
        h_ref[...] = jnp.sin(f2_ref[...] * (_dot(h1, w2_ref[...]) + b2_ref[...]))

    h = h_ref[...]
    hf = _dot(h, w3f_ref[...])
    hb = _dot(h, w3b_ref[...])
    row = lax.broadcasted_iota(jnp.int32, hf.shape, 0)
    lag = jnp.where(row < hh, 2 * row, 2 * (row - hh) + 1)
    t = lag.astype(F32) * (1.0 / (n - 1))
    dec = jnp.exp(-t * dl_ref[...])
    hf = hf * dec
    hb = jnp.where(lag == 0, 0.0, hb * dec)
    ssq = jnp.sum(hf * hf, 0, keepdims=True) + jnp.sum(hb * hb, 0, keepdims=True)
    nrm = lax.rsqrt(ssq + KERN_EPS) * (1.0 / n)
    hs = (hf + hb) * nrm
    hd = (hf - hb) * nrm
    hs_ref[...] = hs.astype(hs_ref.dtype)
    hd_ref[...] = hd.astype(hd_ref.dtype)
    sign = jnp.where(row % 2 == 0, 1.0, -1.0)
    mid_ref[0:1, :] = jnp.sum(jnp.where(row < hh, hs * sign, 0.0), 0, keepdims=True)
    mid_ref[1:2, :] = jnp.sum(jnp.where(row < hh, 0.0, hd * sign), 0, keepdims=True)


def _filters(n, emb, w1, b1, f1, w2, b2, f2, w3, absdelta):
    od = HYENA_ORDER * D_HYENA
    tn = 512
    fh = w2.shape[0]
    ke = emb.shape[1]
    full = lambda shape: pl.BlockSpec(shape, lambda j: (0, 0))
    return pl.pallas_call(
        functools.partial(_filter_kernel, n=n),
        grid=(od // tn,),
        in_specs=[full((n, ke)), full((ke, fh)), full((1, fh)), full((1, fh)),
                  full((fh, fh)), full((1, fh)), full((1, fh)),
                  pl.BlockSpec((fh, tn), lambda j: (0, j)),
                  pl.BlockSpec((fh, tn), lambda j: (0, j + od // tn)),
                  pl.BlockSpec((1, tn), lambda j: (0, j))],
        out_specs=[pl.BlockSpec((n, tn), lambda j: (0, j)),
                   pl.BlockSpec((n, tn), lambda j: (0, j)),
                   pl.BlockSpec((2, tn), lambda j: (0, j))],
        out_shape=[jax.ShapeDtypeStruct((n, od), BF16), jax.ShapeDtypeStruct((n, od), BF16),
                   jax.ShapeDtypeStruct((2, od), F32)],
        scratch_shapes=[pltpu.VMEM((n, fh), F32)],
        compiler_params=_params("arbitrary"),
        name="hyena_filters",
    )(emb, w1, b1, f1, w2, b2, f2, w3, w3, absdelta)


def _spectrum_kernel(fe_re, fe_im, fo_re, fo_im, hse, hso, hde, hdo, k4_ref):
    p = _dot(fe_re[...], hse[...])
    q = _dot(fo_re[...], hso[...])
    pp = _dot(fe_im[...], hde[...])
    qp = _dot(fo_im[...], hdo[...])
    row0 = (lax.broadcasted_iota(jnp.int32, p.shape, 0) + pl.program_id(0) * p.shape[0]) == 0
    k4_ref[0] = jnp.where(row0, 0.5 * (p + q), p + q)
    k4_ref[1] = jnp.where(row0, 0.0, pp + qp)
    k4_ref[2] = jnp.where(row0, 0.5 * (p - q), p - q)
    k4_ref[3] = jnp.where(row0, 0.0, qp - pp)


def _spectrum(mfe, mfo, hs, hd):
    n, od = hs.shape
    h = n // 2
    tm = min(h, 512)
    tn = 512
    nb = h // tm
    mat = lambda off: pl.BlockSpec((tm, h), functools.partial(lambda i, j, o: (i + o, 0), o=off))
    half = lambda a: pl.BlockSpec((h, tn), functools.partial(lambda i, j, a: (a, j), a=a))
    return pl.pallas_call(
        _spectrum_kernel,
        grid=(nb, od // tn),
        in_specs=[mat(0), mat(nb), mat(0), mat(nb), half(0), half(1), half(0), half(1)],
        out_specs=pl.BlockSpec((4, tm, tn), lambda i, j: (0, i, j)),
        out_shape=jax.ShapeDtypeStruct((4, h, od), F32),
        compiler_params=_params("arbitrary", "arbitrary"),
        name="hyena_spectrum",
    )(mfe, mfe, mfo, mfo, hs, hs, hd, hd)


def _short_conv(x, w_ref, b_ref):
    hh = x.shape[0] // 2
    e = x[:hh]
    o = x[hh:]
    row = lax.broadcasted_iota(jnp.int32, e.shape, 0)
    o_prev = jnp.where(row == 0, 0.0, pltpu.roll(o, 1, 0))
    e_next = jnp.where(row == hh - 1, 0.0, pltpu.roll(e, hh - 1, 0))
    w0, w1, w2 = w_ref[0:1, :], w_ref[1:2, :], w_ref[2:3, :]
    return (b_ref[...] + o_prev * w0 + e * w1 + o * w2,
            b_ref[...] + e * w0 + o * w1 + e_next * w2)


def _hyconv_kernel(*refs, conv_z, cw, n):
    if conv_z:
        z_ref, cwz_ref, cbz_ref = refs[:3]
        refs = refs[3:]
    else:
        z_ref = refs[0]
        refs = refs[1:]
    x_ref, cwx_ref, cbx_ref, mfe_ref, mfo_ref, mie_ref, mio_ref, k4_ref, mid_ref, skip_ref, o_ref = refs
    hh = n // 2
    first8 = lax.broadcasted_iota(jnp.int32, (8, cw), 0) == 0
    for c in range(o_ref.shape[1] // cw):
        cs = slice(c * cw, (c + 1) * cw)
        if conv_z:
            ze, zo = _short_conv(z_ref[:, cs].astype(F32), cwz_ref.at[:, cs], cbz_ref.at[:, cs])
            ze, zo = ze.astype(BF16), zo.astype(BF16)
        else:
            ze, zo = z_ref[0:hh, cs], z_ref[hh:n, cs]
        fe = _dot(mfe_ref[...], ze)
        fo = _dot(mfo_ref[...], zo)
        p, pp, q, qp = fe[:hh], fe[hh:], fo[:hh], fo[hh:]
        kra, kia, krb, kib = k4_ref[0, :, cs], k4_ref[1, :, cs], k4_ref[2, :, cs], k4_ref[3, :, cs]
        fra, fia = p + q, pp + qp
        frb, fib = p - q, qp - pp
        wra = fra * kra - fia * kia
        wia = fra * kia + fia * kra
        wrb = frb * krb - fib * kib
        wib = frb * kib + fib * krb
        ge_im = wia - wib
        go_im = wia + wib
        kmr, kmi = mid_ref[0:1, cs], mid_ref[1:2, cs]
        pp8, qp8 = pp[:8], qp[:8]
        ge_im = jnp.concatenate([jnp.where(first8, pp8 * kmr - qp8 * kmi, ge_im[:8]), ge_im[8:]], 0)
        go_im = jnp.concatenate([jnp.where(first8, pp8 * kmi + qp8 * kmr, go_im[:8]), go_im[8:]], 0)
        ge = jnp.concatenate([(wra + wrb).astype(BF16), ge_im.astype(BF16)], 0)
        go = jnp.concatenate([(wra - wrb).astype(BF16), go_im.astype(BF16)], 0)
        ye = _dot(mie_ref[...], ge)
        yo = _dot(mio_ref[...], go)
        xe, xo = _short_conv(x_ref[:, cs].astype(F32), cwx_ref.at[:, cs], cbx_ref.at[:, cs])
        skip = skip_ref[:, cs]
        o_ref[0:hh, cs] = (xe * (ye + ze.astype(F32) * skip)).astype(o_ref.dtype)
        o_ref[hh:n, cs] = (xo * (yo + zo.astype(F32) * skip)).astype(o_ref.dtype)


def _hyconv(order, z_arr, p, conv_w, conv_b, tabs, k4, mid, skip):
    bsz, n, _ = p.shape
    h = n // 2
    tn = 512
    cw = 256
    nct = D_HYENA // tn
    conv_z = z_arr is None
    once = pl.Buffered(1)

    def sect(part):
        off = P_HY // tn + part * nct
        return pl.BlockSpec((None, n, tn), functools.partial(lambda c, bi, o: (bi, 0, o + c), o=off))

    def wsect(rows, part):
        return pl.BlockSpec((rows, tn), functools.partial(lambda c, bi, o: (0, o + c), o=part * nct))

    whole = lambda a: pl.BlockSpec(a.shape, lambda c, bi: (0, 0), pipeline_mode=once)
    ocol = functools.partial(lambda c, bi, o: (0, o + c), o=order * nct)
    if conv_z:
        in_specs = [sect(HYENA_ORDER), wsect(SHORT_CONV, HYENA_ORDER), wsect(1, HYENA_ORDER)]
        args = [p, conv_w, conv_b]
    else:
        in_specs = [pl.BlockSpec((None, n, tn), lambda c, bi: (bi, 0, c))]
        args = [z_arr]
    mats = [tabs['mfe'], tabs['mfo'], tabs['mie'], tabs['mio']]
    in_specs += [sect(order), wsect(SHORT_CONV, order), wsect(1, order)] + [whole(m) for m in mats]
    in_specs += [pl.BlockSpec((4, h, tn), functools.partial(lambda c, bi, o: (0, 0, o + c), o=order * nct),
                              pipeline_mode=once),
                 pl.BlockSpec((2, tn), ocol),
                 pl.BlockSpec((1, tn), ocol)]
    args += [p, conv_w, conv_b] + mats + [k4, mid, skip]
    return pl.pallas_call(
        functools.partial(_hyconv_kernel, conv_z=conv_z, cw=cw, n=n),
        grid=(nct, bsz),
        in_specs=in_specs,
        out_specs=pl.BlockSpec((None, n, tn), lambda c, bi: (bi, 0, c)),
        out_shape=jax.ShapeDtypeStruct((bsz, n, D_HYENA), BF16),
        compiler_params=_params("arbitrary", "arbitrary"),
        name="hyena_conv",
    )(*args)


def _fnet_kernel(u_ref, csc_ref, mne_ref, mno_ref, o_ref, *, n, scale):
    hh = n // 2
    qq = n // 4
    for g in range(FNET_GROUPS):
        gsl = slice(g * FNET_GROUP_DIM, (g + 1) * FNET_GROUP_DIM)
        t = _dot(u_ref[:, gsl], csc_ref[...])
        tc = t[:, :FNET_GROUP_DIM].astype(BF16)
        ts = t[:, FNET_GROUP_DIM:].astype(BF16)
        a = _dot(mne_ref[...], jnp.concatenate([tc[:hh], ts[:hh]], 0))
        b = _dot(mno_ref[...], jnp.concatenate([tc[hh:], ts[hh:]], 0))
        lo = ((a + b) * scale).astype(o_ref.dtype)
        hi = ((a - b) * scale).astype(o_ref.dtype)
        o_ref[0:qq, gsl] = lo[:qq]
        o_ref[qq:hh, gsl] = hi[:qq]
        o_ref[hh:hh + qq, gsl] = lo[qq:]
        o_ref[hh + qq:n, gsl] = hi[qq:]


def _fnet(p, csc, mne, mno):
    bsz, n, _ = p.shape
    scale = 1.0 / math.sqrt(n * FNET_GROUP_DIM)
    whole = lambda a: pl.BlockSpec(a.shape, lambda bi: (0, 0), pipeline_mode=pl.Buffered(1))
    return pl.pallas_call(
        functools.partial(_fnet_kernel, n=n, scale=scale),
        grid=(bsz,),
        in_specs=[pl.BlockSpec((None, n, D_MODEL), lambda bi: (bi, 0, P_FN // D_MODEL)),
                  whole(csc), whole(mne), whole(mno)],
        out_specs=pl.BlockSpec((None, n, D_MODEL), lambda bi: (bi, 0, 0)),
        out_shape=jax.ShapeDtypeStruct((bsz, n, D_MODEL), BF16),
        compiler_params=_params("arbitrary"),
        name="fnet",
    )(p, csc, mne, mno)


MERGE_ROW_CHUNK = 256


def _merge_kernel(oa_ref, hy_ref, fn_ref, ga_ref, gh_ref, gf_ref, x_ref, mod_ref,
                  wa_ref, wh_ref, wf_ref, wo_ref, bo_ref, lg_ref, lb_ref, o_ref):
    rc = min(MERGE_ROW_CHUNK, x_ref.shape[0])
    for r0 in range(0, x_ref.shape[0], rc):
        rs = slice(r0, r0 + rc)
        m = _sigmoid(ga_ref[rs, :].astype(F32)) * _dot(oa_ref[rs, :], wa_ref[...])
        m = m + _sigmoid(gh_ref[rs, :].astype(F32)) * _dot(hy_ref[rs, :], wh_ref[...])
        m = m + _sigmoid(gf_ref[rs, :].astype(F32)) * _dot(fn_ref[rs, :], wf_ref[...])
        y = _dot(m.astype(BF16), wo_ref[...]) + bo_ref[...]
        r = ALPHA * x_ref[rs, :] + mod_ref[2:3, :] * y
        o_ref[rs, :] = _layer_norm(r, lg_ref[...], lb_ref[...])


def _merge(o_att, z_hy, y_fn, p, x, mod, wa, wh, wf, wo, bo, lg, lb):
    bsz, n, d = x.shape
    tm = min(n, 512)
    tok = lambda c: pl.BlockSpec((None, tm, d), functools.partial(lambda bi, i, c: (bi, i, c), c=c))
    wspec = pl.BlockSpec((d, d), lambda bi, i: (0, 0))
    vspec = pl.BlockSpec((1, d), lambda bi, i: (0, 0))
    g0 = P_G // d
    return pl.pallas_call(
        _merge_kernel,
        grid=(bsz, n // tm),
        in_specs=[tok(0), tok(0), tok(0), tok(g0), tok(g0 + 1), tok(g0 + 2), tok(0),
                  pl.BlockSpec((None, N_MOD, d), lambda bi, i: (bi, 0, 0)),
                  wspec, wspec, wspec, wspec, vspec, vspec, vspec],
        out_specs=tok(0),
        out_shape=jax.ShapeDtypeStruct((bsz, n, d), F32),
        compiler_params=_params("arbitrary", "arbitrary"),
        name="merge_ln",
    )(o_att, z_hy, y_fn, p, p, p, x, mod, wa, wh, wf, wo, bo, lg, lb)


MLP_ROW_CHUNK = 512
MLP_FF_CHUNK = 1024


def _mlp_kernel(x_ref, mod_ref, w1_ref, b1_ref, w2_ref, b2_ref, lg_ref, lb_ref, o_ref):
    ff = w1_ref.shape[1]
    rc = min(MLP_ROW_CHUNK, x_ref.shape[0])
    for r0 in range(0, x_ref.shape[0], rc):
        rs = slice(r0, r0 + rc)
        x = x_ref[rs, :]
        h = (x * (1.0 + mod_ref[4:5, :]) + mod_ref[3:4, :]).astype(BF16)
        y = b2_ref[...]
        for c0 in range(0, ff, MLP_FF_CHUNK):
            cs = slice(c0, c0 + MLP_FF_CHUNK)
            a = jnp.maximum(_dot(h, w1_ref[:, cs]) + b1_ref[:, cs], 0.0)
            y = y + _dot((a * a).astype(BF16), w2_ref[cs, :])
        r = ALPHA * x + mod_ref[5:6, :] * y
        o_ref[rs, :] = _layer_norm(r, lg_ref[...], lb_ref[...])


def _mlp(x, mod, w1, b1, w2, b2, lg, lb):
    bsz, n, d = x.shape
    tm = min(n, 2 * MLP_ROW_CHUNK)
    whole = lambda a: pl.BlockSpec(a.shape, lambda bi, i: (0, 0), pipeline_mode=pl.Buffered(1))
    return pl.pallas_call(
        _mlp_kernel,
        grid=(bsz, n // tm),
        in_specs=[pl.BlockSpec((None, tm, d), lambda bi, i: (bi, i, 0)),
                  pl.BlockSpec((None, N_MOD, d), lambda bi, i: (bi, 0, 0)),
                  whole(w1), whole(b1), whole(w2), whole(b2), whole(lg), whole(lb)],
        out_specs=pl.BlockSpec((None, tm, d), lambda bi, i: (bi, i, 0)),
        out_shape=jax.ShapeDtypeStruct((bsz, n, d), F32),
        compiler_params=_params("arbitrary", "arbitrary"),
        name="mlp_ln",
    )(x, mod, w1, b1, w2, b2, lg, lb)


def _int_grid(rows, cols):
    k = lax.broadcasted_iota(jnp.int32, (rows, cols), 0)
    s = lax.broadcasted_iota(jnp.int32, (rows, cols), 1)
    return k, s


TRIG_ROW_BLOCK = 32


def _trig_table(n_rows, samp, period):
    def base(kv):
        ang = ((kv[:, None] * samp[None, :]) % period).astype(F32) * (2.0 * math.pi / period)
        return jnp.cos(ang), jnp.sin(ang)

    ch, sh = base(jnp.arange(n_rows // TRIG_ROW_BLOCK, dtype=jnp.int32) * TRIG_ROW_BLOCK)
    cl, sl = base(jnp.arange(TRIG_ROW_BLOCK, dtype=jnp.int32))
    ch, sh, cl, sl = ch[:, None, :], sh[:, None, :], cl[None], sl[None]
    shape = (n_rows, samp.shape[0])
    return (ch * cl - sh * sl).reshape(shape), (sh * cl + ch * sl).reshape(shape)


def _hyena_dft(n):
    h = n // 2
    s = jnp.arange(h, dtype=jnp.int32)
    alt = jnp.where(s % 2 == 0, 1.0, -1.0)[None, :]
    first = lax.broadcasted_iota(jnp.int32, (h, h), 0) == 0
    out = {}
    for name, samp in (('e', 2 * s), ('o', 2 * s + 1)):
        cos, sin = _trig_table(h, samp, 2 * n)
        mf = jnp.concatenate([cos, jnp.where(first, alt, sin)], 0)
        out['mf' + name] = mf.astype(BF16)
        out['mi' + name] = mf.T.astype(BF16)
    return out


def _fnet_dft(n):
    h = n // 2
    s = jnp.arange(h, dtype=jnp.int32)
    out = {}
    for name, samp in (('e', 2 * s), ('o', 2 * s + 1)):
        cos, sin = _trig_table(h, samp, n)
        out['mn' + name] = _even_odd_rows(jnp.concatenate([cos, -sin], 1)).astype(BF16)
    k, s = _int_grid(FNET_GROUP_DIM, FNET_GROUP_DIM)
    ang = ((k * s) % FNET_GROUP_DIM).astype(F32) * (2.0 * math.pi / FNET_GROUP_DIM)
    out['csc'] = jnp.concatenate([jnp.cos(ang), jnp.sin(ang)], 1).astype(BF16)
    return out


def _filter_embedding(n, width):
    t = jnp.linspace(0.0, 1.0, n, dtype=F32)[:, None]
    w = (2.0 * math.pi / n) * jnp.arange(n, dtype=F32)[:, None]
    f = jnp.linspace(1e-4, HYENA_BANDS - 1, HYENA_BANDS, dtype=F32)[None, :]
    emb = jnp.concatenate([t, jnp.cos(f * w), -jnp.sin(f * w)], -1)
    return _even_odd_rows(jnp.pad(emb, ((0, 0), (0, width - HYENA_EMB))))


def _rope_tables(n):
    rows = n // GRID_W
    row = jnp.repeat(jnp.arange(rows, dtype=F32), GRID_W)
    col = jnp.tile(jnp.arange(GRID_W, dtype=F32), rows)
    inv_freq = ROPE_THETA ** (-jnp.arange(ROPE_AXIS_DIM // 2, dtype=F32) * 2.0 / ROPE_AXIS_DIM)
    ar = row[:, None] * inv_freq[None, :]
    ac = col[:, None] * inv_freq[None, :]
    cos = jnp.concatenate([jnp.cos(ar), jnp.cos(ar), jnp.cos(ac), jnp.cos(ac)] * 2, -1)
    sin = jnp.concatenate([-jnp.sin(ar), jnp.sin(ar), -jnp.sin(ac), jnp.sin(ac)] * 2, -1)
    return _even_odd_rows(cos), _even_odd_rows(sin)


def _permute_in_cols(a):
    return jnp.concatenate([a[..., OFF_Q:OFF_K], a[..., OFF_G:D_IN], a[..., OFF_FN:OFF_G],
                            a[..., OFF_K:OFF_V], a[..., OFF_V:OFF_HY], a[..., OFF_HY:OFF_FN]], -1)


def _mixer(p_tok, seq_shape, lw, tabs, rope_tabs, ctx_parts):
    bsz, n = seq_shape
    p = p_tok.reshape(bsz, n, p_tok.shape[-1])
    q, k = _headnorm(p, [(P_Q // Q_W, N_Q_HEADS, lw['q_gain'], ATTN_SCALE),
                         (P_K // KV_W, N_KV_HEADS, lw['k_gain'], 1.0)], rope_tabs)
    o_att = _attention(q, [(k, p, P_V // KV_W)] + ctx_parts)

    hs, hd, mid = _filters(n, tabs['emb'], lw['hy_w1'], lw['hy_b1'], lw['hy_freq1'], lw['hy_w2'], lw['hy_b2'],
                           lw['hy_freq2'], lw['hy_w3'], tabs['absdelta'])
    k4 = _spectrum(tabs['mfe'], tabs['mfo'], hs, hd)
    z = None
    for o in range(HYENA_ORDER):
        z = _hyconv(o, z, p, lw['conv_w'], lw['conv_b'], tabs, k4, mid, lw['hy_skip'])
    y_fn = _fnet(p, tabs['csc'], tabs['mne'], tabs['mno'])
    tok = lambda a: a.reshape(p_tok.shape[0], p_tok.shape[1], a.shape[-1])
    return tok(o_att), tok(z), tok(y_fn), k, p


def _block(x_tok, mod, seq_shape, lw, tabs, rope_tabs, ctx_parts):
    p_tok = _inproj(x_tok, mod, lw['w_in'], lw['b_in'])
    o_att, z_hy, y_fn, k, p = _mixer(p_tok, seq_shape, lw, tabs, rope_tabs, ctx_parts)
    x_tok = _merge(o_att, z_hy, y_fn, p_tok, x_tok, mod, lw['w_att_o'], lw['w_hy_o'], lw['w_fn_o'],
                   lw['w_out'], lw['b_out'], lw['ln1_g'], lw['ln1_b'])
    x_tok = _mlp(x_tok, mod, lw['w_mlp1'], lw['b_mlp1'], lw['w_mlp2'], lw['b_mlp2'], lw['ln2_g'], lw['ln2_b'])
    return x_tok, k, p


def _seq_tables(n):
    tabs = dict(_hyena_dft(n))
    tabs.update(_fnet_dft(n))
    deltas = jnp.abs(jnp.linspace(MIN_DECAY, MAX_DECAY, D_HYENA, dtype=F32))
    tabs.update(emb=_filter_embedding(n, 128), absdelta=jnp.tile(deltas, HYENA_ORDER)[None, :])
    return tabs


def kernel(x, c, ctx, c_ctx, w_ada, b_ada, w_in, b_in, conv_w, conv_b, hy_w1, hy_b1, hy_freq1, hy_w2, hy_b2,
           hy_freq2, hy_w3, hy_skip, q_gain, k_gain, w_att_o, w_hy_o, w_fn_o, w_out, b_out, ln1_g, ln1_b,
           w_mlp1, b_mlp1, w_mlp2, b_mlp2, ln2_g, ln2_b):
    bsz, n_lat, d = x.shape
    n_ctx = ctx.shape[1]
    depth = w_ada.shape[0]
    tabs_lat = _seq_tables(n_lat)
    tabs_ctx = _seq_tables(n_ctx)
    rope_tabs = _rope_tables(n_lat)

    n_rows = -(-(bsz + 1) // 16) * 16
    cond = jnp.concatenate([c, c_ctx[None, :], jnp.zeros((n_rows - bsz - 1, d), F32)], 0)

    x = _deinterleave(x)
    ctx_tok = _deinterleave(ctx).reshape(1, bsz * n_ctx, d)
    row = lambda a: a[None, :]
    mods = _matmul(cond, w_ada, b_ada[:, None, :], silu=True)
    for i in range(depth):
        last = i == depth - 1
        lw = dict(
            w_in=_permute_in_cols(w_in[i]).astype(BF16), b_in=row(_permute_in_cols(b_in[i])),
            conv_w=conv_w[i], conv_b=row(conv_b[i]),
            hy_w1=jnp.pad(hy_w1[i], ((0, 128 - HYENA_EMB), (0, 0))), hy_b1=row(hy_b1[i]), hy_freq1=row(hy_freq1[i]),
            hy_w2=hy_w2[i], hy_b2=row(hy_b2[i]), hy_freq2=row(hy_freq2[i]), hy_w3=hy_w3[i],
            hy_skip=hy_skip[i].reshape(1, HYENA_ORDER * D_HYENA),
            q_gain=row(q_gain[i]), k_gain=row(k_gain[i]),
            w_att_o=w_att_o[i].astype(BF16), w_hy_o=w_hy_o[i].astype(BF16), w_fn_o=w_fn_o[i].astype(BF16),
            w_out=w_out[i].astype(BF16), b_out=row(b_out[i]), ln1_g=row(ln1_g[i]), ln1_b=row(ln1_b[i]),
            w_mlp1=w_mlp1[i].astype(BF16), b_mlp1=row(b_mlp1[i]), w_mlp2=w_mlp2[i].astype(BF16),
            b_mlp2=row(b_mlp2[i]), ln2_g=row(ln2_g[i]), ln2_b=row(ln2_b[i]))
        mod_l = mods[i, :bsz].reshape(bsz, N_MOD, d)
        mod_c = mods[i, bsz:bsz + 1].reshape(1, N_MOD, d)

        if last:
            kv_cols = slice(P_K, P_V + KV_W)
            p_c = _inproj(ctx_tok, mod_c, lw['w_in'][:, kv_cols], lw['b_in'][:, kv_cols])
            p_c = p_c.reshape(bsz, n_ctx, 2 * KV_W)
            k_c, = _headnorm(p_c, [(0, N_KV_HEADS, lw['k_gain'], 1.0)], None)
            v_blk = 1
        else:
            ctx_tok, k_c, p_c = _block(ctx_tok, mod_c, (bsz, n_ctx), lw, tabs_ctx, None, [])
            v_blk = P_V // KV_W
        x, _, _ = _block(x, mod_l, (bsz, n_lat), lw, tabs_lat, rope_tabs, [(k_c, p_c, v_blk)])
    return _interleave(x)
```

```python
import functools
import math

import jax
import jax.numpy as jnp
from jax import lax
from jax.experimental import pallas as pl
from jax.experimental.pallas import tpu as pltpu

F32 = jnp.float32
BF16 = jnp.bfloat16

D_MODEL = 1024
GRID_W = 64
HEAD_DIM = 128
N_Q_HEADS = D_MODEL // HEAD_DIM
N_KV_HEADS = 2
GQA_GROUP = N_Q_HEADS // N_KV_HEADS
ROPE_THETA = 10000.0
ROPE_AXIS_DIM = HEAD_DIM // 2
ATTN_SCALE = HEAD_DIM ** -0.5

D_HYENA = D_MODEL
HYENA_ORDER = 2
HYENA_BANDS = 16
HYENA_EMB = 1 + 2 * HYENA_BANDS
SHORT_CONV = 3
DECAY_TARGET = 1e-2
MIN_DECAY = math.log(DECAY_TARGET) / 0.3
MAX_DECAY = math.log(DECAY_TARGET) / 1.5

FNET_GROUPS = 4
FNET_GROUP_DIM = D_MODEL // FNET_GROUPS
D_FF = 4 * D_MODEL
N_MOD = 6
DEPTH = 4

Q_W = N_Q_HEADS * HEAD_DIM
KV_W = N_KV_HEADS * HEAD_DIM
HY_W = (HYENA_ORDER + 1) * D_HYENA
OFF_Q = 0
OFF_K = OFF_Q + Q_W
OFF_V = OFF_K + KV_W
OFF_HY = OFF_V + KV_W
OFF_FN = OFF_HY + HY_W
OFF_G = OFF_FN + D_MODEL
D_IN = OFF_G + 3 * D_MODEL

P_Q = 0
P_G = P_Q + Q_W
P_FN = P_G + 3 * D_MODEL
P_K = P_FN + D_MODEL
P_V = P_K + KV_W
P_HY = P_V + KV_W

ALPHA = (2 * DEPTH) ** 0.25
LN_EPS = 1e-6
RMS_EPS = 1e-6
KERN_EPS = 1e-6

V7X_VMEM_LIMIT_BYTES = 56 * 1024 * 1024


def _params(*sem):
    return pltpu.CompilerParams(dimension_semantics=sem, vmem_limit_bytes=V7X_VMEM_LIMIT_BYTES)


def _dot(a, b):
    return jnp.dot(a, b, preferred_element_type=F32)


def _layer_norm(r, g, b):
    mu = jnp.mean(r, -1, keepdims=True)
    d = r - mu
    var = jnp.mean(d * d, -1, keepdims=True)
    return d * lax.rsqrt(var + LN_EPS) * g + b


def _sigmoid(x):
    return 1.0 / (1.0 + jnp.exp(-x))


def _deinterleave_kernel(x_ref, o_ref):
    hh = x_ref.shape[1] // 2
    o_ref[:, 0:hh, :] = x_ref[:, pl.ds(0, hh, stride=2), :]
    o_ref[:, hh:2 * hh, :] = x_ref[:, pl.ds(1, hh, stride=2), :]


def _interleave_kernel(x_ref, o_ref):
    hh = x_ref.shape[1] // 2
    o_ref[:, pl.ds(0, hh, stride=2), :] = x_ref[:, 0:hh, :]
    o_ref[:, pl.ds(1, hh, stride=2), :] = x_ref[:, hh:2 * hh, :]


REORDER_BLOCK_ROWS = 2048


def _reorder_tokens(x, body, name):
    bsz, n, d = x.shape
    tc = 128
    bb = min(bsz, max(1, REORDER_BLOCK_ROWS // n))
    spec = pl.BlockSpec((bb, n, tc), lambda bi, c: (bi, 0, c))
    return pl.pallas_call(
        body,
        grid=(bsz // bb, d // tc),
        in_specs=[spec],
        out_specs=spec,
        out_shape=jax.ShapeDtypeStruct(x.shape, x.dtype),
        compiler_params=_params("arbitrary", "arbitrary"),
        name=name,
    )(x)


def _deinterleave(x):
    return _reorder_tokens(x, _deinterleave_kernel, "deinterleave")


def _interleave(x):
    return _reorder_tokens(x, _interleave_kernel, "interleave")


def _even_odd_rows(a):
    return jnp.concatenate([a[0::2], a[1::2]], 0)


def _matmul_kernel(a_ref, b_ref, bias_ref, o_ref, *, silu):
    a = a_ref[...]
    if silu:
        a = a * _sigmoid(a)
    o_ref[...] = _dot(a.astype(BF16), b_ref[...].astype(BF16)) + bias_ref[...]


def _matmul(a, b, bias, *, silu=False, tn=1024):
    m, k = a.shape
    nl, _, n = b.shape
    return pl.pallas_call(
        functools.partial(_matmul_kernel, silu=silu),
        grid=(nl, n // tn),
        in_specs=[pl.BlockSpec((m, k), lambda l, j: (0, 0)),
                  pl.BlockSpec((None, k, tn), lambda l, j: (l, 0, j)),
                  pl.BlockSpec((None, 1, tn), lambda l, j: (l, 0, j))],
        out_specs=pl.BlockSpec((None, m, tn), lambda l, j: (l, 0, j)),
        out_shape=jax.ShapeDtypeStruct((nl, m, n), F32),
        compiler_params=_params("arbitrary", "arbitrary"),
        name="ada_matmul",
    )(a, b, bias)


def _inproj_kernel(x_ref, mod_ref, w_ref, b_ref, o_ref, h_ref):
    @pl.when(pl.program_id(2) == 0)
    def _():
        h_ref[...] = (x_ref[...] * (1.0 + mod_ref[1:2, :]) + mod_ref[0:1, :]).astype(BF16)

    o_ref[...] = (_dot(h_ref[...], w_ref[...]) + b_ref[...]).astype(o_ref.dtype)


INPROJ_MAX_COLS = 2176


def _inproj(x, mod, w, b, layer):
    bsz, n, d = x.shape
    nout = w.shape[2]
    tm = min(n, 1024)
    tn = max(t for t in range(128, min(nout, INPROJ_MAX_COLS) + 1, 128) if nout % t == 0)
    return pl.pallas_call(
        _inproj_kernel,
        grid=(bsz, n // tm, nout // tn),
        in_specs=[pl.BlockSpec((None, tm, d), lambda bi, i, j: (bi, i, 0)),
                  pl.BlockSpec((None, N_MOD, d), lambda bi, i, j: (bi, 0, 0)),
                  pl.BlockSpec((None, d, tn), lambda bi, i, j: (layer, 0, j)),
                  pl.BlockSpec((None, 1, tn), lambda bi, i, j: (layer, 0, j))],
        out_specs=pl.BlockSpec((None, tm, tn), lambda bi, i, j: (bi, i, j)),
        out_shape=jax.ShapeDtypeStruct((bsz, n, nout), BF16),
        scratch_shapes=[pltpu.VMEM((tm, d), BF16)],
        compiler_params=_params("arbitrary", "arbitrary", "arbitrary"),
        name="inproj",
    )(x, mod, w, b)


HEAD_PAIR = 2 * HEAD_DIM


def _headnorm_kernel(*refs, n_sect, rope):
    p_refs, g_refs = refs[:n_sect], refs[n_sect:2 * n_sect]
    avg_ref = refs[2 * n_sect]
    if rope:
        swap_ref, cos_ref, sin_ref = refs[2 * n_sect + 1:2 * n_sect + 4]
    o_refs = refs[-n_sect:]
    for p_ref, g_ref, o_ref in zip(p_refs, g_refs, o_refs):
        gain = g_ref[...]
        for hp in range(p_ref.shape[1] // HEAD_PAIR):
            sl = slice(hp * HEAD_PAIR, (hp + 1) * HEAD_PAIR)
            xf = p_ref[:, sl].astype(F32)
            ms = _dot((xf * xf).astype(BF16), avg_ref[...])
            y = xf * lax.rsqrt(ms + RMS_EPS) * gain
            if rope:
                y = y * cos_ref[...] + _dot(y.astype(BF16), swap_ref[...]) * sin_ref[...]
            o_ref[:, sl] = y.astype(o_ref.dtype)


def _head_pair_matrices():
    i, j = _int_grid(HEAD_PAIR, HEAD_PAIR)
    avg = jnp.where(i // HEAD_DIM == j // HEAD_DIM, 1.0 / HEAD_DIM, 0.0)
    half = ROPE_AXIS_DIM // 2
    partner = jnp.where(j % ROPE_AXIS_DIM < half, j + half, j - half)
    return avg.astype(BF16), jnp.where(i == partner, 1.0, 0.0).astype(BF16)


def _headnorm(p, sections, rope_tabs):
    bsz, n, _ = p.shape
    tm = min(n, 1024)
    rope = rope_tabs is not None
    avg, swap = _head_pair_matrices()
    const = lambda shape: pl.BlockSpec(shape, lambda bi, i: (0, 0))
    widths = [nh * HEAD_DIM for _, nh, _, _ in sections]
    in_specs = [pl.BlockSpec((None, tm, w), functools.partial(lambda bi, i, c: (bi, i, c), c=cb))
                for w, (cb, _, _, _) in zip(widths, sections)]
    in_specs += [const((1, HEAD_PAIR))] * len(sections) + [const((HEAD_PAIR, HEAD_PAIR))]
    args = [p] * len(sections) + [jnp.tile(g * s, (1, 2)) for _, _, g, s in sections] + [avg]
    if rope:
        in_specs += [const((HEAD_PAIR, HEAD_PAIR))] + [pl.BlockSpec((tm, HEAD_PAIR), lambda bi, i: (i, 0))] * 2
        args += [swap] + list(rope_tabs)
    return pl.pallas_call(
        functools.partial(_headnorm_kernel, n_sect=len(sections), rope=rope),
        grid=(bsz, n // tm),
        in_specs=in_specs,
        out_specs=[pl.BlockSpec((None, tm, w), lambda bi, i: (bi, i, 0)) for w in widths],
        out_shape=[jax.ShapeDtypeStruct((bsz, n, w), BF16) for w in widths],
        compiler_params=_params("arbitrary", "arbitrary"),
        name="headnorm",
    )(*args)


def _attn_kernel(*refs, n_parts):
    q_ref = refs[0]
    kv_refs = refs[1:1 + 2 * n_parts]
    o_ref = refs[1 + 2 * n_parts]
    for g in range(N_KV_HEADS):
        gsl = slice(g * HEAD_DIM, (g + 1) * HEAD_DIM)
        ks = [kv_refs[2 * i][:, gsl] for i in range(n_parts)]
        vs = [jnp.concatenate([kv_refs[2 * i + 1][:, gsl], jnp.ones((k.shape[0], HEAD_DIM), BF16)], 1)
              for i, k in enumerate(ks)]
        for r in range(GQA_GROUP):
            h = g * GQA_GROUP + r
            hsl = slice(h * HEAD_DIM, (h + 1) * HEAD_DIM)
            q = q_ref[:, hsl]
            ss = [lax.dot_general(q, k, (((1,), (1,)), ((), ())), preferred_element_type=F32) for k in ks]
            m = jnp.max(ss[0], -1, keepdims=True)
            for s in ss[1:]:
                m = jnp.maximum(m, jnp.max(s, -1, keepdims=True))
            o = None
            for s, v in zip(ss, vs):
                pv = _dot(jnp.exp((s - m).astype(BF16)), v)
                o = pv if o is None else o + pv
            o_ref[:, hsl] = (o[:, :HEAD_DIM] / o[:, HEAD_DIM:]).astype(o_ref.dtype)


def _attention(q, parts):
    bsz, n, w = q.shape
    tq = min(n, 512)
    in_specs = [pl.BlockSpec((None, tq, w), lambda bi, i: (bi, i, 0))]
    args = [q]
    for k, v, vblk in parts:
        nk = k.shape[1]
        in_specs.append(pl.BlockSpec((None, nk, KV_W), lambda bi, i: (bi, 0, 0)))
        in_specs.append(pl.BlockSpec((None, nk, KV_W), functools.partial(lambda bi, i, c: (bi, 0, c), c=vblk)))
        args += [k, v]
    return pl.pallas_call(
        functools.partial(_attn_kernel, n_parts=len(parts)),
        grid=(bsz, n // tq),
        in_specs=in_specs,
        out_specs=pl.BlockSpec((None, tq, w), lambda bi, i: (bi, i, 0)),
        out_shape=jax.ShapeDtypeStruct((bsz, n, w), BF16),
        compiler_params=_params("arbitrary", "arbitrary"),
        name="attention",
    )(*args)


def _filter_kernel(emb_ref, w1_ref, b1_ref, f1_ref, w2_ref, b2_ref, f2_ref, w3f_ref, w3b_ref, dl_ref,
                   hs_ref, hd_ref, mid_ref, h_ref, *, n):
    hh = n // 2

    @pl.when(pl.program_id(0) == 0)
    def _():
        h1 = jnp.sin(f1_ref[...] * (_dot(emb_ref[...], w1_ref[...]) + b1_ref[...]))
        h_ref[...] = jnp.sin(f2_ref[...] * (_dot(h1, w2_ref[...]) + b2_ref[...]))

    h = h_ref[...]
    hf = _dot(h, w3f_ref[...])
    hb = _dot(h, w3b_ref[...])
    row = lax.broadcasted_iota(jnp.int32, hf.shape, 0)
    lag = jnp.where(row < hh, 2 * row, 2 * (row - hh) + 1)
    t = lag.astype(F32) * (1.0 / (n - 1))
    dec = jnp.exp(-t * dl_ref[...])
    hf = hf * dec
    hb = jnp.where(lag == 0, 0.0, hb * dec)
    ssq = jnp.sum(hf * hf, 0, keepdims=True) + jnp.sum(hb * hb, 0, keepdims=True)
    nrm = lax.rsqrt(ssq + KERN_EPS) * (1.0 / n)
    hs = (hf + hb) * nrm
    hd = (hf - hb) * nrm
    hs_ref[...] = hs.astype(hs_ref.dtype)
    hd_ref[...] = hd.astype(hd_ref.dtype)
    sign = jnp.where(row % 2 == 0, 1.0, -1.0)
    mid_ref[0:1, :] = jnp.sum(jnp.where(row < hh, hs * sign, 0.0), 0, keepdims=True)
    mid_ref[1:2, :] = jnp.sum(jnp.where(row < hh, 0.0, hd * sign), 0, keepdims=True)


def _filters(n, emb, w1, b1, f1, w2, b2, f2, w3, absdelta):
    od = HYENA_ORDER * D_HYENA
    tn = 512
    fh = w2.shape[0]
    ke = emb.shape[1]
    full = lambda shape: pl.BlockSpec(shape, lambda j: (0, 0))
    return pl.pallas_call(
        functools.partial(_filter_kernel, n=n),
        grid=(od // tn,),
        in_specs=[full((n, ke)), full((ke, fh)), full((1, fh)), full((1, fh)),
                  full((fh, fh)), full((1, fh)), full((1, fh)),
                  pl.BlockSpec((fh, tn), lambda j: (0, j)),
                  pl.BlockSpec((fh, tn), lambda j: (0, j + od // tn)),
                  pl.BlockSpec((1, tn), lambda j: (0, j))],
        out_specs=[pl.BlockSpec((n, tn), lambda j: (0, j)),
                   pl.BlockSpec((n, tn), lambda j: (0, j)),
                   pl.BlockSpec((2, tn), lambda j: (0, j))],
        out_shape=[jax.ShapeDtypeStruct((n, od), BF16), jax.ShapeDtypeStruct((n, od), BF16),
                   jax.ShapeDtypeStruct((2, od), F32)],
        scratch_shapes=[pltpu.VMEM((n, fh), F32)],
        compiler_params=_params("arbitrary"),
        name="hyena_filters",
    )(emb, w1, b1, f1, w2, b2, f2, w3, w3, absdelta)


def _spectrum_kernel(fe_re, fe_im, fo_re, fo_im, hse, hso, hde, hdo, k4_ref):
    p = _dot(fe_re[...], hse[...])
    q = _dot(fo_re[...], hso[...])
    pp = _dot(fe_im[...], hde[...])
    qp = _dot(fo_im[...], hdo[...])
    row0 = (lax.broadcasted_iota(jnp.int32, p.shape, 0) + pl.program_id(0) * p.shape[0]) == 0
    k4_ref[0] = jnp.where(row0, 0.5 * (p + q), p + q)
    k4_ref[1] = jnp.where(row0, 0.0, pp + qp)
    k4_ref[2] = jnp.where(row0, 0.5 * (p - q), p - q)
    k4_ref[3] = jnp.where(row0, 0.0, qp - pp)


def _spectrum(mfe, mfo, hs, hd):
    n, od = hs.shape
    h = n // 2
    tm = min(h, 512)
    tn = 512
    nb = h // tm
    mat = lambda off: pl.BlockSpec((tm, h), functools.partial(lambda i, j, o: (i + o, 0), o=off))
    half = lambda a: pl.BlockSpec((h, tn), functools.partial(lambda i, j, a: (a, j), a=a))
    return pl.pallas_call(
        _spectrum_kernel,
        grid=(nb, od // tn),
        in_specs=[mat(0), mat(nb), mat(0), mat(nb), half(0), half(1), half(0), half(1)],
        out_specs=pl.BlockSpec((4, tm, tn), lambda i, j: (0, i, j)),
        out_shape=jax.ShapeDtypeStruct((4, h, od), F32),
        compiler_params=_params("arbitrary", "arbitrary"),
        name="hyena_spectrum",
    )(mfe, mfe, mfo, mfo, hs, hs, hd, hd)


def _short_conv(x, w_ref, b_ref):
    hh = x.shape[0] // 2
    e = x[:hh]
    o = x[hh:]
    row = lax.broadcasted_iota(jnp.int32, e.shape, 0)
    o_prev = jnp.where(row == 0, 0.0, pltpu.roll(o, 1, 0))
    e_next = jnp.where(row == hh - 1, 0.0, pltpu.roll(e, hh - 1, 0))
    w0, w1, w2 = w_ref[0:1, :], w_ref[1:2, :], w_ref[2:3, :]
    return (b_ref[...] + o_prev * w0 + e * w1 + o * w2,
            b_ref[...] + e * w0 + o * w1 + e_next * w2)


def _hyconv_kernel(*refs, conv_z, cw, n):
    if conv_z:
        z_ref, cwz_ref, cbz_ref = refs[:3]
        refs = refs[3:]
    else:
        z_ref = refs[0]
        refs = refs[1:]
    x_ref, cwx_ref, cbx_ref, mfe_ref, mfo_ref, mie_ref, mio_ref, k4_ref, mid_ref, skip_ref, o_ref = refs
    hh = n // 2
    first8 = lax.broadcasted_iota(jnp.int32, (8, cw), 0) == 0
    for c in range(o_ref.shape[1] // cw):
        cs = slice(c * cw, (c + 1) * cw)
        if conv_z:
            ze, zo = _short_conv(z_ref[:, cs].astype(F32), cwz_ref.at[:, cs], cbz_ref.at[:, cs])
            ze, zo = ze.astype(BF16), zo.astype(BF16)
        else:
            ze, zo = z_ref[0:hh, cs], z_ref[hh:n, cs]
        fe = _dot(mfe_ref[...], ze)
        fo = _dot(mfo_ref[...], zo)
        p, pp, q, qp = fe[:hh], fe[hh:], fo[:hh], fo[hh:]
        kra, kia, krb, kib = k4_ref[0, :, cs], k4_ref[1, :, cs], k4_ref[2, :, cs], k4_ref[3, :, cs]
        fra, fia = p + q, pp + qp
        frb, fib = p - q, qp - pp
        wra = fra * kra - fia * kia
        wia = fra * kia + fia * kra
        wrb = frb * krb - fib * kib
        wib = frb * kib + fib * krb
        ge_im = wia - wib
        go_im = wia + wib
        kmr, kmi = mid_ref[0:1, cs], mid_ref[1:2, cs]
        pp8, qp8 = pp[:8], qp[:8]
        ge_im = jnp.concatenate([jnp.where(first8, pp8 * kmr - qp8 * kmi, ge_im[:8]), ge_im[8:]], 0)
        go_im = jnp.concatenate([jnp.where(first8, pp8 * kmi + qp8 * kmr, go_im[:8]), go_im[8:]], 0)
        ge = jnp.concatenate([(wra + wrb).astype(BF16), ge_im.astype(BF16)], 0)
        go = jnp.concatenate([(wra - wrb).astype(BF16), go_im.astype(BF16)], 0)
        ye = _dot(mie_ref[...], ge)
        yo = _dot(mio_ref[...], go)
        xe, xo = _short_conv(x_ref[:, cs].astype(F32), cwx_ref.at[:, cs], cbx_ref.at[:, cs])
        skip = skip_ref[:, cs]
        o_ref[0:hh, cs] = (xe * (ye + ze.astype(F32) * skip)).astype(o_ref.dtype)
        o_ref[hh:n, cs] = (xo * (yo + zo.astype(F32) * skip)).astype(o_ref.dtype)


def _hyconv(order, z_arr, p, conv_w, conv_b, tabs, k4, mid, skip):
    bsz, n, _ = p.shape
    h = n // 2
    tn = 512
    cw = 256
    nct = D_HYENA // tn
    conv_z = z_arr is None
    once = pl.Buffered(1)

    def sect(part):
        off = P_HY // tn + part * nct
        return pl.BlockSpec((None, n, tn), functools.partial(lambda c, bi, o: (bi, 0, o + c), o=off))

    def wsect(rows, part):
        return pl.BlockSpec((rows, tn), functools.partial(lambda c, bi, o: (0, o + c), o=part * nct))

    whole = lambda a: pl.BlockSpec(a.shape, lambda c, bi: (0, 0), pipeline_mode=once)
    ocol = functools.partial(lambda c, bi, o: (0, o + c), o=order * nct)
    if conv_z:
        in_specs = [sect(HYENA_ORDER), wsect(SHORT_CONV, HYENA_ORDER), wsect(1, HYENA_ORDER)]
        args = [p, conv_w, conv_b]
    else:
        in_specs = [pl.BlockSpec((None, n, tn), lambda c, bi: (bi, 0, c))]
        args = [z_arr]
    mats = [tabs['mfe'], tabs['mfo'], tabs['mie'], tabs['mio']]
    in_specs += [sect(order), wsect(SHORT_CONV, order), wsect(1, order)] + [whole(m) for m in mats]
    in_specs += [pl.BlockSpec((4, h, tn), functools.partial(lambda c, bi, o: (0, 0, o + c), o=order * nct),
                              pipeline_mode=once),
                 pl.BlockSpec((2, tn), ocol),
                 pl.BlockSpec((1, tn), ocol)]
    args += [p, conv_w, conv_b] + mats + [k4, mid, skip]
    return pl.pallas_call(
        functools.partial(_hyconv_kernel, conv_z=conv_z, cw=cw, n=n),
        grid=(nct, bsz),
        in_specs=in_specs,
        out_specs=pl.BlockSpec((None, n, tn), lambda c, bi: (bi, 0, c)),
        out_shape=jax.ShapeDtypeStruct((bsz, n, D_HYENA), BF16),
        compiler_params=_params("arbitrary", "arbitrary"),
        name="hyena_conv",
    )(*args)


def _fnet_kernel(u_ref, csc_ref, mne_ref, mno_ref, o_ref, *, n, scale):
    hh = n // 2
    qq = n // 4
    for g in range(FNET_GROUPS):
        gsl = slice(g * FNET_GROUP_DIM, (g + 1) * FNET_GROUP_DIM)
        t = _dot(u_ref[:, gsl], csc_ref[...])
        tc = t[:, :FNET_GROUP_DIM].astype(BF16)
        ts = t[:, FNET_GROUP_DIM:].astype(BF16)
        a = _dot(mne_ref[...], jnp.concatenate([tc[:hh], ts[:hh]], 0))
        b = _dot(mno_ref[...], jnp.concatenate([tc[hh:], ts[hh:]], 0))
        lo = ((a + b) * scale).astype(o_ref.dtype)
        hi = ((a - b) * scale).astype(o_ref.dtype)
        o_ref[0:qq, gsl] = lo[:qq]
        o_ref[qq:hh, gsl] = hi[:qq]
        o_ref[hh:hh + qq, gsl] = lo[qq:]
        o_ref[hh + qq:n, gsl] = hi[qq:]


def _fnet(p, csc, mne, mno):
    bsz, n, _ = p.shape
    scale = 1.0 / math.sqrt(n * FNET_GROUP_DIM)
    whole = lambda a: pl.BlockSpec(a.shape, lambda bi: (0, 0), pipeline_mode=pl.Buffered(1))
    return pl.pallas_call(
        functools.partial(_fnet_kernel, n=n, scale=scale),
        grid=(bsz,),
        in_specs=[pl.BlockSpec((None, n, D_MODEL), lambda bi: (bi, 0, P_FN // D_MODEL)),
                  whole(csc), whole(mne), whole(mno)],
        out_specs=pl.BlockSpec((None, n, D_MODEL), lambda bi: (bi, 0, 0)),
        out_shape=jax.ShapeDtypeStruct((bsz, n, D_MODEL), BF16),
        compiler_params=_params("arbitrary"),
        name="fnet",
    )(p, csc, mne, mno)


MERGE_ROW_CHUNK = 256


def _merge_kernel(oa_ref, hy_ref, fn_ref, ga_ref, gh_ref, gf_ref, x_ref, mod_ref,
                  wa_ref, wh_ref, wf_ref, wo_ref, bo_ref, lg_ref, lb_ref, o_ref):
    rc = min(MERGE_ROW_CHUNK, x_ref.shape[0])
    for r0 in range(0, x_ref.shape[0], rc):
        rs = slice(r0, r0 + rc)
        m = _sigmoid(ga_ref[rs, :].astype(F32)) * _dot(oa_ref[rs, :], wa_ref[...])
        m = m + _sigmoid(gh_ref[rs, :].astype(F32)) * _dot(hy_ref[rs, :], wh_ref[...])
        m = m + _sigmoid(gf_ref[rs, :].astype(F32)) * _dot(fn_ref[rs, :], wf_ref[...])
        y = _dot(m.astype(BF16), wo_ref[...]) + bo_ref[...]
        r = ALPHA * x_ref[rs, :] + mod_ref[2:3, :] * y
        o_ref[rs, :] = _layer_norm(r, lg_ref[...], lb_ref[...])


def _merge(o_att, z_hy, y_fn, p, x, mod, wa, wh, wf, wo, bo, lg, lb, layer):
    bsz, n, d = x.shape
    tm = min(n, 512)
    tok = lambda c: pl.BlockSpec((None, tm, d), functools.partial(lambda bi, i, c: (bi, i, c), c=c))
    wspec = pl.BlockSpec((None, d, d), lambda bi, i: (layer, 0, 0))
    vspec = pl.BlockSpec((None, 1, d), lambda bi, i: (layer, 0, 0))
    g0 = P_G // d
    return pl.pallas_call(
        _merge_kernel,
        grid=(bsz, n // tm),
        in_specs=[tok(0), tok(0), tok(0), tok(g0), tok(g0 + 1), tok(g0 + 2), tok(0),
                  pl.BlockSpec((None, N_MOD, d), lambda bi, i: (bi, 0, 0)),
                  wspec, wspec, wspec, wspec, vspec, vspec, vspec],
        out_specs=tok(0),
        out_shape=jax.ShapeDtypeStruct((bsz, n, d), F32),
        compiler_params=_params("arbitrary", "arbitrary"),
        name="merge_ln",
    )(o_att, z_hy, y_fn, p, p, p, x, mod, wa, wh, wf, wo, bo, lg, lb)


MLP_ROW_CHUNK = 512
MLP_FF_CHUNK = 1024


def _mlp_kernel(x_ref, mod_ref, w1_ref, b1_ref, w2_ref, b2_ref, lg_ref, lb_ref, o_ref):
    ff = w1_ref.shape[1]
    rc = min(MLP_ROW_CHUNK, x_ref.shape[0])
    for r0 in range(0, x_ref.shape[0], rc):
        rs = slice(r0, r0 + rc)
        x = x_ref[rs, :]
        h = (x * (1.0 + mod_ref[4:5, :]) + mod_ref[3:4, :]).astype(BF16)
        y = b2_ref[...]
        for c0 in range(0, ff, MLP_FF_CHUNK):
            cs = slice(c0, c0 + MLP_FF_CHUNK)
            a = jnp.maximum(_dot(h, w1_ref[:, cs]) + b1_ref[:, cs], 0.0)
            y = y + _dot((a * a).astype(BF16), w2_ref[cs, :])
        r = ALPHA * x + mod_ref[5:6, :] * y
        o_ref[rs, :] = _layer_norm(r, lg_ref[...], lb_ref[...])


def _mlp(x, mod, w1, b1, w2, b2, lg, lb, layer):
    bsz, n, d = x.shape
    tm = min(n, 2 * MLP_ROW_CHUNK)
    whole = lambda a: pl.BlockSpec((None,) + a.shape[1:], lambda bi, i: (layer, 0, 0),
                                   pipeline_mode=pl.Buffered(1))
    return pl.pallas_call(
        _mlp_kernel,
        grid=(bsz, n // tm),
        in_specs=[pl.BlockSpec((None, tm, d), lambda bi, i: (bi, i, 0)),
                  pl.BlockSpec((None, N_MOD, d), lambda bi, i: (bi, 0, 0)),
                  whole(w1), whole(b1), whole(w2), whole(b2), whole(lg), whole(lb)],
        out_specs=pl.BlockSpec((None, tm, d), lambda bi, i: (bi, i, 0)),
        out_shape=jax.ShapeDtypeStruct((bsz, n, d), F32),
        compiler_params=_params("arbitrary", "arbitrary"),
        name="mlp_ln",
    )(x, mod, w1, b1, w2, b2, lg, lb)


def _int_grid(rows, cols):
    k = lax.broadcasted_iota(jnp.int32, (rows, cols), 0)
    s = lax.broadcasted_iota(jnp.int32, (rows, cols), 1)
    return k, s


TRIG_ROW_BLOCK = 32


def _trig_table(n_rows, samp, period):
    def base(kv):
        ang = ((kv[:, None] * samp[None, :]) % period).astype(F32) * (2.0 * math.pi / period)
        return jnp.cos(ang), jnp.sin(ang)

    ch, sh = base(jnp.arange(n_rows // TRIG_ROW_BLOCK, dtype=jnp.int32) * TRIG_ROW_BLOCK)
    cl, sl = base(jnp.arange(TRIG_ROW_BLOCK, dtype=jnp.int32))
    ch, sh, cl, sl = ch[:, None, :], sh[:, None, :], cl[None], sl[None]
    shape = (n_rows, samp.shape[0])
    return (ch * cl - sh * sl).reshape(shape), (sh * cl + ch * sl).reshape(shape)


def _hyena_dft(n):
    h = n // 2
    s = jnp.arange(h, dtype=jnp.int32)
    alt = jnp.where(s % 2 == 0, 1.0, -1.0)[None, :]
    first = lax.broadcasted_iota(jnp.int32, (h, h), 0) == 0
    out = {}
    for name, samp in (('e', 2 * s), ('o', 2 * s + 1)):
        cos, sin = _trig_table(h, samp, 2 * n)
        mf = jnp.concatenate([cos, jnp.where(first, alt, sin)], 0)
        out['mf' + name] = mf.astype(BF16)
        out['mi' + name] = mf.T.astype(BF16)
    return out


def _fnet_dft(n):
    h = n // 2
    s = jnp.arange(h, dtype=jnp.int32)
    out = {}
    for name, samp in (('e', 2 * s), ('o', 2 * s + 1)):
        cos, sin = _trig_table(h, samp, n)
        out['mn' + name] = _even_odd_rows(jnp.concatenate([cos, -sin], 1)).astype(BF16)
    k, s = _int_grid(FNET_GROUP_DIM, FNET_GROUP_DIM)
    ang = ((k * s) % FNET_GROUP_DIM).astype(F32) * (2.0 * math.pi / FNET_GROUP_DIM)
    out['csc'] = jnp.concatenate([jnp.cos(ang), jnp.sin(ang)], 1).astype(BF16)
    return out


def _filter_embedding(n, width):
    t = jnp.linspace(0.0, 1.0, n, dtype=F32)[:, None]
    w = (2.0 * math.pi / n) * jnp.arange(n, dtype=F32)[:, None]
    f = jnp.linspace(1e-4, HYENA_BANDS - 1, HYENA_BANDS, dtype=F32)[None, :]
    emb = jnp.concatenate([t, jnp.cos(f * w), -jnp.sin(f * w)], -1)
    return _even_odd_rows(jnp.pad(emb, ((0, 0), (0, width - HYENA_EMB))))


def _rope_tables(n):
    rows = n // GRID_W
    row = jnp.repeat(jnp.arange(rows, dtype=F32), GRID_W)
    col = jnp.tile(jnp.arange(GRID_W, dtype=F32), rows)
    inv_freq = ROPE_THETA ** (-jnp.arange(ROPE_AXIS_DIM // 2, dtype=F32) * 2.0 / ROPE_AXIS_DIM)
    ar = row[:, None] * inv_freq[None, :]
    ac = col[:, None] * inv_freq[None, :]
    cos = jnp.concatenate([jnp.cos(ar), jnp.cos(ar), jnp.cos(ac), jnp.cos(ac)] * 2, -1)
    sin = jnp.concatenate([-jnp.sin(ar), jnp.sin(ar), -jnp.sin(ac), jnp.sin(ac)] * 2, -1)
    return _even_odd_rows(cos), _even_odd_rows(sin)


def _permute_in_cols(a):
    return jnp.concatenate([a[..., OFF_Q:OFF_K], a[..., OFF_G:D_IN], a[..., OFF_FN:OFF_G],
                            a[..., OFF_K:OFF_V], a[..., OFF_V:OFF_HY], a[..., OFF_HY:OFF_FN]], -1)


def _mixer(p_tok, seq_shape, lw, tabs, rope_tabs, ctx_parts):
    bsz, n = seq_shape
    p = p_tok.reshape(bsz, n, p_tok.shape[-1])
    q, k = _headnorm(p, [(P_Q // Q_W, N_Q_HEADS, lw['q_gain'], ATTN_SCALE),
                         (P_K // KV_W, N_KV_HEADS, lw['k_gain'], 1.0)], rope_tabs)
    o_att = _attention(q, [(k, p, P_V // KV_W)] + ctx_parts)

    hs, hd, mid = _filters(n, tabs['emb'], lw['hy_w1'], lw['hy_b1'], lw['hy_freq1'], lw['hy_w2'], lw['hy_b2'],
                           lw['hy_freq2'], lw['hy_w3'], tabs['absdelta'])
    k4 = _spectrum(tabs['mfe'], tabs['mfo'], hs, hd)
    z = None
    for o in range(HYENA_ORDER):
        z = _hyconv(o, z, p, lw['conv_w'], lw['conv_b'], tabs, k4, mid, lw['hy_skip'])
    y_fn = _fnet(p, tabs['csc'], tabs['mne'], tabs['mno'])
    tok = lambda a: a.reshape(p_tok.shape[0], p_tok.shape[1], a.shape[-1])
    return tok(o_att), tok(z), tok(y_fn), k, p


def _block(x_tok, mod, seq_shape, layer, sw, lw, tabs, rope_tabs, ctx_parts):
    p_tok = _inproj(x_tok, mod, sw['w_in'], sw['b_in'], layer)
    o_att, z_hy, y_fn, k, p = _mixer(p_tok, seq_shape, lw, tabs, rope_tabs, ctx_parts)
    x_tok = _merge(o_att, z_hy, y_fn, p_tok, x_tok, mod, sw['w_att_o'], sw['w_hy_o'], sw['w_fn_o'],
                   sw['w_out'], sw['b_out'], sw['ln1_g'], sw['ln1_b'], layer)
    x_tok = _mlp(x_tok, mod, sw['w_mlp1'], sw['b_mlp1'], sw['w_mlp2'], sw['b_mlp2'], sw['ln2_g'], sw['ln2_b'],
                 layer)
    return x_tok, k, p


def _seq_tables(n):
    tabs = dict(_hyena_dft(n))
    tabs.update(_fnet_dft(n))
    deltas = jnp.abs(jnp.linspace(MIN_DECAY, MAX_DECAY, D_HYENA, dtype=F32))
    tabs.update(emb=_filter_embedding(n, 128), absdelta=jnp.tile(deltas, HYENA_ORDER)[None, :])
    return tabs


def kernel(x, c, ctx, c_ctx, w_ada, b_ada, w_in, b_in, conv_w, conv_b, hy_w1, hy_b1, hy_freq1, hy_w2, hy_b2,
           hy_freq2, hy_w3, hy_skip, q_gain, k_gain, w_att_o, w_hy_o, w_fn_o, w_out, b_out, ln1_g, ln1_b,
           w_mlp1, b_mlp1, w_mlp2, b_mlp2, ln2_g, ln2_b):
    bsz, n_lat, d = x.shape
    n_ctx = ctx.shape[1]
    depth = w_ada.shape[0]
    tabs_lat = _seq_tables(n_lat)
    tabs_ctx = _seq_tables(n_ctx)
    rope_tabs = _rope_tables(n_lat)

    n_rows = -(-(bsz + 1) // 16) * 16
    cond = jnp.concatenate([c, c_ctx[None, :], jnp.zeros((n_rows - bsz - 1, d), F32)], 0)

    x = _deinterleave(x)
    ctx_tok = _deinterleave(ctx).reshape(1, bsz * n_ctx, d)
    row = lambda a: a[None, :]
    rows = lambda a: a[:, None, :]
    mods = _matmul(cond, w_ada, rows(b_ada), silu=True)
    sw = dict(
        w_in=_permute_in_cols(w_in).astype(BF16), b_in=rows(_permute_in_cols(b_in)),
        w_att_o=w_att_o.astype(BF16), w_hy_o=w_hy_o.astype(BF16), w_fn_o=w_fn_o.astype(BF16),
        w_out=w_out.astype(BF16), b_out=rows(b_out), ln1_g=rows(ln1_g), ln1_b=rows(ln1_b),
        w_mlp1=w_mlp1.astype(BF16), b_mlp1=rows(b_mlp1), w_mlp2=w_mlp2.astype(BF16), b_mlp2=rows(b_mlp2),
        ln2_g=rows(ln2_g), ln2_b=rows(ln2_b))
    for i in range(depth):
        last = i == depth - 1
        lw = dict(
            conv_w=conv_w[i], conv_b=row(conv_b[i]),
            hy_w1=jnp.pad(hy_w1[i], ((0, 128 - HYENA_EMB), (0, 0))), hy_b1=row(hy_b1[i]), hy_freq1=row(hy_freq1[i]),
            hy_w2=hy_w2[i], hy_b2=row(hy_b2[i]), hy_freq2=row(hy_freq2[i]), hy_w3=hy_w3[i],
            hy_skip=hy_skip[i].reshape(1, HYENA_ORDER * D_HYENA),
            q_gain=row(q_gain[i]), k_gain=row(k_gain[i]))
        mod_l = mods[i, :bsz].reshape(bsz, N_MOD, d)
        mod_c = mods[i, bsz:bsz + 1].reshape(1, N_MOD, d)

        if last:
            kv_cols = slice(P_K, P_V + KV_W)
            p_c = _inproj(ctx_tok, mod_c, sw['w_in'][i:i + 1, :, kv_cols], sw['b_in'][i:i + 1, :, kv_cols], 0)
            p_c = p_c.reshape(bsz, n_ctx, 2 * KV_W)
            k_c, = _headnorm(p_c, [(0, N_KV_HEADS, lw['k_gain'], 1.0)], None)
            v_blk = 1
        else:
            ctx_tok, k_c, p_c = _block(ctx_tok, mod_c, (bsz, n_ctx), i, sw, lw, tabs_ctx, None, [])
            v_blk = P_V // KV_W
        x, _, _ = _block(x, mod_l, (bsz, n_lat), i, sw, lw, tabs_lat, rope_tabs, [(k_c, p_c, v_blk)])
    return _interleave(x)
```

```python
import functools
import math

import jax
import jax.numpy as jnp
from jax import lax
from jax.experimental import pallas as pl
from jax.experimental.pallas import tpu as pltpu

F32 = jnp.float32
BF16 = jnp.bfloat16

D_MODEL = 1024
GRID_W = 64
HEAD_DIM = 128
N_Q_HEADS = D_MODEL // HEAD_DIM
N_KV_HEADS = 2
GQA_GROUP = N_Q_HEADS // N_KV_HEADS
ROPE_THETA = 10000.0
ROPE_AXIS_DIM = HEAD_DIM // 2
ATTN_SCALE = HEAD_DIM ** -0.5

D_HYENA = D_MODEL
HYENA_ORDER = 2
HYENA_BANDS = 16
HYENA_EMB = 1 + 2 * HYENA_BANDS
SHORT_CONV = 3
DECAY_TARGET = 1e-2
MIN_DECAY = math.log(DECAY_TARGET) / 0.3
MAX_DECAY = math.log(DECAY_TARGET) / 1.5

FNET_GROUPS = 4
FNET_GROUP_DIM = D_MODEL // FNET_GROUPS
D_FF = 4 * D_MODEL
N_MOD = 6
DEPTH = 4

Q_W = N_Q_HEADS * HEAD_DIM
KV_W = N_KV_HEADS * HEAD_DIM
HY_W = (HYENA_ORDER + 1) * D_HYENA
OFF_Q = 0
OFF_K = OFF_Q + Q_W
OFF_V = OFF_K + KV_W
OFF_HY = OFF_V + KV_W
OFF_FN = OFF_HY + HY_W
OFF_G = OFF_FN + D_MODEL
D_IN = OFF_G + 3 * D_MODEL

HALF_D = D_MODEL // 2

ALPHA = (2 * DEPTH) ** 0.25
LN_EPS = 1e-6
RMS_EPS = 1e-6
KERN_EPS = 1e-6

V7X_VMEM_LIMIT_BYTES = 56 * 1024 * 1024


def _params(*sem):
    return pltpu.CompilerParams(dimension_semantics=sem, vmem_limit_bytes=V7X_VMEM_LIMIT_BYTES)


def _dot(a, b):
    return jnp.dot(a, b, preferred_element_type=F32)


def _layer_norm(r, g, b):
    mu = jnp.mean(r, -1, keepdims=True)
    d = r - mu
    var = jnp.mean(d * d, -1, keepdims=True)
    return d * lax.rsqrt(var + LN_EPS) * g + b


def _sigmoid(x):
    return 1.0 / (1.0 + jnp.exp(-x))


def _deinterleave_kernel(x_ref, o_ref):
    hh = x_ref.shape[1] // 2
    o_ref[:, 0:hh, :] = x_ref[:, pl.ds(0, hh, stride=2), :]
    o_ref[:, hh:2 * hh, :] = x_ref[:, pl.ds(1, hh, stride=2), :]


def _interleave_kernel(x_ref, o_ref):
    hh = x_ref.shape[1] // 2
    o_ref[:, pl.ds(0, hh, stride=2), :] = x_ref[:, 0:hh, :]
    o_ref[:, pl.ds(1, hh, stride=2), :] = x_ref[:, hh:2 * hh, :]


REORDER_BLOCK_ROWS = 2048


def _reorder_tokens(x, body, name):
    bsz, n, d = x.shape
    tc = 128
    bb = min(bsz, max(1, REORDER_BLOCK_ROWS // n))
    spec = pl.BlockSpec((bb, n, tc), lambda bi, c: (bi, 0, c))
    return pl.pallas_call(
        body,
        grid=(bsz // bb, d // tc),
        in_specs=[spec],
        out_specs=spec,
        out_shape=jax.ShapeDtypeStruct(x.shape, x.dtype),
        compiler_params=_params("arbitrary", "arbitrary"),
        name=name,
    )(x)


def _deinterleave(x):
    return _reorder_tokens(x, _deinterleave_kernel, "deinterleave")


def _interleave(x):
    return _reorder_tokens(x, _interleave_kernel, "interleave")


def _even_odd_rows(a):
    return jnp.concatenate([a[0::2], a[1::2]], 0)


def _matmul_kernel(a_ref, b_ref, bias_ref, o_ref, *, silu):
    a = a_ref[...]
    if silu:
        a = a * _sigmoid(a)
    o_ref[...] = _dot(a.astype(BF16), b_ref[...].astype(BF16)) + bias_ref[...]


def _matmul(a, b, bias, *, silu=False, tn=1024):
    m, k = a.shape
    nl, _, n = b.shape
    return pl.pallas_call(
        functools.partial(_matmul_kernel, silu=silu),
        grid=(nl, n // tn),
        in_specs=[pl.BlockSpec((m, k), lambda l, j: (0, 0)),
                  pl.BlockSpec((None, k, tn), lambda l, j: (l, 0, j)),
                  pl.BlockSpec((None, 1, tn), lambda l, j: (l, 0, j))],
        out_specs=pl.BlockSpec((None, m, tn), lambda l, j: (l, 0, j)),
        out_shape=jax.ShapeDtypeStruct((nl, m, n), F32),
        compiler_params=_params("arbitrary", "arbitrary"),
        name="ada_matmul",
    )(a, b, bias)


INPROJ_COL_CHUNK = 2048


def _inproj_kernel(x_ref, mod_ref, w_ref, b_ref, o_ref):
    h = (x_ref[...] * (1.0 + mod_ref[1:2, :]) + mod_ref[0:1, :]).astype(BF16)
    nout = o_ref.shape[1]
    for c0 in range(0, nout, INPROJ_COL_CHUNK):
        cs = slice(c0, min(c0 + INPROJ_COL_CHUNK, nout))
        o_ref[:, cs] = (_dot(h, w_ref[:, cs]) + b_ref[:, cs]).astype(o_ref.dtype)


def _inproj(x, mod, w, b, layer):
    bsz, n, d = x.shape
    nout = w.shape[2]
    tm = min(n, 512)
    whole = lambda a: pl.BlockSpec((None,) + a.shape[1:], lambda bi, i: (layer, 0, 0),
                                   pipeline_mode=pl.Buffered(1))
    return pl.pallas_call(
        _inproj_kernel,
        grid=(bsz, n // tm),
        in_specs=[pl.BlockSpec((None, tm, d), lambda bi, i: (bi, i, 0)),
                  pl.BlockSpec((None, N_MOD, d), lambda bi, i: (bi, 0, 0)),
                  whole(w), whole(b)],
        out_specs=pl.BlockSpec((None, tm, nout), lambda bi, i: (bi, i, 0)),
        out_shape=jax.ShapeDtypeStruct((bsz, n, nout), BF16),
        compiler_params=_params("arbitrary", "arbitrary"),
        name="inproj",
    )(x, mod, w, b)


HEAD_PAIR = 2 * HEAD_DIM


def _headnorm_kernel(*refs, n_sect, rope):
    p_refs, g_refs = refs[:n_sect], refs[n_sect:2 * n_sect]
    avg_ref = refs[2 * n_sect]
    if rope:
        swap_ref, cos_ref, sin_ref = refs[2 * n_sect + 1:2 * n_sect + 4]
    o_refs = refs[-n_sect:]
    for p_ref, g_ref, o_ref in zip(p_refs, g_refs, o_refs):
        gain = g_ref[...]
        for hp in range(p_ref.shape[1] // HEAD_PAIR):
            sl = slice(hp * HEAD_PAIR, (hp + 1) * HEAD_PAIR)
            xf = p_ref[:, sl].astype(F32)
            ms = _dot((xf * xf).astype(BF16), avg_ref[...])
            y = xf * lax.rsqrt(ms + RMS_EPS) * gain
            if rope:
                y = y * cos_ref[...] + _dot(y.astype(BF16), swap_ref[...]) * sin_ref[...]
            o_ref[:, sl] = y.astype(o_ref.dtype)


def _head_pair_matrices():
    i, j = _int_grid(HEAD_PAIR, HEAD_PAIR)
    avg = jnp.where(i // HEAD_DIM == j // HEAD_DIM, 1.0 / HEAD_DIM, 0.0)
    half = ROPE_AXIS_DIM // 2
    partner = jnp.where(j % ROPE_AXIS_DIM < half, j + half, j - half)
    return avg.astype(BF16), jnp.where(i == partner, 1.0, 0.0).astype(BF16)


def _headnorm(p, sections, rope_tabs):
    bsz, n, _ = p.shape
    tm = min(n, 1024)
    rope = rope_tabs is not None
    avg, swap = _head_pair_matrices()
    const = lambda shape: pl.BlockSpec(shape, lambda bi, i: (0, 0))
    widths = [nh * HEAD_DIM for _, nh, _, _ in sections]
    in_specs = [pl.BlockSpec((None, tm, w), functools.partial(lambda bi, i, c: (bi, i, c), c=cb))
                for w, (cb, _, _, _) in zip(widths, sections)]
    in_specs += [const((1, HEAD_PAIR))] * len(sections) + [const((HEAD_PAIR, HEAD_PAIR))]
    args = [p] * len(sections) + [jnp.tile(g * s, (1, 2)) for _, _, g, s in sections] + [avg]
    if rope:
        in_specs += [const((HEAD_PAIR, HEAD_PAIR))] + [pl.BlockSpec((tm, HEAD_PAIR), lambda bi, i: (i, 0))] * 2
        args += [swap] + list(rope_tabs)
    return pl.pallas_call(
        functools.partial(_headnorm_kernel, n_sect=len(sections), rope=rope),
        grid=(bsz, n // tm),
        in_specs=in_specs,
        out_specs=[pl.BlockSpec((None, tm, w), lambda bi, i: (bi, i, 0)) for w in widths],
        out_shape=[jax.ShapeDtypeStruct((bsz, n, w), BF16) for w in widths],
        compiler_params=_params("arbitrary", "arbitrary"),
        name="headnorm",
    )(*args)


def _attn_kernel(*refs, n_parts):
    q_ref = refs[0]
    kv_refs = refs[1:1 + 2 * n_parts]
    o_ref = refs[1 + 2 * n_parts]
    for g in range(N_KV_HEADS):
        gsl = slice(g * HEAD_DIM, (g + 1) * HEAD_DIM)
        ks = [kv_refs[2 * i][:, gsl] for i in range(n_parts)]
        vs = [jnp.concatenate([kv_refs[2 * i + 1][:, gsl], jnp.ones((k.shape[0], HEAD_DIM), BF16)], 1)
              for i, k in enumerate(ks)]
        for r in range(GQA_GROUP):
            h = g * GQA_GROUP + r
            hsl = slice(h * HEAD_DIM, (h + 1) * HEAD_DIM)
            q = q_ref[:, hsl]
            ss = [lax.dot_general(q, k, (((1,), (1,)), ((), ())), preferred_element_type=F32) for k in ks]
            m = jnp.max(ss[0], -1, keepdims=True)
            for s in ss[1:]:
                m = jnp.maximum(m, jnp.max(s, -1, keepdims=True))
            o = None
            for s, v in zip(ss, vs):
                pv = _dot(jnp.exp((s - m).astype(BF16)), v)
                o = pv if o is None else o + pv
            o_ref[:, hsl] = (o[:, :HEAD_DIM] / o[:, HEAD_DIM:]).astype(o_ref.dtype)


def _attention(q, parts):
    bsz, n, w = q.shape
    tq = min(n, 512)
    in_specs = [pl.BlockSpec((None, tq, w), lambda bi, i: (bi, i, 0))]
    args = [q]
    for k, v, vblk in parts:
        nk = k.shape[1]
        in_specs.append(pl.BlockSpec((None, nk, KV_W), lambda bi, i: (bi, 0, 0)))
        in_specs.append(pl.BlockSpec((None, nk, KV_W), functools.partial(lambda bi, i, c: (bi, 0, c), c=vblk)))
        args += [k, v]
    return pl.pallas_call(
        functools.partial(_attn_kernel, n_parts=len(parts)),
        grid=(bsz, n // tq),
        in_specs=in_specs,
        out_specs=pl.BlockSpec((None, tq, w), lambda bi, i: (bi, i, 0)),
        out_shape=jax.ShapeDtypeStruct((bsz, n, w), BF16),
        compiler_params=_params("arbitrary", "arbitrary"),
        name="attention",
    )(*args)


def _filter_kernel(emb_ref, w1_ref, b1_ref, f1_ref, w2_ref, b2_ref, f2_ref, w3f_ref, w3b_ref, dl_ref,
                   hs_ref, hd_ref, mid_ref, h_ref, *, n):
    hh = n // 2

    @pl.when(pl.program_id(0) == 0)
    def _():
        h1 = jnp.sin(f1_ref[...] * (_dot(emb_ref[...], w1_ref[...]) + b1_ref[...]))
        h_ref[...] = jnp.sin(f2_ref[...] * (_dot(h1, w2_ref[...]) + b2_ref[...]))

    h = h_ref[...]
    hf = _dot(h, w3f_ref[...])
    hb = _dot(h, w3b_ref[...])
    row = lax.broadcasted_iota(jnp.int32, hf.shape, 0)
    lag = jnp.where(row < hh, 2 * row, 2 * (row - hh) + 1)
    t = lag.astype(F32) * (1.0 / (n - 1))
    dec = jnp.exp(-t * dl_ref[...])
    hf = hf * dec
    hb = jnp.where(lag == 0, 0.0, hb * dec)
    ssq = jnp.sum(hf * hf, 0, keepdims=True) + jnp.sum(hb * hb, 0, keepdims=True)
    nrm = lax.rsqrt(ssq + KERN_EPS) * (1.0 / n)
    hs = (hf + hb) * nrm
    hd = (hf - hb) * nrm
    hs_ref[...] = hs.astype(hs_ref.dtype)
    hd_ref[...] = hd.astype(hd_ref.dtype)
    sign = jnp.where(row % 2 == 0, 1.0, -1.0)
    mid_ref[0:1, :] = jnp.sum(jnp.where(row < hh, hs * sign, 0.0), 0, keepdims=True)
    mid_ref[1:2, :] = jnp.sum(jnp.where(row < hh, 0.0, hd * sign), 0, keepdims=True)


def _filters(n, emb, w1, b1, f1, w2, b2, f2, w3, absdelta):
    od = HYENA_ORDER * D_HYENA
    tn = 512
    fh = w2.shape[0]
    ke = emb.shape[1]
    full = lambda shape: pl.BlockSpec(shape, lambda j: (0, 0))
    return pl.pallas_call(
        functools.partial(_filter_kernel, n=n),
        grid=(od // tn,),
        in_specs=[full((n, ke)), full((ke, fh)), full((1, fh)), full((1, fh)),
                  full((fh, fh)), full((1, fh)), full((1, fh)),
                  pl.BlockSpec((fh, tn), lambda j: (0, j)),
                  pl.BlockSpec((fh, tn), lambda j: (0, j + od // tn)),
                  pl.BlockSpec((1, tn), lambda j: (0, j))],
        out_specs=[pl.BlockSpec((n, tn), lambda j: (0, j)),
                   pl.BlockSpec((n, tn), lambda j: (0, j)),
                   pl.BlockSpec((2, tn), lambda j: (0, j))],
        out_shape=[jax.ShapeDtypeStruct((n, od), BF16), jax.ShapeDtypeStruct((n, od), BF16),
                   jax.ShapeDtypeStruct((2, od), F32)],
        scratch_shapes=[pltpu.VMEM((n, fh), F32)],
        compiler_params=_params("arbitrary"),
        name="hyena_filters",
    )(emb, w1, b1, f1, w2, b2, f2, w3, w3, absdelta)


def _spectrum_kernel(fe_re, fe_im, fo_re, fo_im, hse, hso, hde, hdo, k4_ref):
    p = _dot(fe_re[...], hse[...])
    q = _dot(fo_re[...], hso[...])
    pp = _dot(fe_im[...], hde[...])
    qp = _dot(fo_im[...], hdo[...])
    row0 = (lax.broadcasted_iota(jnp.int32, p.shape, 0) + pl.program_id(0) * p.shape[0]) == 0
    k4_ref[0] = jnp.where(row0, 0.5 * (p + q), p + q)
    k4_ref[1] = jnp.where(row0, 0.0, pp + qp)
    k4_ref[2] = jnp.where(row0, 0.5 * (p - q), p - q)
    k4_ref[3] = jnp.where(row0, 0.0, qp - pp)


def _spectrum(mfe, mfo, hs, hd):
    n, od = hs.shape
    h = n // 2
    tm = min(h, 512)
    tn = 512
    nb = h // tm
    mat = lambda off: pl.BlockSpec((tm, h), functools.partial(lambda i, j, o: (i + o, 0), o=off))
    half = lambda a: pl.BlockSpec((h, tn), functools.partial(lambda i, j, a: (a, j), a=a))
    return pl.pallas_call(
        _spectrum_kernel,
        grid=(nb, od // tn),
        in_specs=[mat(0), mat(nb), mat(0), mat(nb), half(0), half(1), half(0), half(1)],
        out_specs=pl.BlockSpec((4, tm, tn), lambda i, j: (0, i, j)),
        out_shape=jax.ShapeDtypeStruct((4, h, od), F32),
        compiler_params=_params("arbitrary", "arbitrary"),
        name="hyena_spectrum",
    )(mfe, mfe, mfo, mfo, hs, hs, hd, hd)


def _short_conv(x, w_ref, b_ref):
    hh = x.shape[0] // 2
    e = x[:hh]
    o = x[hh:]
    row = lax.broadcasted_iota(jnp.int32, e.shape, 0)
    o_prev = jnp.where(row == 0, 0.0, pltpu.roll(o, 1, 0))
    e_next = jnp.where(row == hh - 1, 0.0, pltpu.roll(e, hh - 1, 0))
    w0, w1, w2 = w_ref[0:1, :], w_ref[1:2, :], w_ref[2:3, :]
    return (b_ref[...] + o_prev * w0 + e * w1 + o * w2,
            b_ref[...] + e * w0 + o * w1 + e_next * w2)


def _hyconv_kernel(*refs, conv_z, cw, n):
    if conv_z:
        z_ref, cwz_ref, cbz_ref = refs[:3]
        refs = refs[3:]
    else:
        z_ref = refs[0]
        refs = refs[1:]
    x_ref, cwx_ref, cbx_ref, mfe_ref, mfo_ref, mie_ref, mio_ref, k4_ref, mid_ref, skip_ref, o_ref = refs
    hh = n // 2
    first8 = lax.broadcasted_iota(jnp.int32, (8, cw), 0) == 0
    for c in range(o_ref.shape[1] // cw):
        cs = slice(c * cw, (c + 1) * cw)
        if conv_z:
            ze, zo = _short_conv(z_ref[:, cs].astype(F32), cwz_ref.at[:, cs], cbz_ref.at[:, cs])
            ze, zo = ze.astype(BF16), zo.astype(BF16)
        else:
            ze, zo = z_ref[0:hh, cs], z_ref[hh:n, cs]
        fe = _dot(mfe_ref[...], ze)
        fo = _dot(mfo_ref[...], zo)
        p, pp, q, qp = fe[:hh], fe[hh:], fo[:hh], fo[hh:]
        kra, kia, krb, kib = k4_ref[0, :, cs], k4_ref[1, :, cs], k4_ref[2, :, cs], k4_ref[3, :, cs]
        fra, fia = p + q, pp + qp
        frb, fib = p - q, qp - pp
        wra = fra * kra - fia * kia
        wia = fra * kia + fia * kra
        wrb = frb * krb - fib * kib
        wib = frb * kib + fib * krb
        ge_im = wia - wib
        go_im = wia + wib
        kmr, kmi = mid_ref[0:1, cs], mid_ref[1:2, cs]
        pp8, qp8 = pp[:8], qp[:8]
        ge_im = jnp.concatenate([jnp.where(first8, pp8 * kmr - qp8 * kmi, ge_im[:8]), ge_im[8:]], 0)
        go_im = jnp.concatenate([jnp.where(first8, pp8 * kmi + qp8 * kmr, go_im[:8]), go_im[8:]], 0)
        ge = jnp.concatenate([(wra + wrb).astype(BF16), ge_im.astype(BF16)], 0)
        go = jnp.concatenate([(wra - wrb).astype(BF16), go_im.astype(BF16)], 0)
        ye = _dot(mie_ref[...], ge)
        yo = _dot(mio_ref[...], go)
        xe, xo = _short_conv(x_ref[:, cs].astype(F32), cwx_ref.at[:, cs], cbx_ref.at[:, cs])
        skip = skip_ref[:, cs]
        o_ref[0:hh, cs] = (xe * (ye + ze.astype(F32) * skip)).astype(o_ref.dtype)
        o_ref[hh:n, cs] = (xo * (yo + zo.astype(F32) * skip)).astype(o_ref.dtype)


def _hyconv(order, z_arr, p, conv_w, conv_b, tabs, k4, mid, skip):
    bsz, n, _ = p.shape
    h = n // 2
    tn = 512
    cw = 256
    nct = D_HYENA // tn
    conv_z = z_arr is None
    once = pl.Buffered(1)

    def sect(part):
        off = OFF_HY // tn + part * nct
        return pl.BlockSpec((None, n, tn), functools.partial(lambda c, bi, o: (bi, 0, o + c), o=off))

    def wsect(rows, part):
        return pl.BlockSpec((rows, tn), functools.partial(lambda c, bi, o: (0, o + c), o=part * nct))

    whole = lambda a: pl.BlockSpec(a.shape, lambda c, bi: (0, 0), pipeline_mode=once)
    ocol = functools.partial(lambda c, bi, o: (0, o + c), o=order * nct)
    if conv_z:
        in_specs = [sect(HYENA_ORDER), wsect(SHORT_CONV, HYENA_ORDER), wsect(1, HYENA_ORDER)]
        args = [p, conv_w, conv_b]
    else:
        in_specs = [pl.BlockSpec((None, n, tn), lambda c, bi: (bi, 0, c))]
        args = [z_arr]
    mats = [tabs['mfe'], tabs['mfo'], tabs['mie'], tabs['mio']]
    in_specs += [sect(order), wsect(SHORT_CONV, order), wsect(1, order)] + [whole(m) for m in mats]
    in_specs += [pl.BlockSpec((4, h, tn), functools.partial(lambda c, bi, o: (0, 0, o + c), o=order * nct),
                              pipeline_mode=once),
                 pl.BlockSpec((2, tn), ocol),
                 pl.BlockSpec((1, tn), ocol)]
    args += [p, conv_w, conv_b] + mats + [k4, mid, skip]
    return pl.pallas_call(
        functools.partial(_hyconv_kernel, conv_z=conv_z, cw=cw, n=n),
        grid=(nct, bsz),
        in_specs=in_specs,
        out_specs=pl.BlockSpec((None, n, tn), lambda c, bi: (bi, 0, c)),
        out_shape=jax.ShapeDtypeStruct((bsz, n, D_HYENA), BF16),
        compiler_params=_params("arbitrary", "arbitrary"),
        name="hyena_conv",
    )(*args)


def _fnet_kernel(ua_ref, ub_ref, csc_ref, mne_ref, mno_ref, o_ref, *, n, scale):
    hh = n // 2
    qq = n // 4
    gpb = HALF_D // FNET_GROUP_DIM
    for g in range(FNET_GROUPS):
        gsl = slice(g * FNET_GROUP_DIM, (g + 1) * FNET_GROUP_DIM)
        u_ref = (ua_ref, ub_ref)[g // gpb]
        u = u_ref[:, (g % gpb) * FNET_GROUP_DIM:(g % gpb + 1) * FNET_GROUP_DIM]
        t = _dot(u, csc_ref[...])
        tc = t[:, :FNET_GROUP_DIM].astype(BF16)
        ts = t[:, FNET_GROUP_DIM:].astype(BF16)
        a = _dot(mne_ref[...], jnp.concatenate([tc[:hh], ts[:hh]], 0))
        b = _dot(mno_ref[...], jnp.concatenate([tc[hh:], ts[hh:]], 0))
        lo = ((a + b) * scale).astype(o_ref.dtype)
        hi = ((a - b) * scale).astype(o_ref.dtype)
        o_ref[0:qq, gsl] = lo[:qq]
        o_ref[qq:hh, gsl] = hi[:qq]
        o_ref[hh:hh + qq, gsl] = lo[qq:]
        o_ref[hh + qq:n, gsl] = hi[qq:]


def _fnet(p, csc, mne, mno):
    bsz, n, _ = p.shape
    scale = 1.0 / math.sqrt(n * FNET_GROUP_DIM)
    whole = lambda a: pl.BlockSpec(a.shape, lambda bi: (0, 0), pipeline_mode=pl.Buffered(1))
    half = lambda c: pl.BlockSpec((None, n, HALF_D), functools.partial(lambda bi, c: (bi, 0, c), c=c))
    return pl.pallas_call(
        functools.partial(_fnet_kernel, n=n, scale=scale),
        grid=(bsz,),
        in_specs=[half(OFF_FN // HALF_D), half(OFF_FN // HALF_D + 1), whole(csc), whole(mne), whole(mno)],
        out_specs=pl.BlockSpec((None, n, D_MODEL), lambda bi: (bi, 0, 0)),
        out_shape=jax.ShapeDtypeStruct((bsz, n, D_MODEL), BF16),
        compiler_params=_params("arbitrary"),
        name="fnet",
    )(p, p, csc, mne, mno)


MERGE_ROW_CHUNK = 256


def _merge_kernel(oa_ref, hy_ref, fn_ref, ga0, ga1, gh0, gh1, gf0, gf1, x_ref, mod_ref,
                  wa_ref, wh_ref, wf_ref, wo_ref, bo_ref, lg_ref, lb_ref, o_ref):
    rc = min(MERGE_ROW_CHUNK, x_ref.shape[0])
    for r0 in range(0, x_ref.shape[0], rc):
        rs = slice(r0, r0 + rc)
        gate = lambda g0, g1: _sigmoid(jnp.concatenate([g0[rs, :], g1[rs, :]], 1).astype(F32))
        m = gate(ga0, ga1) * _dot(oa_ref[rs, :], wa_ref[...])
        m = m + gate(gh0, gh1) * _dot(hy_ref[rs, :], wh_ref[...])
        m = m + gate(gf0, gf1) * _dot(fn_ref[rs, :], wf_ref[...])
        y = _dot(m.astype(BF16), wo_ref[...]) + bo_ref[...]
        r = ALPHA * x_ref[rs, :] + mod_ref[2:3, :] * y
        o_ref[rs, :] = _layer_norm(r, lg_ref[...], lb_ref[...])


def _merge(o_att, z_hy, y_fn, p, x, mod, wa, wh, wf, wo, bo, lg, lb, layer):
    bsz, n, d = x.shape
    tm = min(n, 512)
    tok = lambda c: pl.BlockSpec((None, tm, d), functools.partial(lambda bi, i, c: (bi, i, c), c=c))
    wspec = pl.BlockSpec((None, d, d), lambda bi, i: (layer, 0, 0))
    vspec = pl.BlockSpec((None, 1, d), lambda bi, i: (layer, 0, 0))
    gates = [pl.BlockSpec((None, tm, HALF_D), functools.partial(lambda bi, i, c: (bi, i, c), c=OFF_G // HALF_D + c))
             for c in range(3 * d // HALF_D)]
    return pl.pallas_call(
        _merge_kernel,
        grid=(bsz, n // tm),
        in_specs=[tok(0), tok(0), tok(0)] + gates + [tok(0),
                  pl.BlockSpec((None, N_MOD, d), lambda bi, i: (bi, 0, 0)),
                  wspec, wspec, wspec, wspec, vspec, vspec, vspec],
        out_specs=tok(0),
        out_shape=jax.ShapeDtypeStruct((bsz, n, d), F32),
        compiler_params=_params("arbitrary", "arbitrary"),
        name="merge_ln",
    )(o_att, z_hy, y_fn, *([p] * len(gates)), x, mod, wa, wh, wf, wo, bo, lg, lb)


MLP_ROW_CHUNK = 512
MLP_FF_CHUNK = 1024


def _mlp_kernel(x_ref, mod_ref, w1_ref, b1_ref, w2_ref, b2_ref, lg_ref, lb_ref, o_ref):
    ff = w1_ref.shape[1]
    rc = min(MLP_ROW_CHUNK, x_ref.shape[0])
    for r0 in range(0, x_ref.shape[0], rc):
        rs = slice(r0, r0 + rc)
        x = x_ref[rs, :]
        h = (x * (1.0 + mod_ref[4:5, :]) + mod_ref[3:4, :]).astype(BF16)
        y = b2_ref[...]
        for c0 in range(0, ff, MLP_FF_CHUNK):
            cs = slice(c0, c0 + MLP_FF_CHUNK)
            a = jnp.maximum(_dot(h, w1_ref[:, cs]) + b1_ref[:, cs], 0.0)
            y = y + _dot((a * a).astype(BF16), w2_ref[cs, :])
        r = ALPHA * x + mod_ref[5:6, :] * y
        o_ref[rs, :] = _layer_norm(r, lg_ref[...], lb_ref[...])


def _mlp(x, mod, w1, b1, w2, b2, lg, lb, layer):
    bsz, n, d = x.shape
    tm = min(n, 2 * MLP_ROW_CHUNK)
    whole = lambda a: pl.BlockSpec((None,) + a.shape[1:], lambda bi, i: (layer, 0, 0),
                                   pipeline_mode=pl.Buffered(1))
    return pl.pallas_call(
        _mlp_kernel,
        grid=(bsz, n // tm),
        in_specs=[pl.BlockSpec((None, tm, d), lambda bi, i: (bi, i, 0)),
                  pl.BlockSpec((None, N_MOD, d), lambda bi, i: (bi, 0, 0)),
                  whole(w1), whole(b1), whole(w2), whole(b2), whole(lg), whole(lb)],
        out_specs=pl.BlockSpec((None, tm, d), lambda bi, i: (bi, i, 0)),
        out_shape=jax.ShapeDtypeStruct((bsz, n, d), F32),
        compiler_params=_params("arbitrary", "arbitrary"),
        name="mlp_ln",
    )(x, mod, w1, b1, w2, b2, lg, lb)


def _int_grid(rows, cols):
    k = lax.broadcasted_iota(jnp.int32, (rows, cols), 0)
    s = lax.broadcasted_iota(jnp.int32, (rows, cols), 1)
    return k, s


TRIG_ROW_BLOCK = 32


def _trig_table(n_rows, samp, period, k_step=1, k_first=0):
    def base(kv):
        ang = ((kv[:, None] * samp[None, :]) % period).astype(F32) * (2.0 * math.pi / period)
        return jnp.cos(ang), jnp.sin(ang)

    ch, sh = base(jnp.arange(n_rows // TRIG_ROW_BLOCK, dtype=jnp.int32) * (TRIG_ROW_BLOCK * k_step))
    cl, sl = base(jnp.arange(TRIG_ROW_BLOCK, dtype=jnp.int32) * k_step + k_first)
    ch, sh, cl, sl = ch[:, None, :], sh[:, None, :], cl[None], sl[None]
    shape = (n_rows, samp.shape[0])
    return (ch * cl - sh * sl).reshape(shape), (sh * cl + ch * sl).reshape(shape)


def _hyena_dft(n):
    h = n // 2
    s = jnp.arange(h, dtype=jnp.int32)
    alt = jnp.where(s % 2 == 0, 1.0, -1.0)[None, :]
    first = lax.broadcasted_iota(jnp.int32, (h, h), 0) == 0
    out = {}
    for name, samp in (('e', 2 * s), ('o', 2 * s + 1)):
        cos, sin = _trig_table(h, samp, 2 * n)
        mf = jnp.concatenate([cos, jnp.where(first, alt, sin)], 0)
        out['mf' + name] = mf.astype(BF16)
        out['mi' + name] = mf.T.astype(BF16)
    return out


def _fnet_dft(n):
    h = n // 2
    s = jnp.arange(h, dtype=jnp.int32)
    out = {}
    for name, samp in (('e', 2 * s), ('o', 2 * s + 1)):
        blocks = []
        for parity in range(2):
            cos, sin = _trig_table(h // 2, samp, n, k_step=2, k_first=parity)
            blocks.append(jnp.concatenate([cos, -sin], 1))
        out['mn' + name] = jnp.concatenate(blocks, 0).astype(BF16)
    k, s = _int_grid(FNET_GROUP_DIM, FNET_GROUP_DIM)
    ang = ((k * s) % FNET_GROUP_DIM).astype(F32) * (2.0 * math.pi / FNET_GROUP_DIM)
    out['csc'] = jnp.concatenate([jnp.cos(ang), jnp.sin(ang)], 1).astype(BF16)
    return out


def _filter_embedding(n, width):
    t = jnp.linspace(0.0, 1.0, n, dtype=F32)[:, None]
    w = (2.0 * math.pi / n) * jnp.arange(n, dtype=F32)[:, None]
    f = jnp.linspace(1e-4, HYENA_BANDS - 1, HYENA_BANDS, dtype=F32)[None, :]
    emb = jnp.concatenate([t, jnp.cos(f * w), -jnp.sin(f * w)], -1)
    return _even_odd_rows(jnp.pad(emb, ((0, 0), (0, width - HYENA_EMB))))


def _rope_tables(n):
    rows = n // GRID_W
    row = jnp.repeat(jnp.arange(rows, dtype=F32), GRID_W)
    col = jnp.tile(jnp.arange(GRID_W, dtype=F32), rows)
    inv_freq = ROPE_THETA ** (-jnp.arange(ROPE_AXIS_DIM // 2, dtype=F32) * 2.0 / ROPE_AXIS_DIM)
    ar = row[:, None] * inv_freq[None, :]
    ac = col[:, None] * inv_freq[None, :]
    cos = jnp.concatenate([jnp.cos(ar), jnp.cos(ar), jnp.cos(ac), jnp.cos(ac)] * 2, -1)
    sin = jnp.concatenate([-jnp.sin(ar), jnp.sin(ar), -jnp.sin(ac), jnp.sin(ac)] * 2, -1)
    return _even_odd_rows(cos), _even_odd_rows(sin)


def _mixer(p_tok, seq_shape, lw, tabs, rope_tabs, ctx_parts):
    bsz, n = seq_shape
    p = p_tok.reshape(bsz, n, p_tok.shape[-1])
    q, k = _headnorm(p, [(OFF_Q // Q_W, N_Q_HEADS, lw['q_gain'], ATTN_SCALE),
                         (OFF_K // KV_W, N_KV_HEADS, lw['k_gain'], 1.0)], rope_tabs)
    o_att = _attention(q, [(k, p, OFF_V // KV_W)] + ctx_parts)

    hs, hd, mid = _filters(n, tabs['emb'], lw['hy_w1'], lw['hy_b1'], lw['hy_freq1'], lw['hy_w2'], lw['hy_b2'],
                           lw['hy_freq2'], lw['hy_w3'], tabs['absdelta'])
    k4 = _spectrum(tabs['mfe'], tabs['mfo'], hs, hd)
    z = None
    for o in range(HYENA_ORDER):
        z = _hyconv(o, z, p, lw['conv_w'], lw['conv_b'], tabs, k4, mid, lw['hy_skip'])
    y_fn = _fnet(p, tabs['csc'], tabs['mne'], tabs['mno'])
    tok = lambda a: a.reshape(p_tok.shape[0], p_tok.shape[1], a.shape[-1])
    return tok(o_att), tok(z), tok(y_fn), k, p


def _block(x_tok, mod, seq_shape, layer, sw, lw, tabs, rope_tabs, ctx_parts):
    p_tok = _inproj(x_tok, mod, sw['w_in'], sw['b_in'], layer)
    o_att, z_hy, y_fn, k, p = _mixer(p_tok, seq_shape, lw, tabs, rope_tabs, ctx_parts)
    x_tok = _merge(o_att, z_hy, y_fn, p_tok, x_tok, mod, sw['w_att_o'], sw['w_hy_o'], sw['w_fn_o'],
                   sw['w_out'], sw['b_out'], sw['ln1_g'], sw['ln1_b'], layer)
    x_tok = _mlp(x_tok, mod, sw['w_mlp1'], sw['b_mlp1'], sw['w_mlp2'], sw['b_mlp2'], sw['ln2_g'], sw['ln2_b'],
                 layer)
    return x_tok, k, p


def _seq_tables(n):
    tabs = dict(_hyena_dft(n))
    tabs.update(_fnet_dft(n))
    deltas = jnp.abs(jnp.linspace(MIN_DECAY, MAX_DECAY, D_HYENA, dtype=F32))
    tabs.update(emb=_filter_embedding(n, 128), absdelta=jnp.tile(deltas, HYENA_ORDER)[None, :])
    return tabs


def kernel(x, c, ctx, c_ctx, w_ada, b_ada, w_in, b_in, conv_w, conv_b, hy_w1, hy_b1, hy_freq1, hy_w2, hy_b2,
           hy_freq2, hy_w3, hy_skip, q_gain, k_gain, w_att_o, w_hy_o, w_fn_o, w_out, b_out, ln1_g, ln1_b,
           w_mlp1, b_mlp1, w_mlp2, b_mlp2, ln2_g, ln2_b):
    bsz, n_lat, d = x.shape
    n_ctx = ctx.shape[1]
    depth = w_ada.shape[0]
    tabs_lat = _seq_tables(n_lat)
    tabs_ctx = _seq_tables(n_ctx)
    rope_tabs = _rope_tables(n_lat)

    n_rows = -(-(bsz + 1) // 16) * 16
    cond = jnp.concatenate([c, c_ctx[None, :], jnp.zeros((n_rows - bsz - 1, d), F32)], 0)

    x = _deinterleave(x)
    ctx_tok = _deinterleave(ctx).reshape(1, bsz * n_ctx, d)
    row = lambda a: a[None, :]
    rows = lambda a: a[:, None, :]
    mods = _matmul(cond, w_ada, rows(b_ada), silu=True)
    sw = dict(
        w_in=w_in.astype(BF16), b_in=rows(b_in),
        w_att_o=w_att_o.astype(BF16), w_hy_o=w_hy_o.astype(BF16), w_fn_o=w_fn_o.astype(BF16),
        w_out=w_out.astype(BF16), b_out=rows(b_out), ln1_g=rows(ln1_g), ln1_b=rows(ln1_b),
        w_mlp1=w_mlp1.astype(BF16), b_mlp1=rows(b_mlp1), w_mlp2=w_mlp2.astype(BF16), b_mlp2=rows(b_mlp2),
        ln2_g=rows(ln2_g), ln2_b=rows(ln2_b))
    for i in range(depth):
        last = i == depth - 1
        lw = dict(
            conv_w=conv_w[i], conv_b=row(conv_b[i]),
            hy_w1=jnp.pad(hy_w1[i], ((0, 128 - HYENA_EMB), (0, 0))), hy_b1=row(hy_b1[i]), hy_freq1=row(hy_freq1[i]),
            hy_w2=hy_w2[i], hy_b2=row(hy_b2[i]), hy_freq2=row(hy_freq2[i]), hy_w3=hy_w3[i],
            hy_skip=hy_skip[i].reshape(1, HYENA_ORDER * D_HYENA),
            q_gain=row(q_gain[i]), k_gain=row(k_gain[i]))
        mod_l = mods[i, :bsz].reshape(bsz, N_MOD, d)
        mod_c = mods[i, bsz:bsz + 1].reshape(1, N_MOD, d)

        if last:
            kv_cols = slice(OFF_K, OFF_HY)
            p_c = _inproj(ctx_tok, mod_c, sw['w_in'][i:i + 1, :, kv_cols], sw['b_in'][i:i + 1, :, kv_cols], 0)
            p_c = p_c.reshape(bsz, n_ctx, 2 * KV_W)
            k_c, = _headnorm(p_c, [(0, N_KV_HEADS, lw['k_gain'], 1.0)], None)
            v_blk = 1
        else:
            ctx_tok, k_c, p_c = _block(ctx_tok, mod_c, (bsz, n_ctx), i, sw, lw, tabs_ctx, None, [])
            v_blk = OFF_V // KV_W
        x, _, _ = _block(x, mod_l, (bsz, n_lat), i, sw, lw, tabs_lat, rope_tabs, [(k_c, p_c, v_blk)])
    return _interleave(x)
```

```python
import functools
import math

import jax
import jax.numpy as jnp
from jax import lax
from jax.experimental import pallas as pl
from jax.experimental.pallas import tpu as pltpu

F32 = jnp.float32
BF16 = jnp.bfloat16

D_MODEL = 1024
GRID_W = 64
HEAD_DIM = 128
N_Q_HEADS = D_MODEL // HEAD_DIM
N_KV_HEADS = 2
GQA_GROUP = N_Q_HEADS // N_KV_HEADS
ROPE_THETA = 10000.0
ROPE_AXIS_DIM = HEAD_DIM // 2
ATTN_SCALE = HEAD_DIM ** -0.5

D_HYENA = D_MODEL
HYENA_ORDER = 2
HYENA_BANDS = 16
HYENA_EMB = 1 + 2 * HYENA_BANDS
SHORT_CONV = 3
DECAY_TARGET = 1e-2
MIN_DECAY = math.log(DECAY_TARGET) / 0.3
MAX_DECAY = math.log(DECAY_TARGET) / 1.5

FNET_GROUPS = 4
FNET_GROUP_DIM = D_MODEL // FNET_GROUPS
D_FF = 4 * D_MODEL
N_MOD = 6
DEPTH = 4

Q_W = N_Q_HEADS * HEAD_DIM
KV_W = N_KV_HEADS * HEAD_DIM
HY_W = (HYENA_ORDER + 1) * D_HYENA
OFF_Q = 0
OFF_K = OFF_Q + Q_W
OFF_V = OFF_K + KV_W
OFF_HY = OFF_V + KV_W
OFF_FN = OFF_HY + HY_W
OFF_G = OFF_FN + D_MODEL
D_IN = OFF_G + 3 * D_MODEL

HALF_D = D_MODEL // 2

ALPHA = (2 * DEPTH) ** 0.25
LN_EPS = 1e-6
RMS_EPS = 1e-6
KERN_EPS = 1e-6

V7X_VMEM_LIMIT_BYTES = 56 * 1024 * 1024


def _params(*sem):
    return pltpu.CompilerParams(dimension_semantics=sem, vmem_limit_bytes=V7X_VMEM_LIMIT_BYTES)


def _dot(a, b):
    return jnp.dot(a, b, preferred_element_type=F32)


def _layer_norm(r, g, b):
    mu = jnp.mean(r, -1, keepdims=True)
    d = r - mu
    var = jnp.mean(d * d, -1, keepdims=True)
    return d * lax.rsqrt(var + LN_EPS) * g + b


def _sigmoid(x):
    return 1.0 / (1.0 + jnp.exp(-x))


def _deinterleave_kernel(x_ref, o_ref):
    hh = x_ref.shape[1] // 2
    o_ref[:, 0:hh, :] = x_ref[:, pl.ds(0, hh, stride=2), :]
    o_ref[:, hh:2 * hh, :] = x_ref[:, pl.ds(1, hh, stride=2), :]


def _interleave_kernel(x_ref, o_ref):
    hh = x_ref.shape[1] // 2
    o_ref[:, pl.ds(0, hh, stride=2), :] = x_ref[:, 0:hh, :]
    o_ref[:, pl.ds(1, hh, stride=2), :] = x_ref[:, hh:2 * hh, :]


REORDER_BLOCK_ROWS = 2048


def _reorder_tokens(x, body, name):
    bsz, n, d = x.shape
    tc = 128
    bb = min(bsz, max(1, REORDER_BLOCK_ROWS // n))
    spec = pl.BlockSpec((bb, n, tc), lambda bi, c: (bi, 0, c))
    return pl.pallas_call(
        body,
        grid=(bsz // bb, d // tc),
        in_specs=[spec],
        out_specs=spec,
        out_shape=jax.ShapeDtypeStruct(x.shape, x.dtype),
        compiler_params=_params("arbitrary", "arbitrary"),
        name=name,
    )(x)


def _deinterleave(x):
    return _reorder_tokens(x, _deinterleave_kernel, "deinterleave")


def _interleave(x):
    return _reorder_tokens(x, _interleave_kernel, "interleave")


def _even_odd_rows(a):
    return jnp.concatenate([a[0::2], a[1::2]], 0)


def _matmul_kernel(a_ref, b_ref, bias_ref, o_ref, *, silu):
    a = a_ref[...]
    if silu:
        a = a * _sigmoid(a)
    o_ref[...] = _dot(a.astype(BF16), b_ref[...].astype(BF16)) + bias_ref[...]


def _matmul(a, b, bias, *, silu=False, tn=1024):
    m, k = a.shape
    nl, _, n = b.shape
    return pl.pallas_call(
        functools.partial(_matmul_kernel, silu=silu),
        grid=(nl, n // tn),
        in_specs=[pl.BlockSpec((m, k), lambda l, j: (0, 0)),
                  pl.BlockSpec((None, k, tn), lambda l, j: (l, 0, j)),
                  pl.BlockSpec((None, 1, tn), lambda l, j: (l, 0, j))],
        out_specs=pl.BlockSpec((None, m, tn), lambda l, j: (l, 0, j)),
        out_shape=jax.ShapeDtypeStruct((nl, m, n), F32),
        compiler_params=_params("arbitrary", "arbitrary"),
        name="ada_matmul",
    )(a, b, bias)


INPROJ_COL_CHUNK = 2048


def _inproj_kernel(x_ref, mod_ref, w_ref, b_ref, o_ref):
    h = (x_ref[...] * (1.0 + mod_ref[1:2, :]) + mod_ref[0:1, :]).astype(BF16)
    nout = o_ref.shape[1]
    for c0 in range(0, nout, INPROJ_COL_CHUNK):
        cs = slice(c0, min(c0 + INPROJ_COL_CHUNK, nout))
        o_ref[:, cs] = (_dot(h, w_ref[:, cs]) + b_ref[:, cs]).astype(o_ref.dtype)


def _inproj(x, mod, w, b, layer):
    bsz, n, d = x.shape
    nout = w.shape[2]
    tm = min(n, 512)
    whole = lambda a: pl.BlockSpec((None,) + a.shape[1:], lambda bi, i: (layer, 0, 0),
                                   pipeline_mode=pl.Buffered(1))
    return pl.pallas_call(
        _inproj_kernel,
        grid=(bsz, n // tm),
        in_specs=[pl.BlockSpec((None, tm, d), lambda bi, i: (bi, i, 0)),
                  pl.BlockSpec((None, N_MOD, d), lambda bi, i: (bi, 0, 0)),
                  whole(w), whole(b)],
        out_specs=pl.BlockSpec((None, tm, nout), lambda bi, i: (bi, i, 0)),
        out_shape=jax.ShapeDtypeStruct((bsz, n, nout), BF16),
        compiler_params=_params("arbitrary", "arbitrary"),
        name="inproj",
    )(x, mod, w, b)


HEAD_PAIR = 2 * HEAD_DIM


def _headnorm_kernel(*refs, n_sect, rope):
    p_refs, g_refs = refs[:n_sect], refs[n_sect:2 * n_sect]
    avg_ref = refs[2 * n_sect]
    if rope:
        swap_ref, cos_ref, sin_ref = refs[2 * n_sect + 1:2 * n_sect + 4]
    o_refs = refs[-n_sect:]
    for p_ref, g_ref, o_ref in zip(p_refs, g_refs, o_refs):
        gain = g_ref[...]
        for hp in range(p_ref.shape[1] // HEAD_PAIR):
            sl = slice(hp * HEAD_PAIR, (hp + 1) * HEAD_PAIR)
            xf = p_ref[:, sl].astype(F32)
            ms = _dot((xf * xf).astype(BF16), avg_ref[...])
            y = xf * lax.rsqrt(ms + RMS_EPS) * gain
            if rope:
                y = y * cos_ref[...] + _dot(y.astype(BF16), swap_ref[...]) * sin_ref[...]
            o_ref[:, sl] = y.astype(o_ref.dtype)


def _head_pair_matrices():
    i, j = _int_grid(HEAD_PAIR, HEAD_PAIR)
    avg = jnp.where(i // HEAD_DIM == j // HEAD_DIM, 1.0 / HEAD_DIM, 0.0)
    half = ROPE_AXIS_DIM // 2
    partner = jnp.where(j % ROPE_AXIS_DIM < half, j + half, j - half)
    return avg.astype(BF16), jnp.where(i == partner, 1.0, 0.0).astype(BF16)


def _headnorm(p, sections, rope_tabs):
    bsz, n, _ = p.shape
    tm = min(n, 1024)
    rope = rope_tabs is not None
    avg, swap = _head_pair_matrices()
    const = lambda shape: pl.BlockSpec(shape, lambda bi, i: (0, 0))
    widths = [nh * HEAD_DIM for _, nh, _, _ in sections]
    in_specs = [pl.BlockSpec((None, tm, w), functools.partial(lambda bi, i, c: (bi, i, c), c=cb))
                for w, (cb, _, _, _) in zip(widths, sections)]
    in_specs += [const((1, HEAD_PAIR))] * len(sections) + [const((HEAD_PAIR, HEAD_PAIR))]
    args = [p] * len(sections) + [jnp.tile(g * s, (1, 2)) for _, _, g, s in sections] + [avg]
    if rope:
        in_specs += [const((HEAD_PAIR, HEAD_PAIR))] + [pl.BlockSpec((tm, HEAD_PAIR), lambda bi, i: (i, 0))] * 2
        args += [swap] + list(rope_tabs)
    return pl.pallas_call(
        functools.partial(_headnorm_kernel, n_sect=len(sections), rope=rope),
        grid=(bsz, n // tm),
        in_specs=in_specs,
        out_specs=[pl.BlockSpec((None, tm, w), lambda bi, i: (bi, i, 0)) for w in widths],
        out_shape=[jax.ShapeDtypeStruct((bsz, n, w), BF16) for w in widths],
        compiler_params=_params("arbitrary", "arbitrary"),
        name="headnorm",
    )(*args)


def _attn_kernel(*refs, n_parts):
    q_ref = refs[0]
    kv_refs = refs[1:1 + 2 * n_parts]
    o_ref = refs[1 + 2 * n_parts]
    for g in range(N_KV_HEADS):
        gsl = slice(g * HEAD_DIM, (g + 1) * HEAD_DIM)
        ks = [kv_refs[2 * i][:, gsl] for i in range(n_parts)]
        vs = [jnp.concatenate([kv_refs[2 * i + 1][:, gsl], jnp.ones((k.shape[0], HEAD_DIM), BF16)], 1)
              for i, k in enumerate(ks)]
        for r in range(GQA_GROUP):
            h = g * GQA_GROUP + r
            hsl = slice(h * HEAD_DIM, (h + 1) * HEAD_DIM)
            q = q_ref[:, hsl]
            ss = [lax.dot_general(q, k, (((1,), (1,)), ((), ())), preferred_element_type=F32) for k in ks]
            m = jnp.max(ss[0], -1, keepdims=True)
            for s in ss[1:]:
                m = jnp.maximum(m, jnp.max(s, -1, keepdims=True))
            o = None
            for s, v in zip(ss, vs):
                pv = _dot(jnp.exp((s - m).astype(BF16)), v)
                o = pv if o is None else o + pv
            o_ref[:, hsl] = (o[:, :HEAD_DIM] / o[:, HEAD_DIM:]).astype(o_ref.dtype)


def _attention(q, parts):
    bsz, n, w = q.shape
    tq = min(n, 512)
    in_specs = [pl.BlockSpec((None, tq, w), lambda bi, i: (bi, i, 0))]
    args = [q]
    for k, v, vblk in parts:
        nk = k.shape[1]
        in_specs.append(pl.BlockSpec((None, nk, KV_W), lambda bi, i: (bi, 0, 0)))
        in_specs.append(pl.BlockSpec((None, nk, KV_W), functools.partial(lambda bi, i, c: (bi, 0, c), c=vblk)))
        args += [k, v]
    return pl.pallas_call(
        functools.partial(_attn_kernel, n_parts=len(parts)),
        grid=(bsz, n // tq),
        in_specs=in_specs,
        out_specs=pl.BlockSpec((None, tq, w), lambda bi, i: (bi, i, 0)),
        out_shape=jax.ShapeDtypeStruct((bsz, n, w), BF16),
        compiler_params=_params("arbitrary", "arbitrary"),
        name="attention",
    )(*args)


FILTER_COL_CHUNK = 256


def _filter_kernel(emb_ref, w1_ref, b1_ref, f1_ref, w2_ref, b2_ref, f2_ref, w3f_ref, w3b_ref, dl_ref,
                   mfe_ref, mfo_ref, k4_ref, mid_ref, h_ref, *, n):
    hh = n // 2

    @pl.when(pl.program_id(0) == 0)
    def _():
        h1 = jnp.sin(f1_ref[...] * (_dot(emb_ref[...], w1_ref[...]) + b1_ref[...]))
        h_ref[...] = jnp.sin(f2_ref[...] * (_dot(h1, w2_ref[...]) + b2_ref[...])).astype(BF16)

    h = h_ref[...]
    cw = FILTER_COL_CHUNK
    row = lax.broadcasted_iota(jnp.int32, (n, cw), 0)
    lag = jnp.where(row < hh, 2 * row, 2 * (row - hh) + 1)
    t = lag.astype(F32) * (1.0 / (n - 1))
    sign = jnp.where(row % 2 == 0, 1.0, -1.0)
    row0 = lax.broadcasted_iota(jnp.int32, (hh, cw), 0) == 0
    for c0 in range(0, k4_ref.shape[2], cw):
        cs = slice(c0, c0 + cw)
        dec = jnp.exp(-t * dl_ref[:, cs])
        hf = _dot(h, w3f_ref[:, cs].astype(BF16)) * dec
        hb = jnp.where(lag == 0, 0.0, _dot(h, w3b_ref[:, cs].astype(BF16)) * dec)
        ssq = jnp.sum(hf * hf, 0, keepdims=True) + jnp.sum(hb * hb, 0, keepdims=True)
        nrm = lax.rsqrt(ssq + KERN_EPS) * (1.0 / n)
        hs = (hf + hb) * nrm
        hd = (hf - hb) * nrm
        mid_ref[0:1, cs] = jnp.sum(jnp.where(row < hh, hs * sign, 0.0), 0, keepdims=True)
        mid_ref[1:2, cs] = jnp.sum(jnp.where(row < hh, 0.0, hd * sign), 0, keepdims=True)
        hs, hd = hs.astype(BF16), hd.astype(BF16)
        p = _dot(mfe_ref[0:hh, :], hs[:hh])
        q = _dot(mfo_ref[0:hh, :], hs[hh:])
        pp = _dot(mfe_ref[hh:n, :], hd[:hh])
        qp = _dot(mfo_ref[hh:n, :], hd[hh:])
        k4_ref[0, :, cs] = jnp.where(row0, 0.5 * (p + q), p + q)
        k4_ref[1, :, cs] = jnp.where(row0, 0.0, pp + qp)
        k4_ref[2, :, cs] = jnp.where(row0, 0.5 * (p - q), p - q)
        k4_ref[3, :, cs] = jnp.where(row0, 0.0, qp - pp)


def _filters(n, emb, w1, b1, f1, w2, b2, f2, w3, absdelta, mfe, mfo):
    od = HYENA_ORDER * D_HYENA
    tn = 512
    fh = w2.shape[0]
    full = lambda a: pl.BlockSpec(a.shape, lambda j: (0, 0), pipeline_mode=pl.Buffered(1))
    return pl.pallas_call(
        functools.partial(_filter_kernel, n=n),
        grid=(od // tn,),
        in_specs=[full(emb), full(w1), full(b1), full(f1), full(w2), full(b2), full(f2),
                  pl.BlockSpec((fh, tn), lambda j: (0, j)),
                  pl.BlockSpec((fh, tn), lambda j: (0, j + od // tn)),
                  pl.BlockSpec((1, tn), lambda j: (0, j)),
                  full(mfe), full(mfo)],
        out_specs=[pl.BlockSpec((4, n // 2, tn), lambda j: (0, 0, j)),
                   pl.BlockSpec((2, tn), lambda j: (0, j))],
        out_shape=[jax.ShapeDtypeStruct((4, n // 2, od), F32), jax.ShapeDtypeStruct((2, od), F32)],
        scratch_shapes=[pltpu.VMEM((n, fh), BF16)],
        compiler_params=_params("arbitrary"),
        name="hyena_filters",
    )(emb, w1, b1, f1, w2, b2, f2, w3, w3, absdelta, mfe, mfo)


def _short_conv(x, w_ref, b_ref):
    hh = x.shape[0] // 2
    e = x[:hh]
    o = x[hh:]
    row = lax.broadcasted_iota(jnp.int32, e.shape, 0)
    o_prev = jnp.where(row == 0, 0.0, pltpu.roll(o, 1, 0))
    e_next = jnp.where(row == hh - 1, 0.0, pltpu.roll(e, hh - 1, 0))
    w0, w1, w2 = w_ref[0:1, :], w_ref[1:2, :], w_ref[2:3, :]
    return (b_ref[...] + o_prev * w0 + e * w1 + o * w2,
            b_ref[...] + e * w0 + o * w1 + e_next * w2)


def _hyconv_kernel(*refs, conv_z, cw, n):
    if conv_z:
        z_ref, cwz_ref, cbz_ref = refs[:3]
        refs = refs[3:]
    else:
        z_ref = refs[0]
        refs = refs[1:]
    x_ref, cwx_ref, cbx_ref, mfe_ref, mfo_ref, mie_ref, mio_ref, k4_ref, mid_ref, skip_ref, o_ref = refs
    hh = n // 2
    first8 = lax.broadcasted_iota(jnp.int32, (8, cw), 0) == 0
    for c in range(o_ref.shape[1] // cw):
        cs = slice(c * cw, (c + 1) * cw)
        if conv_z:
            ze, zo = _short_conv(z_ref[:, cs].astype(F32), cwz_ref.at[:, cs], cbz_ref.at[:, cs])
            ze, zo = ze.astype(BF16), zo.astype(BF16)
        else:
            ze, zo = z_ref[0:hh, cs], z_ref[hh:n, cs]
        fe = _dot(mfe_ref[...], ze)
        fo = _dot(mfo_ref[...], zo)
        p, pp, q, qp = fe[:hh], fe[hh:], fo[:hh], fo[hh:]
        kra, kia, krb, kib = k4_ref[0, :, cs], k4_ref[1, :, cs], k4_ref[2, :, cs], k4_ref[3, :, cs]
        fra, fia = p + q, pp + qp
        frb, fib = p - q, qp - pp
        wra = fra * kra - fia * kia
        wia = fra * kia + fia * kra
        wrb = frb * krb - fib * kib
        wib = frb * kib + fib * krb
        ge_im = wia - wib
        go_im = wia + wib
        kmr, kmi = mid_ref[0:1, cs], mid_ref[1:2, cs]
        pp8, qp8 = pp[:8], qp[:8]
        ge_im = jnp.concatenate([jnp.where(first8, pp8 * kmr - qp8 * kmi, ge_im[:8]), ge_im[8:]], 0)
        go_im = jnp.concatenate([jnp.where(first8, pp8 * kmi + qp8 * kmr, go_im[:8]), go_im[8:]], 0)
        ge = jnp.concatenate([(wra + wrb).astype(BF16), ge_im.astype(BF16)], 0)
        go = jnp.concatenate([(wra - wrb).astype(BF16), go_im.astype(BF16)], 0)
        ye = _dot(mie_ref[...], ge)
        yo = _dot(mio_ref[...], go)
        xe, xo = _short_conv(x_ref[:, cs].astype(F32), cwx_ref.at[:, cs], cbx_ref.at[:, cs])
        skip = skip_ref[:, cs]
        o_ref[0:hh, cs] = (xe * (ye + ze.astype(F32) * skip)).astype(o_ref.dtype)
        o_ref[hh:n, cs] = (xo * (yo + zo.astype(F32) * skip)).astype(o_ref.dtype)


def _hyconv(order, z_arr, p, conv_w, conv_b, tabs, k4, mid, skip):
    bsz, n, _ = p.shape
    h = n // 2
    tn = 512
    cw = 256
    nct = D_HYENA // tn
    conv_z = z_arr is None
    once = pl.Buffered(1)

    def sect(part):
        off = OFF_HY // tn + part * nct
        return pl.BlockSpec((None, n, tn), functools.partial(lambda c, bi, o: (bi, 0, o + c), o=off))

    def wsect(rows, part):
        return pl.BlockSpec((rows, tn), functools.partial(lambda c, bi, o: (0, o + c), o=part * nct))

    whole = lambda a: pl.BlockSpec(a.shape, lambda c, bi: (0, 0), pipeline_mode=once)
    ocol = functools.partial(lambda c, bi, o: (0, o + c), o=order * nct)
    if conv_z:
        in_specs = [sect(HYENA_ORDER), wsect(SHORT_CONV, HYENA_ORDER), wsect(1, HYENA_ORDER)]
        args = [p, conv_w, conv_b]
    else:
        in_specs = [pl.BlockSpec((None, n, tn), lambda c, bi: (bi, 0, c))]
        args = [z_arr]
    mats = [tabs['mfe'], tabs['mfo'], tabs['mie'], tabs['mio']]
    in_specs += [sect(order), wsect(SHORT_CONV, order), wsect(1, order)] + [whole(m) for m in mats]
    in_specs += [pl.BlockSpec((4, h, tn), functools.partial(lambda c, bi, o: (0, 0, o + c), o=order * nct),
                              pipeline_mode=once),
                 pl.BlockSpec((2, tn), ocol),
                 pl.BlockSpec((1, tn), ocol)]
    args += [p, conv_w, conv_b] + mats + [k4, mid, skip]
    return pl.pallas_call(
        functools.partial(_hyconv_kernel, conv_z=conv_z, cw=cw, n=n),
        grid=(nct, bsz),
        in_specs=in_specs,
        out_specs=pl.BlockSpec((None, n, tn), lambda c, bi: (bi, 0, c)),
        out_shape=jax.ShapeDtypeStruct((bsz, n, D_HYENA), BF16),
        compiler_params=_params("arbitrary", "arbitrary"),
        name="hyena_conv",
    )(*args)


def _fnet_kernel(ua_ref, ub_ref, csc_ref, mne_ref, mno_ref, o_ref, *, n, scale):
    hh = n // 2
    qq = n // 4
    gpb = HALF_D // FNET_GROUP_DIM
    for g in range(FNET_GROUPS):
        gsl = slice(g * FNET_GROUP_DIM, (g + 1) * FNET_GROUP_DIM)
        u_ref = (ua_ref, ub_ref)[g // gpb]
        u = u_ref[:, (g % gpb) * FNET_GROUP_DIM:(g % gpb + 1) * FNET_GROUP_DIM]
        t = _dot(u, csc_ref[...])
        tc = t[:, :FNET_GROUP_DIM].astype(BF16)
        ts = t[:, FNET_GROUP_DIM:].astype(BF16)
        a = _dot(mne_ref[...], jnp.concatenate([tc[:hh], ts[:hh]], 0))
        b = _dot(mno_ref[...], jnp.concatenate([tc[hh:], ts[hh:]], 0))
        lo = ((a + b) * scale).astype(o_ref.dtype)
        hi = ((a - b) * scale).astype(o_ref.dtype)
        o_ref[0:qq, gsl] = lo[:qq]
        o_ref[qq:hh, gsl] = hi[:qq]
        o_ref[hh:hh + qq, gsl] = lo[qq:]
        o_ref[hh + qq:n, gsl] = hi[qq:]


def _fnet(p, csc, mne, mno):
    bsz, n, _ = p.shape
    scale = 1.0 / math.sqrt(n * FNET_GROUP_DIM)
    whole = lambda a: pl.BlockSpec(a.shape, lambda bi: (0, 0), pipeline_mode=pl.Buffered(1))
    half = lambda c: pl.BlockSpec((None, n, HALF_D), functools.partial(lambda bi, c: (bi, 0, c), c=c))
    return pl.pallas_call(
        functools.partial(_fnet_kernel, n=n, scale=scale),
        grid=(bsz,),
        in_specs=[half(OFF_FN // HALF_D), half(OFF_FN // HALF_D + 1), whole(csc), whole(mne), whole(mno)],
        out_specs=pl.BlockSpec((None, n, D_MODEL), lambda bi: (bi, 0, 0)),
        out_shape=jax.ShapeDtypeStruct((bsz, n, D_MODEL), BF16),
        compiler_params=_params("arbitrary"),
        name="fnet",
    )(p, p, csc, mne, mno)


MERGE_ROW_CHUNK = 256


def _merge_kernel(oa_ref, hy_ref, fn_ref, ga0, ga1, gh0, gh1, gf0, gf1, x_ref, mod_ref,
                  wa_ref, wh_ref, wf_ref, wo_ref, bo_ref, lg_ref, lb_ref, o_ref):
    rc = min(MERGE_ROW_CHUNK, x_ref.shape[0])
    for r0 in range(0, x_ref.shape[0], rc):
        rs = slice(r0, r0 + rc)
        gate = lambda g0, g1: _sigmoid(jnp.concatenate([g0[rs, :], g1[rs, :]], 1).astype(F32))
        m = gate(ga0, ga1) * _dot(oa_ref[rs, :], wa_ref[...])
        m = m + gate(gh0, gh1) * _dot(hy_ref[rs, :], wh_ref[...])
        m = m + gate(gf0, gf1) * _dot(fn_ref[rs, :], wf_ref[...])
        y = _dot(m.astype(BF16), wo_ref[...]) + bo_ref[...]
        r = ALPHA * x_ref[rs, :] + mod_ref[2:3, :] * y
        o_ref[rs, :] = _layer_norm(r, lg_ref[...], lb_ref[...])


def _merge(o_att, z_hy, y_fn, p, x, mod, wa, wh, wf, wo, bo, lg, lb, layer):
    bsz, n, d = x.shape
    tm = min(n, 512)
    tok = lambda c: pl.BlockSpec((None, tm, d), functools.partial(lambda bi, i, c: (bi, i, c), c=c))
    wspec = pl.BlockSpec((None, d, d), lambda bi, i: (layer, 0, 0))
    vspec = pl.BlockSpec((None, 1, d), lambda bi, i: (layer, 0, 0))
    gates = [pl.BlockSpec((None, tm, HALF_D), functools.partial(lambda bi, i, c: (bi, i, c), c=OFF_G // HALF_D + c))
             for c in range(3 * d // HALF_D)]
    return pl.pallas_call(
        _merge_kernel,
        grid=(bsz, n // tm),
        in_specs=[tok(0), tok(0), tok(0)] + gates + [tok(0),
                  pl.BlockSpec((None, N_MOD, d), lambda bi, i: (bi, 0, 0)),
                  wspec, wspec, wspec, wspec, vspec, vspec, vspec],
        out_specs=tok(0),
        out_shape=jax.ShapeDtypeStruct((bsz, n, d), F32),
        compiler_params=_params("arbitrary", "arbitrary"),
        name="merge_ln",
    )(o_att, z_hy, y_fn, *([p] * len(gates)), x, mod, wa, wh, wf, wo, bo, lg, lb)


MLP_ROW_CHUNK = 512
MLP_FF_CHUNK = 1024


def _mlp_kernel(x_ref, mod_ref, w1_ref, b1_ref, w2_ref, b2_ref, lg_ref, lb_ref, o_ref):
    ff = w1_ref.shape[1]
    rc = min(MLP_ROW_CHUNK, x_ref.shape[0])
    for r0 in range(0, x_ref.shape[0], rc):
        rs = slice(r0, r0 + rc)
        x = x_ref[rs, :]
        h = (x * (1.0 + mod_ref[4:5, :]) + mod_ref[3:4, :]).astype(BF16)
        y = b2_ref[...]
        for c0 in range(0, ff, MLP_FF_CHUNK):
            cs = slice(c0, c0 + MLP_FF_CHUNK)
            a = jnp.maximum(_dot(h, w1_ref[:, cs]) + b1_ref[:, cs], 0.0)
            y = y + _dot((a * a).astype(BF16), w2_ref[cs, :])
        r = ALPHA * x + mod_ref[5:6, :] * y
        o_ref[rs, :] = _layer_norm(r, lg_ref[...], lb_ref[...])


def _mlp(x, mod, w1, b1, w2, b2, lg, lb, layer):
    bsz, n, d = x.shape
    tm = min(n, 2 * MLP_ROW_CHUNK)
    whole = lambda a: pl.BlockSpec((None,) + a.shape[1:], lambda bi, i: (layer, 0, 0),
                                   pipeline_mode=pl.Buffered(1))
    return pl.pallas_call(
        _mlp_kernel,
        grid=(bsz, n // tm),
        in_specs=[pl.BlockSpec((None, tm, d), lambda bi, i: (bi, i, 0)),
                  pl.BlockSpec((None, N_MOD, d), lambda bi, i: (bi, 0, 0)),
                  whole(w1), whole(b1), whole(w2), whole(b2), whole(lg), whole(lb)],
        out_specs=pl.BlockSpec((None, tm, d), lambda bi, i: (bi, i, 0)),
        out_shape=jax.ShapeDtypeStruct((bsz, n, d), F32),
        compiler_params=_params("arbitrary", "arbitrary"),
        name="mlp_ln",
    )(x, mod, w1, b1, w2, b2, lg, lb)


def _int_grid(rows, cols):
    k = lax.broadcasted_iota(jnp.int32, (rows, cols), 0)
    s = lax.broadcasted_iota(jnp.int32, (rows, cols), 1)
    return k, s


TRIG_ROW_BLOCK = 32


def _trig_table(n_rows, samp, period, k_step=1, k_first=0):
    def base(kv):
        ang = ((kv[:, None] * samp[None, :]) % period).astype(F32) * (2.0 * math.pi / period)
        return jnp.cos(ang), jnp.sin(ang)

    ch, sh = base(jnp.arange(n_rows // TRIG_ROW_BLOCK, dtype=jnp.int32) * (TRIG_ROW_BLOCK * k_step))
    cl, sl = base(jnp.arange(TRIG_ROW_BLOCK, dtype=jnp.int32) * k_step + k_first)
    ch, sh, cl, sl = ch[:, None, :], sh[:, None, :], cl[None], sl[None]
    shape = (n_rows, samp.shape[0])
    return (ch * cl - sh * sl).reshape(shape), (sh * cl + ch * sl).reshape(shape)


def _hyena_dft(n):
    h = n // 2
    s = jnp.arange(h, dtype=jnp.int32)
    alt = jnp.where(s % 2 == 0, 1.0, -1.0)[None, :]
    first = lax.broadcasted_iota(jnp.int32, (h, h), 0) == 0
    out = {}
    for name, samp in (('e', 2 * s), ('o', 2 * s + 1)):
        cos, sin = _trig_table(h, samp, 2 * n)
        mf = jnp.concatenate([cos, jnp.where(first, alt, sin)], 0)
        out['mf' + name] = mf.astype(BF16)
        out['mi' + name] = mf.T.astype(BF16)
    return out


def _fnet_dft(n):
    h = n // 2
    s = jnp.arange(h, dtype=jnp.int32)
    out = {}
    for name, samp in (('e', 2 * s), ('o', 2 * s + 1)):
        blocks = []
        for parity in range(2):
            cos, sin = _trig_table(h // 2, samp, n, k_step=2, k_first=parity)
            blocks.append(jnp.concatenate([cos, -sin], 1))
        out['mn' + name] = jnp.concatenate(blocks, 0).astype(BF16)
    k, s = _int_grid(FNET_GROUP_DIM, FNET_GROUP_DIM)
    ang = ((k * s) % FNET_GROUP_DIM).astype(F32) * (2.0 * math.pi / FNET_GROUP_DIM)
    out['csc'] = jnp.concatenate([jnp.cos(ang), jnp.sin(ang)], 1).astype(BF16)
    return out


def _filter_embedding(n, width):
    t = jnp.linspace(0.0, 1.0, n, dtype=F32)[:, None]
    w = (2.0 * math.pi / n) * jnp.arange(n, dtype=F32)[:, None]
    f = jnp.linspace(1e-4, HYENA_BANDS - 1, HYENA_BANDS, dtype=F32)[None, :]
    emb = jnp.concatenate([t, jnp.cos(f * w), -jnp.sin(f * w)], -1)
    return _even_odd_rows(jnp.pad(emb, ((0, 0), (0, width - HYENA_EMB))))


def _rope_tables(n):
    rows = n // GRID_W
    row = jnp.repeat(jnp.arange(rows, dtype=F32), GRID_W)
    col = jnp.tile(jnp.arange(GRID_W, dtype=F32), rows)
    inv_freq = ROPE_THETA ** (-jnp.arange(ROPE_AXIS_DIM // 2, dtype=F32) * 2.0 / ROPE_AXIS_DIM)
    ar = row[:, None] * inv_freq[None, :]
    ac = col[:, None] * inv_freq[None, :]
    cos = jnp.concatenate([jnp.cos(ar), jnp.cos(ar), jnp.cos(ac), jnp.cos(ac)] * 2, -1)
    sin = jnp.concatenate([-jnp.sin(ar), jnp.sin(ar), -jnp.sin(ac), jnp.sin(ac)] * 2, -1)
    return _even_odd_rows(cos), _even_odd_rows(sin)


def _mixer(p_tok, seq_shape, lw, tabs, rope_tabs, ctx_parts):
    bsz, n = seq_shape
    p = p_tok.reshape(bsz, n, p_tok.shape[-1])
    q, k = _headnorm(p, [(OFF_Q // Q_W, N_Q_HEADS, lw['q_gain'], ATTN_SCALE),
                         (OFF_K // KV_W, N_KV_HEADS, lw['k_gain'], 1.0)], rope_tabs)
    o_att = _attention(q, [(k, p, OFF_V // KV_W)] + ctx_parts)

    k4, mid = _filters(n, tabs['emb'], lw['hy_w1'], lw['hy_b1'], lw['hy_freq1'], lw['hy_w2'], lw['hy_b2'],
                       lw['hy_freq2'], lw['hy_w3'], tabs['absdelta'], tabs['mfe'], tabs['mfo'])
    z = None
    for o in range(HYENA_ORDER):
        z = _hyconv(o, z, p, lw['conv_w'], lw['conv_b'], tabs, k4, mid, lw['hy_skip'])
    y_fn = _fnet(p, tabs['csc'], tabs['mne'], tabs['mno'])
    tok = lambda a: a.reshape(p_tok.shape[0], p_tok.shape[1], a.shape[-1])
    return tok(o_att), tok(z), tok(y_fn), k, p


def _block(x_tok, mod, seq_shape, layer, sw, lw, tabs, rope_tabs, ctx_parts):
    p_tok = _inproj(x_tok, mod, sw['w_in'], sw['b_in'], layer)
    o_att, z_hy, y_fn, k, p = _mixer(p_tok, seq_shape, lw, tabs, rope_tabs, ctx_parts)
    x_tok = _merge(o_att, z_hy, y_fn, p_tok, x_tok, mod, sw['w_att_o'], sw['w_hy_o'], sw['w_fn_o'],
                   sw['w_out'], sw['b_out'], sw['ln1_g'], sw['ln1_b'], layer)
    x_tok = _mlp(x_tok, mod, sw['w_mlp1'], sw['b_mlp1'], sw['w_mlp2'], sw['b_mlp2'], sw['ln2_g'], sw['ln2_b'],
                 layer)
    return x_tok, k, p


def _seq_tables(n):
    tabs = dict(_hyena_dft(n))
    tabs.update(_fnet_dft(n))
    deltas = jnp.abs(jnp.linspace(MIN_DECAY, MAX_DECAY, D_HYENA, dtype=F32))
    tabs.update(emb=_filter_embedding(n, 128), absdelta=jnp.tile(deltas, HYENA_ORDER)[None, :])
    return tabs


def kernel(x, c, ctx, c_ctx, w_ada, b_ada, w_in, b_in, conv_w, conv_b, hy_w1, hy_b1, hy_freq1, hy_w2, hy_b2,
           hy_freq2, hy_w3, hy_skip, q_gain, k_gain, w_att_o, w_hy_o, w_fn_o, w_out, b_out, ln1_g, ln1_b,
           w_mlp1, b_mlp1, w_mlp2, b_mlp2, ln2_g, ln2_b):
    bsz, n_lat, d = x.shape
    n_ctx = ctx.shape[1]
    depth = w_ada.shape[0]
    tabs_lat = _seq_tables(n_lat)
    tabs_ctx = _seq_tables(n_ctx)
    rope_tabs = _rope_tables(n_lat)

    n_rows = -(-(bsz + 1) // 16) * 16
    cond = jnp.concatenate([c, c_ctx[None, :], jnp.zeros((n_rows - bsz - 1, d), F32)], 0)

    x = _deinterleave(x)
    ctx_tok = _deinterleave(ctx).reshape(1, bsz * n_ctx, d)
    row = lambda a: a[None, :]
    rows = lambda a: a[:, None, :]
    mods = _matmul(cond, w_ada, rows(b_ada), silu=True)
    sw = dict(
        w_in=w_in.astype(BF16), b_in=rows(b_in),
        w_att_o=w_att_o.astype(BF16), w_hy_o=w_hy_o.astype(BF16), w_fn_o=w_fn_o.astype(BF16),
        w_out=w_out.astype(BF16), b_out=rows(b_out), ln1_g=rows(ln1_g), ln1_b=rows(ln1_b),
        w_mlp1=w_mlp1.astype(BF16), b_mlp1=rows(b_mlp1), w_mlp2=w_mlp2.astype(BF16), b_mlp2=rows(b_mlp2),
        ln2_g=rows(ln2_g), ln2_b=rows(ln2_b))
    for i in range(depth):
        last = i == depth - 1
        lw = dict(
            conv_w=conv_w[i], conv_b=row(conv_b[i]),
            hy_w1=jnp.pad(hy_w1[i], ((0, 128 - HYENA_EMB), (0, 0))), hy_b1=row(hy_b1[i]), hy_freq1=row(hy_freq1[i]),
            hy_w2=hy_w2[i], hy_b2=row(hy_b2[i]), hy_freq2=row(hy_freq2[i]), hy_w3=hy_w3[i],
            hy_skip=hy_skip[i].reshape(1, HYENA_ORDER * D_HYENA),
            q_gain=row(q_gain[i]), k_gain=row(k_gain[i]))
        mod_l = mods[i, :bsz].reshape(bsz, N_MOD, d)
        mod_c = mods[i, bsz:bsz + 1].reshape(1, N_MOD, d)

        if last:
            kv_cols = slice(OFF_K, OFF_HY)
            p_c = _inproj(ctx_tok, mod_c, sw['w_in'][i:i + 1, :, kv_cols], sw['b_in'][i:i + 1, :, kv_cols], 0)
            p_c = p_c.reshape(bsz, n_ctx, 2 * KV_W)
            k_c, = _headnorm(p_c, [(0, N_KV_HEADS, lw['k_gain'], 1.0)], None)
            v_blk = 1
        else:
            ctx_tok, k_c, p_c = _block(ctx_tok, mod_c, (bsz, n_ctx), i, sw, lw, tabs_ctx, None, [])
            v_blk = OFF_V // KV_W
        x, _, _ = _block(x, mod_l, (bsz, n_lat), i, sw, lw, tabs_lat, rope_tabs, [(k_c, p_c, v_blk)])
    return _interleave(x)
```

```python
import functools
import math

import jax
import jax.numpy as jnp
from jax import lax
from jax.experimental import pallas as pl
from jax.experimental.pallas import tpu as pltpu

F32 = jnp.float32
BF16 = jnp.bfloat16

D_MODEL = 1024
GRID_W = 64
HEAD_DIM = 128
N_Q_HEADS = D_MODEL // HEAD_DIM
N_KV_HEADS = 2
GQA_GROUP = N_Q_HEADS // N_KV_HEADS
ROPE_THETA = 10000.0
ROPE_AXIS_DIM = HEAD_DIM // 2
ATTN_SCALE = HEAD_DIM ** -0.5

D_HYENA = D_MODEL
HYENA_ORDER = 2
HYENA_BANDS = 16
HYENA_EMB = 1 + 2 * HYENA_BANDS
SHORT_CONV = 3
DECAY_TARGET = 1e-2
MIN_DECAY = math.log(DECAY_TARGET) / 0.3
MAX_DECAY = math.log(DECAY_TARGET) / 1.5

FNET_GROUPS = 4
FNET_GROUP_DIM = D_MODEL // FNET_GROUPS
D_FF = 4 * D_MODEL
N_MOD = 6
DEPTH = 4

Q_W = N_Q_HEADS * HEAD_DIM
KV_W = N_KV_HEADS * HEAD_DIM
HY_W = (HYENA_ORDER + 1) * D_HYENA
OFF_Q = 0
OFF_K = OFF_Q + Q_W
OFF_V = OFF_K + KV_W
OFF_HY = OFF_V + KV_W
OFF_FN = OFF_HY + HY_W
OFF_G = OFF_FN + D_MODEL
D_IN = OFF_G + 3 * D_MODEL

HALF_D = D_MODEL // 2

ALPHA = (2 * DEPTH) ** 0.25
LN_EPS = 1e-6
RMS_EPS = 1e-6
KERN_EPS = 1e-6

V7X_VMEM_LIMIT_BYTES = 56 * 1024 * 1024


def _params(*sem):
    return pltpu.CompilerParams(dimension_semantics=sem, vmem_limit_bytes=V7X_VMEM_LIMIT_BYTES)


def _dot(a, b):
    return jnp.dot(a, b, preferred_element_type=F32)


def _layer_norm(r, g, b):
    mu = jnp.mean(r, -1, keepdims=True)
    d = r - mu
    var = jnp.mean(d * d, -1, keepdims=True)
    return d * lax.rsqrt(var + LN_EPS) * g + b


def _sigmoid(x):
    return 1.0 / (1.0 + jnp.exp(-x))


def _deinterleave_kernel(x_ref, o_ref):
    hh = x_ref.shape[1] // 2
    o_ref[:, 0:hh, :] = x_ref[:, pl.ds(0, hh, stride=2), :]
    o_ref[:, hh:2 * hh, :] = x_ref[:, pl.ds(1, hh, stride=2), :]


def _interleave_kernel(x_ref, o_ref):
    hh = x_ref.shape[1] // 2
    o_ref[:, pl.ds(0, hh, stride=2), :] = x_ref[:, 0:hh, :]
    o_ref[:, pl.ds(1, hh, stride=2), :] = x_ref[:, hh:2 * hh, :]


REORDER_BLOCK_ROWS = 2048


def _reorder_tokens(x, body, name):
    bsz, n, d = x.shape
    tc = 128
    bb = min(bsz, max(1, REORDER_BLOCK_ROWS // n))
    spec = pl.BlockSpec((bb, n, tc), lambda bi, c: (bi, 0, c))
    return pl.pallas_call(
        body,
        grid=(bsz // bb, d // tc),
        in_specs=[spec],
        out_specs=spec,
        out_shape=jax.ShapeDtypeStruct(x.shape, x.dtype),
        compiler_params=_params("arbitrary", "arbitrary"),
        name=name,
    )(x)


def _deinterleave(x):
    return _reorder_tokens(x, _deinterleave_kernel, "deinterleave")


def _interleave(x):
    return _reorder_tokens(x, _interleave_kernel, "interleave")


def _even_odd_rows(a):
    return jnp.concatenate([a[0::2], a[1::2]], 0)


def _matmul_kernel(a_ref, b_ref, bias_ref, o_ref, *, silu):
    a = a_ref[...]
    if silu:
        a = a * _sigmoid(a)
    o_ref[...] = _dot(a.astype(BF16), b_ref[...].astype(BF16)) + bias_ref[...]


def _matmul(a, b, bias, *, silu=False, tn=1024):
    m, k = a.shape
    nl, _, n = b.shape
    return pl.pallas_call(
        functools.partial(_matmul_kernel, silu=silu),
        grid=(nl, n // tn),
        in_specs=[pl.BlockSpec((m, k), lambda l, j: (0, 0)),
                  pl.BlockSpec((None, k, tn), lambda l, j: (l, 0, j)),
                  pl.BlockSpec((None, 1, tn), lambda l, j: (l, 0, j))],
        out_specs=pl.BlockSpec((None, m, tn), lambda l, j: (l, 0, j)),
        out_shape=jax.ShapeDtypeStruct((nl, m, n), F32),
        compiler_params=_params("arbitrary", "arbitrary"),
        name="ada_matmul",
    )(a, b, bias)


HEAD_PAIR = 2 * HEAD_DIM

INPROJ_PLAN = ((OFF_Q, OFF_HY, 'heads', tuple((c, 0) for c in range(0, Q_W, HEAD_PAIR)) + ((OFF_K, 1),)),
               (OFF_HY, OFF_FN, 'plain', ()),
               (OFF_FN, OFF_G, 'plain', ()),
               (OFF_G, D_IN, 'sigmoid', ()))
INPROJ_KV_PLAN = ((0, 2 * KV_W, 'heads', ((0, 1),)),)


def _inproj_kernel(*refs, plan, rope):
    x_ref, mod_ref, w_ref, b_ref, gq_ref, gk_ref, avg_ref = refs[:7]
    if rope:
        swap_ref, cos_ref, sin_ref = refs[7:10]
    o_ref = refs[-1]
    gains = (gq_ref, gk_ref)
    h = (x_ref[...] * (1.0 + mod_ref[1:2, :]) + mod_ref[0:1, :]).astype(BF16)
    for c0, c1, mode, pairs in plan:
        y = _dot(h, w_ref[:, c0:c1]) + b_ref[:, c0:c1]
        if mode == 'sigmoid':
            y = _sigmoid(y)
        if mode != 'heads':
            o_ref[:, c0:c1] = y.astype(o_ref.dtype)
            continue
        done = c0
        for off, gi in pairs:
            if c0 + off > done:
                o_ref[:, done:c0 + off] = y[:, done - c0:off].astype(o_ref.dtype)
            yp = y[:, off:off + HEAD_PAIR]
            ms = _dot((yp * yp).astype(BF16), avg_ref[...])
            yn = yp * lax.rsqrt(ms + RMS_EPS) * gains[gi][...]
            if rope:
                yn = yn * cos_ref[...] + _dot(yn.astype(BF16), swap_ref[...]) * sin_ref[...]
            o_ref[:, c0 + off:c0 + off + HEAD_PAIR] = yn.astype(o_ref.dtype)
            done = c0 + off + HEAD_PAIR
        if done < c1:
            o_ref[:, done:c1] = y[:, done - c0:].astype(o_ref.dtype)


def _inproj(x, mod, w, b, layer, plan, q_gain, k_gain, rope_tabs):
    bsz, n, d = x.shape
    nout = w.shape[2]
    tm = min(n, 512)
    rope = rope_tabs is not None
    avg, swap = _head_pair_matrices()
    whole = lambda a: pl.BlockSpec((None,) + a.shape[1:], lambda bi, i: (layer, 0, 0),
                                   pipeline_mode=pl.Buffered(1))
    const = lambda a: pl.BlockSpec(a.shape, lambda bi, i: (0, 0))
    args = [x, mod, w, b, jnp.tile(q_gain * ATTN_SCALE, (1, 2)), jnp.tile(k_gain, (1, 2)), avg]
    in_specs = [pl.BlockSpec((None, tm, d), lambda bi, i: (bi, i, 0)),
                pl.BlockSpec((None, N_MOD, d), lambda bi, i: (bi, 0, 0)),
                whole(w), whole(b), const(args[4]), const(args[5]), const(avg)]
    if rope:
        in_specs += [const(swap)] + [pl.BlockSpec((tm, HEAD_PAIR), lambda bi, i: (i, 0))] * 2
        args += [swap] + list(rope_tabs)
    return pl.pallas_call(
        functools.partial(_inproj_kernel, plan=plan, rope=rope),
        grid=(bsz, n // tm),
        in_specs=in_specs,
        out_specs=pl.BlockSpec((None, tm, nout), lambda bi, i: (bi, i, 0)),
        out_shape=jax.ShapeDtypeStruct((bsz, n, nout), BF16),
        compiler_params=_params("arbitrary", "arbitrary"),
        name="inproj",
    )(*args)


def _head_pair_matrices():
    i, j = _int_grid(HEAD_PAIR, HEAD_PAIR)
    avg = jnp.where(i // HEAD_DIM == j // HEAD_DIM, 1.0 / HEAD_DIM, 0.0)
    half = ROPE_AXIS_DIM // 2
    partner = jnp.where(j % ROPE_AXIS_DIM < half, j + half, j - half)
    return avg.astype(BF16), jnp.where(i == partner, 1.0, 0.0).astype(BF16)


def _attn_kernel(*refs, n_parts):
    q_ref = refs[0]
    kv_refs = refs[1:1 + 2 * n_parts]
    o_ref = refs[1 + 2 * n_parts]
    for g in range(N_KV_HEADS):
        gsl = slice(g * HEAD_DIM, (g + 1) * HEAD_DIM)
        ks = [kv_refs[2 * i][:, gsl] for i in range(n_parts)]
        vs = [jnp.concatenate([kv_refs[2 * i + 1][:, gsl], jnp.ones((k.shape[0], HEAD_DIM), BF16)], 1)
              for i, k in enumerate(ks)]
        for r in range(GQA_GROUP):
            h = g * GQA_GROUP + r
            hsl = slice(h * HEAD_DIM, (h + 1) * HEAD_DIM)
            q = q_ref[:, hsl]
            ss = [lax.dot_general(q, k, (((1,), (1,)), ((), ())), preferred_element_type=F32) for k in ks]
            m = jnp.max(ss[0], -1, keepdims=True)
            for s in ss[1:]:
                m = jnp.maximum(m, jnp.max(s, -1, keepdims=True))
            o = None
            for s, v in zip(ss, vs):
                pv = _dot(jnp.exp((s - m).astype(BF16)), v)
                o = pv if o is None else o + pv
            o_ref[:, hsl] = (o[:, :HEAD_DIM] / o[:, HEAD_DIM:]).astype(o_ref.dtype)


def _attention(p, parts):
    bsz, n, _ = p.shape
    w = Q_W
    tq = min(n, 512)
    in_specs = [pl.BlockSpec((None, tq, w), lambda bi, i: (bi, i, OFF_Q // Q_W))]
    args = [p]
    for kv, kblk, vblk in parts:
        nk = kv.shape[1]
        for blk in (kblk, vblk):
            in_specs.append(pl.BlockSpec((None, nk, KV_W), functools.partial(lambda bi, i, c: (bi, 0, c), c=blk)))
        args += [kv, kv]
    return pl.pallas_call(
        functools.partial(_attn_kernel, n_parts=len(parts)),
        grid=(bsz, n // tq),
        in_specs=in_specs,
        out_specs=pl.BlockSpec((None, tq, w), lambda bi, i: (bi, i, 0)),
        out_shape=jax.ShapeDtypeStruct((bsz, n, w), BF16),
        compiler_params=_params("arbitrary", "arbitrary"),
        name="attention",
    )(*args)


FILTER_COL_CHUNK = 256


def _filter_kernel(emb_ref, w1_ref, b1_ref, f1_ref, w2_ref, b2_ref, f2_ref, w3f_ref, w3b_ref, dl_ref,
                   mfe_ref, mfo_ref, k4_ref, mid_ref, h_ref, *, n):
    hh = n // 2

    @pl.when(pl.program_id(0) == 0)
    def _():
        h1 = jnp.sin(f1_ref[...] * (_dot(emb_ref[...], w1_ref[...]) + b1_ref[...]))
        h_ref[...] = jnp.sin(f2_ref[...] * (_dot(h1, w2_ref[...]) + b2_ref[...])).astype(BF16)

    h = h_ref[...]
    cw = FILTER_COL_CHUNK
    row = lax.broadcasted_iota(jnp.int32, (n, cw), 0)
    lag = jnp.where(row < hh, 2 * row, 2 * (row - hh) + 1)
    t = lag.astype(F32) * (1.0 / (n - 1))
    sign = jnp.where(row % 2 == 0, 1.0, -1.0)
    row0 = lax.broadcasted_iota(jnp.int32, (hh, cw), 0) == 0
    for c0 in range(0, k4_ref.shape[2], cw):
        cs = slice(c0, c0 + cw)
        dec = jnp.exp(-t * dl_ref[:, cs])
        hf = _dot(h, w3f_ref[:, cs].astype(BF16)) * dec
        hb = jnp.where(lag == 0, 0.0, _dot(h, w3b_ref[:, cs].astype(BF16)) * dec)
        ssq = jnp.sum(hf * hf, 0, keepdims=True) + jnp.sum(hb * hb, 0, keepdims=True)
        nrm = lax.rsqrt(ssq + KERN_EPS) * (1.0 / n)
        hs = (hf + hb) * nrm
        hd = (hf - hb) * nrm
        mid_ref[0:1, cs] = jnp.sum(jnp.where(row < hh, hs * sign, 0.0), 0, keepdims=True)
        mid_ref[1:2, cs] = jnp.sum(jnp.where(row < hh, 0.0, hd * sign), 0, keepdims=True)
        hs, hd = hs.astype(BF16), hd.astype(BF16)
        p = _dot(mfe_ref[0:hh, :], hs[:hh])
        q = _dot(mfo_ref[0:hh, :], hs[hh:])
        pp = _dot(mfe_ref[hh:n, :], hd[:hh])
        qp = _dot(mfo_ref[hh:n, :], hd[hh:])
        k4_ref[0, :, cs] = jnp.where(row0, 0.5 * (p + q), p + q)
        k4_ref[1, :, cs] = jnp.where(row0, 0.0, pp + qp)
        k4_ref[2, :, cs] = jnp.where(row0, 0.5 * (p - q), p - q)
        k4_ref[3, :, cs] = jnp.where(row0, 0.0, qp - pp)


def _filters(n, emb, w1, b1, f1, w2, b2, f2, w3, absdelta, mfe, mfo):
    od = HYENA_ORDER * D_HYENA
    tn = 512
    fh = w2.shape[0]
    full = lambda a: pl.BlockSpec(a.shape, lambda j: (0, 0), pipeline_mode=pl.Buffered(1))
    return pl.pallas_call(
        functools.partial(_filter_kernel, n=n),
        grid=(od // tn,),
        in_specs=[full(emb), full(w1), full(b1), full(f1), full(w2), full(b2), full(f2),
                  pl.BlockSpec((fh, tn), lambda j: (0, j)),
                  pl.BlockSpec((fh, tn), lambda j: (0, j + od // tn)),
                  pl.BlockSpec((1, tn), lambda j: (0, j)),
                  full(mfe), full(mfo)],
        out_specs=[pl.BlockSpec((4, n // 2, tn), lambda j: (0, 0, j)),
                   pl.BlockSpec((2, tn), lambda j: (0, j))],
        out_shape=[jax.ShapeDtypeStruct((4, n // 2, od), F32), jax.ShapeDtypeStruct((2, od), F32)],
        scratch_shapes=[pltpu.VMEM((n, fh), BF16)],
        compiler_params=_params("arbitrary"),
        name="hyena_filters",
    )(emb, w1, b1, f1, w2, b2, f2, w3, w3, absdelta, mfe, mfo)


def _short_conv(x, w_ref, b_ref):
    hh = x.shape[0] // 2
    e = x[:hh]
    o = x[hh:]
    row = lax.broadcasted_iota(jnp.int32, e.shape, 0)
    o_prev = jnp.where(row == 0, 0.0, pltpu.roll(o, 1, 0))
    e_next = jnp.where(row == hh - 1, 0.0, pltpu.roll(e, hh - 1, 0))
    w0, w1, w2 = w_ref[0:1, :], w_ref[1:2, :], w_ref[2:3, :]
    return (b_ref[...] + o_prev * w0 + e * w1 + o * w2,
            b_ref[...] + e * w0 + o * w1 + e_next * w2)


def _hyconv_kernel(*refs, conv_z, cw, n):
    if conv_z:
        z_ref, cwz_ref, cbz_ref = refs[:3]
        refs = refs[3:]
    else:
        z_ref = refs[0]
        refs = refs[1:]
    x_ref, cwx_ref, cbx_ref, mfe_ref, mfo_ref, mie_ref, mio_ref, k4_ref, mid_ref, skip_ref, o_ref = refs
    hh = n // 2
    first8 = lax.broadcasted_iota(jnp.int32, (8, cw), 0) == 0
    for c in range(o_ref.shape[1] // cw):
        cs = slice(c * cw, (c + 1) * cw)
        if conv_z:
            ze, zo = _short_conv(z_ref[:, cs].astype(F32), cwz_ref.at[:, cs], cbz_ref.at[:, cs])
            ze, zo = ze.astype(BF16), zo.astype(BF16)
        else:
            ze, zo = z_ref[0:hh, cs], z_ref[hh:n, cs]
        fe = _dot(mfe_ref[...], ze)
        fo = _dot(mfo_ref[...], zo)
        p, pp, q, qp = fe[:hh], fe[hh:], fo[:hh], fo[hh:]
        kra, kia, krb, kib = k4_ref[0, :, cs], k4_ref[1, :, cs], k4_ref[2, :, cs], k4_ref[3, :, cs]
        fra, fia = p + q, pp + qp
        frb, fib = p - q, qp - pp
        wra = fra * kra - fia * kia
        wia = fra * kia + fia * kra
        wrb = frb * krb - fib * kib
        wib = frb * kib + fib * krb
        ge_im = wia - wib
        go_im = wia + wib
        kmr, kmi = mid_ref[0:1, cs], mid_ref[1:2, cs]
        pp8, qp8 = pp[:8], qp[:8]
        ge_im = jnp.concatenate([jnp.where(first8, pp8 * kmr - qp8 * kmi, ge_im[:8]), ge_im[8:]], 0)
        go_im = jnp.concatenate([jnp.where(first8, pp8 * kmi + qp8 * kmr, go_im[:8]), go_im[8:]], 0)
        ge = jnp.concatenate([(wra + wrb).astype(BF16), ge_im.astype(BF16)], 0)
        go = jnp.concatenate([(wra - wrb).astype(BF16), go_im.astype(BF16)], 0)
        ye = _dot(mie_ref[...], ge)
        yo = _dot(mio_ref[...], go)
        xe, xo = _short_conv(x_ref[:, cs].astype(F32), cwx_ref.at[:, cs], cbx_ref.at[:, cs])
        skip = skip_ref[:, cs]
        o_ref[0:hh, cs] = (xe * (ye + ze.astype(F32) * skip)).astype(o_ref.dtype)
        o_ref[hh:n, cs] = (xo * (yo + zo.astype(F32) * skip)).astype(o_ref.dtype)


def _hyconv(order, z_arr, p, conv_w, conv_b, tabs, k4, mid, skip):
    bsz, n, _ = p.shape
    h = n // 2
    tn = 512
    cw = 256
    nct = D_HYENA // tn
    conv_z = z_arr is None
    once = pl.Buffered(1)

    def sect(part):
        off = OFF_HY // tn + part * nct
        return pl.BlockSpec((None, n, tn), functools.partial(lambda c, bi, o: (bi, 0, o + c), o=off))

    def wsect(rows, part):
        return pl.BlockSpec((rows, tn), functools.partial(lambda c, bi, o: (0, o + c), o=part * nct))

    whole = lambda a: pl.BlockSpec(a.shape, lambda c, bi: (0, 0), pipeline_mode=once)
    ocol = functools.partial(lambda c, bi, o: (0, o + c), o=order * nct)
    if conv_z:
        in_specs = [sect(HYENA_ORDER), wsect(SHORT_CONV, HYENA_ORDER), wsect(1, HYENA_ORDER)]
        args = [p, conv_w, conv_b]
    else:
        in_specs = [pl.BlockSpec((None, n, tn), lambda c, bi: (bi, 0, c))]
        args = [z_arr]
    mats = [tabs['mfe'], tabs['mfo'], tabs['mie'], tabs['mio']]
    in_specs += [sect(order), wsect(SHORT_CONV, order), wsect(1, order)] + [whole(m) for m in mats]
    in_specs += [pl.BlockSpec((4, h, tn), functools.partial(lambda c, bi, o: (0, 0, o + c), o=order * nct),
                              pipeline_mode=once),
                 pl.BlockSpec((2, tn), ocol),
                 pl.BlockSpec((1, tn), ocol)]
    args += [p, conv_w, conv_b] + mats + [k4, mid, skip]
    return pl.pallas_call(
        functools.partial(_hyconv_kernel, conv_z=conv_z, cw=cw, n=n),
        grid=(nct, bsz),
        in_specs=in_specs,
        out_specs=pl.BlockSpec((None, n, tn), lambda c, bi: (bi, 0, c)),
        out_shape=jax.ShapeDtypeStruct((bsz, n, D_HYENA), BF16),
        compiler_params=_params("arbitrary", "arbitrary"),
        name="hyena_conv",
    )(*args)


def _fnet_kernel(ua_ref, ub_ref, csc_ref, mne_ref, mno_ref, o_ref, *, n, scale):
    hh = n // 2
    qq = n // 4
    gpb = HALF_D // FNET_GROUP_DIM
    for g in range(FNET_GROUPS):
        gsl = slice(g * FNET_GROUP_DIM, (g + 1) * FNET_GROUP_DIM)
        u_ref = (ua_ref, ub_ref)[g // gpb]
        u = u_ref[:, (g % gpb) * FNET_GROUP_DIM:(g % gpb + 1) * FNET_GROUP_DIM]
        t = _dot(u, csc_ref[...])
        tc = t[:, :FNET_GROUP_DIM].astype(BF16)
        ts = t[:, FNET_GROUP_DIM:].astype(BF16)
        a = _dot(mne_ref[...], jnp.concatenate([tc[:hh], ts[:hh]], 0))
        b = _dot(mno_ref[...], jnp.concatenate([tc[hh:], ts[hh:]], 0))
        lo = ((a + b) * scale).astype(o_ref.dtype)
        hi = ((a - b) * scale).astype(o_ref.dtype)
        o_ref[0:qq, gsl] = lo[:qq]
        o_ref[qq:hh, gsl] = hi[:qq]
        o_ref[hh:hh + qq, gsl] = lo[qq:]
        o_ref[hh + qq:n, gsl] = hi[qq:]


def _fnet(p, csc, mne, mno):
    bsz, n, _ = p.shape
    scale = 1.0 / math.sqrt(n * FNET_GROUP_DIM)
    whole = lambda a: pl.BlockSpec(a.shape, lambda bi: (0, 0), pipeline_mode=pl.Buffered(1))
    half = lambda c: pl.BlockSpec((None, n, HALF_D), functools.partial(lambda bi, c: (bi, 0, c), c=c))
    return pl.pallas_call(
        functools.partial(_fnet_kernel, n=n, scale=scale),
        grid=(bsz,),
        in_specs=[half(OFF_FN // HALF_D), half(OFF_FN // HALF_D + 1), whole(csc), whole(mne), whole(mno)],
        out_specs=pl.BlockSpec((None, n, D_MODEL), lambda bi: (bi, 0, 0)),
        out_shape=jax.ShapeDtypeStruct((bsz, n, D_MODEL), BF16),
        compiler_params=_params("arbitrary"),
        name="fnet",
    )(p, p, csc, mne, mno)


MERGE_ROW_CHUNK = 256


def _merge_kernel(oa_ref, hy_ref, fn_ref, ga0, ga1, gh0, gh1, gf0, gf1, x_ref, mod_ref,
                  wa_ref, wh_ref, wf_ref, wo_ref, bo_ref, lg_ref, lb_ref, o_ref):
    rc = min(MERGE_ROW_CHUNK, x_ref.shape[0])
    for r0 in range(0, x_ref.shape[0], rc):
        rs = slice(r0, r0 + rc)
        gate = lambda g0, g1: jnp.concatenate([g0[rs, :], g1[rs, :]], 1).astype(F32)
        m = gate(ga0, ga1) * _dot(oa_ref[rs, :], wa_ref[...])
        m = m + gate(gh0, gh1) * _dot(hy_ref[rs, :], wh_ref[...])
        m = m + gate(gf0, gf1) * _dot(fn_ref[rs, :], wf_ref[...])
        y = _dot(m.astype(BF16), wo_ref[...]) + bo_ref[...]
        r = ALPHA * x_ref[rs, :] + mod_ref[2:3, :] * y
        o_ref[rs, :] = _layer_norm(r, lg_ref[...], lb_ref[...])


def _merge(o_att, z_hy, y_fn, p, x, mod, wa, wh, wf, wo, bo, lg, lb, layer):
    bsz, n, d = x.shape
    tm = min(n, 512)
    tok = lambda c: pl.BlockSpec((None, tm, d), functools.partial(lambda bi, i, c: (bi, i, c), c=c))
    wspec = pl.BlockSpec((None, d, d), lambda bi, i: (layer, 0, 0))
    vspec = pl.BlockSpec((None, 1, d), lambda bi, i: (layer, 0, 0))
    gates = [pl.BlockSpec((None, tm, HALF_D), functools.partial(lambda bi, i, c: (bi, i, c), c=OFF_G // HALF_D + c))
             for c in range(3 * d // HALF_D)]
    return pl.pallas_call(
        _merge_kernel,
        grid=(bsz, n // tm),
        in_specs=[tok(0), tok(0), tok(0)] + gates + [tok(0),
                  pl.BlockSpec((None, N_MOD, d), lambda bi, i: (bi, 0, 0)),
                  wspec, wspec, wspec, wspec, vspec, vspec, vspec],
        out_specs=tok(0),
        out_shape=jax.ShapeDtypeStruct((bsz, n, d), F32),
        compiler_params=_params("arbitrary", "arbitrary"),
        name="merge_ln",
    )(o_att, z_hy, y_fn, *([p] * len(gates)), x, mod, wa, wh, wf, wo, bo, lg, lb)


MLP_ROW_CHUNK = 512
MLP_FF_CHUNK = 1024


def _mlp_kernel(x_ref, mod_ref, w1_ref, b1_ref, w2_ref, b2_ref, lg_ref, lb_ref, o_ref):
    ff = w1_ref.shape[1]
    rc = min(MLP_ROW_CHUNK, x_ref.shape[0])
    for r0 in range(0, x_ref.shape[0], rc):
        rs = slice(r0, r0 + rc)
        x = x_ref[rs, :]
        h = (x * (1.0 + mod_ref[4:5, :]) + mod_ref[3:4, :]).astype(BF16)
        y = b2_ref[...]
        for c0 in range(0, ff, MLP_FF_CHUNK):
            cs = slice(c0, c0 + MLP_FF_CHUNK)
            a = jnp.maximum(_dot(h, w1_ref[:, cs]) + b1_ref[:, cs], 0.0)
            y = y + _dot((a * a).astype(BF16), w2_ref[cs, :])
        r = ALPHA * x + mod_ref[5:6, :] * y
        o_ref[rs, :] = _layer_norm(r, lg_ref[...], lb_ref[...])


def _mlp(x, mod, w1, b1, w2, b2, lg, lb, layer):
    bsz, n, d = x.shape
    tm = min(n, 2 * MLP_ROW_CHUNK)
    whole = lambda a: pl.BlockSpec((None,) + a.shape[1:], lambda bi, i: (layer, 0, 0),
                                   pipeline_mode=pl.Buffered(1))
    return pl.pallas_call(
        _mlp_kernel,
        grid=(bsz, n // tm),
        in_specs=[pl.BlockSpec((None, tm, d), lambda bi, i: (bi, i, 0)),
                  pl.BlockSpec((None, N_MOD, d), lambda bi, i: (bi, 0, 0)),
                  whole(w1), whole(b1), whole(w2), whole(b2), whole(lg), whole(lb)],
        out_specs=pl.BlockSpec((None, tm, d), lambda bi, i: (bi, i, 0)),
        out_shape=jax.ShapeDtypeStruct((bsz, n, d), F32),
        compiler_params=_params("arbitrary", "arbitrary"),
        name="mlp_ln",
    )(x, mod, w1, b1, w2, b2, lg, lb)


def _int_grid(rows, cols):
    k = lax.broadcasted_iota(jnp.int32, (rows, cols), 0)
    s = lax.broadcasted_iota(jnp.int32, (rows, cols), 1)
    return k, s


TRIG_ROW_BLOCK = 32


def _trig_table(n_rows, samp, period, k_step=1, k_first=0):
    def base(kv):
        ang = ((kv[:, None] * samp[None, :]) % period).astype(F32) * (2.0 * math.pi / period)
        return jnp.cos(ang), jnp.sin(ang)

    ch, sh = base(jnp.arange(n_rows // TRIG_ROW_BLOCK, dtype=jnp.int32) * (TRIG_ROW_BLOCK * k_step))
    cl, sl = base(jnp.arange(TRIG_ROW_BLOCK, dtype=jnp.int32) * k_step + k_first)
    ch, sh, cl, sl = ch[:, None, :], sh[:, None, :], cl[None], sl[None]
    shape = (n_rows, samp.shape[0])
    return (ch * cl - sh * sl).reshape(shape), (sh * cl + ch * sl).reshape(shape)


def _hyena_dft(n):
    h = n // 2
    s = jnp.arange(h, dtype=jnp.int32)
    alt = jnp.where(s % 2 == 0, 1.0, -1.0)[None, :]
    first = lax.broadcasted_iota(jnp.int32, (h, h), 0) == 0
    out = {}
    for name, samp in (('e', 2 * s), ('o', 2 * s + 1)):
        cos, sin = _trig_table(h, samp, 2 * n)
        mf = jnp.concatenate([cos, jnp.where(first, alt, sin)], 0)
        out['mf' + name] = mf.astype(BF16)
        out['mi' + name] = mf.T.astype(BF16)
    return out


def _fnet_dft(n):
    h = n // 2
    s = jnp.arange(h, dtype=jnp.int32)
    out = {}
    for name, samp in (('e', 2 * s), ('o', 2 * s + 1)):
        blocks = []
        for parity in range(2):
            cos, sin = _trig_table(h // 2, samp, n, k_step=2, k_first=parity)
            blocks.append(jnp.concatenate([cos, -sin], 1))
        out['mn' + name] = jnp.concatenate(blocks, 0).astype(BF16)
    k, s = _int_grid(FNET_GROUP_DIM, FNET_GROUP_DIM)
    ang = ((k * s) % FNET_GROUP_DIM).astype(F32) * (2.0 * math.pi / FNET_GROUP_DIM)
    out['csc'] = jnp.concatenate([jnp.cos(ang), jnp.sin(ang)], 1).astype(BF16)
    return out


def _filter_embedding(n, width):
    t = jnp.linspace(0.0, 1.0, n, dtype=F32)[:, None]
    w = (2.0 * math.pi / n) * jnp.arange(n, dtype=F32)[:, None]
    f = jnp.linspace(1e-4, HYENA_BANDS - 1, HYENA_BANDS, dtype=F32)[None, :]
    emb = jnp.concatenate([t, jnp.cos(f * w), -jnp.sin(f * w)], -1)
    return _even_odd_rows(jnp.pad(emb, ((0, 0), (0, width - HYENA_EMB))))


def _rope_tables(n):
    rows = n // GRID_W
    row = jnp.repeat(jnp.arange(rows, dtype=F32), GRID_W)
    col = jnp.tile(jnp.arange(GRID_W, dtype=F32), rows)
    inv_freq = ROPE_THETA ** (-jnp.arange(ROPE_AXIS_DIM // 2, dtype=F32) * 2.0 / ROPE_AXIS_DIM)
    ar = row[:, None] * inv_freq[None, :]
    ac = col[:, None] * inv_freq[None, :]
    cos = jnp.concatenate([jnp.cos(ar), jnp.cos(ar), jnp.cos(ac), jnp.cos(ac)] * 2, -1)
    sin = jnp.concatenate([-jnp.sin(ar), jnp.sin(ar), -jnp.sin(ac), jnp.sin(ac)] * 2, -1)
    return _even_odd_rows(cos), _even_odd_rows(sin)


def _mixer(p_tok, seq_shape, lw, tabs, ctx_parts):
    bsz, n = seq_shape
    p = p_tok.reshape(bsz, n, p_tok.shape[-1])
    o_att = _attention(p, [(p, OFF_K // KV_W, OFF_V // KV_W)] + ctx_parts)

    k4, mid = _filters(n, tabs['emb'], lw['hy_w1'], lw['hy_b1'], lw['hy_freq1'], lw['hy_w2'], lw['hy_b2'],
                       lw['hy_freq2'], lw['hy_w3'], tabs['absdelta'], tabs['mfe'], tabs['mfo'])
    z = None
    for o in range(HYENA_ORDER):
        z = _hyconv(o, z, p, lw['conv_w'], lw['conv_b'], tabs, k4, mid, lw['hy_skip'])
    y_fn = _fnet(p, tabs['csc'], tabs['mne'], tabs['mno'])
    tok = lambda a: a.reshape(p_tok.shape[0], p_tok.shape[1], a.shape[-1])
    return tok(o_att), tok(z), tok(y_fn), p


def _block(x_tok, mod, seq_shape, layer, sw, lw, tabs, rope_tabs, ctx_parts):
    p_tok = _inproj(x_tok, mod, sw['w_in'], sw['b_in'], layer, INPROJ_PLAN, lw['q_gain'], lw['k_gain'], rope_tabs)
    o_att, z_hy, y_fn, p = _mixer(p_tok, seq_shape, lw, tabs, ctx_parts)
    x_tok = _merge(o_att, z_hy, y_fn, p_tok, x_tok, mod, sw['w_att_o'], sw['w_hy_o'], sw['w_fn_o'],
                   sw['w_out'], sw['b_out'], sw['ln1_g'], sw['ln1_b'], layer)
    x_tok = _mlp(x_tok, mod, sw['w_mlp1'], sw['b_mlp1'], sw['w_mlp2'], sw['b_mlp2'], sw['ln2_g'], sw['ln2_b'],
                 layer)
    return x_tok, p


def _seq_tables(n):
    tabs = dict(_hyena_dft(n))
    tabs.update(_fnet_dft(n))
    deltas = jnp.abs(jnp.linspace(MIN_DECAY, MAX_DECAY, D_HYENA, dtype=F32))
    tabs.update(emb=_filter_embedding(n, 128), absdelta=jnp.tile(deltas, HYENA_ORDER)[None, :])
    return tabs


def kernel(x, c, ctx, c_ctx, w_ada, b_ada, w_in, b_in, conv_w, conv_b, hy_w1, hy_b1, hy_freq1, hy_w2, hy_b2,
           hy_freq2, hy_w3, hy_skip, q_gain, k_gain, w_att_o, w_hy_o, w_fn_o, w_out, b_out, ln1_g, ln1_b,
           w_mlp1, b_mlp1, w_mlp2, b_mlp2, ln2_g, ln2_b):
    bsz, n_lat, d = x.shape
    n_ctx = ctx.shape[1]
    depth = w_ada.shape[0]
    tabs_lat = _seq_tables(n_lat)
    tabs_ctx = _seq_tables(n_ctx)
    rope_tabs = _rope_tables(n_lat)

    n_rows = -(-(bsz + 1) // 16) * 16
    cond = jnp.concatenate([c, c_ctx[None, :], jnp.zeros((n_rows - bsz - 1, d), F32)], 0)

    x = _deinterleave(x)
    ctx_tok = _deinterleave(ctx).reshape(1, bsz * n_ctx, d)
    row = lambda a: a[None, :]
    rows = lambda a: a[:, None, :]
    mods = _matmul(cond, w_ada, rows(b_ada), silu=True)
    sw = dict(
        w_in=w_in.astype(BF16), b_in=rows(b_in),
        w_att_o=w_att_o.astype(BF16), w_hy_o=w_hy_o.astype(BF16), w_fn_o=w_fn_o.astype(BF16),
        w_out=w_out.astype(BF16), b_out=rows(b_out), ln1_g=rows(ln1_g), ln1_b=rows(ln1_b),
        w_mlp1=w_mlp1.astype(BF16), b_mlp1=rows(b_mlp1), w_mlp2=w_mlp2.astype(BF16), b_mlp2=rows(b_mlp2),
        ln2_g=rows(ln2_g), ln2_b=rows(ln2_b))
    for i in range(depth):
        last = i == depth - 1
        lw = dict(
            conv_w=conv_w[i], conv_b=row(conv_b[i]),
            hy_w1=jnp.pad(hy_w1[i], ((0, 128 - HYENA_EMB), (0, 0))), hy_b1=row(hy_b1[i]), hy_freq1=row(hy_freq1[i]),
            hy_w2=hy_w2[i], hy_b2=row(hy_b2[i]), hy_freq2=row(hy_freq2[i]), hy_w3=hy_w3[i],
            hy_skip=hy_skip[i].reshape(1, HYENA_ORDER * D_HYENA),
            q_gain=row(q_gain[i]), k_gain=row(k_gain[i]))
        mod_l = mods[i, :bsz].reshape(bsz, N_MOD, d)
        mod_c = mods[i, bsz:bsz + 1].reshape(1, N_MOD, d)

        if last:
            kv_cols = slice(OFF_K, OFF_HY)
            p_c = _inproj(ctx_tok, mod_c, sw['w_in'][i:i + 1, :, kv_cols], sw['b_in'][i:i + 1, :, kv_cols], 0,
                          INPROJ_KV_PLAN, lw['q_gain'], lw['k_gain'], None)
            ctx_kv = (p_c.reshape(bsz, n_ctx, 2 * KV_W), 0, 1)
        else:
            ctx_tok, p_c = _block(ctx_tok, mod_c, (bsz, n_ctx), i, sw, lw, tabs_ctx, None, [])
            ctx_kv = (p_c, OFF_K // KV_W, OFF_V // KV_W)
        x, _ = _block(x, mod_l, (bsz, n_lat), i, sw, lw, tabs_lat, rope_tabs, [ctx_kv])
    return _interleave(x)
```

```python
import functools
import math

import jax
import jax.numpy as jnp
from jax import lax
from jax.experimental import pallas as pl
from jax.experimental.pallas import tpu as pltpu

F32 = jnp.float32
BF16 = jnp.bfloat16

D_MODEL = 1024
GRID_W = 64
HEAD_DIM = 128
N_Q_HEADS = D_MODEL // HEAD_DIM
N_KV_HEADS = 2
GQA_GROUP = N_Q_HEADS // N_KV_HEADS
ROPE_THETA = 10000.0
ROPE_AXIS_DIM = HEAD_DIM // 2
ATTN_SCALE = HEAD_DIM ** -0.5

D_HYENA = D_MODEL
HYENA_ORDER = 2
HYENA_BANDS = 16
HYENA_EMB = 1 + 2 * HYENA_BANDS
SHORT_CONV = 3
DECAY_TARGET = 1e-2
MIN_DECAY = math.log(DECAY_TARGET) / 0.3
MAX_DECAY = math.log(DECAY_TARGET) / 1.5

FNET_GROUPS = 4
FNET_GROUP_DIM = D_MODEL // FNET_GROUPS
D_FF = 4 * D_MODEL
N_MOD = 6
DEPTH = 4

Q_W = N_Q_HEADS * HEAD_DIM
KV_W = N_KV_HEADS * HEAD_DIM
HY_W = (HYENA_ORDER + 1) * D_HYENA
OFF_Q = 0
OFF_K = OFF_Q + Q_W
OFF_V = OFF_K + KV_W
OFF_HY = OFF_V + KV_W
OFF_FN = OFF_HY + HY_W
OFF_G = OFF_FN + D_MODEL
D_IN = OFF_G + 3 * D_MODEL

HALF_D = D_MODEL // 2

ALPHA = (2 * DEPTH) ** 0.25
LN_EPS = 1e-6
RMS_EPS = 1e-6
KERN_EPS = 1e-6

V7X_VMEM_LIMIT_BYTES = 56 * 1024 * 1024


def _params(*sem):
    return pltpu.CompilerParams(dimension_semantics=sem, vmem_limit_bytes=V7X_VMEM_LIMIT_BYTES)


def _dot(a, b):
    return jnp.dot(a, b, preferred_element_type=F32)


def _layer_norm(r, g, b):
    mu = jnp.mean(r, -1, keepdims=True)
    d = r - mu
    var = jnp.mean(d * d, -1, keepdims=True)
    return d * lax.rsqrt(var + LN_EPS) * g + b


def _sigmoid(x):
    return 1.0 / (1.0 + jnp.exp(-x))


def _deinterleave_kernel(x_ref, o_ref):
    hh = x_ref.shape[1] // 2
    o_ref[:, 0:hh, :] = x_ref[:, pl.ds(0, hh, stride=2), :]
    o_ref[:, hh:2 * hh, :] = x_ref[:, pl.ds(1, hh, stride=2), :]


def _interleave_kernel(x_ref, o_ref):
    hh = x_ref.shape[1] // 2
    o_ref[:, pl.ds(0, hh, stride=2), :] = x_ref[:, 0:hh, :]
    o_ref[:, pl.ds(1, hh, stride=2), :] = x_ref[:, hh:2 * hh, :]


REORDER_BLOCK_ROWS = 2048


def _reorder_tokens(x, body, name):
    bsz, n, d = x.shape
    tc = 128
    bb = min(bsz, max(1, REORDER_BLOCK_ROWS // n))
    spec = pl.BlockSpec((bb, n, tc), lambda bi, c: (bi, 0, c))
    return pl.pallas_call(
        body,
        grid=(bsz // bb, d // tc),
        in_specs=[spec],
        out_specs=spec,
        out_shape=jax.ShapeDtypeStruct(x.shape, x.dtype),
        compiler_params=_params("arbitrary", "arbitrary"),
        name=name,
    )(x)


def _deinterleave(x):
    return _reorder_tokens(x, _deinterleave_kernel, "deinterleave")


def _interleave(x):
    return _reorder_tokens(x, _interleave_kernel, "interleave")


def _even_odd_rows(a):
    return jnp.concatenate([a[0::2], a[1::2]], 0)


def _matmul_kernel(a_ref, b_ref, bias_ref, o_ref, *, silu):
    a = a_ref[...]
    if silu:
        a = a * _sigmoid(a)
    o_ref[...] = _dot(a.astype(BF16), b_ref[...].astype(BF16)) + bias_ref[...]


def _matmul(a, b, bias, *, silu=False, tn=1024):
    m, k = a.shape
    nl, _, n = b.shape
    return pl.pallas_call(
        functools.partial(_matmul_kernel, silu=silu),
        grid=(nl, n // tn),
        in_specs=[pl.BlockSpec((m, k), lambda l, j: (0, 0)),
                  pl.BlockSpec((None, k, tn), lambda l, j: (l, 0, j)),
                  pl.BlockSpec((None, 1, tn), lambda l, j: (l, 0, j))],
        out_specs=pl.BlockSpec((None, m, tn), lambda l, j: (l, 0, j)),
        out_shape=jax.ShapeDtypeStruct((nl, m, n), F32),
        compiler_params=_params("arbitrary", "arbitrary"),
        name="ada_matmul",
    )(a, b, bias)


HEAD_PAIR = 2 * HEAD_DIM

INPROJ_PLAN = ((OFF_Q, OFF_HY, 'heads', tuple((c, 0) for c in range(0, Q_W, HEAD_PAIR)) + ((OFF_K, 1),)),
               (OFF_HY, OFF_FN, 'plain', ()),
               (OFF_FN, OFF_G, 'plain', ()),
               (OFF_G, D_IN, 'sigmoid', ()))
INPROJ_KV_PLAN = ((0, 2 * KV_W, 'heads', ((0, 1),)),)


def _inproj_kernel(*refs, plan, rope):
    x_ref, mod_ref, w_ref, b_ref, gq_ref, gk_ref, avg_ref = refs[:7]
    if rope:
        swap_ref, cos_ref, sin_ref = refs[7:10]
    o_ref = refs[-1]
    gains = (gq_ref, gk_ref)
    h = (x_ref[...] * (1.0 + mod_ref[1:2, :]) + mod_ref[0:1, :]).astype(BF16)
    for c0, c1, mode, pairs in plan:
        y = _dot(h, w_ref[:, c0:c1]) + b_ref[:, c0:c1]
        if mode == 'sigmoid':
            y = _sigmoid(y)
        if mode != 'heads':
            o_ref[:, c0:c1] = y.astype(o_ref.dtype)
            continue
        done = c0
        for off, gi in pairs:
            if c0 + off > done:
                o_ref[:, done:c0 + off] = y[:, done - c0:off].astype(o_ref.dtype)
            yp = y[:, off:off + HEAD_PAIR]
            ms = _dot((yp * yp).astype(BF16), avg_ref[...])
            yn = yp * lax.rsqrt(ms + RMS_EPS) * gains[gi][...]
            if rope:
                yn = yn * cos_ref[...] + _dot(yn.astype(BF16), swap_ref[...]) * sin_ref[...]
            o_ref[:, c0 + off:c0 + off + HEAD_PAIR] = yn.astype(o_ref.dtype)
            done = c0 + off + HEAD_PAIR
        if done < c1:
            o_ref[:, done:c1] = y[:, done - c0:].astype(o_ref.dtype)


def _inproj(x, mod, w, b, layer, plan, q_gain, k_gain, rope_tabs):
    bsz, n, d = x.shape
    nout = w.shape[2]
    tm = min(n, 512)
    rope = rope_tabs is not None
    avg, swap = _head_pair_matrices()
    whole = lambda a: pl.BlockSpec((None,) + a.shape[1:], lambda bi, i: (layer, 0, 0),
                                   pipeline_mode=pl.Buffered(1))
    const = lambda a: pl.BlockSpec(a.shape, lambda bi, i: (0, 0))
    args = [x, mod, w, b, jnp.tile(q_gain * ATTN_SCALE, (1, 2)), jnp.tile(k_gain, (1, 2)), avg]
    in_specs = [pl.BlockSpec((None, tm, d), lambda bi, i: (bi, i, 0)),
                pl.BlockSpec((None, N_MOD, d), lambda bi, i: (bi, 0, 0)),
                whole(w), whole(b), const(args[4]), const(args[5]), const(avg)]
    if rope:
        in_specs += [const(swap)] + [pl.BlockSpec((tm, HEAD_PAIR), lambda bi, i: (i, 0))] * 2
        args += [swap] + list(rope_tabs)
    return pl.pallas_call(
        functools.partial(_inproj_kernel, plan=plan, rope=rope),
        grid=(bsz, n // tm),
        in_specs=in_specs,
        out_specs=pl.BlockSpec((None, tm, nout), lambda bi, i: (bi, i, 0)),
        out_shape=jax.ShapeDtypeStruct((bsz, n, nout), BF16),
        compiler_params=_params("arbitrary", "arbitrary"),
        name="inproj",
    )(*args)


def _head_pair_matrices():
    i, j = _int_grid(HEAD_PAIR, HEAD_PAIR)
    avg = jnp.where(i // HEAD_DIM == j // HEAD_DIM, 1.0 / HEAD_DIM, 0.0)
    half = ROPE_AXIS_DIM // 2
    partner = jnp.where(j % ROPE_AXIS_DIM < half, j + half, j - half)
    return avg.astype(BF16), jnp.where(i == partner, 1.0, 0.0).astype(BF16)


def _attn_kernel(*refs, n_parts):
    q_ref = refs[0]
    kv_refs = refs[1:1 + 2 * n_parts]
    o_ref = refs[1 + 2 * n_parts]
    for g in range(N_KV_HEADS):
        gsl = slice(g * HEAD_DIM, (g + 1) * HEAD_DIM)
        ks = [kv_refs[2 * i][:, gsl] for i in range(n_parts)]
        vs = [jnp.concatenate([kv_refs[2 * i + 1][:, gsl], jnp.ones((k.shape[0], HEAD_DIM), BF16)], 1)
              for i, k in enumerate(ks)]
        for r in range(GQA_GROUP):
            h = g * GQA_GROUP + r
            hsl = slice(h * HEAD_DIM, (h + 1) * HEAD_DIM)
            q = q_ref[:, hsl]
            ss = [lax.dot_general(q, k, (((1,), (1,)), ((), ())), preferred_element_type=F32) for k in ks]
            m = jnp.max(ss[0], -1, keepdims=True)
            for s in ss[1:]:
                m = jnp.maximum(m, jnp.max(s, -1, keepdims=True))
            o = None
            for s, v in zip(ss, vs):
                pv = _dot(jnp.exp((s - m).astype(BF16)), v)
                o = pv if o is None else o + pv
            o_ref[:, hsl] = (o[:, :HEAD_DIM] / o[:, HEAD_DIM:]).astype(o_ref.dtype)


def _attention(p, parts):
    bsz, n, _ = p.shape
    w = Q_W
    tq = min(n, 1024)
    in_specs = [pl.BlockSpec((None, tq, w), lambda bi, i: (bi, i, OFF_Q // Q_W))]
    args = [p]
    for kv, kblk, vblk in parts:
        nk = kv.shape[1]
        for blk in (kblk, vblk):
            in_specs.append(pl.BlockSpec((None, nk, KV_W), functools.partial(lambda bi, i, c: (bi, 0, c), c=blk)))
        args += [kv, kv]
    return pl.pallas_call(
        functools.partial(_attn_kernel, n_parts=len(parts)),
        grid=(bsz, n // tq),
        in_specs=in_specs,
        out_specs=pl.BlockSpec((None, tq, w), lambda bi, i: (bi, i, 0)),
        out_shape=jax.ShapeDtypeStruct((bsz, n, w), BF16),
        compiler_params=_params("arbitrary", "arbitrary"),
        name="attention",
    )(*args)


FILTER_COL_CHUNK = 256


def _filter_kernel(emb_ref, w1_ref, b1_ref, f1_ref, w2_ref, b2_ref, f2_ref, w3f_ref, w3b_ref, dl_ref,
                   mfe_ref, mfo_ref, k4_ref, mid_ref, h_ref, *, n):
    hh = n // 2

    @pl.when(pl.program_id(0) == 0)
    def _():
        h1 = jnp.sin(f1_ref[...] * (_dot(emb_ref[...], w1_ref[...]) + b1_ref[...]))
        h_ref[...] = jnp.sin(f2_ref[...] * (_dot(h1, w2_ref[...]) + b2_ref[...])).astype(BF16)

    h = h_ref[...]
    cw = FILTER_COL_CHUNK
    row = lax.broadcasted_iota(jnp.int32, (n, cw), 0)
    lag = jnp.where(row < hh, 2 * row, 2 * (row - hh) + 1)
    t = lag.astype(F32) * (1.0 / (n - 1))
    sign = jnp.where(row % 2 == 0, 1.0, -1.0)
    row0 = lax.broadcasted_iota(jnp.int32, (hh, cw), 0) == 0
    for c0 in range(0, k4_ref.shape[2], cw):
        cs = slice(c0, c0 + cw)
        dec = jnp.exp(-t * dl_ref[:, cs])
        hf = _dot(h, w3f_ref[:, cs].astype(BF16)) * dec
        hb = jnp.where(lag == 0, 0.0, _dot(h, w3b_ref[:, cs].astype(BF16)) * dec)
        ssq = jnp.sum(hf * hf, 0, keepdims=True) + jnp.sum(hb * hb, 0, keepdims=True)
        nrm = lax.rsqrt(ssq + KERN_EPS) * (1.0 / n)
        hs = (hf + hb) * nrm
        hd = (hf - hb) * nrm
        mid_ref[0:1, cs] = jnp.sum(jnp.where(row < hh, hs * sign, 0.0), 0, keepdims=True)
        mid_ref[1:2, cs] = jnp.sum(jnp.where(row < hh, 0.0, hd * sign), 0, keepdims=True)
        hs, hd = hs.astype(BF16), hd.astype(BF16)
        p = _dot(mfe_ref[0:hh, :], hs[:hh])
        q = _dot(mfo_ref[0:hh, :], hs[hh:])
        pp = _dot(mfe_ref[hh:n, :], hd[:hh])
        qp = _dot(mfo_ref[hh:n, :], hd[hh:])
        k4_ref[0, :, cs] = jnp.where(row0, 0.5 * (p + q), p + q)
        k4_ref[1, :, cs] = jnp.where(row0, 0.0, pp + qp)
        k4_ref[2, :, cs] = jnp.where(row0, 0.5 * (p - q), p - q)
        k4_ref[3, :, cs] = jnp.where(row0, 0.0, qp - pp)


def _filters(n, emb, w1, b1, f1, w2, b2, f2, w3, absdelta, mfe, mfo):
    od = HYENA_ORDER * D_HYENA
    tn = 512
    fh = w2.shape[0]
    full = lambda a: pl.BlockSpec(a.shape, lambda j: (0, 0), pipeline_mode=pl.Buffered(1))
    return pl.pallas_call(
        functools.partial(_filter_kernel, n=n),
        grid=(od // tn,),
        in_specs=[full(emb), full(w1), full(b1), full(f1), full(w2), full(b2), full(f2),
                  pl.BlockSpec((fh, tn), lambda j: (0, j)),
                  pl.BlockSpec((fh, tn), lambda j: (0, j + od // tn)),
                  pl.BlockSpec((1, tn), lambda j: (0, j)),
                  full(mfe), full(mfo)],
        out_specs=[pl.BlockSpec((4, n // 2, tn), lambda j: (0, 0, j)),
                   pl.BlockSpec((2, tn), lambda j: (0, j))],
        out_shape=[jax.ShapeDtypeStruct((4, n // 2, od), F32), jax.ShapeDtypeStruct((2, od), F32)],
        scratch_shapes=[pltpu.VMEM((n, fh), BF16)],
        compiler_params=_params("arbitrary"),
        name="hyena_filters",
    )(emb, w1, b1, f1, w2, b2, f2, w3, w3, absdelta, mfe, mfo)


def _short_conv(x, w_ref, b_ref):
    hh = x.shape[0] // 2
    e = x[:hh]
    o = x[hh:]
    row = lax.broadcasted_iota(jnp.int32, e.shape, 0)
    o_prev = jnp.where(row == 0, 0.0, pltpu.roll(o, 1, 0))
    e_next = jnp.where(row == hh - 1, 0.0, pltpu.roll(e, hh - 1, 0))
    w0, w1, w2 = w_ref[0:1, :], w_ref[1:2, :], w_ref[2:3, :]
    return (b_ref[...] + o_prev * w0 + e * w1 + o * w2,
            b_ref[...] + e * w0 + o * w1 + e_next * w2)


def _hyconv_kernel(*refs, conv_z, cw, n):
    if conv_z:
        z_ref, cwz_ref, cbz_ref = refs[:3]
        refs = refs[3:]
    else:
        z_ref = refs[0]
        refs = refs[1:]
    x_ref, cwx_ref, cbx_ref, mfe_ref, mfo_ref, mie_ref, mio_ref, k4_ref, mid_ref, skip_ref, o_ref = refs
    hh = n // 2
    first8 = lax.broadcasted_iota(jnp.int32, (8, cw), 0) == 0
    chunks = [slice(c * cw, (c + 1) * cw) for c in range(o_ref.shape[1] // cw)]
    zs, fwd, gs, inv = [], [], [], []
    for cs in chunks:
        if conv_z:
            ze, zo = _short_conv(z_ref[:, cs].astype(F32), cwz_ref.at[:, cs], cbz_ref.at[:, cs])
            zs.append((ze.astype(BF16), zo.astype(BF16)))
        else:
            zs.append((z_ref[0:hh, cs], z_ref[hh:n, cs]))
    for ze, zo in zs:
        fwd.append((_dot(mfe_ref[...], ze), _dot(mfo_ref[...], zo)))
    xs = [_short_conv(x_ref[:, cs].astype(F32), cwx_ref.at[:, cs], cbx_ref.at[:, cs]) for cs in chunks]
    for cs, (fe, fo) in zip(chunks, fwd):
        p, pp, q, qp = fe[:hh], fe[hh:], fo[:hh], fo[hh:]
        kra, kia, krb, kib = k4_ref[0, :, cs], k4_ref[1, :, cs], k4_ref[2, :, cs], k4_ref[3, :, cs]
        fra, fia = p + q, pp + qp
        frb, fib = p - q, qp - pp
        wra = fra * kra - fia * kia
        wia = fra * kia + fia * kra
        wrb = frb * krb - fib * kib
        wib = frb * kib + fib * krb
        ge_im = wia - wib
        go_im = wia + wib
        kmr, kmi = mid_ref[0:1, cs], mid_ref[1:2, cs]
        pp8, qp8 = pp[:8], qp[:8]
        ge_im = jnp.concatenate([jnp.where(first8, pp8 * kmr - qp8 * kmi, ge_im[:8]), ge_im[8:]], 0)
        go_im = jnp.concatenate([jnp.where(first8, pp8 * kmi + qp8 * kmr, go_im[:8]), go_im[8:]], 0)
        gs.append(((wra + wrb).astype(BF16), ge_im.astype(BF16), (wra - wrb).astype(BF16), go_im.astype(BF16)))
    for ge_re, ge_im, go_re, go_im in gs:
        inv.append((_dot(mie_ref[:, :hh], ge_re) + _dot(mie_ref[:, hh:], ge_im),
                    _dot(mio_ref[:, :hh], go_re) + _dot(mio_ref[:, hh:], go_im)))
    for cs, (ze, zo), (ye, yo), (xe, xo) in zip(chunks, zs, inv, xs):
        skip = skip_ref[:, cs]
        o_ref[0:hh, cs] = (xe * (ye + ze.astype(F32) * skip)).astype(o_ref.dtype)
        o_ref[hh:n, cs] = (xo * (yo + zo.astype(F32) * skip)).astype(o_ref.dtype)


def _hyconv(order, z_arr, p, conv_w, conv_b, tabs, k4, mid, skip):
    bsz, n, _ = p.shape
    h = n // 2
    tn = 512
    cw = 256
    nct = D_HYENA // tn
    conv_z = z_arr is None
    once = pl.Buffered(1)

    def sect(part):
        off = OFF_HY // tn + part * nct
        return pl.BlockSpec((None, n, tn), functools.partial(lambda c, bi, o: (bi, 0, o + c), o=off))

    def wsect(rows, part):
        return pl.BlockSpec((rows, tn), functools.partial(lambda c, bi, o: (0, o + c), o=part * nct))

    whole = lambda a: pl.BlockSpec(a.shape, lambda c, bi: (0, 0), pipeline_mode=once)
    ocol = functools.partial(lambda c, bi, o: (0, o + c), o=order * nct)
    if conv_z:
        in_specs = [sect(HYENA_ORDER), wsect(SHORT_CONV, HYENA_ORDER), wsect(1, HYENA_ORDER)]
        args = [p, conv_w, conv_b]
    else:
        in_specs = [pl.BlockSpec((None, n, tn), lambda c, bi: (bi, 0, c))]
        args = [z_arr]
    mats = [tabs['mfe'], tabs['mfo'], tabs['mie'], tabs['mio']]
    in_specs += [sect(order), wsect(SHORT_CONV, order), wsect(1, order)] + [whole(m) for m in mats]
    in_specs += [pl.BlockSpec((4, h, tn), functools.partial(lambda c, bi, o: (0, 0, o + c), o=order * nct),
                              pipeline_mode=once),
                 pl.BlockSpec((2, tn), ocol),
                 pl.BlockSpec((1, tn), ocol)]
    args += [p, conv_w, conv_b] + mats + [k4, mid, skip]
    return pl.pallas_call(
        functools.partial(_hyconv_kernel, conv_z=conv_z, cw=cw, n=n),
        grid=(nct, bsz),
        in_specs=in_specs,
        out_specs=pl.BlockSpec((None, n, tn), lambda c, bi: (bi, 0, c)),
        out_shape=jax.ShapeDtypeStruct((bsz, n, D_HYENA), BF16),
        compiler_params=_params("arbitrary", "arbitrary"),
        name="hyena_conv",
    )(*args)


def _fnet_kernel(ua_ref, ub_ref, csc_ref, mne_ref, mno_ref, o_ref, *, n, scale):
    hh = n // 2
    qq = n // 4
    gpb = HALF_D // FNET_GROUP_DIM
    for g in range(FNET_GROUPS):
        gsl = slice(g * FNET_GROUP_DIM, (g + 1) * FNET_GROUP_DIM)
        u_ref = (ua_ref, ub_ref)[g // gpb]
        u = u_ref[:, (g % gpb) * FNET_GROUP_DIM:(g % gpb + 1) * FNET_GROUP_DIM]
        t = _dot(u, csc_ref[...])
        tc = t[:, :FNET_GROUP_DIM].astype(BF16)
        ts = t[:, FNET_GROUP_DIM:].astype(BF16)
        a = _dot(mne_ref[...], jnp.concatenate([tc[:hh], ts[:hh]], 0))
        b = _dot(mno_ref[...], jnp.concatenate([tc[hh:], ts[hh:]], 0))
        lo = ((a + b) * scale).astype(o_ref.dtype)
        hi = ((a - b) * scale).astype(o_ref.dtype)
        o_ref[0:qq, gsl] = lo[:qq]
        o_ref[qq:hh, gsl] = hi[:qq]
        o_ref[hh:hh + qq, gsl] = lo[qq:]
        o_ref[hh + qq:n, gsl] = hi[qq:]


def _fnet(p, csc, mne, mno):
    bsz, n, _ = p.shape
    scale = 1.0 / math.sqrt(n * FNET_GROUP_DIM)
    whole = lambda a: pl.BlockSpec(a.shape, lambda bi: (0, 0), pipeline_mode=pl.Buffered(1))
    half = lambda c: pl.BlockSpec((None, n, HALF_D), functools.partial(lambda bi, c: (bi, 0, c), c=c))
    return pl.pallas_call(
        functools.partial(_fnet_kernel, n=n, scale=scale),
        grid=(bsz,),
        in_specs=[half(OFF_FN // HALF_D), half(OFF_FN // HALF_D + 1), whole(csc), whole(mne), whole(mno)],
        out_specs=pl.BlockSpec((None, n, D_MODEL), lambda bi: (bi, 0, 0)),
        out_shape=jax.ShapeDtypeStruct((bsz, n, D_MODEL), BF16),
        compiler_params=_params("arbitrary"),
        name="fnet",
    )(p, p, csc, mne, mno)


MERGE_ROW_CHUNK = 256


def _merge_kernel(oa_ref, hy_ref, fn_ref, ga0, ga1, gh0, gh1, gf0, gf1, x_ref, mod_ref,
                  wa_ref, wh_ref, wf_ref, wo_ref, bo_ref, lg_ref, lb_ref, o_ref):
    rc = min(MERGE_ROW_CHUNK, x_ref.shape[0])
    for r0 in range(0, x_ref.shape[0], rc):
        rs = slice(r0, r0 + rc)
        gate = lambda g0, g1: jnp.concatenate([g0[rs, :], g1[rs, :]], 1).astype(F32)
        m = gate(ga0, ga1) * _dot(oa_ref[rs, :], wa_ref[...])
        m = m + gate(gh0, gh1) * _dot(hy_ref[rs, :], wh_ref[...])
        m = m + gate(gf0, gf1) * _dot(fn_ref[rs, :], wf_ref[...])
        y = _dot(m.astype(BF16), wo_ref[...]) + bo_ref[...]
        r = ALPHA * x_ref[rs, :] + mod_ref[2:3, :] * y
        o_ref[rs, :] = _layer_norm(r, lg_ref[...], lb_ref[...])


def _merge(o_att, z_hy, y_fn, p, x, mod, wa, wh, wf, wo, bo, lg, lb, layer):
    bsz, n, d = x.shape
    tm = min(n, 512)
    tok = lambda c: pl.BlockSpec((None, tm, d), functools.partial(lambda bi, i, c: (bi, i, c), c=c))
    wspec = pl.BlockSpec((None, d, d), lambda bi, i: (layer, 0, 0))
    vspec = pl.BlockSpec((None, 1, d), lambda bi, i: (layer, 0, 0))
    gates = [pl.BlockSpec((None, tm, HALF_D), functools.partial(lambda bi, i, c: (bi, i, c), c=OFF_G // HALF_D + c))
             for c in range(3 * d // HALF_D)]
    return pl.pallas_call(
        _merge_kernel,
        grid=(bsz, n // tm),
        in_specs=[tok(0), tok(0), tok(0)] + gates + [tok(0),
                  pl.BlockSpec((None, N_MOD, d), lambda bi, i: (bi, 0, 0)),
                  wspec, wspec, wspec, wspec, vspec, vspec, vspec],
        out_specs=tok(0),
        out_shape=jax.ShapeDtypeStruct((bsz, n, d), F32),
        compiler_params=_params("arbitrary", "arbitrary"),
        name="merge_ln",
    )(o_att, z_hy, y_fn, *([p] * len(gates)), x, mod, wa, wh, wf, wo, bo, lg, lb)


MLP_ROW_CHUNK = 512
MLP_FF_CHUNK = 1024


def _mlp_kernel(x_ref, mod_ref, w1_ref, b1_ref, w2_ref, b2_ref, lg_ref, lb_ref, o_ref):
    ff = w1_ref.shape[1]
    rc = min(MLP_ROW_CHUNK, x_ref.shape[0])
    for r0 in range(0, x_ref.shape[0], rc):
        rs = slice(r0, r0 + rc)
        x = x_ref[rs, :]
        h = (x * (1.0 + mod_ref[4:5, :]) + mod_ref[3:4, :]).astype(BF16)
        y = b2_ref[...]
        for c0 in range(0, ff, MLP_FF_CHUNK):
            cs = slice(c0, c0 + MLP_FF_CHUNK)
            a = jnp.maximum(_dot(h, w1_ref[:, cs]) + b1_ref[:, cs], 0.0)
            y = y + _dot((a * a).astype(BF16), w2_ref[cs, :])
        r = ALPHA * x + mod_ref[5:6, :] * y
        o_ref[rs, :] = _layer_norm(r, lg_ref[...], lb_ref[...])


def _mlp(x, mod, w1, b1, w2, b2, lg, lb, layer):
    bsz, n, d = x.shape
    tm = min(n, 2 * MLP_ROW_CHUNK)
    whole = lambda a: pl.BlockSpec((None,) + a.shape[1:], lambda bi, i: (layer, 0, 0),
                                   pipeline_mode=pl.Buffered(1))
    return pl.pallas_call(
        _mlp_kernel,
        grid=(bsz, n // tm),
        in_specs=[pl.BlockSpec((None, tm, d), lambda bi, i: (bi, i, 0)),
                  pl.BlockSpec((None, N_MOD, d), lambda bi, i: (bi, 0, 0)),
                  whole(w1), whole(b1), whole(w2), whole(b2), whole(lg), whole(lb)],
        out_specs=pl.BlockSpec((None, tm, d), lambda bi, i: (bi, i, 0)),
        out_shape=jax.ShapeDtypeStruct((bsz, n, d), F32),
        compiler_params=_params("arbitrary", "arbitrary"),
        name="mlp_ln",
    )(x, mod, w1, b1, w2, b2, lg, lb)


def _int_grid(rows, cols):
    k = lax.broadcasted_iota(jnp.int32, (rows, cols), 0)
    s = lax.broadcasted_iota(jnp.int32, (rows, cols), 1)
    return k, s


TRIG_ROW_BLOCK = 32


def _trig_table(n_rows, samp, period, k_step=1, k_first=0):
    def base(kv):
        ang = ((kv[:, None] * samp[None, :]) % period).astype(F32) * (2.0 * math.pi / period)
        return jnp.cos(ang), jnp.sin(ang)

    ch, sh = base(jnp.arange(n_rows // TRIG_ROW_BLOCK, dtype=jnp.int32) * (TRIG_ROW_BLOCK * k_step))
    cl, sl = base(jnp.arange(TRIG_ROW_BLOCK, dtype=jnp.int32) * k_step + k_first)
    ch, sh, cl, sl = ch[:, None, :], sh[:, None, :], cl[None], sl[None]
    shape = (n_rows, samp.shape[0])
    return (ch * cl - sh * sl).reshape(shape), (sh * cl + ch * sl).reshape(shape)


def _hyena_dft(n):
    h = n // 2
    s = jnp.arange(h, dtype=jnp.int32)
    alt = jnp.where(s % 2 == 0, 1.0, -1.0)[None, :]
    first = lax.broadcasted_iota(jnp.int32, (h, h), 0) == 0
    out = {}
    for name, samp in (('e', 2 * s), ('o', 2 * s + 1)):
        cos, sin = _trig_table(h, samp, 2 * n)
        mf = jnp.concatenate([cos, jnp.where(first, alt, sin)], 0)
        out['mf' + name] = mf.astype(BF16)
        out['mi' + name] = mf.T.astype(BF16)
    return out


def _fnet_dft(n):
    h = n // 2
    s = jnp.arange(h, dtype=jnp.int32)
    out = {}
    for name, samp in (('e', 2 * s), ('o', 2 * s + 1)):
        blocks = []
        for parity in range(2):
            cos, sin = _trig_table(h // 2, samp, n, k_step=2, k_first=parity)
            blocks.append(jnp.concatenate([cos, -sin], 1))
        out['mn' + name] = jnp.concatenate(blocks, 0).astype(BF16)
    k, s = _int_grid(FNET_GROUP_DIM, FNET_GROUP_DIM)
    ang = ((k * s) % FNET_GROUP_DIM).astype(F32) * (2.0 * math.pi / FNET_GROUP_DIM)
    out['csc'] = jnp.concatenate([jnp.cos(ang), jnp.sin(ang)], 1).astype(BF16)
    return out


def _filter_embedding(n, width):
    t = jnp.linspace(0.0, 1.0, n, dtype=F32)[:, None]
    w = (2.0 * math.pi / n) * jnp.arange(n, dtype=F32)[:, None]
    f = jnp.linspace(1e-4, HYENA_BANDS - 1, HYENA_BANDS, dtype=F32)[None, :]
    emb = jnp.concatenate([t, jnp.cos(f * w), -jnp.sin(f * w)], -1)
    return _even_odd_rows(jnp.pad(emb, ((0, 0), (0, width - HYENA_EMB))))


def _rope_tables(n):
    rows = n // GRID_W
    row = jnp.repeat(jnp.arange(rows, dtype=F32), GRID_W)
    col = jnp.tile(jnp.arange(GRID_W, dtype=F32), rows)
    inv_freq = ROPE_THETA ** (-jnp.arange(ROPE_AXIS_DIM // 2, dtype=F32) * 2.0 / ROPE_AXIS_DIM)
    ar = row[:, None] * inv_freq[None, :]
    ac = col[:, None] * inv_freq[None, :]
    cos = jnp.concatenate([jnp.cos(ar), jnp.cos(ar), jnp.cos(ac), jnp.cos(ac)] * 2, -1)
    sin = jnp.concatenate([-jnp.sin(ar), jnp.sin(ar), -jnp.sin(ac), jnp.sin(ac)] * 2, -1)
    return _even_odd_rows(cos), _even_odd_rows(sin)


def _mixer(p_tok, seq_shape, lw, tabs, ctx_parts):
    bsz, n = seq_shape
    p = p_tok.reshape(bsz, n, p_tok.shape[-1])
    o_att = _attention(p, [(p, OFF_K // KV_W, OFF_V // KV_W)] + ctx_parts)

    k4, mid = _filters(n, tabs['emb'], lw['hy_w1'], lw['hy_b1'], lw['hy_freq1'], lw['hy_w2'], lw['hy_b2'],
                       lw['hy_freq2'], lw['hy_w3'], tabs['absdelta'], tabs['mfe'], tabs['mfo'])
    z = None
    for o in range(HYENA_ORDER):
        z = _hyconv(o, z, p, lw['conv_w'], lw['conv_b'], tabs, k4, mid, lw['hy_skip'])
    y_fn = _fnet(p, tabs['csc'], tabs['mne'], tabs['mno'])
    tok = lambda a: a.reshape(p_tok.shape[0], p_tok.shape[1], a.shape[-1])
    return tok(o_att), tok(z), tok(y_fn), p


def _block(x_tok, mod, seq_shape, layer, sw, lw, tabs, rope_tabs, ctx_parts):
    p_tok = _inproj(x_tok, mod, sw['w_in'], sw['b_in'], layer, INPROJ_PLAN, lw['q_gain'], lw['k_gain'], rope_tabs)
    o_att, z_hy, y_fn, p = _mixer(p_tok, seq_shape, lw, tabs, ctx_parts)
    x_tok = _merge(o_att, z_hy, y_fn, p_tok, x_tok, mod, sw['w_att_o'], sw['w_hy_o'], sw['w_fn_o'],
                   sw['w_out'], sw['b_out'], sw['ln1_g'], sw['ln1_b'], layer)
    x_tok = _mlp(x_tok, mod, sw['w_mlp1'], sw['b_mlp1'], sw['w_mlp2'], sw['b_mlp2'], sw['ln2_g'], sw['ln2_b'],
                 layer)
    return x_tok, p


def _seq_tables(n):
    tabs = dict(_hyena_dft(n))
    tabs.update(_fnet_dft(n))
    deltas = jnp.abs(jnp.linspace(MIN_DECAY, MAX_DECAY, D_HYENA, dtype=F32))
    tabs.update(emb=_filter_embedding(n, 128), absdelta=jnp.tile(deltas, HYENA_ORDER)[None, :])
    return tabs


def kernel(x, c, ctx, c_ctx, w_ada, b_ada, w_in, b_in, conv_w, conv_b, hy_w1, hy_b1, hy_freq1, hy_w2, hy_b2,
           hy_freq2, hy_w3, hy_skip, q_gain, k_gain, w_att_o, w_hy_o, w_fn_o, w_out, b_out, ln1_g, ln1_b,
           w_mlp1, b_mlp1, w_mlp2, b_mlp2, ln2_g, ln2_b):
    bsz, n_lat, d = x.shape
    n_ctx = ctx.shape[1]
    depth = w_ada.shape[0]
    tabs_lat = _seq_tables(n_lat)
    tabs_ctx = _seq_tables(n_ctx)
    rope_tabs = _rope_tables(n_lat)

    n_rows = -(-(bsz + 1) // 16) * 16
    cond = jnp.concatenate([c, c_ctx[None, :], jnp.zeros((n_rows - bsz - 1, d), F32)], 0)

    x = _deinterleave(x)
    ctx_tok = _deinterleave(ctx).reshape(1, bsz * n_ctx, d)
    row = lambda a: a[None, :]
    rows = lambda a: a[:, None, :]
    mods = _matmul(cond, w_ada, rows(b_ada), silu=True)
    sw = dict(
        w_in=w_in.astype(BF16), b_in=rows(b_in),
        w_att_o=w_att_o.astype(BF16), w_hy_o=w_hy_o.astype(BF16), w_fn_o=w_fn_o.astype(BF16),
        w_out=w_out.astype(BF16), b_out=rows(b_out), ln1_g=rows(ln1_g), ln1_b=rows(ln1_b),
        w_mlp1=w_mlp1.astype(BF16), b_mlp1=rows(b_mlp1), w_mlp2=w_mlp2.astype(BF16), b_mlp2=rows(b_mlp2),
        ln2_g=rows(ln2_g), ln2_b=rows(ln2_b))
    for i in range(depth):
        last = i == depth - 1
        lw = dict(
            conv_w=conv_w[i], conv_b=row(conv_b[i]),
            hy_w1=jnp.pad(hy_w1[i], ((0, 128 - HYENA_EMB), (0, 0))), hy_b1=row(hy_b1[i]), hy_freq1=row(hy_freq1[i]),
            hy_w2=hy_w2[i], hy_b2=row(hy_b2[i]), hy_freq2=row(hy_freq2[i]), hy_w3=hy_w3[i],
            hy_skip=hy_skip[i].reshape(1, HYENA_ORDER * D_HYENA),
            q_gain=row(q_gain[i]), k_gain=row(k_gain[i]))
        mod_l = mods[i, :bsz].reshape(bsz, N_MOD, d)
        mod_c = mods[i, bsz:bsz + 1].reshape(1, N_MOD, d)

        if last:
            kv_cols = slice(OFF_K, OFF_HY)
            p_c = _inproj(ctx_tok, mod_c, sw['w_in'][i:i + 1, :, kv_cols], sw['b_in'][i:i + 1, :, kv_cols], 0,
                          INPROJ_KV_PLAN, lw['q_gain'], lw['k_gain'], None)
            ctx_kv = (p_c.reshape(bsz, n_ctx, 2 * KV_W), 0, 1)
        else:
            ctx_tok, p_c = _block(ctx_tok, mod_c, (bsz, n_ctx), i, sw, lw, tabs_ctx, None, [])
            ctx_kv = (p_c, OFF_K // KV_W, OFF_V // KV_W)
        x, _ = _block(x, mod_l, (bsz, n_lat), i, sw, lw, tabs_lat, rope_tabs, [ctx_kv])
    return _interleave(x)
```

```python
import functools
import math

import jax
import jax.numpy as jnp
from jax import lax
from jax.experimental import pallas as pl
from jax.experimental.pallas import tpu as pltpu

F32 = jnp.float32
BF16 = jnp.bfloat16

D_MODEL = 1024
GRID_W = 64
HEAD_DIM = 128
N_Q_HEADS = D_MODEL // HEAD_DIM
N_KV_HEADS = 2
GQA_GROUP = N_Q_HEADS // N_KV_HEADS
ROPE_THETA = 10000.0
ROPE_AXIS_DIM = HEAD_DIM // 2
ATTN_SCALE = HEAD_DIM ** -0.5

D_HYENA = D_MODEL
HYENA_ORDER = 2
HYENA_BANDS = 16
HYENA_EMB = 1 + 2 * HYENA_BANDS
SHORT_CONV = 3
DECAY_TARGET = 1e-2
MIN_DECAY = math.log(DECAY_TARGET) / 0.3
MAX_DECAY = math.log(DECAY_TARGET) / 1.5

FNET_GROUPS = 4
FNET_GROUP_DIM = D_MODEL // FNET_GROUPS
D_FF = 4 * D_MODEL
N_MOD = 6
DEPTH = 4

Q_W = N_Q_HEADS * HEAD_DIM
KV_W = N_KV_HEADS * HEAD_DIM
HY_W = (HYENA_ORDER + 1) * D_HYENA
OFF_Q = 0
OFF_K = OFF_Q + Q_W
OFF_V = OFF_K + KV_W
OFF_HY = OFF_V + KV_W
OFF_FN = OFF_HY + HY_W
OFF_G = OFF_FN + D_MODEL
D_IN = OFF_G + 3 * D_MODEL

HALF_D = D_MODEL // 2

ALPHA = (2 * DEPTH) ** 0.25
LN_EPS = 1e-6
RMS_EPS = 1e-6
KERN_EPS = 1e-6

V7X_VMEM_LIMIT_BYTES = 56 * 1024 * 1024


def _params(*sem):
    return pltpu.CompilerParams(dimension_semantics=sem, vmem_limit_bytes=V7X_VMEM_LIMIT_BYTES)


def _dot(a, b):
    return jnp.dot(a, b, preferred_element_type=F32)


def _layer_norm(r, g, b):
    mu = jnp.mean(r, -1, keepdims=True)
    d = r - mu
    var = jnp.mean(d * d, -1, keepdims=True)
    return d * lax.rsqrt(var + LN_EPS) * g + b


def _sigmoid(x):
    return 1.0 / (1.0 + jnp.exp(-x))


def _deinterleave_kernel(x_ref, o_ref):
    hh = x_ref.shape[1] // 2
    o_ref[:, 0:hh, :] = x_ref[:, pl.ds(0, hh, stride=2), :]
    o_ref[:, hh:2 * hh, :] = x_ref[:, pl.ds(1, hh, stride=2), :]


def _interleave_kernel(x_ref, o_ref):
    hh = x_ref.shape[1] // 2
    o_ref[:, pl.ds(0, hh, stride=2), :] = x_ref[:, 0:hh, :]
    o_ref[:, pl.ds(1, hh, stride=2), :] = x_ref[:, hh:2 * hh, :]


REORDER_BLOCK_ROWS = 2048


def _reorder_tokens(x, body, name):
    bsz, n, d = x.shape
    tc = 128
    bb = min(bsz, max(1, REORDER_BLOCK_ROWS // n))
    spec = pl.BlockSpec((bb, n, tc), lambda bi, c: (bi, 0, c))
    return pl.pallas_call(
        body,
        grid=(bsz // bb, d // tc),
        in_specs=[spec],
        out_specs=spec,
        out_shape=jax.ShapeDtypeStruct(x.shape, x.dtype),
        compiler_params=_params("arbitrary", "arbitrary"),
        name=name,
    )(x)


def _deinterleave(x):
    return _reorder_tokens(x, _deinterleave_kernel, "deinterleave")


def _interleave(x):
    return _reorder_tokens(x, _interleave_kernel, "interleave")


def _even_odd_rows(a):
    return jnp.concatenate([a[0::2], a[1::2]], 0)


def _matmul_kernel(a_ref, b_ref, bias_ref, o_ref, *, silu):
    a = a_ref[...]
    if silu:
        a = a * _sigmoid(a)
    o_ref[...] = _dot(a.astype(BF16), b_ref[...].astype(BF16)) + bias_ref[...]


def _matmul(a, b, bias, *, silu=False, tn=1024):
    m, k = a.shape
    nl, _, n = b.shape
    return pl.pallas_call(
        functools.partial(_matmul_kernel, silu=silu),
        grid=(nl, n // tn),
        in_specs=[pl.BlockSpec((m, k), lambda l, j: (0, 0)),
                  pl.BlockSpec((None, k, tn), lambda l, j: (l, 0, j)),
                  pl.BlockSpec((None, 1, tn), lambda l, j: (l, 0, j))],
        out_specs=pl.BlockSpec((None, m, tn), lambda l, j: (l, 0, j)),
        out_shape=jax.ShapeDtypeStruct((nl, m, n), F32),
        compiler_params=_params("arbitrary", "arbitrary"),
        name="ada_matmul",
    )(a, b, bias)


HEAD_PAIR = 2 * HEAD_DIM

INPROJ_PLAN = ((OFF_Q, OFF_HY, 'heads', tuple((c, 0) for c in range(0, Q_W, HEAD_PAIR)) + ((OFF_K, 1),)),
               (OFF_HY, OFF_FN, 'plain', ()),
               (OFF_FN, OFF_G, 'plain', ()),
               (OFF_G, D_IN, 'sigmoid', ()))
INPROJ_KV_PLAN = ((0, 2 * KV_W, 'heads', ((0, 1),)),)


def _inproj_kernel(*refs, plan, rope):
    x_ref, mod_ref, w_ref, b_ref, gq_ref, gk_ref, avg_ref = refs[:7]
    if rope:
        swap_ref, cos_ref, sin_ref = refs[7:10]
    o_ref = refs[-1]
    gains = (gq_ref, gk_ref)
    h = (x_ref[...] * (1.0 + mod_ref[1:2, :]) + mod_ref[0:1, :]).astype(BF16)
    for c0, c1, mode, pairs in plan:
        y = _dot(h, w_ref[:, c0:c1]) + b_ref[:, c0:c1]
        if mode == 'sigmoid':
            y = _sigmoid(y)
        if mode != 'heads':
            o_ref[:, c0:c1] = y.astype(o_ref.dtype)
            continue
        done = c0
        for off, gi in pairs:
            if c0 + off > done:
                o_ref[:, done:c0 + off] = y[:, done - c0:off].astype(o_ref.dtype)
            yp = y[:, off:off + HEAD_PAIR]
            ms = _dot((yp * yp).astype(BF16), avg_ref[...])
            yn = yp * lax.rsqrt(ms + RMS_EPS) * gains[gi][...]
            if rope:
                yn = yn * cos_ref[...] + _dot(yn.astype(BF16), swap_ref[...]) * sin_ref[...]
            o_ref[:, c0 + off:c0 + off + HEAD_PAIR] = yn.astype(o_ref.dtype)
            done = c0 + off + HEAD_PAIR
        if done < c1:
            o_ref[:, done:c1] = y[:, done - c0:].astype(o_ref.dtype)


def _inproj(x, mod, w, b, layer, plan, q_gain, k_gain, rope_tabs):
    bsz, n, d = x.shape
    nout = w.shape[2]
    tm = min(n, 512)
    rope = rope_tabs is not None
    avg, swap = _head_pair_matrices()
    whole = lambda a: pl.BlockSpec((None,) + a.shape[1:], lambda bi, i: (layer, 0, 0),
                                   pipeline_mode=pl.Buffered(1))
    const = lambda a: pl.BlockSpec(a.shape, lambda bi, i: (0, 0))
    args = [x, mod, w, b, jnp.tile(q_gain * ATTN_SCALE, (1, 2)), jnp.tile(k_gain, (1, 2)), avg]
    in_specs = [pl.BlockSpec((None, tm, d), lambda bi, i: (bi, i, 0)),
                pl.BlockSpec((None, N_MOD, d), lambda bi, i: (bi, 0, 0)),
                whole(w), whole(b), const(args[4]), const(args[5]), const(avg)]
    if rope:
        in_specs += [const(swap)] + [pl.BlockSpec((tm, HEAD_PAIR), lambda bi, i: (i, 0))] * 2
        args += [swap] + list(rope_tabs)
    return pl.pallas_call(
        functools.partial(_inproj_kernel, plan=plan, rope=rope),
        grid=(bsz, n // tm),
        in_specs=in_specs,
        out_specs=pl.BlockSpec((None, tm, nout), lambda bi, i: (bi, i, 0)),
        out_shape=jax.ShapeDtypeStruct((bsz, n, nout), BF16),
        compiler_params=_params("arbitrary", "arbitrary"),
        name="inproj",
    )(*args)


def _head_pair_matrices():
    i, j = _int_grid(HEAD_PAIR, HEAD_PAIR)
    avg = jnp.where(i // HEAD_DIM == j // HEAD_DIM, 1.0 / HEAD_DIM, 0.0)
    half = ROPE_AXIS_DIM // 2
    partner = jnp.where(j % ROPE_AXIS_DIM < half, j + half, j - half)
    return avg.astype(BF16), jnp.where(i == partner, 1.0, 0.0).astype(BF16)


def _attn_kernel(*refs, n_parts):
    q_ref = refs[0]
    kv_refs = refs[1:1 + 2 * n_parts]
    o_ref = refs[1 + 2 * n_parts]
    for g in range(N_KV_HEADS):
        gsl = slice(g * HEAD_DIM, (g + 1) * HEAD_DIM)
        ks = [kv_refs[2 * i][:, gsl] for i in range(n_parts)]
        vs = [jnp.concatenate([kv_refs[2 * i + 1][:, gsl], jnp.ones((k.shape[0], HEAD_DIM), BF16)], 1)
              for i, k in enumerate(ks)]
        for r in range(GQA_GROUP):
            h = g * GQA_GROUP + r
            hsl = slice(h * HEAD_DIM, (h + 1) * HEAD_DIM)
            q = q_ref[:, hsl]
            ss = [lax.dot_general(q, k, (((1,), (1,)), ((), ())), preferred_element_type=F32) for k in ks]
            m = jnp.max(ss[0], -1, keepdims=True)
            for s in ss[1:]:
                m = jnp.maximum(m, jnp.max(s, -1, keepdims=True))
            o = None
            for s, v in zip(ss, vs):
                pv = _dot(jnp.exp((s - m).astype(BF16)), v)
                o = pv if o is None else o + pv
            o_ref[:, hsl] = (o[:, :HEAD_DIM] / o[:, HEAD_DIM:]).astype(o_ref.dtype)


def _attention(p, parts):
    bsz, n, _ = p.shape
    w = Q_W
    tq = min(n, 1024)
    in_specs = [pl.BlockSpec((None, tq, w), lambda bi, i: (bi, i, OFF_Q // Q_W))]
    args = [p]
    for kv, kblk, vblk in parts:
        nk = kv.shape[1]
        for blk in (kblk, vblk):
            in_specs.append(pl.BlockSpec((None, nk, KV_W), functools.partial(lambda bi, i, c: (bi, 0, c), c=blk)))
        args += [kv, kv]
    return pl.pallas_call(
        functools.partial(_attn_kernel, n_parts=len(parts)),
        grid=(bsz, n // tq),
        in_specs=in_specs,
        out_specs=pl.BlockSpec((None, tq, w), lambda bi, i: (bi, i, 0)),
        out_shape=jax.ShapeDtypeStruct((bsz, n, w), BF16),
        compiler_params=_params("arbitrary", "arbitrary"),
        name="attention",
    )(*args)


FILTER_COL_CHUNK = 256


def _filter_kernel(emb_ref, w1_ref, b1_ref, f1_ref, w2_ref, b2_ref, f2_ref, w3f_ref, w3b_ref, dl_ref,
                   mfe_ref, mfo_ref, k4_ref, mid_ref, h_ref, *, n):
    hh = n // 2

    @pl.when(pl.program_id(0) == 0)
    def _():
        h1 = jnp.sin(f1_ref[...] * (_dot(emb_ref[...], w1_ref[...]) + b1_ref[...]))
        h_ref[...] = jnp.sin(f2_ref[...] * (_dot(h1, w2_ref[...]) + b2_ref[...])).astype(BF16)

    h = h_ref[...]
    cw = FILTER_COL_CHUNK
    row = lax.broadcasted_iota(jnp.int32, (n, cw), 0)
    lag = jnp.where(row < hh, 2 * row, 2 * (row - hh) + 1)
    t = lag.astype(F32) * (1.0 / (n - 1))
    sign = jnp.where(row % 2 == 0, 1.0, -1.0)
    row0 = lax.broadcasted_iota(jnp.int32, (hh, cw), 0) == 0
    for c0 in range(0, k4_ref.shape[2], cw):
        cs = slice(c0, c0 + cw)
        dec = jnp.exp(-t * dl_ref[:, cs])
        hf = _dot(h, w3f_ref[:, cs].astype(BF16)) * dec
        hb = jnp.where(lag == 0, 0.0, _dot(h, w3b_ref[:, cs].astype(BF16)) * dec)
        ssq = jnp.sum(hf * hf, 0, keepdims=True) + jnp.sum(hb * hb, 0, keepdims=True)
        nrm = lax.rsqrt(ssq + KERN_EPS) * (1.0 / n)
        hs = (hf + hb) * nrm
        hd = (hf - hb) * nrm
        mid_ref[0:1, cs] = jnp.sum(jnp.where(row < hh, hs * sign, 0.0), 0, keepdims=True)
        mid_ref[1:2, cs] = jnp.sum(jnp.where(row < hh, 0.0, hd * sign), 0, keepdims=True)
        hs, hd = hs.astype(BF16), hd.astype(BF16)
        p = _dot(mfe_ref[0:hh, :], hs[:hh])
        q = _dot(mfo_ref[0:hh, :], hs[hh:])
        pp = _dot(mfe_ref[hh:n, :], hd[:hh])
        qp = _dot(mfo_ref[hh:n, :], hd[hh:])
        k4_ref[0, :, cs] = jnp.where(row0, 0.5 * (p + q), p + q)
        k4_ref[1, :, cs] = jnp.where(row0, 0.0, pp + qp)
        k4_ref[2, :, cs] = jnp.where(row0, 0.5 * (p - q), p - q)
        k4_ref[3, :, cs] = jnp.where(row0, 0.0, qp - pp)


def _filters(n, emb, w1, b1, f1, w2, b2, f2, w3, absdelta, mfe, mfo):
    od = HYENA_ORDER * D_HYENA
    tn = 512
    fh = w2.shape[0]
    full = lambda a: pl.BlockSpec(a.shape, lambda j: (0, 0), pipeline_mode=pl.Buffered(1))
    return pl.pallas_call(
        functools.partial(_filter_kernel, n=n),
        grid=(od // tn,),
        in_specs=[full(emb), full(w1), full(b1), full(f1), full(w2), full(b2), full(f2),
                  pl.BlockSpec((fh, tn), lambda j: (0, j)),
                  pl.BlockSpec((fh, tn), lambda j: (0, j + od // tn)),
                  pl.BlockSpec((1, tn), lambda j: (0, j)),
                  full(mfe), full(mfo)],
        out_specs=[pl.BlockSpec((4, n // 2, tn), lambda j: (0, 0, j)),
                   pl.BlockSpec((2, tn), lambda j: (0, j))],
        out_shape=[jax.ShapeDtypeStruct((4, n // 2, od), F32), jax.ShapeDtypeStruct((2, od), F32)],
        scratch_shapes=[pltpu.VMEM((n, fh), BF16)],
        compiler_params=_params("arbitrary"),
        name="hyena_filters",
    )(emb, w1, b1, f1, w2, b2, f2, w3, w3, absdelta, mfe, mfo)


def _short_conv(x, w_ref, b_ref):
    hh = x.shape[0] // 2
    e = x[:hh]
    o = x[hh:]
    row = lax.broadcasted_iota(jnp.int32, e.shape, 0)
    o_prev = jnp.where(row == 0, 0.0, pltpu.roll(o, 1, 0))
    e_next = jnp.where(row == hh - 1, 0.0, pltpu.roll(e, hh - 1, 0))
    w0, w1, w2 = w_ref[0:1, :], w_ref[1:2, :], w_ref[2:3, :]
    return (b_ref[...] + o_prev * w0 + e * w1 + o * w2,
            b_ref[...] + e * w0 + o * w1 + e_next * w2)


HYENA_STEP_ROWS = 2048


def _hyconv_kernel(*refs, conv_z, cw, n):
    if conv_z:
        z_ref, cwz_ref, cbz_ref = refs[:3]
        refs = refs[3:]
    else:
        z_ref = refs[0]
        refs = refs[1:]
    x_ref, cwx_ref, cbx_ref, mfe_ref, mfo_ref, mie_ref, mio_ref, k4_ref, mid_ref, skip_ref, o_ref = refs
    hh = n // 2
    first8 = lax.broadcasted_iota(jnp.int32, (8, cw), 0) == 0
    chunks = [(s, slice(c * cw, (c + 1) * cw)) for s in range(o_ref.shape[0]) for c in range(o_ref.shape[2] // cw)]
    zs, fwd, gs, inv = [], [], [], []
    for s, cs in chunks:
        if conv_z:
            ze, zo = _short_conv(z_ref[s, :, cs].astype(F32), cwz_ref.at[:, cs], cbz_ref.at[:, cs])
            zs.append((ze.astype(BF16), zo.astype(BF16)))
        else:
            zs.append((z_ref[s, 0:hh, cs], z_ref[s, hh:n, cs]))
    for ze, zo in zs:
        fwd.append((_dot(mfe_ref[...], ze), _dot(mfo_ref[...], zo)))
    xs = [_short_conv(x_ref[s, :, cs].astype(F32), cwx_ref.at[:, cs], cbx_ref.at[:, cs]) for s, cs in chunks]
    for (_, cs), (fe, fo) in zip(chunks, fwd):
        p, pp, q, qp = fe[:hh], fe[hh:], fo[:hh], fo[hh:]
        kra, kia, krb, kib = k4_ref[0, :, cs], k4_ref[1, :, cs], k4_ref[2, :, cs], k4_ref[3, :, cs]
        fra, fia = p + q, pp + qp
        frb, fib = p - q, qp - pp
        wra = fra * kra - fia * kia
        wia = fra * kia + fia * kra
        wrb = frb * krb - fib * kib
        wib = frb * kib + fib * krb
        ge_im = wia - wib
        go_im = wia + wib
        kmr, kmi = mid_ref[0:1, cs], mid_ref[1:2, cs]
        pp8, qp8 = pp[:8], qp[:8]
        ge_im = jnp.concatenate([jnp.where(first8, pp8 * kmr - qp8 * kmi, ge_im[:8]), ge_im[8:]], 0)
        go_im = jnp.concatenate([jnp.where(first8, pp8 * kmi + qp8 * kmr, go_im[:8]), go_im[8:]], 0)
        gs.append(((wra + wrb).astype(BF16), ge_im.astype(BF16), (wra - wrb).astype(BF16), go_im.astype(BF16)))
    for ge_re, ge_im, go_re, go_im in gs:
        inv.append((_dot(mie_ref[:, :hh], ge_re) + _dot(mie_ref[:, hh:], ge_im),
                    _dot(mio_ref[:, :hh], go_re) + _dot(mio_ref[:, hh:], go_im)))
    for (s, cs), (ze, zo), (ye, yo), (xe, xo) in zip(chunks, zs, inv, xs):
        skip = skip_ref[:, cs]
        o_ref[s, 0:hh, cs] = (xe * (ye + ze.astype(F32) * skip)).astype(o_ref.dtype)
        o_ref[s, hh:n, cs] = (xo * (yo + zo.astype(F32) * skip)).astype(o_ref.dtype)


def _hyconv(order, z_arr, p, conv_w, conv_b, tabs, k4, mid, skip):
    bsz, n, _ = p.shape
    h = n // 2
    tn = 512
    cw = 256
    nct = D_HYENA // tn
    conv_z = z_arr is None
    once = pl.Buffered(1)
    sb = min(bsz, max(1, HYENA_STEP_ROWS // n))

    def sect(part):
        off = OFF_HY // tn + part * nct
        return pl.BlockSpec((sb, n, tn), functools.partial(lambda c, bi, o: (bi, 0, o + c), o=off))

    def wsect(rows, part):
        return pl.BlockSpec((rows, tn), functools.partial(lambda c, bi, o: (0, o + c), o=part * nct))

    whole = lambda a: pl.BlockSpec(a.shape, lambda c, bi: (0, 0), pipeline_mode=once)
    ocol = functools.partial(lambda c, bi, o: (0, o + c), o=order * nct)
    if conv_z:
        in_specs = [sect(HYENA_ORDER), wsect(SHORT_CONV, HYENA_ORDER), wsect(1, HYENA_ORDER)]
        args = [p, conv_w, conv_b]
    else:
        in_specs = [pl.BlockSpec((sb, n, tn), lambda c, bi: (bi, 0, c))]
        args = [z_arr]
    mats = [tabs['mfe'], tabs['mfo'], tabs['mie'], tabs['mio']]
    in_specs += [sect(order), wsect(SHORT_CONV, order), wsect(1, order)] + [whole(m) for m in mats]
    in_specs += [pl.BlockSpec((4, h, tn), functools.partial(lambda c, bi, o: (0, 0, o + c), o=order * nct),
                              pipeline_mode=once),
                 pl.BlockSpec((2, tn), ocol),
                 pl.BlockSpec((1, tn), ocol)]
    args += [p, conv_w, conv_b] + mats + [k4, mid, skip]
    return pl.pallas_call(
        functools.partial(_hyconv_kernel, conv_z=conv_z, cw=cw, n=n),
        grid=(nct, bsz // sb),
        in_specs=in_specs,
        out_specs=pl.BlockSpec((sb, n, tn), lambda c, bi: (bi, 0, c)),
        out_shape=jax.ShapeDtypeStruct((bsz, n, D_HYENA), BF16),
        compiler_params=_params("arbitrary", "arbitrary"),
        name="hyena_conv",
    )(*args)


def _fnet_kernel(ua_ref, ub_ref, csc_ref, mne_ref, mno_ref, o_ref, *, n, scale):
    hh = n // 2
    qq = n // 4
    gpb = HALF_D // FNET_GROUP_DIM
    for s in range(o_ref.shape[0]):
        for g in range(FNET_GROUPS):
            gsl = slice(g * FNET_GROUP_DIM, (g + 1) * FNET_GROUP_DIM)
            u_ref = (ua_ref, ub_ref)[g // gpb]
            u = u_ref[s, :, (g % gpb) * FNET_GROUP_DIM:(g % gpb + 1) * FNET_GROUP_DIM]
            t = _dot(u, csc_ref[...])
            tc = t[:, :FNET_GROUP_DIM].astype(BF16)
            ts = t[:, FNET_GROUP_DIM:].astype(BF16)
            a = _dot(mne_ref[...], jnp.concatenate([tc[:hh], ts[:hh]], 0))
            b = _dot(mno_ref[...], jnp.concatenate([tc[hh:], ts[hh:]], 0))
            lo = ((a + b) * scale).astype(o_ref.dtype)
            hi = ((a - b) * scale).astype(o_ref.dtype)
            o_ref[s, 0:qq, gsl] = lo[:qq]
            o_ref[s, qq:hh, gsl] = hi[:qq]
            o_ref[s, hh:hh + qq, gsl] = lo[qq:]
            o_ref[s, hh + qq:n, gsl] = hi[qq:]


def _fnet(p, csc, mne, mno):
    bsz, n, _ = p.shape
    scale = 1.0 / math.sqrt(n * FNET_GROUP_DIM)
    sb = min(bsz, max(1, HYENA_STEP_ROWS // n))
    whole = lambda a: pl.BlockSpec(a.shape, lambda bi: (0, 0), pipeline_mode=pl.Buffered(1))
    half = lambda c: pl.BlockSpec((sb, n, HALF_D), functools.partial(lambda bi, c: (bi, 0, c), c=c))
    return pl.pallas_call(
        functools.partial(_fnet_kernel, n=n, scale=scale),
        grid=(bsz // sb,),
        in_specs=[half(OFF_FN // HALF_D), half(OFF_FN // HALF_D + 1), whole(csc), whole(mne), whole(mno)],
        out_specs=pl.BlockSpec((sb, n, D_MODEL), lambda bi: (bi, 0, 0)),
        out_shape=jax.ShapeDtypeStruct((bsz, n, D_MODEL), BF16),
        compiler_params=_params("arbitrary"),
        name="fnet",
    )(p, p, csc, mne, mno)


MERGE_ROW_CHUNK = 256


def _merge_kernel(oa_ref, hy_ref, fn_ref, ga0, ga1, gh0, gh1, gf0, gf1, x_ref, mod_ref,
                  wa_ref, wh_ref, wf_ref, wo_ref, bo_ref, lg_ref, lb_ref, o_ref):
    rc = min(MERGE_ROW_CHUNK, x_ref.shape[0])
    for r0 in range(0, x_ref.shape[0], rc):
        rs = slice(r0, r0 + rc)
        gate = lambda g0, g1: jnp.concatenate([g0[rs, :], g1[rs, :]], 1).astype(F32)
        m = gate(ga0, ga1) * _dot(oa_ref[rs, :], wa_ref[...])
        m = m + gate(gh0, gh1) * _dot(hy_ref[rs, :], wh_ref[...])
        m = m + gate(gf0, gf1) * _dot(fn_ref[rs, :], wf_ref[...])
        y = _dot(m.astype(BF16), wo_ref[...]) + bo_ref[...]
        r = ALPHA * x_ref[rs, :] + mod_ref[2:3, :] * y
        o_ref[rs, :] = _layer_norm(r, lg_ref[...], lb_ref[...])


def _merge(o_att, z_hy, y_fn, p, x, mod, wa, wh, wf, wo, bo, lg, lb, layer):
    bsz, n, d = x.shape
    tm = min(n, 512)
    tok = lambda c: pl.BlockSpec((None, tm, d), functools.partial(lambda bi, i, c: (bi, i, c), c=c))
    wspec = pl.BlockSpec((None, d, d), lambda bi, i: (layer, 0, 0))
    vspec = pl.BlockSpec((None, 1, d), lambda bi, i: (layer, 0, 0))
    gates = [pl.BlockSpec((None, tm, HALF_D), functools.partial(lambda bi, i, c: (bi, i, c), c=OFF_G // HALF_D + c))
             for c in range(3 * d // HALF_D)]
    return pl.pallas_call(
        _merge_kernel,
        grid=(bsz, n // tm),
        in_specs=[tok(0), tok(0), tok(0)] + gates + [tok(0),
                  pl.BlockSpec((None, N_MOD, d), lambda bi, i: (bi, 0, 0)),
                  wspec, wspec, wspec, wspec, vspec, vspec, vspec],
        out_specs=tok(0),
        out_shape=jax.ShapeDtypeStruct((bsz, n, d), F32),
        compiler_params=_params("arbitrary", "arbitrary"),
        name="merge_ln",
    )(o_att, z_hy, y_fn, *([p] * len(gates)), x, mod, wa, wh, wf, wo, bo, lg, lb)


MLP_ROW_CHUNK = 512
MLP_FF_CHUNK = 1024


def _mlp_kernel(x_ref, mod_ref, w1_ref, b1_ref, w2_ref, b2_ref, lg_ref, lb_ref, o_ref):
    ff = w1_ref.shape[1]
    rc = min(MLP_ROW_CHUNK, x_ref.shape[0])
    for r0 in range(0, x_ref.shape[0], rc):
        rs = slice(r0, r0 + rc)
        x = x_ref[rs, :]
        h = (x * (1.0 + mod_ref[4:5, :]) + mod_ref[3:4, :]).astype(BF16)
        y = b2_ref[...]
        for c0 in range(0, ff, MLP_FF_CHUNK):
            cs = slice(c0, c0 + MLP_FF_CHUNK)
            a = jnp.maximum(_dot(h, w1_ref[:, cs]) + b1_ref[:, cs], 0.0)
            y = y + _dot((a * a).astype(BF16), w2_ref[cs, :])
        r = ALPHA * x + mod_ref[5:6, :] * y
        o_ref[rs, :] = _layer_norm(r, lg_ref[...], lb_ref[...])


def _mlp(x, mod, w1, b1, w2, b2, lg, lb, layer):
    bsz, n, d = x.shape
    tm = min(n, 2 * MLP_ROW_CHUNK)
    whole = lambda a: pl.BlockSpec((None,) + a.shape[1:], lambda bi, i: (layer, 0, 0),
                                   pipeline_mode=pl.Buffered(1))
    return pl.pallas_call(
        _mlp_kernel,
        grid=(bsz, n // tm),
        in_specs=[pl.BlockSpec((None, tm, d), lambda bi, i: (bi, i, 0)),
                  pl.BlockSpec((None, N_MOD, d), lambda bi, i: (bi, 0, 0)),
                  whole(w1), whole(b1), whole(w2), whole(b2), whole(lg), whole(lb)],
        out_specs=pl.BlockSpec((None, tm, d), lambda bi, i: (bi, i, 0)),
        out_shape=jax.ShapeDtypeStruct((bsz, n, d), F32),
        compiler_params=_params("arbitrary", "arbitrary"),
        name="mlp_ln",
    )(x, mod, w1, b1, w2, b2, lg, lb)


def _int_grid(rows, cols):
    k = lax.broadcasted_iota(jnp.int32, (rows, cols), 0)
    s = lax.broadcasted_iota(jnp.int32, (rows, cols), 1)
    return k, s


TRIG_ROW_BLOCK = 32


def _trig_table(n_rows, samp, period, k_step=1, k_first=0):
    def base(kv):
        ang = ((kv[:, None] * samp[None, :]) % period).astype(F32) * (2.0 * math.pi / period)
        return jnp.cos(ang), jnp.sin(ang)

    ch, sh = base(jnp.arange(n_rows // TRIG_ROW_BLOCK, dtype=jnp.int32) * (TRIG_ROW_BLOCK * k_step))
    cl, sl = base(jnp.arange(TRIG_ROW_BLOCK, dtype=jnp.int32) * k_step + k_first)
    ch, sh, cl, sl = ch[:, None, :], sh[:, None, :], cl[None], sl[None]
    shape = (n_rows, samp.shape[0])
    return (ch * cl - sh * sl).reshape(shape), (sh * cl + ch * sl).reshape(shape)


def _hyena_dft(n):
    h = n // 2
    s = jnp.arange(h, dtype=jnp.int32)
    alt = jnp.where(s % 2 == 0, 1.0, -1.0)[None, :]
    first = lax.broadcasted_iota(jnp.int32, (h, h), 0) == 0
    out = {}
    for name, samp in (('e', 2 * s), ('o', 2 * s + 1)):
        cos, sin = _trig_table(h, samp, 2 * n)
        mf = jnp.concatenate([cos, jnp.where(first, alt, sin)], 0)
        out['mf' + name] = mf.astype(BF16)
        out['mi' + name] = mf.T.astype(BF16)
    return out


def _fnet_dft(n):
    h = n // 2
    s = jnp.arange(h, dtype=jnp.int32)
    out = {}
    for name, samp in (('e', 2 * s), ('o', 2 * s + 1)):
        blocks = []
        for parity in range(2):
            cos, sin = _trig_table(h // 2, samp, n, k_step=2, k_first=parity)
            blocks.append(jnp.concatenate([cos, -sin], 1))
        out['mn' + name] = jnp.concatenate(blocks, 0).astype(BF16)
    k, s = _int_grid(FNET_GROUP_DIM, FNET_GROUP_DIM)
    ang = ((k * s) % FNET_GROUP_DIM).astype(F32) * (2.0 * math.pi / FNET_GROUP_DIM)
    out['csc'] = jnp.concatenate([jnp.cos(ang), jnp.sin(ang)], 1).astype(BF16)
    return out


def _filter_embedding(n, width):
    t = jnp.linspace(0.0, 1.0, n, dtype=F32)[:, None]
    w = (2.0 * math.pi / n) * jnp.arange(n, dtype=F32)[:, None]
    f = jnp.linspace(1e-4, HYENA_BANDS - 1, HYENA_BANDS, dtype=F32)[None, :]
    emb = jnp.concatenate([t, jnp.cos(f * w), -jnp.sin(f * w)], -1)
    return _even_odd_rows(jnp.pad(emb, ((0, 0), (0, width - HYENA_EMB))))


def _rope_tables(n):
    rows = n // GRID_W
    row = jnp.repeat(jnp.arange(rows, dtype=F32), GRID_W)
    col = jnp.tile(jnp.arange(GRID_W, dtype=F32), rows)
    inv_freq = ROPE_THETA ** (-jnp.arange(ROPE_AXIS_DIM // 2, dtype=F32) * 2.0 / ROPE_AXIS_DIM)
    ar = row[:, None] * inv_freq[None, :]
    ac = col[:, None] * inv_freq[None, :]
    cos = jnp.concatenate([jnp.cos(ar), jnp.cos(ar), jnp.cos(ac), jnp.cos(ac)] * 2, -1)
    sin = jnp.concatenate([-jnp.sin(ar), jnp.sin(ar), -jnp.sin(ac), jnp.sin(ac)] * 2, -1)
    return _even_odd_rows(cos), _even_odd_rows(sin)


def _mixer(p_tok, seq_shape, lw, tabs, ctx_parts):
    bsz, n = seq_shape
    p = p_tok.reshape(bsz, n, p_tok.shape[-1])
    o_att = _attention(p, [(p, OFF_K // KV_W, OFF_V // KV_W)] + ctx_parts)

    k4, mid = _filters(n, tabs['emb'], lw['hy_w1'], lw['hy_b1'], lw['hy_freq1'], lw['hy_w2'], lw['hy_b2'],
                       lw['hy_freq2'], lw['hy_w3'], tabs['absdelta'], tabs['mfe'], tabs['mfo'])
    z = None
    for o in range(HYENA_ORDER):
        z = _hyconv(o, z, p, lw['conv_w'], lw['conv_b'], tabs, k4, mid, lw['hy_skip'])
    y_fn = _fnet(p, tabs['csc'], tabs['mne'], tabs['mno'])
    tok = lambda a: a.reshape(p_tok.shape[0], p_tok.shape[1], a.shape[-1])
    return tok(o_att), tok(z), tok(y_fn), p


def _block(x_tok, mod, seq_shape, layer, sw, lw, tabs, rope_tabs, ctx_parts):
    p_tok = _inproj(x_tok, mod, sw['w_in'], sw['b_in'], layer, INPROJ_PLAN, lw['q_gain'], lw['k_gain'], rope_tabs)
    o_att, z_hy, y_fn, p = _mixer(p_tok, seq_shape, lw, tabs, ctx_parts)
    x_tok = _merge(o_att, z_hy, y_fn, p_tok, x_tok, mod, sw['w_att_o'], sw['w_hy_o'], sw['w_fn_o'],
                   sw['w_out'], sw['b_out'], sw['ln1_g'], sw['ln1_b'], layer)
    x_tok = _mlp(x_tok, mod, sw['w_mlp1'], sw['b_mlp1'], sw['w_mlp2'], sw['b_mlp2'], sw['ln2_g'], sw['ln2_b'],
                 layer)
    return x_tok, p


def _seq_tables(n):
    tabs = dict(_hyena_dft(n))
    tabs.update(_fnet_dft(n))
    deltas = jnp.abs(jnp.linspace(MIN_DECAY, MAX_DECAY, D_HYENA, dtype=F32))
    tabs.update(emb=_filter_embedding(n, 128), absdelta=jnp.tile(deltas, HYENA_ORDER)[None, :])
    return tabs


def kernel(x, c, ctx, c_ctx, w_ada, b_ada, w_in, b_in, conv_w, conv_b, hy_w1, hy_b1, hy_freq1, hy_w2, hy_b2,
           hy_freq2, hy_w3, hy_skip, q_gain, k_gain, w_att_o, w_hy_o, w_fn_o, w_out, b_out, ln1_g, ln1_b,
           w_mlp1, b_mlp1, w_mlp2, b_mlp2, ln2_g, ln2_b):
    bsz, n_lat, d = x.shape
    n_ctx = ctx.shape[1]
    depth = w_ada.shape[0]
    tabs_lat = _seq_tables(n_lat)
    tabs_ctx = _seq_tables(n_ctx)
    rope_tabs = _rope_tables(n_lat)

    n_rows = -(-(bsz + 1) // 16) * 16
    cond = jnp.concatenate([c, c_ctx[None, :], jnp.zeros((n_rows - bsz - 1, d), F32)], 0)

    x = _deinterleave(x)
    ctx_tok = _deinterleave(ctx).reshape(1, bsz * n_ctx, d)
    row = lambda a: a[None, :]
    rows = lambda a: a[:, None, :]
    mods = _matmul(cond, w_ada, rows(b_ada), silu=True)
    sw = dict(
        w_in=w_in.astype(BF16), b_in=rows(b_in),
        w_att_o=w_att_o.astype(BF16), w_hy_o=w_hy_o.astype(BF16), w_fn_o=w_fn_o.astype(BF16),
        w_out=w_out.astype(BF16), b_out=rows(b_out), ln1_g=rows(ln1_g), ln1_b=rows(ln1_b),
        w_mlp1=w_mlp1.astype(BF16), b_mlp1=rows(b_mlp1), w_mlp2=w_mlp2.astype(BF16), b_mlp2=rows(b_mlp2),
        ln2_g=rows(ln2_g), ln2_b=rows(ln2_b))
    for i in range(depth):
        last = i == depth - 1
        lw = dict(
            conv_w=conv_w[i], conv_b=row(conv_b[i]),
            hy_w1=jnp.pad(hy_w1[i], ((0, 128 - HYENA_EMB), (0, 0))), hy_b1=row(hy_b1[i]), hy_freq1=row(hy_freq1[i]),
            hy_w2=hy_w2[i], hy_b2=row(hy_b2[i]), hy_freq2=row(hy_freq2[i]), hy_w3=hy_w3[i],
            hy_skip=hy_skip[i].reshape(1, HYENA_ORDER * D_HYENA),
            q_gain=row(q_gain[i]), k_gain=row(k_gain[i]))
        mod_l = mods[i, :bsz].reshape(bsz, N_MOD, d)
        mod_c = mods[i, bsz:bsz + 1].reshape(1, N_MOD, d)

        if last:
            kv_cols = slice(OFF_K, OFF_HY)
            p_c = _inproj(ctx_tok, mod_c, sw['w_in'][i:i + 1, :, kv_cols], sw['b_in'][i:i + 1, :, kv_cols], 0,
                          INPROJ_KV_PLAN, lw['q_gain'], lw['k_gain'], None)
            ctx_kv = (p_c.reshape(bsz, n_ctx, 2 * KV_W), 0, 1)
        else:
            ctx_tok, p_c = _block(ctx_tok, mod_c, (bsz, n_ctx), i, sw, lw, tabs_ctx, None, [])
            ctx_kv = (p_c, OFF_K // KV_W, OFF_V // KV_W)
        x, _ = _block(x, mod_l, (bsz, n_lat), i, sw, lw, tabs_lat, rope_tabs, [ctx_kv])
    return _interleave(x)
```

```python
import functools
import math

import jax
import jax.numpy as jnp
from jax import lax
from jax.experimental import pallas as pl
from jax.experimental.pallas import tpu as pltpu

F32 = jnp.float32
BF16 = jnp.bfloat16

D_MODEL = 1024
GRID_W = 64
HEAD_DIM = 128
N_Q_HEADS = D_MODEL // HEAD_DIM
N_KV_HEADS = 2
GQA_GROUP = N_Q_HEADS // N_KV_HEADS
ROPE_THETA = 10000.0
ROPE_AXIS_DIM = HEAD_DIM // 2
ATTN_SCALE = HEAD_DIM ** -0.5

D_HYENA = D_MODEL
HYENA_ORDER = 2
HYENA_BANDS = 16
HYENA_EMB = 1 + 2 * HYENA_BANDS
SHORT_CONV = 3
DECAY_TARGET = 1e-2
MIN_DECAY = math.log(DECAY_TARGET) / 0.3
MAX_DECAY = math.log(DECAY_TARGET) / 1.5

FNET_GROUPS = 4
FNET_GROUP_DIM = D_MODEL // FNET_GROUPS
N_MOD = 6
DEPTH = 4

Q_W = N_Q_HEADS * HEAD_DIM
KV_W = N_KV_HEADS * HEAD_DIM
HY_W = (HYENA_ORDER + 1) * D_HYENA
OFF_Q = 0
OFF_K = OFF_Q + Q_W
OFF_V = OFF_K + KV_W
OFF_HY = OFF_V + KV_W
OFF_FN = OFF_HY + HY_W
OFF_G = OFF_FN + D_MODEL
D_IN = OFF_G + 3 * D_MODEL

HALF_D = D_MODEL // 2

ALPHA = (2 * DEPTH) ** 0.25
LN_EPS = 1e-6
RMS_EPS = 1e-6
KERN_EPS = 1e-6

V7X_VMEM_LIMIT_BYTES = 56 * 1024 * 1024
V7X_LANES = 128
V7X_MXU_DIM = 256
HYENA_COL_TILE = 512


def _params(*sem):
    return pltpu.CompilerParams(dimension_semantics=sem, vmem_limit_bytes=V7X_VMEM_LIMIT_BYTES)


def _dot(a, b):
    return jnp.dot(a, b, preferred_element_type=F32)


def _layer_norm(r, g, b):
    mu = jnp.mean(r, -1, keepdims=True)
    d = r - mu
    var = jnp.mean(d * d, -1, keepdims=True)
    return d * lax.rsqrt(var + LN_EPS) * g + b


def _sigmoid(x):
    return 1.0 / (1.0 + jnp.exp(-x))


def _deinterleave_kernel(x_ref, o_ref):
    hh = x_ref.shape[1] // 2
    o_ref[:, 0:hh, :] = x_ref[:, pl.ds(0, hh, stride=2), :]
    o_ref[:, hh:2 * hh, :] = x_ref[:, pl.ds(1, hh, stride=2), :]


def _interleave_kernel(x_ref, o_ref):
    hh = x_ref.shape[1] // 2
    o_ref[:, pl.ds(0, hh, stride=2), :] = x_ref[:, 0:hh, :]
    o_ref[:, pl.ds(1, hh, stride=2), :] = x_ref[:, hh:2 * hh, :]


REORDER_BLOCK_ROWS = 2048


def _reorder_tokens(x, body, name):
    bsz, n, d = x.shape
    tc = V7X_LANES
    bb = min(bsz, max(1, REORDER_BLOCK_ROWS // n))
    spec = pl.BlockSpec((bb, n, tc), lambda bi, c: (bi, 0, c))
    return pl.pallas_call(
        body,
        grid=(bsz // bb, d // tc),
        in_specs=[spec],
        out_specs=spec,
        out_shape=jax.ShapeDtypeStruct(x.shape, x.dtype),
        compiler_params=_params("arbitrary", "arbitrary"),
        name=name,
    )(x)


def _deinterleave(x):
    return _reorder_tokens(x, _deinterleave_kernel, "deinterleave")


def _interleave(x):
    return _reorder_tokens(x, _interleave_kernel, "interleave")


def _even_odd_rows(a):
    return jnp.concatenate([a[0::2], a[1::2]], 0)


def _matmul_kernel(a_ref, b_ref, bias_ref, o_ref, *, silu):
    a = a_ref[...]
    if silu:
        a = a * _sigmoid(a)
    o_ref[...] = _dot(a.astype(BF16), b_ref[...].astype(BF16)) + bias_ref[...]


def _matmul(a, b, bias, *, silu=False, tn=1024):
    m, k = a.shape
    nl, _, n = b.shape
    return pl.pallas_call(
        functools.partial(_matmul_kernel, silu=silu),
        grid=(nl, n // tn),
        in_specs=[pl.BlockSpec((m, k), lambda l, j: (0, 0)),
                  pl.BlockSpec((None, k, tn), lambda l, j: (l, 0, j)),
                  pl.BlockSpec((None, 1, tn), lambda l, j: (l, 0, j))],
        out_specs=pl.BlockSpec((None, m, tn), lambda l, j: (l, 0, j)),
        out_shape=jax.ShapeDtypeStruct((nl, m, n), F32),
        compiler_params=_params("arbitrary", "arbitrary"),
        name="ada_matmul",
    )(a, b, bias)


HEAD_PAIR = 2 * HEAD_DIM

INPROJ_PLAN = ((OFF_Q, OFF_HY, 'heads', tuple((c, 0) for c in range(0, Q_W, HEAD_PAIR)) + ((OFF_K, 1),)),
               (OFF_HY, OFF_FN, 'plain', ()),
               (OFF_FN, OFF_G, 'plain', ()),
               (OFF_G, D_IN, 'sigmoid', ()))
INPROJ_KV_PLAN = ((0, 2 * KV_W, 'heads', ((0, 1),)),)


def _inproj_kernel(*refs, plan, rope):
    x_ref, mod_ref, w_ref, b_ref, gq_ref, gk_ref, avg_ref = refs[:7]
    if rope:
        swap_ref, cos_ref, sin_ref = refs[7:10]
    o_ref = refs[-1]
    gains = (gq_ref, gk_ref)
    h = (x_ref[...] * (1.0 + mod_ref[1:2, :]) + mod_ref[0:1, :]).astype(BF16)
    for c0, c1, mode, pairs in plan:
        y = _dot(h, w_ref[:, c0:c1]) + b_ref[:, c0:c1]
        if mode == 'sigmoid':
            y = _sigmoid(y)
        if mode != 'heads':
            o_ref[:, c0:c1] = y.astype(o_ref.dtype)
            continue
        done = c0
        for off, gi in pairs:
            if c0 + off > done:
                o_ref[:, done:c0 + off] = y[:, done - c0:off].astype(o_ref.dtype)
            yp = y[:, off:off + HEAD_PAIR]
            ms = _dot((yp * yp).astype(BF16), avg_ref[...])
            yn = yp * lax.rsqrt(ms + RMS_EPS) * gains[gi][...]
            if rope:
                yn = yn * cos_ref[...] + _dot(yn.astype(BF16), swap_ref[...]) * sin_ref[...]
            o_ref[:, c0 + off:c0 + off + HEAD_PAIR] = yn.astype(o_ref.dtype)
            done = c0 + off + HEAD_PAIR
        if done < c1:
            o_ref[:, done:c1] = y[:, done - c0:].astype(o_ref.dtype)


def _inproj(x, mod, w, b, layer, plan, q_gain, k_gain, rope_tabs):
    bsz, n, d = x.shape
    nout = w.shape[2]
    tm = min(n, 512)
    rope = rope_tabs is not None
    avg, swap = _head_pair_matrices()
    whole = lambda a: pl.BlockSpec((None,) + a.shape[1:], lambda bi, i: (layer, 0, 0),
                                   pipeline_mode=pl.Buffered(1))
    const = lambda a: pl.BlockSpec(a.shape, lambda bi, i: (0, 0))
    args = [x, mod, w, b, jnp.tile(q_gain * ATTN_SCALE, (1, 2)), jnp.tile(k_gain, (1, 2)), avg]
    in_specs = [pl.BlockSpec((None, tm, d), lambda bi, i: (bi, i, 0)),
                pl.BlockSpec((None, N_MOD, d), lambda bi, i: (bi, 0, 0)),
                whole(w), whole(b), const(args[4]), const(args[5]), const(avg)]
    if rope:
        in_specs += [const(swap)] + [pl.BlockSpec((tm, HEAD_PAIR), lambda bi, i: (i, 0))] * 2
        args += [swap] + list(rope_tabs)
    return pl.pallas_call(
        functools.partial(_inproj_kernel, plan=plan, rope=rope),
        grid=(bsz, n // tm),
        in_specs=in_specs,
        out_specs=pl.BlockSpec((None, tm, nout), lambda bi, i: (bi, i, 0)),
        out_shape=jax.ShapeDtypeStruct((bsz, n, nout), BF16),
        compiler_params=_params("arbitrary", "arbitrary"),
        name="inproj",
    )(*args)


def _head_pair_matrices():
    i, j = _int_grid(HEAD_PAIR, HEAD_PAIR)
    avg = jnp.where(i // HEAD_DIM == j // HEAD_DIM, 1.0 / HEAD_DIM, 0.0)
    half = ROPE_AXIS_DIM // 2
    partner = jnp.where(j % ROPE_AXIS_DIM < half, j + half, j - half)
    return avg.astype(BF16), jnp.where(i == partner, 1.0, 0.0).astype(BF16)


def _attn_kernel(*refs, n_parts):
    q_ref = refs[0]
    kv_refs = refs[1:1 + 2 * n_parts]
    o_ref = refs[1 + 2 * n_parts]
    for g in range(N_KV_HEADS):
        gsl = slice(g * HEAD_DIM, (g + 1) * HEAD_DIM)
        ks = [kv_refs[2 * i][:, gsl] for i in range(n_parts)]
        vs = [jnp.concatenate([kv_refs[2 * i + 1][:, gsl], jnp.ones((k.shape[0], HEAD_DIM), BF16)], 1)
              for i, k in enumerate(ks)]
        for r in range(GQA_GROUP):
            h = g * GQA_GROUP + r
            hsl = slice(h * HEAD_DIM, (h + 1) * HEAD_DIM)
            q = q_ref[:, hsl]
            ss = [lax.dot_general(q, k, (((1,), (1,)), ((), ())), preferred_element_type=F32) for k in ks]
            m = jnp.max(ss[0], -1, keepdims=True)
            for s in ss[1:]:
                m = jnp.maximum(m, jnp.max(s, -1, keepdims=True))
            o = None
            for s, v in zip(ss, vs):
                pv = _dot(jnp.exp((s - m).astype(BF16)), v)
                o = pv if o is None else o + pv
            o_ref[:, hsl] = (o[:, :HEAD_DIM] / o[:, HEAD_DIM:]).astype(o_ref.dtype)


def _attention(p, parts):
    bsz, n, _ = p.shape
    w = Q_W
    tq = min(n, 1024)
    in_specs = [pl.BlockSpec((None, tq, w), lambda bi, i: (bi, i, OFF_Q // Q_W))]
    args = [p]
    for kv, kblk, vblk in parts:
        nk = kv.shape[1]
        for blk in (kblk, vblk):
            in_specs.append(pl.BlockSpec((None, nk, KV_W), functools.partial(lambda bi, i, c: (bi, 0, c), c=blk)))
        args += [kv, kv]
    return pl.pallas_call(
        functools.partial(_attn_kernel, n_parts=len(parts)),
        grid=(bsz, n // tq),
        in_specs=in_specs,
        out_specs=pl.BlockSpec((None, tq, w), lambda bi, i: (bi, i, 0)),
        out_shape=jax.ShapeDtypeStruct((bsz, n, w), BF16),
        compiler_params=_params("arbitrary", "arbitrary"),
        name="attention",
    )(*args)


FILTER_COL_CHUNK = V7X_MXU_DIM


def _filter_kernel(emb_ref, w1_ref, b1_ref, f1_ref, w2_ref, b2_ref, f2_ref, w3f_ref, w3b_ref, dl_ref,
                   mfe_ref, mfo_ref, k4_ref, mid_ref, h_ref, *, n):
    hh = n // 2

    @pl.when(pl.program_id(0) == 0)
    def _():
        h1 = jnp.sin(f1_ref[...] * (_dot(emb_ref[...], w1_ref[...]) + b1_ref[...]))
        h_ref[...] = jnp.sin(f2_ref[...] * (_dot(h1, w2_ref[...]) + b2_ref[...])).astype(BF16)

    h = h_ref[...]
    cw = FILTER_COL_CHUNK
    row = lax.broadcasted_iota(jnp.int32, (n, cw), 0)
    lag = jnp.where(row < hh, 2 * row, 2 * (row - hh) + 1)
    t = lag.astype(F32) * (1.0 / (n - 1))
    sign = jnp.where(row % 2 == 0, 1.0, -1.0)
    row0 = lax.broadcasted_iota(jnp.int32, (hh, cw), 0) == 0
    for c0 in range(0, k4_ref.shape[2], cw):
        cs = slice(c0, c0 + cw)
        dec = jnp.exp(-t * dl_ref[:, cs])
        hf = _dot(h, w3f_ref[:, cs].astype(BF16)) * dec
        hb = jnp.where(lag == 0, 0.0, _dot(h, w3b_ref[:, cs].astype(BF16)) * dec)
        ssq = jnp.sum(hf * hf, 0, keepdims=True) + jnp.sum(hb * hb, 0, keepdims=True)
        nrm = lax.rsqrt(ssq + KERN_EPS) * (1.0 / n)
        hs = (hf + hb) * nrm
        hd = (hf - hb) * nrm
        mid_ref[0:1, cs] = jnp.sum(jnp.where(row < hh, hs * sign, 0.0), 0, keepdims=True)
        mid_ref[1:2, cs] = jnp.sum(jnp.where(row < hh, 0.0, hd * sign), 0, keepdims=True)
        hs, hd = hs.astype(BF16), hd.astype(BF16)
        p = _dot(mfe_ref[0:hh, :], hs[:hh])
        q = _dot(mfo_ref[0:hh, :], hs[hh:])
        pp = _dot(mfe_ref[hh:n, :], hd[:hh])
        qp = _dot(mfo_ref[hh:n, :], hd[hh:])
        k4_ref[0, :, cs] = jnp.where(row0, 0.5 * (p + q), p + q)
        k4_ref[1, :, cs] = jnp.where(row0, 0.0, pp + qp)
        k4_ref[2, :, cs] = jnp.where(row0, 0.5 * (p - q), p - q)
        k4_ref[3, :, cs] = jnp.where(row0, 0.0, qp - pp)


def _filters(n, emb, w1, b1, f1, w2, b2, f2, w3, absdelta, mfe, mfo):
    od = HYENA_ORDER * D_HYENA
    tn = HYENA_COL_TILE
    fh = w2.shape[0]
    full = lambda a: pl.BlockSpec(a.shape, lambda j: (0, 0), pipeline_mode=pl.Buffered(1))
    return pl.pallas_call(
        functools.partial(_filter_kernel, n=n),
        grid=(od // tn,),
        in_specs=[full(emb), full(w1), full(b1), full(f1), full(w2), full(b2), full(f2),
                  pl.BlockSpec((fh, tn), lambda j: (0, j)),
                  pl.BlockSpec((fh, tn), lambda j: (0, j + od // tn)),
                  pl.BlockSpec((1, tn), lambda j: (0, j)),
                  full(mfe), full(mfo)],
        out_specs=[pl.BlockSpec((4, n // 2, tn), lambda j: (0, 0, j)),
                   pl.BlockSpec((2, tn), lambda j: (0, j))],
        out_shape=[jax.ShapeDtypeStruct((4, n // 2, od), F32), jax.ShapeDtypeStruct((2, od), F32)],
        scratch_shapes=[pltpu.VMEM((n, fh), BF16)],
        compiler_params=_params("arbitrary"),
        name="hyena_filters",
    )(emb, w1, b1, f1, w2, b2, f2, w3, w3, absdelta, mfe, mfo)


def _short_conv(x, w_ref, b_ref):
    hh = x.shape[0] // 2
    e = x[:hh]
    o = x[hh:]
    row = lax.broadcasted_iota(jnp.int32, e.shape, 0)
    o_prev = jnp.where(row == 0, 0.0, pltpu.roll(o, 1, 0))
    e_next = jnp.where(row == hh - 1, 0.0, pltpu.roll(e, hh - 1, 0))
    w0, w1, w2 = w_ref[0:1, :], w_ref[1:2, :], w_ref[2:3, :]
    return (b_ref[...] + o_prev * w0 + e * w1 + o * w2,
            b_ref[...] + e * w0 + o * w1 + e_next * w2)


HYENA_STEP_ROWS = 2048


def _hyconv_kernel(*refs, conv_z, cw, n):
    if conv_z:
        z_ref, cwz_ref, cbz_ref = refs[:3]
        refs = refs[3:]
    else:
        z_ref = refs[0]
        refs = refs[1:]
    x_ref, cwx_ref, cbx_ref, mfe_ref, mfo_ref, mie_ref, mio_ref, k4_ref, mid_ref, skip_ref, o_ref = refs
    hh = n // 2
    first8 = lax.broadcasted_iota(jnp.int32, (8, cw), 0) == 0
    chunks = [(s, slice(c * cw, (c + 1) * cw)) for s in range(o_ref.shape[0]) for c in range(o_ref.shape[2] // cw)]
    zs, fwd, gs, inv = [], [], [], []
    for s, cs in chunks:
        if conv_z:
            ze, zo = _short_conv(z_ref[s, :, cs].astype(F32), cwz_ref.at[:, cs], cbz_ref.at[:, cs])
            zs.append((ze.astype(BF16), zo.astype(BF16)))
        else:
            zs.append((z_ref[s, 0:hh, cs], z_ref[s, hh:n, cs]))
    for ze, zo in zs:
        fwd.append((_dot(mfe_ref[...], ze), _dot(mfo_ref[...], zo)))
    xs = [_short_conv(x_ref[s, :, cs].astype(F32), cwx_ref.at[:, cs], cbx_ref.at[:, cs]) for s, cs in chunks]
    for (_, cs), (fe, fo) in zip(chunks, fwd):
        p, pp, q, qp = fe[:hh], fe[hh:], fo[:hh], fo[hh:]
        kra, kia, krb, kib = k4_ref[0, :, cs], k4_ref[1, :, cs], k4_ref[2, :, cs], k4_ref[3, :, cs]
        fra, fia = p + q, pp + qp
        frb, fib = p - q, qp - pp
        wra = fra * kra - fia * kia
        wia = fra * kia + fia * kra
        wrb = frb * krb - fib * kib
        wib = frb * kib + fib * krb
        ge_im = wia - wib
        go_im = wia + wib
        kmr, kmi = mid_ref[0:1, cs], mid_ref[1:2, cs]
        pp8, qp8 = pp[:8], qp[:8]
        ge_im = jnp.concatenate([jnp.where(first8, pp8 * kmr - qp8 * kmi, ge_im[:8]), ge_im[8:]], 0)
        go_im = jnp.concatenate([jnp.where(first8, pp8 * kmi + qp8 * kmr, go_im[:8]), go_im[8:]], 0)
        gs.append(((wra + wrb).astype(BF16), ge_im.astype(BF16), (wra - wrb).astype(BF16), go_im.astype(BF16)))
    for ge_re, ge_im, go_re, go_im in gs:
        inv.append((_dot(mie_ref[:, :hh], ge_re) + _dot(mie_ref[:, hh:], ge_im),
                    _dot(mio_ref[:, :hh], go_re) + _dot(mio_ref[:, hh:], go_im)))
    for (s, cs), (ze, zo), (ye, yo), (xe, xo) in zip(chunks, zs, inv, xs):
        skip = skip_ref[:, cs]
        o_ref[s, 0:hh, cs] = (xe * (ye + ze.astype(F32) * skip)).astype(o_ref.dtype)
        o_ref[s, hh:n, cs] = (xo * (yo + zo.astype(F32) * skip)).astype(o_ref.dtype)


def _hyconv(order, z_arr, p, conv_w, conv_b, tabs, k4, mid, skip):
    bsz, n, _ = p.shape
    h = n // 2
    tn = HYENA_COL_TILE
    cw = V7X_MXU_DIM
    nct = D_HYENA // tn
    conv_z = z_arr is None
    once = pl.Buffered(1)
    sb = min(bsz, max(1, HYENA_STEP_ROWS // n))

    def sect(part):
        off = OFF_HY // tn + part * nct
        return pl.BlockSpec((sb, n, tn), functools.partial(lambda c, bi, o: (bi, 0, o + c), o=off))

    def wsect(rows, part):
        return pl.BlockSpec((rows, tn), functools.partial(lambda c, bi, o: (0, o + c), o=part * nct))

    whole = lambda a: pl.BlockSpec(a.shape, lambda c, bi: (0, 0), pipeline_mode=once)
    ocol = functools.partial(lambda c, bi, o: (0, o + c), o=order * nct)
    if conv_z:
        in_specs = [sect(HYENA_ORDER), wsect(SHORT_CONV, HYENA_ORDER), wsect(1, HYENA_ORDER)]
        args = [p, conv_w, conv_b]
    else:
        in_specs = [pl.BlockSpec((sb, n, tn), lambda c, bi: (bi, 0, c))]
        args = [z_arr]
    mats = [tabs['mfe'], tabs['mfo'], tabs['mie'], tabs['mio']]
    in_specs += [sect(order), wsect(SHORT_CONV, order), wsect(1, order)] + [whole(m) for m in mats]
    in_specs += [pl.BlockSpec((4, h, tn), functools.partial(lambda c, bi, o: (0, 0, o + c), o=order * nct),
                              pipeline_mode=once),
                 pl.BlockSpec((2, tn), ocol),
                 pl.BlockSpec((1, tn), ocol)]
    args += [p, conv_w, conv_b] + mats + [k4, mid, skip]
    return pl.pallas_call(
        functools.partial(_hyconv_kernel, conv_z=conv_z, cw=cw, n=n),
        grid=(nct, bsz // sb),
        in_specs=in_specs,
        out_specs=pl.BlockSpec((sb, n, tn), lambda c, bi: (bi, 0, c)),
        out_shape=jax.ShapeDtypeStruct((bsz, n, D_HYENA), BF16),
        compiler_params=_params("arbitrary", "arbitrary"),
        name="hyena_conv",
    )(*args)


def _fnet_kernel(ua_ref, ub_ref, csc_ref, mne_ref, mno_ref, o_ref, *, n, scale):
    hh = n // 2
    qq = n // 4
    gpb = HALF_D // FNET_GROUP_DIM
    for s in range(o_ref.shape[0]):
        for g in range(FNET_GROUPS):
            gsl = slice(g * FNET_GROUP_DIM, (g + 1) * FNET_GROUP_DIM)
            u_ref = (ua_ref, ub_ref)[g // gpb]
            u = u_ref[s, :, (g % gpb) * FNET_GROUP_DIM:(g % gpb + 1) * FNET_GROUP_DIM]
            t = _dot(u, csc_ref[...])
            tc = t[:, :FNET_GROUP_DIM].astype(BF16)
            ts = t[:, FNET_GROUP_DIM:].astype(BF16)
            a = _dot(mne_ref[...], jnp.concatenate([tc[:hh], ts[:hh]], 0))
            b = _dot(mno_ref[...], jnp.concatenate([tc[hh:], ts[hh:]], 0))
            lo = ((a + b) * scale).astype(o_ref.dtype)
            hi = ((a - b) * scale).astype(o_ref.dtype)
            o_ref[s, 0:qq, gsl] = lo[:qq]
            o_ref[s, qq:hh, gsl] = hi[:qq]
            o_ref[s, hh:hh + qq, gsl] = lo[qq:]
            o_ref[s, hh + qq:n, gsl] = hi[qq:]


def _fnet(p, csc, mne, mno):
    bsz, n, _ = p.shape
    scale = 1.0 / math.sqrt(n * FNET_GROUP_DIM)
    sb = min(bsz, max(1, HYENA_STEP_ROWS // n))
    whole = lambda a: pl.BlockSpec(a.shape, lambda bi: (0, 0), pipeline_mode=pl.Buffered(1))
    half = lambda c: pl.BlockSpec((sb, n, HALF_D), functools.partial(lambda bi, c: (bi, 0, c), c=c))
    return pl.pallas_call(
        functools.partial(_fnet_kernel, n=n, scale=scale),
        grid=(bsz // sb,),
        in_specs=[half(OFF_FN // HALF_D), half(OFF_FN // HALF_D + 1), whole(csc), whole(mne), whole(mno)],
        out_specs=pl.BlockSpec((sb, n, D_MODEL), lambda bi: (bi, 0, 0)),
        out_shape=jax.ShapeDtypeStruct((bsz, n, D_MODEL), BF16),
        compiler_params=_params("arbitrary"),
        name="fnet",
    )(p, p, csc, mne, mno)


MERGE_ROW_CHUNK = 256


def _merge_kernel(oa_ref, hy_ref, fn_ref, ga0, ga1, gh0, gh1, gf0, gf1, x_ref, mod_ref,
                  wa_ref, wh_ref, wf_ref, wo_ref, bo_ref, lg_ref, lb_ref, o_ref):
    rc = min(MERGE_ROW_CHUNK, x_ref.shape[0])
    for r0 in range(0, x_ref.shape[0], rc):
        rs = slice(r0, r0 + rc)
        gate = lambda g0, g1: jnp.concatenate([g0[rs, :], g1[rs, :]], 1).astype(F32)
        m = gate(ga0, ga1) * _dot(oa_ref[rs, :], wa_ref[...])
        m = m + gate(gh0, gh1) * _dot(hy_ref[rs, :], wh_ref[...])
        m = m + gate(gf0, gf1) * _dot(fn_ref[rs, :], wf_ref[...])
        y = _dot(m.astype(BF16), wo_ref[...]) + bo_ref[...]
        r = ALPHA * x_ref[rs, :] + mod_ref[2:3, :] * y
        o_ref[rs, :] = _layer_norm(r, lg_ref[...], lb_ref[...])


def _merge(o_att, z_hy, y_fn, p, x, mod, wa, wh, wf, wo, bo, lg, lb, layer):
    bsz, n, d = x.shape
    tm = min(n, 512)
    tok = lambda c: pl.BlockSpec((None, tm, d), functools.partial(lambda bi, i, c: (bi, i, c), c=c))
    wspec = pl.BlockSpec((None, d, d), lambda bi, i: (layer, 0, 0))
    vspec = pl.BlockSpec((None, 1, d), lambda bi, i: (layer, 0, 0))
    gates = [pl.BlockSpec((None, tm, HALF_D), functools.partial(lambda bi, i, c: (bi, i, c), c=OFF_G // HALF_D + c))
             for c in range(3 * d // HALF_D)]
    return pl.pallas_call(
        _merge_kernel,
        grid=(bsz, n // tm),
        in_specs=[tok(0), tok(0), tok(0)] + gates + [tok(0),
                  pl.BlockSpec((None, N_MOD, d), lambda bi, i: (bi, 0, 0)),
                  wspec, wspec, wspec, wspec, vspec, vspec, vspec],
        out_specs=tok(0),
        out_shape=jax.ShapeDtypeStruct((bsz, n, d), F32),
        compiler_params=_params("arbitrary", "arbitrary"),
        name="merge_ln",
    )(o_att, z_hy, y_fn, *([p] * len(gates)), x, mod, wa, wh, wf, wo, bo, lg, lb)


MLP_ROW_CHUNK = 512
MLP_FF_CHUNK = 1024


def _mlp_kernel(x_ref, mod_ref, w1_ref, b1_ref, w2_ref, b2_ref, lg_ref, lb_ref, o_ref):
    ff = w1_ref.shape[1]
    rc = min(MLP_ROW_CHUNK, x_ref.shape[0])
    for r0 in range(0, x_ref.shape[0], rc):
        rs = slice(r0, r0 + rc)
        x = x_ref[rs, :]
        h = (x * (1.0 + mod_ref[4:5, :]) + mod_ref[3:4, :]).astype(BF16)
        y = b2_ref[...]
        for c0 in range(0, ff, MLP_FF_CHUNK):
            cs = slice(c0, c0 + MLP_FF_CHUNK)
            a = jnp.maximum(_dot(h, w1_ref[:, cs]) + b1_ref[:, cs], 0.0)
            y = y + _dot((a * a).astype(BF16), w2_ref[cs, :])
        r = ALPHA * x + mod_ref[5:6, :] * y
        o_ref[rs, :] = _layer_norm(r, lg_ref[...], lb_ref[...])


def _mlp(x, mod, w1, b1, w2, b2, lg, lb, layer):
    bsz, n, d = x.shape
    tm = min(n, 2 * MLP_ROW_CHUNK)
    whole = lambda a: pl.BlockSpec((None,) + a.shape[1:], lambda bi, i: (layer, 0, 0),
                                   pipeline_mode=pl.Buffered(1))
    return pl.pallas_call(
        _mlp_kernel,
        grid=(bsz, n // tm),
        in_specs=[pl.BlockSpec((None, tm, d), lambda bi, i: (bi, i, 0)),
                  pl.BlockSpec((None, N_MOD, d), lambda bi, i: (bi, 0, 0)),
                  whole(w1), whole(b1), whole(w2), whole(b2), whole(lg), whole(lb)],
        out_specs=pl.BlockSpec((None, tm, d), lambda bi, i: (bi, i, 0)),
        out_shape=jax.ShapeDtypeStruct((bsz, n, d), F32),
        compiler_params=_params("arbitrary", "arbitrary"),
        name="mlp_ln",
    )(x, mod, w1, b1, w2, b2, lg, lb)


def _int_grid(rows, cols):
    k = lax.broadcasted_iota(jnp.int32, (rows, cols), 0)
    s = lax.broadcasted_iota(jnp.int32, (rows, cols), 1)
    return k, s


TRIG_ROW_BLOCK = 32


def _trig_table(n_rows, samp, period, k_step=1, k_first=0):
    def base(kv):
        ang = ((kv[:, None] * samp[None, :]) % period).astype(F32) * (2.0 * math.pi / period)
        return jnp.cos(ang), jnp.sin(ang)

    ch, sh = base(jnp.arange(n_rows // TRIG_ROW_BLOCK, dtype=jnp.int32) * (TRIG_ROW_BLOCK * k_step))
    cl, sl = base(jnp.arange(TRIG_ROW_BLOCK, dtype=jnp.int32) * k_step + k_first)
    ch, sh, cl, sl = ch[:, None, :], sh[:, None, :], cl[None], sl[None]
    shape = (n_rows, samp.shape[0])
    return (ch * cl - sh * sl).reshape(shape), (sh * cl + ch * sl).reshape(shape)


def _hyena_dft(n):
    h = n // 2
    s = jnp.arange(h, dtype=jnp.int32)
    alt = jnp.where(s % 2 == 0, 1.0, -1.0)[None, :]
    first = lax.broadcasted_iota(jnp.int32, (h, h), 0) == 0
    out = {}
    for name, samp in (('e', 2 * s), ('o', 2 * s + 1)):
        cos, sin = _trig_table(h, samp, 2 * n)
        mf = jnp.concatenate([cos, jnp.where(first, alt, sin)], 0)
        out['mf' + name] = mf.astype(BF16)
        out['mi' + name] = mf.T.astype(BF16)
    return out


def _fnet_dft(n):
    h = n // 2
    s = jnp.arange(h, dtype=jnp.int32)
    out = {}
    for name, samp in (('e', 2 * s), ('o', 2 * s + 1)):
        blocks = []
        for parity in range(2):
            cos, sin = _trig_table(h // 2, samp, n, k_step=2, k_first=parity)
            blocks.append(jnp.concatenate([cos, -sin], 1))
        out['mn' + name] = jnp.concatenate(blocks, 0).astype(BF16)
    k, s = _int_grid(FNET_GROUP_DIM, FNET_GROUP_DIM)
    ang = ((k * s) % FNET_GROUP_DIM).astype(F32) * (2.0 * math.pi / FNET_GROUP_DIM)
    out['csc'] = jnp.concatenate([jnp.cos(ang), jnp.sin(ang)], 1).astype(BF16)
    return out


def _filter_embedding(n, width):
    t = jnp.linspace(0.0, 1.0, n, dtype=F32)[:, None]
    w = (2.0 * math.pi / n) * jnp.arange(n, dtype=F32)[:, None]
    f = jnp.linspace(1e-4, HYENA_BANDS - 1, HYENA_BANDS, dtype=F32)[None, :]
    emb = jnp.concatenate([t, jnp.cos(f * w), -jnp.sin(f * w)], -1)
    return _even_odd_rows(jnp.pad(emb, ((0, 0), (0, width - HYENA_EMB))))


def _rope_tables(n):
    rows = n // GRID_W
    row = jnp.repeat(jnp.arange(rows, dtype=F32), GRID_W)
    col = jnp.tile(jnp.arange(GRID_W, dtype=F32), rows)
    inv_freq = ROPE_THETA ** (-jnp.arange(ROPE_AXIS_DIM // 2, dtype=F32) * 2.0 / ROPE_AXIS_DIM)
    ar = row[:, None] * inv_freq[None, :]
    ac = col[:, None] * inv_freq[None, :]
    cos = jnp.concatenate([jnp.cos(ar), jnp.cos(ar), jnp.cos(ac), jnp.cos(ac)] * 2, -1)
    sin = jnp.concatenate([-jnp.sin(ar), jnp.sin(ar), -jnp.sin(ac), jnp.sin(ac)] * 2, -1)
    return _even_odd_rows(cos), _even_odd_rows(sin)


def _mixer(p_tok, seq_shape, lw, tabs, ctx_parts):
    bsz, n = seq_shape
    p = p_tok.reshape(bsz, n, p_tok.shape[-1])
    o_att = _attention(p, [(p, OFF_K // KV_W, OFF_V // KV_W)] + ctx_parts)

    k4, mid = _filters(n, tabs['emb'], lw['hy_w1'], lw['hy_b1'], lw['hy_freq1'], lw['hy_w2'], lw['hy_b2'],
                       lw['hy_freq2'], lw['hy_w3'], tabs['absdelta'], tabs['mfe'], tabs['mfo'])
    z = None
    for o in range(HYENA_ORDER):
        z = _hyconv(o, z, p, lw['conv_w'], lw['conv_b'], tabs, k4, mid, lw['hy_skip'])
    y_fn = _fnet(p, tabs['csc'], tabs['mne'], tabs['mno'])
    tok = lambda a: a.reshape(p_tok.shape[0], p_tok.shape[1], a.shape[-1])
    return tok(o_att), tok(z), tok(y_fn), p


def _block(x_tok, mod, seq_shape, layer, sw, lw, tabs, rope_tabs, ctx_parts):
    p_tok = _inproj(x_tok, mod, sw['w_in'], sw['b_in'], layer, INPROJ_PLAN, lw['q_gain'], lw['k_gain'], rope_tabs)
    o_att, z_hy, y_fn, p = _mixer(p_tok, seq_shape, lw, tabs, ctx_parts)
    x_tok = _merge(o_att, z_hy, y_fn, p_tok, x_tok, mod, sw['w_att_o'], sw['w_hy_o'], sw['w_fn_o'],
                   sw['w_out'], sw['b_out'], sw['ln1_g'], sw['ln1_b'], layer)
    x_tok = _mlp(x_tok, mod, sw['w_mlp1'], sw['b_mlp1'], sw['w_mlp2'], sw['b_mlp2'], sw['ln2_g'], sw['ln2_b'],
                 layer)
    return x_tok, p


def _seq_tables(n):
    tabs = dict(_hyena_dft(n))
    tabs.update(_fnet_dft(n))
    deltas = jnp.abs(jnp.linspace(MIN_DECAY, MAX_DECAY, D_HYENA, dtype=F32))
    tabs.update(emb=_filter_embedding(n, V7X_LANES), absdelta=jnp.tile(deltas, HYENA_ORDER)[None, :])
    return tabs


def kernel(x, c, ctx, c_ctx, w_ada, b_ada, w_in, b_in, conv_w, conv_b, hy_w1, hy_b1, hy_freq1, hy_w2, hy_b2,
           hy_freq2, hy_w3, hy_skip, q_gain, k_gain, w_att_o, w_hy_o, w_fn_o, w_out, b_out, ln1_g, ln1_b,
           w_mlp1, b_mlp1, w_mlp2, b_mlp2, ln2_g, ln2_b):
    bsz, n_lat, d = x.shape
    n_ctx = ctx.shape[1]
    depth = w_ada.shape[0]
    tabs_lat = _seq_tables(n_lat)
    tabs_ctx = _seq_tables(n_ctx)
    rope_tabs = _rope_tables(n_lat)

    n_rows = -(-(bsz + 1) // 16) * 16
    cond = jnp.concatenate([c, c_ctx[None, :], jnp.zeros((n_rows - bsz - 1, d), F32)], 0)

    x = _deinterleave(x)
    ctx_tok = _deinterleave(ctx).reshape(1, bsz * n_ctx, d)
    row = lambda a: a[None, :]
    rows = lambda a: a[:, None, :]
    mods = _matmul(cond, w_ada, rows(b_ada), silu=True)
    sw = dict(
        w_in=w_in.astype(BF16), b_in=rows(b_in),
        w_att_o=w_att_o.astype(BF16), w_hy_o=w_hy_o.astype(BF16), w_fn_o=w_fn_o.astype(BF16),
        w_out=w_out.astype(BF16), b_out=rows(b_out), ln1_g=rows(ln1_g), ln1_b=rows(ln1_b),
        w_mlp1=w_mlp1.astype(BF16), b_mlp1=rows(b_mlp1), w_mlp2=w_mlp2.astype(BF16), b_mlp2=rows(b_mlp2),
        ln2_g=rows(ln2_g), ln2_b=rows(ln2_b))
    for i in range(depth):
        last = i == depth - 1
        lw = dict(
            conv_w=conv_w[i], conv_b=row(conv_b[i]),
            hy_w1=jnp.pad(hy_w1[i], ((0, V7X_LANES - HYENA_EMB), (0, 0))), hy_b1=row(hy_b1[i]), hy_freq1=row(hy_freq1[i]),
            hy_w2=hy_w2[i], hy_b2=row(hy_b2[i]), hy_freq2=row(hy_freq2[i]), hy_w3=hy_w3[i],
            hy_skip=hy_skip[i].reshape(1, HYENA_ORDER * D_HYENA),
            q_gain=row(q_gain[i]), k_gain=row(k_gain[i]))
        mod_l = mods[i, :bsz].reshape(bsz, N_MOD, d)
        mod_c = mods[i, bsz:bsz + 1].reshape(1, N_MOD, d)

        if last:
            kv_cols = slice(OFF_K, OFF_HY)
            p_c = _inproj(ctx_tok, mod_c, sw['w_in'][i:i + 1, :, kv_cols], sw['b_in'][i:i + 1, :, kv_cols], 0,
                          INPROJ_KV_PLAN, lw['q_gain'], lw['k_gain'], None)
            ctx_kv = (p_c.reshape(bsz, n_ctx, 2 * KV_W), 0, 1)
        else:
            ctx_tok, p_c = _block(ctx_tok, mod_c, (bsz, n_ctx), i, sw, lw, tabs_ctx, None, [])
            ctx_kv = (p_c, OFF_K // KV_W, OFF_V // KV_W)
        x, _ = _block(x, mod_l, (bsz, n_lat), i, sw, lw, tabs_lat, rope_tabs, [ctx_kv])
    return _interleave(x)
```

```python
import functools
import math

import jax
import jax.numpy as jnp
from jax import lax
from jax.experimental import pallas as pl
from jax.experimental.pallas import tpu as pltpu

F32 = jnp.float32
BF16 = jnp.bfloat16

D_MODEL = 1024
GRID_W = 64
HEAD_DIM = 128
N_Q_HEADS = D_MODEL // HEAD_DIM
N_KV_HEADS = 2
GQA_GROUP = N_Q_HEADS // N_KV_HEADS
ROPE_THETA = 10000.0
ROPE_AXIS_DIM = HEAD_DIM // 2
ATTN_SCALE = HEAD_DIM ** -0.5

D_HYENA = D_MODEL
HYENA_ORDER = 2
HYENA_BANDS = 16
HYENA_EMB = 1 + 2 * HYENA_BANDS
SHORT_CONV = 3
DECAY_TARGET = 1e-2
MIN_DECAY = math.log(DECAY_TARGET) / 0.3
MAX_DECAY = math.log(DECAY_TARGET) / 1.5

FNET_GROUPS = 4
FNET_GROUP_DIM = D_MODEL // FNET_GROUPS
N_MOD = 6
DEPTH = 4

Q_W = N_Q_HEADS * HEAD_DIM
KV_W = N_KV_HEADS * HEAD_DIM
HY_W = (HYENA_ORDER + 1) * D_HYENA
OFF_Q = 0
OFF_K = OFF_Q + Q_W
OFF_V = OFF_K + KV_W
OFF_HY = OFF_V + KV_W
OFF_FN = OFF_HY + HY_W
OFF_G = OFF_FN + D_MODEL
D_IN = OFF_G + 3 * D_MODEL

HALF_D = D_MODEL // 2

ALPHA = (2 * DEPTH) ** 0.25
LN_EPS = 1e-6
RMS_EPS = 1e-6
KERN_EPS = 1e-6

V7X_VMEM_LIMIT_BYTES = 56 * 1024 * 1024
V7X_LANES = 128
V7X_MXU_DIM = 256
HYENA_COL_TILE = 512


def _params(*sem):
    return pltpu.CompilerParams(dimension_semantics=sem, vmem_limit_bytes=V7X_VMEM_LIMIT_BYTES)


def _dot(a, b):
    return jnp.dot(a, b, preferred_element_type=F32)


def _layer_norm(r, g, b):
    mu = jnp.mean(r, -1, keepdims=True)
    d = r - mu
    var = jnp.mean(d * d, -1, keepdims=True)
    return d * lax.rsqrt(var + LN_EPS) * g + b


def _sigmoid(x):
    return 1.0 / (1.0 + jnp.exp(-x))


def _deinterleave_kernel(x_ref, o_ref):
    hh = x_ref.shape[1] // 2
    o_ref[:, 0:hh, :] = x_ref[:, pl.ds(0, hh, stride=2), :]
    o_ref[:, hh:2 * hh, :] = x_ref[:, pl.ds(1, hh, stride=2), :]


def _interleave_kernel(x_ref, o_ref):
    hh = x_ref.shape[1] // 2
    o_ref[:, pl.ds(0, hh, stride=2), :] = x_ref[:, 0:hh, :]
    o_ref[:, pl.ds(1, hh, stride=2), :] = x_ref[:, hh:2 * hh, :]


REORDER_BLOCK_ROWS = 8192


def _reorder_tokens(x, body, name):
    bsz, n, d = x.shape
    tc = V7X_LANES
    bb = min(bsz, max(1, REORDER_BLOCK_ROWS // n))
    spec = pl.BlockSpec((bb, n, tc), lambda bi, c: (bi, 0, c))
    return pl.pallas_call(
        body,
        grid=(bsz // bb, d // tc),
        in_specs=[spec],
        out_specs=spec,
        out_shape=jax.ShapeDtypeStruct(x.shape, x.dtype),
        compiler_params=_params("arbitrary", "arbitrary"),
        name=name,
    )(x)


def _deinterleave(x):
    return _reorder_tokens(x, _deinterleave_kernel, "deinterleave")


def _interleave(x):
    return _reorder_tokens(x, _interleave_kernel, "interleave")


def _even_odd_rows(a):
    return jnp.concatenate([a[0::2], a[1::2]], 0)


def _matmul_kernel(a_ref, b_ref, bias_ref, o_ref, *, silu):
    a = a_ref[...]
    if silu:
        a = a * _sigmoid(a)
    o_ref[...] = _dot(a.astype(BF16), b_ref[...].astype(BF16)) + bias_ref[...]


def _matmul(a, b, bias, *, silu=False, tn=1024):
    m, k = a.shape
    nl, _, n = b.shape
    return pl.pallas_call(
        functools.partial(_matmul_kernel, silu=silu),
        grid=(nl, n // tn),
        in_specs=[pl.BlockSpec((m, k), lambda l, j: (0, 0)),
                  pl.BlockSpec((None, k, tn), lambda l, j: (l, 0, j)),
                  pl.BlockSpec((None, 1, tn), lambda l, j: (l, 0, j))],
        out_specs=pl.BlockSpec((None, m, tn), lambda l, j: (l, 0, j)),
        out_shape=jax.ShapeDtypeStruct((nl, m, n), F32),
        compiler_params=_params("arbitrary", "arbitrary"),
        name="ada_matmul",
    )(a, b, bias)


HEAD_PAIR = 2 * HEAD_DIM

INPROJ_PLAN = ((OFF_Q, OFF_HY, 'heads', tuple((c, 0) for c in range(0, Q_W, HEAD_PAIR)) + ((OFF_K, 1),)),
               (OFF_HY, OFF_FN, 'plain', ()),
               (OFF_FN, OFF_G, 'plain', ()),
               (OFF_G, D_IN, 'sigmoid', ()))
INPROJ_KV_PLAN = ((0, 2 * KV_W, 'heads', ((0, 1),)),)


def _inproj_kernel(*refs, plan, rope):
    x_ref, mod_ref, w_ref, b_ref, gq_ref, gk_ref, avg_ref = refs[:7]
    if rope:
        swap_ref, cos_ref, sin_ref = refs[7:10]
    o_ref = refs[-1]
    gains = (gq_ref, gk_ref)
    h = (x_ref[...] * (1.0 + mod_ref[1:2, :]) + mod_ref[0:1, :]).astype(BF16)
    for c0, c1, mode, pairs in plan:
        y = _dot(h, w_ref[:, c0:c1]) + b_ref[:, c0:c1]
        if mode == 'sigmoid':
            y = _sigmoid(y)
        if mode != 'heads':
            o_ref[:, c0:c1] = y.astype(o_ref.dtype)
            continue
        done = c0
        for off, gi in pairs:
            if c0 + off > done:
                o_ref[:, done:c0 + off] = y[:, done - c0:off].astype(o_ref.dtype)
            yp = y[:, off:off + HEAD_PAIR]
            ms = _dot((yp * yp).astype(BF16), avg_ref[...])
            yn = yp * lax.rsqrt(ms + RMS_EPS) * gains[gi][...]
            if rope:
                yn = yn * cos_ref[...] + _dot(yn.astype(BF16), swap_ref[...]) * sin_ref[...]
            o_ref[:, c0 + off:c0 + off + HEAD_PAIR] = yn.astype(o_ref.dtype)
            done = c0 + off + HEAD_PAIR
        if done < c1:
            o_ref[:, done:c1] = y[:, done - c0:].astype(o_ref.dtype)


def _inproj(x, mod, w, b, layer, plan, q_gain, k_gain, rope_tabs):
    bsz, n, d = x.shape
    nout = w.shape[2]
    tm = min(n, 512)
    rope = rope_tabs is not None
    avg, swap = _head_pair_matrices()
    whole = lambda a: pl.BlockSpec((None,) + a.shape[1:], lambda bi, i: (layer, 0, 0),
                                   pipeline_mode=pl.Buffered(1))
    const = lambda a: pl.BlockSpec(a.shape, lambda bi, i: (0, 0))
    args = [x, mod, w, b, jnp.tile(q_gain * ATTN_SCALE, (1, 2)), jnp.tile(k_gain, (1, 2)), avg]
    in_specs = [pl.BlockSpec((None, tm, d), lambda bi, i: (bi, i, 0)),
                pl.BlockSpec((None, N_MOD, d), lambda bi, i: (bi, 0, 0)),
                whole(w), whole(b), const(args[4]), const(args[5]), const(avg)]
    if rope:
        in_specs += [const(swap)] + [pl.BlockSpec((tm, HEAD_PAIR), lambda bi, i: (i, 0))] * 2
        args += [swap] + list(rope_tabs)
    return pl.pallas_call(
        functools.partial(_inproj_kernel, plan=plan, rope=rope),
        grid=(bsz, n // tm),
        in_specs=in_specs,
        out_specs=pl.BlockSpec((None, tm, nout), lambda bi, i: (bi, i, 0)),
        out_shape=jax.ShapeDtypeStruct((bsz, n, nout), BF16),
        compiler_params=_params("arbitrary", "arbitrary"),
        name="inproj",
    )(*args)


def _head_pair_matrices():
    i, j = _int_grid(HEAD_PAIR, HEAD_PAIR)
    avg = jnp.where(i // HEAD_DIM == j // HEAD_DIM, 1.0 / HEAD_DIM, 0.0)
    half = ROPE_AXIS_DIM // 2
    partner = jnp.where(j % ROPE_AXIS_DIM < half, j + half, j - half)
    return avg.astype(BF16), jnp.where(i == partner, 1.0, 0.0).astype(BF16)


def _attn_kernel(*refs, n_parts):
    q_ref = refs[0]
    kv_refs = refs[1:1 + 2 * n_parts]
    o_ref = refs[1 + 2 * n_parts]
    for g in range(N_KV_HEADS):
        gsl = slice(g * HEAD_DIM, (g + 1) * HEAD_DIM)
        ks = [kv_refs[2 * i][:, gsl] for i in range(n_parts)]
        vs = [jnp.concatenate([kv_refs[2 * i + 1][:, gsl], jnp.ones((k.shape[0], HEAD_DIM), BF16)], 1)
              for i, k in enumerate(ks)]
        for r in range(GQA_GROUP):
            h = g * GQA_GROUP + r
            hsl = slice(h * HEAD_DIM, (h + 1) * HEAD_DIM)
            q = q_ref[:, hsl]
            ss = [lax.dot_general(q, k, (((1,), (1,)), ((), ())), preferred_element_type=F32) for k in ks]
            m = jnp.max(ss[0], -1, keepdims=True)
            for s in ss[1:]:
                m = jnp.maximum(m, jnp.max(s, -1, keepdims=True))
            o = None
            for s, v in zip(ss, vs):
                pv = _dot(jnp.exp((s - m).astype(BF16)), v)
                o = pv if o is None else o + pv
            o_ref[:, hsl] = (o[:, :HEAD_DIM] / o[:, HEAD_DIM:]).astype(o_ref.dtype)


def _attention(p, parts):
    bsz, n, _ = p.shape
    w = Q_W
    tq = min(n, 1024)
    in_specs = [pl.BlockSpec((None, tq, w), lambda bi, i: (bi, i, OFF_Q // Q_W))]
    args = [p]
    for kv, kblk, vblk in parts:
        nk = kv.shape[1]
        for blk in (kblk, vblk):
            in_specs.append(pl.BlockSpec((None, nk, KV_W), functools.partial(lambda bi, i, c: (bi, 0, c), c=blk)))
        args += [kv, kv]
    return pl.pallas_call(
        functools.partial(_attn_kernel, n_parts=len(parts)),
        grid=(bsz, n // tq),
        in_specs=in_specs,
        out_specs=pl.BlockSpec((None, tq, w), lambda bi, i: (bi, i, 0)),
        out_shape=jax.ShapeDtypeStruct((bsz, n, w), BF16),
        compiler_params=_params("arbitrary", "arbitrary"),
        name="attention",
    )(*args)


FILTER_COL_CHUNK = V7X_MXU_DIM


def _filter_kernel(emb_ref, w1_ref, b1_ref, f1_ref, w2_ref, b2_ref, f2_ref, w3f_ref, w3b_ref, dl_ref,
                   mfe_ref, mfo_ref, k4_ref, mid_ref, h_ref, *, n):
    hh = n // 2

    @pl.when(pl.program_id(0) == 0)
    def _():
        h1 = jnp.sin(f1_ref[...] * (_dot(emb_ref[...], w1_ref[...]) + b1_ref[...]))
        h_ref[...] = jnp.sin(f2_ref[...] * (_dot(h1, w2_ref[...]) + b2_ref[...])).astype(BF16)

    h = h_ref[...]
    cw = FILTER_COL_CHUNK
    row = lax.broadcasted_iota(jnp.int32, (n, cw), 0)
    lag = jnp.where(row < hh, 2 * row, 2 * (row - hh) + 1)
    t = lag.astype(F32) * (1.0 / (n - 1))
    sign = jnp.where(row % 2 == 0, 1.0, -1.0)
    row0 = lax.broadcasted_iota(jnp.int32, (hh, cw), 0) == 0
    for c0 in range(0, k4_ref.shape[2], cw):
        cs = slice(c0, c0 + cw)
        dec = jnp.exp(-t * dl_ref[:, cs])
        hf = _dot(h, w3f_ref[:, cs].astype(BF16)) * dec
        hb = jnp.where(lag == 0, 0.0, _dot(h, w3b_ref[:, cs].astype(BF16)) * dec)
        ssq = jnp.sum(hf * hf, 0, keepdims=True) + jnp.sum(hb * hb, 0, keepdims=True)
        nrm = lax.rsqrt(ssq + KERN_EPS) * (1.0 / n)
        hs = (hf + hb) * nrm
        hd = (hf - hb) * nrm
        mid_ref[0:1, cs] = jnp.sum(jnp.where(row < hh, hs * sign, 0.0), 0, keepdims=True)
        mid_ref[1:2, cs] = jnp.sum(jnp.where(row < hh, 0.0, hd * sign), 0, keepdims=True)
        hs, hd = hs.astype(BF16), hd.astype(BF16)
        p = _dot(mfe_ref[0:hh, :], hs[:hh])
        q = _dot(mfo_ref[0:hh, :], hs[hh:])
        pp = _dot(mfe_ref[hh:n, :], hd[:hh])
        qp = _dot(mfo_ref[hh:n, :], hd[hh:])
        k4_ref[0, :, cs] = jnp.where(row0, 0.5 * (p + q), p + q)
        k4_ref[1, :, cs] = jnp.where(row0, 0.0, pp + qp)
        k4_ref[2, :, cs] = jnp.where(row0, 0.5 * (p - q), p - q)
        k4_ref[3, :, cs] = jnp.where(row0, 0.0, qp - pp)


def _filters(n, emb, w1, b1, f1, w2, b2, f2, w3, absdelta, mfe, mfo):
    od = HYENA_ORDER * D_HYENA
    tn = HYENA_COL_TILE
    fh = w2.shape[0]
    full = lambda a: pl.BlockSpec(a.shape, lambda j: (0, 0), pipeline_mode=pl.Buffered(1))
    return pl.pallas_call(
        functools.partial(_filter_kernel, n=n),
        grid=(od // tn,),
        in_specs=[full(emb), full(w1), full(b1), full(f1), full(w2), full(b2), full(f2),
                  pl.BlockSpec((fh, tn), lambda j: (0, j)),
                  pl.BlockSpec((fh, tn), lambda j: (0, j + od // tn)),
                  pl.BlockSpec((1, tn), lambda j: (0, j)),
                  full(mfe), full(mfo)],
        out_specs=[pl.BlockSpec((4, n // 2, tn), lambda j: (0, 0, j)),
                   pl.BlockSpec((2, tn), lambda j: (0, j))],
        out_shape=[jax.ShapeDtypeStruct((4, n // 2, od), F32), jax.ShapeDtypeStruct((2, od), F32)],
        scratch_shapes=[pltpu.VMEM((n, fh), BF16)],
        compiler_params=_params("arbitrary"),
        name="hyena_filters",
    )(emb, w1, b1, f1, w2, b2, f2, w3, w3, absdelta, mfe, mfo)


def _short_conv(x, w_ref, b_ref):
    hh = x.shape[0] // 2
    e = x[:hh]
    o = x[hh:]
    row = lax.broadcasted_iota(jnp.int32, e.shape, 0)
    o_prev = jnp.where(row == 0, 0.0, pltpu.roll(o, 1, 0))
    e_next = jnp.where(row == hh - 1, 0.0, pltpu.roll(e, hh - 1, 0))
    w0, w1, w2 = w_ref[0:1, :], w_ref[1:2, :], w_ref[2:3, :]
    return (b_ref[...] + o_prev * w0 + e * w1 + o * w2,
            b_ref[...] + e * w0 + o * w1 + e_next * w2)


HYENA_STEP_ROWS = 2048


def _hyconv_kernel(*refs, conv_z, cw, n):
    if conv_z:
        z_ref, cwz_ref, cbz_ref = refs[:3]
        refs = refs[3:]
    else:
        z_ref = refs[0]
        refs = refs[1:]
    x_ref, cwx_ref, cbx_ref, mfe_ref, mfo_ref, mie_ref, mio_ref, k4_ref, mid_ref, skip_ref, o_ref = refs
    hh = n // 2
    first8 = lax.broadcasted_iota(jnp.int32, (8, cw), 0) == 0
    chunks = [(s, slice(c * cw, (c + 1) * cw)) for s in range(o_ref.shape[0]) for c in range(o_ref.shape[2] // cw)]
    zs, fwd, gs, inv = [], [], [], []
    for s, cs in chunks:
        if conv_z:
            ze, zo = _short_conv(z_ref[s, :, cs].astype(F32), cwz_ref.at[:, cs], cbz_ref.at[:, cs])
            zs.append((ze.astype(BF16), zo.astype(BF16)))
        else:
            zs.append((z_ref[s, 0:hh, cs], z_ref[s, hh:n, cs]))
    for ze, zo in zs:
        fwd.append((_dot(mfe_ref[...], ze), _dot(mfo_ref[...], zo)))
    xs = [_short_conv(x_ref[s, :, cs].astype(F32), cwx_ref.at[:, cs], cbx_ref.at[:, cs]) for s, cs in chunks]
    for (_, cs), (fe, fo) in zip(chunks, fwd):
        p, pp, q, qp = fe[:hh], fe[hh:], fo[:hh], fo[hh:]
        kra, kia, krb, kib = k4_ref[0, :, cs], k4_ref[1, :, cs], k4_ref[2, :, cs], k4_ref[3, :, cs]
        fra, fia = p + q, pp + qp
        frb, fib = p - q, qp - pp
        wra = fra * kra - fia * kia
        wia = fra * kia + fia * kra
        wrb = frb * krb - fib * kib
        wib = frb * kib + fib * krb
        ge_im = wia - wib
        go_im = wia + wib
        kmr, kmi = mid_ref[0:1, cs], mid_ref[1:2, cs]
        pp8, qp8 = pp[:8], qp[:8]
        ge_im = jnp.concatenate([jnp.where(first8, pp8 * kmr - qp8 * kmi, ge_im[:8]), ge_im[8:]], 0)
        go_im = jnp.concatenate([jnp.where(first8, pp8 * kmi + qp8 * kmr, go_im[:8]), go_im[8:]], 0)
        gs.append(((wra + wrb).astype(BF16), ge_im.astype(BF16), (wra - wrb).astype(BF16), go_im.astype(BF16)))
    for ge_re, ge_im, go_re, go_im in gs:
        inv.append((_dot(mie_ref[:, :hh], ge_re) + _dot(mie_ref[:, hh:], ge_im),
                    _dot(mio_ref[:, :hh], go_re) + _dot(mio_ref[:, hh:], go_im)))
    for (s, cs), (ze, zo), (ye, yo), (xe, xo) in zip(chunks, zs, inv, xs):
        skip = skip_ref[:, cs]
        o_ref[s, 0:hh, cs] = (xe * (ye + ze.astype(F32) * skip)).astype(o_ref.dtype)
        o_ref[s, hh:n, cs] = (xo * (yo + zo.astype(F32) * skip)).astype(o_ref.dtype)


def _hyconv(order, z_arr, p, conv_w, conv_b, tabs, k4, mid, skip):
    bsz, n, _ = p.shape
    h = n // 2
    tn = HYENA_COL_TILE
    cw = V7X_MXU_DIM
    nct = D_HYENA // tn
    conv_z = z_arr is None
    once = pl.Buffered(1)
    sb = min(bsz, max(1, HYENA_STEP_ROWS // n))

    def sect(part):
        off = OFF_HY // tn + part * nct
        return pl.BlockSpec((sb, n, tn), functools.partial(lambda c, bi, o: (bi, 0, o + c), o=off))

    def wsect(rows, part):
        return pl.BlockSpec((rows, tn), functools.partial(lambda c, bi, o: (0, o + c), o=part * nct))

    whole = lambda a: pl.BlockSpec(a.shape, lambda c, bi: (0, 0), pipeline_mode=once)
    ocol = functools.partial(lambda c, bi, o: (0, o + c), o=order * nct)
    if conv_z:
        in_specs = [sect(HYENA_ORDER), wsect(SHORT_CONV, HYENA_ORDER), wsect(1, HYENA_ORDER)]
        args = [p, conv_w, conv_b]
    else:
        in_specs = [pl.BlockSpec((sb, n, tn), lambda c, bi: (bi, 0, c))]
        args = [z_arr]
    mats = [tabs['mfe'], tabs['mfo'], tabs['mie'], tabs['mio']]
    in_specs += [sect(order), wsect(SHORT_CONV, order), wsect(1, order)] + [whole(m) for m in mats]
    in_specs += [pl.BlockSpec((4, h, tn), functools.partial(lambda c, bi, o: (0, 0, o + c), o=order * nct),
                              pipeline_mode=once),
                 pl.BlockSpec((2, tn), ocol),
                 pl.BlockSpec((1, tn), ocol)]
    args += [p, conv_w, conv_b] + mats + [k4, mid, skip]
    return pl.pallas_call(
        functools.partial(_hyconv_kernel, conv_z=conv_z, cw=cw, n=n),
        grid=(nct, bsz // sb),
        in_specs=in_specs,
        out_specs=pl.BlockSpec((sb, n, tn), lambda c, bi: (bi, 0, c)),
        out_shape=jax.ShapeDtypeStruct((bsz, n, D_HYENA), BF16),
        compiler_params=_params("arbitrary", "arbitrary"),
        name="hyena_conv",
    )(*args)


def _fnet_kernel(ua_ref, ub_ref, csc_ref, mne_ref, mno_ref, o_ref, *, n, scale):
    hh = n // 2
    qq = n // 4
    gpb = HALF_D // FNET_GROUP_DIM
    for s in range(o_ref.shape[0]):
        for g in range(FNET_GROUPS):
            gsl = slice(g * FNET_GROUP_DIM, (g + 1) * FNET_GROUP_DIM)
            u_ref = (ua_ref, ub_ref)[g // gpb]
            u = u_ref[s, :, (g % gpb) * FNET_GROUP_DIM:(g % gpb + 1) * FNET_GROUP_DIM]
            t = _dot(u, csc_ref[...])
            tc = t[:, :FNET_GROUP_DIM].astype(BF16)
            ts = t[:, FNET_GROUP_DIM:].astype(BF16)
            a = _dot(mne_ref[...], jnp.concatenate([tc[:hh], ts[:hh]], 0))
            b = _dot(mno_ref[...], jnp.concatenate([tc[hh:], ts[hh:]], 0))
            lo = ((a + b) * scale).astype(o_ref.dtype)
            hi = ((a - b) * scale).astype(o_ref.dtype)
            o_ref[s, 0:qq, gsl] = lo[:qq]
            o_ref[s, qq:hh, gsl] = hi[:qq]
            o_ref[s, hh:hh + qq, gsl] = lo[qq:]
            o_ref[s, hh + qq:n, gsl] = hi[qq:]


def _fnet(p, csc, mne, mno):
    bsz, n, _ = p.shape
    scale = 1.0 / math.sqrt(n * FNET_GROUP_DIM)
    sb = min(bsz, max(1, HYENA_STEP_ROWS // n))
    whole = lambda a: pl.BlockSpec(a.shape, lambda bi: (0, 0), pipeline_mode=pl.Buffered(1))
    half = lambda c: pl.BlockSpec((sb, n, HALF_D), functools.partial(lambda bi, c: (bi, 0, c), c=c))
    return pl.pallas_call(
        functools.partial(_fnet_kernel, n=n, scale=scale),
        grid=(bsz // sb,),
        in_specs=[half(OFF_FN // HALF_D), half(OFF_FN // HALF_D + 1), whole(csc), whole(mne), whole(mno)],
        out_specs=pl.BlockSpec((sb, n, D_MODEL), lambda bi: (bi, 0, 0)),
        out_shape=jax.ShapeDtypeStruct((bsz, n, D_MODEL), BF16),
        compiler_params=_params("arbitrary"),
        name="fnet",
    )(p, p, csc, mne, mno)


MERGE_ROW_CHUNK = 256


def _merge_kernel(oa_ref, hy_ref, fn_ref, ga0, ga1, gh0, gh1, gf0, gf1, x_ref, mod_ref,
                  wa_ref, wh_ref, wf_ref, wo_ref, bo_ref, lg_ref, lb_ref, o_ref):
    rc = min(MERGE_ROW_CHUNK, x_ref.shape[0])
    for r0 in range(0, x_ref.shape[0], rc):
        rs = slice(r0, r0 + rc)
        gate = lambda g0, g1: jnp.concatenate([g0[rs, :], g1[rs, :]], 1).astype(F32)
        m = gate(ga0, ga1) * _dot(oa_ref[rs, :], wa_ref[...])
        m = m + gate(gh0, gh1) * _dot(hy_ref[rs, :], wh_ref[...])
        m = m + gate(gf0, gf1) * _dot(fn_ref[rs, :], wf_ref[...])
        y = _dot(m.astype(BF16), wo_ref[...]) + bo_ref[...]
        r = ALPHA * x_ref[rs, :] + mod_ref[2:3, :] * y
        o_ref[rs, :] = _layer_norm(r, lg_ref[...], lb_ref[...])


def _merge(o_att, z_hy, y_fn, p, x, mod, wa, wh, wf, wo, bo, lg, lb, layer):
    bsz, n, d = x.shape
    tm = min(n, 512)
    tok = lambda c: pl.BlockSpec((None, tm, d), functools.partial(lambda bi, i, c: (bi, i, c), c=c))
    wspec = pl.BlockSpec((None, d, d), lambda bi, i: (layer, 0, 0))
    vspec = pl.BlockSpec((None, 1, d), lambda bi, i: (layer, 0, 0))
    gates = [pl.BlockSpec((None, tm, HALF_D), functools.partial(lambda bi, i, c: (bi, i, c), c=OFF_G // HALF_D + c))
             for c in range(3 * d // HALF_D)]
    return pl.pallas_call(
        _merge_kernel,
        grid=(bsz, n // tm),
        in_specs=[tok(0), tok(0), tok(0)] + gates + [tok(0),
                  pl.BlockSpec((None, N_MOD, d), lambda bi, i: (bi, 0, 0)),
                  wspec, wspec, wspec, wspec, vspec, vspec, vspec],
        out_specs=tok(0),
        out_shape=jax.ShapeDtypeStruct((bsz, n, d), F32),
        compiler_params=_params("arbitrary", "arbitrary"),
        name="merge_ln",
    )(o_att, z_hy, y_fn, *([p] * len(gates)), x, mod, wa, wh, wf, wo, bo, lg, lb)


MLP_ROW_CHUNK = 512
MLP_FF_CHUNK = 1024


def _mlp_kernel(x_ref, mod_ref, w1_ref, b1_ref, w2_ref, b2_ref, lg_ref, lb_ref, o_ref):
    ff = w1_ref.shape[1]
    rc = min(MLP_ROW_CHUNK, x_ref.shape[0])
    for r0 in range(0, x_ref.shape[0], rc):
        rs = slice(r0, r0 + rc)
        x = x_ref[rs, :]
        h = (x * (1.0 + mod_ref[4:5, :]) + mod_ref[3:4, :]).astype(BF16)
        y = b2_ref[...]
        for c0 in range(0, ff, MLP_FF_CHUNK):
            cs = slice(c0, c0 + MLP_FF_CHUNK)
            a = jnp.maximum(_dot(h, w1_ref[:, cs]) + b1_ref[:, cs], 0.0)
            y = y + _dot((a * a).astype(BF16), w2_ref[cs, :])
        r = ALPHA * x + mod_ref[5:6, :] * y
        o_ref[rs, :] = _layer_norm(r, lg_ref[...], lb_ref[...])


def _mlp(x, mod, w1, b1, w2, b2, lg, lb, layer):
    bsz, n, d = x.shape
    tm = min(n, 2 * MLP_ROW_CHUNK)
    whole = lambda a: pl.BlockSpec((None,) + a.shape[1:], lambda bi, i: (layer, 0, 0),
                                   pipeline_mode=pl.Buffered(1))
    return pl.pallas_call(
        _mlp_kernel,
        grid=(bsz, n // tm),
        in_specs=[pl.BlockSpec((None, tm, d), lambda bi, i: (bi, i, 0)),
                  pl.BlockSpec((None, N_MOD, d), lambda bi, i: (bi, 0, 0)),
                  whole(w1), whole(b1), whole(w2), whole(b2), whole(lg), whole(lb)],
        out_specs=pl.BlockSpec((None, tm, d), lambda bi, i: (bi, i, 0)),
        out_shape=jax.ShapeDtypeStruct((bsz, n, d), F32),
        compiler_params=_params("arbitrary", "arbitrary"),
        name="mlp_ln",
    )(x, mod, w1, b1, w2, b2, lg, lb)


def _int_grid(rows, cols):
    k = lax.broadcasted_iota(jnp.int32, (rows, cols), 0)
    s = lax.broadcasted_iota(jnp.int32, (rows, cols), 1)
    return k, s


TRIG_ROW_BLOCK = 32


def _trig_table(n_rows, samp, period, k_step=1, k_first=0):
    def base(kv):
        ang = ((kv[:, None] * samp[None, :]) % period).astype(F32) * (2.0 * math.pi / period)
        return jnp.cos(ang), jnp.sin(ang)

    ch, sh = base(jnp.arange(n_rows // TRIG_ROW_BLOCK, dtype=jnp.int32) * (TRIG_ROW_BLOCK * k_step))
    cl, sl = base(jnp.arange(TRIG_ROW_BLOCK, dtype=jnp.int32) * k_step + k_first)
    ch, sh, cl, sl = ch[:, None, :], sh[:, None, :], cl[None], sl[None]
    shape = (n_rows, samp.shape[0])
    return (ch * cl - sh * sl).reshape(shape), (sh * cl + ch * sl).reshape(shape)


def _hyena_dft(n):
    h = n // 2
    s = jnp.arange(h, dtype=jnp.int32)
    alt = jnp.where(s % 2 == 0, 1.0, -1.0)[None, :]
    first = lax.broadcasted_iota(jnp.int32, (h, h), 0) == 0
    out = {}
    for name, samp in (('e', 2 * s), ('o', 2 * s + 1)):
        cos, sin = _trig_table(h, samp, 2 * n)
        mf = jnp.concatenate([cos, jnp.where(first, alt, sin)], 0)
        out['mf' + name] = mf.astype(BF16)
        out['mi' + name] = mf.T.astype(BF16)
    return out


def _fnet_dft(n):
    h = n // 2
    s = jnp.arange(h, dtype=jnp.int32)
    out = {}
    for name, samp in (('e', 2 * s), ('o', 2 * s + 1)):
        blocks = []
        for parity in range(2):
            cos, sin = _trig_table(h // 2, samp, n, k_step=2, k_first=parity)
            blocks.append(jnp.concatenate([cos, -sin], 1))
        out['mn' + name] = jnp.concatenate(blocks, 0).astype(BF16)
    k, s = _int_grid(FNET_GROUP_DIM, FNET_GROUP_DIM)
    ang = ((k * s) % FNET_GROUP_DIM).astype(F32) * (2.0 * math.pi / FNET_GROUP_DIM)
    out['csc'] = jnp.concatenate([jnp.cos(ang), jnp.sin(ang)], 1).astype(BF16)
    return out


def _filter_embedding(n, width):
    t = jnp.linspace(0.0, 1.0, n, dtype=F32)[:, None]
    w = (2.0 * math.pi / n) * jnp.arange(n, dtype=F32)[:, None]
    f = jnp.linspace(1e-4, HYENA_BANDS - 1, HYENA_BANDS, dtype=F32)[None, :]
    emb = jnp.concatenate([t, jnp.cos(f * w), -jnp.sin(f * w)], -1)
    return _even_odd_rows(jnp.pad(emb, ((0, 0), (0, width - HYENA_EMB))))


def _rope_tables(n):
    rows = n // GRID_W
    row = jnp.repeat(jnp.arange(rows, dtype=F32), GRID_W)
    col = jnp.tile(jnp.arange(GRID_W, dtype=F32), rows)
    inv_freq = ROPE_THETA ** (-jnp.arange(ROPE_AXIS_DIM // 2, dtype=F32) * 2.0 / ROPE_AXIS_DIM)
    ar = row[:, None] * inv_freq[None, :]
    ac = col[:, None] * inv_freq[None, :]
    cos = jnp.concatenate([jnp.cos(ar), jnp.cos(ar), jnp.cos(ac), jnp.cos(ac)] * 2, -1)
    sin = jnp.concatenate([-jnp.sin(ar), jnp.sin(ar), -jnp.sin(ac), jnp.sin(ac)] * 2, -1)
    return _even_odd_rows(cos), _even_odd_rows(sin)


def _mixer(p_tok, seq_shape, lw, tabs, ctx_parts):
    bsz, n = seq_shape
    p = p_tok.reshape(bsz, n, p_tok.shape[-1])
    o_att = _attention(p, [(p, OFF_K // KV_W, OFF_V // KV_W)] + ctx_parts)

    k4, mid = _filters(n, tabs['emb'], lw['hy_w1'], lw['hy_b1'], lw['hy_freq1'], lw['hy_w2'], lw['hy_b2'],
                       lw['hy_freq2'], lw['hy_w3'], tabs['absdelta'], tabs['mfe'], tabs['mfo'])
    z = None
    for o in range(HYENA_ORDER):
        z = _hyconv(o, z, p, lw['conv_w'], lw['conv_b'], tabs, k4, mid, lw['hy_skip'])
    y_fn = _fnet(p, tabs['csc'], tabs['mne'], tabs['mno'])
    tok = lambda a: a.reshape(p_tok.shape[0], p_tok.shape[1], a.shape[-1])
    return tok(o_att), tok(z), tok(y_fn), p


def _block(x_tok, mod, seq_shape, layer, sw, lw, tabs, rope_tabs, ctx_parts):
    p_tok = _inproj(x_tok, mod, sw['w_in'], sw['b_in'], layer, INPROJ_PLAN, lw['q_gain'], lw['k_gain'], rope_tabs)
    o_att, z_hy, y_fn, p = _mixer(p_tok, seq_shape, lw, tabs, ctx_parts)
    x_tok = _merge(o_att, z_hy, y_fn, p_tok, x_tok, mod, sw['w_att_o'], sw['w_hy_o'], sw['w_fn_o'],
                   sw['w_out'], sw['b_out'], sw['ln1_g'], sw['ln1_b'], layer)
    x_tok = _mlp(x_tok, mod, sw['w_mlp1'], sw['b_mlp1'], sw['w_mlp2'], sw['b_mlp2'], sw['ln2_g'], sw['ln2_b'],
                 layer)
    return x_tok, p


def _seq_tables(n):
    tabs = dict(_hyena_dft(n))
    tabs.update(_fnet_dft(n))
    deltas = jnp.abs(jnp.linspace(MIN_DECAY, MAX_DECAY, D_HYENA, dtype=F32))
    tabs.update(emb=_filter_embedding(n, V7X_LANES), absdelta=jnp.tile(deltas, HYENA_ORDER)[None, :])
    return tabs


def kernel(x, c, ctx, c_ctx, w_ada, b_ada, w_in, b_in, conv_w, conv_b, hy_w1, hy_b1, hy_freq1, hy_w2, hy_b2,
           hy_freq2, hy_w3, hy_skip, q_gain, k_gain, w_att_o, w_hy_o, w_fn_o, w_out, b_out, ln1_g, ln1_b,
           w_mlp1, b_mlp1, w_mlp2, b_mlp2, ln2_g, ln2_b):
    bsz, n_lat, d = x.shape
    n_ctx = ctx.shape[1]
    depth = w_ada.shape[0]
    tabs_lat = _seq_tables(n_lat)
    tabs_ctx = _seq_tables(n_ctx)
    rope_tabs = _rope_tables(n_lat)

    n_rows = -(-(bsz + 1) // 16) * 16
    cond = jnp.concatenate([c, c_ctx[None, :], jnp.zeros((n_rows - bsz - 1, d), F32)], 0)

    x = _deinterleave(x)
    ctx_tok = _deinterleave(ctx).reshape(1, bsz * n_ctx, d)
    row = lambda a: a[None, :]
    rows = lambda a: a[:, None, :]
    mods = _matmul(cond, w_ada, rows(b_ada), silu=True)
    sw = dict(
        w_in=w_in.astype(BF16), b_in=rows(b_in),
        w_att_o=w_att_o.astype(BF16), w_hy_o=w_hy_o.astype(BF16), w_fn_o=w_fn_o.astype(BF16),
        w_out=w_out.astype(BF16), b_out=rows(b_out), ln1_g=rows(ln1_g), ln1_b=rows(ln1_b),
        w_mlp1=w_mlp1.astype(BF16), b_mlp1=rows(b_mlp1), w_mlp2=w_mlp2.astype(BF16), b_mlp2=rows(b_mlp2),
        ln2_g=rows(ln2_g), ln2_b=rows(ln2_b))
    for i in range(depth):
        last = i == depth - 1
        lw = dict(
            conv_w=conv_w[i], conv_b=row(conv_b[i]),
            hy_w1=jnp.pad(hy_w1[i], ((0, V7X_LANES - HYENA_EMB), (0, 0))), hy_b1=row(hy_b1[i]), hy_freq1=row(hy_freq1[i]),
            hy_w2=hy_w2[i], hy_b2=row(hy_b2[i]), hy_freq2=row(hy_freq2[i]), hy_w3=hy_w3[i],
            hy_skip=hy_skip[i].reshape(1, HYENA_ORDER * D_HYENA),
            q_gain=row(q_gain[i]), k_gain=row(k_gain[i]))
        mod_l = mods[i, :bsz].reshape(bsz, N_MOD, d)
        mod_c = mods[i, bsz:bsz + 1].reshape(1, N_MOD, d)

        if last:
            kv_cols = slice(OFF_K, OFF_HY)
            p_c = _inproj(ctx_tok, mod_c, sw['w_in'][i:i + 1, :, kv_cols], sw['b_in'][i:i + 1, :, kv_cols], 0,
                          INPROJ_KV_PLAN, lw['q_gain'], lw['k_gain'], None)
            ctx_kv = (p_c.reshape(bsz, n_ctx, 2 * KV_W), 0, 1)
        else:
            ctx_tok, p_c = _block(ctx_tok, mod_c, (bsz, n_ctx), i, sw, lw, tabs_ctx, None, [])
            ctx_kv = (p_c, OFF_K // KV_W, OFF_V // KV_W)
        x, _ = _block(x, mod_l, (bsz, n_lat), i, sw, lw, tabs_lat, rope_tabs, [ctx_kv])
    return _interleave(x)
```
